```python
import jax, jax.numpy as jnp
from jax import lax
import numpy as np

D_MODEL = 1024
BATCH = 8
SEQ = 16384
DEPTH = 2

HG_HEADS = 8
HG_KEY_DIM = 128
HG_VAL_DIM = D_MODEL // HG_HEADS
HG_KEY = HG_HEADS * HG_KEY_DIM
HG_VAL = HG_HEADS * HG_VAL_DIM
HG_CHUNK = 64
SG_GROUPS = 8
SG_GROUP_DIM = 64
SG_WIDTH = SG_GROUPS * SG_GROUP_DIM
SG_CHUNK = 128
FFN_HIDDEN = ((8 * D_MODEL // 3 + 255) // 256) * 256
PLE_DIM = 256
EPS = 1e-6
IN_SPLITS = (HG_KEY, HG_KEY, HG_KEY, HG_VAL, HG_VAL, SG_WIDTH, SG_WIDTH, D_MODEL, D_MODEL)
N_IN = HG_KEY * 3 + HG_VAL * 2 + SG_WIDTH * 2 + D_MODEL * 2

kernel_name = "hgrn2_gmlp_gated_hybrid_encoder"


def rms_norm(x, g):
    xf = x.astype(jnp.float32)
    y = xf * lax.rsqrt(jnp.mean(xf * xf, axis=-1, keepdims=True) + EPS)
    return (y * g.astype(jnp.float32)).astype(x.dtype)


def layer_norm(x, g, b):
    xf = x.astype(jnp.float32)
    mu = jnp.mean(xf, axis=-1, keepdims=True)
    xc = xf - mu
    y = xc * lax.rsqrt(jnp.mean(xc * xc, axis=-1, keepdims=True) + EPS)
    return (y * g.astype(jnp.float32) + b.astype(jnp.float32)).astype(x.dtype)


def layer_lower_bounds(gamma):
    sm = jax.nn.softmax(gamma.astype(jnp.float32), axis=0)
    return jnp.cumsum(sm, axis=0) - sm[0:1]


def _to_chunks(t):
    b, s, h, d = t.shape
    return t.reshape(b, s // HG_CHUNK, HG_CHUNK, h, d).transpose(1, 0, 3, 2, 4)


def hgrn2_direction(q, k, v, logf):
    bsz, s, h, dk = q.shape
    dv = v.shape[-1]
    mask = jnp.tril(jnp.ones((HG_CHUNK, HG_CHUNK), dtype=jnp.float32))

    def step(state, inp):
        qc, kc, vc, gc = inp
        b = jnp.cumsum(gc, axis=2)
        o_inter = jnp.einsum('bhtk,bhkv->bhtv', qc * jnp.exp(b), state)
        diff = b[:, :, :, None, :] - b[:, :, None, :, :]
        decay = jnp.exp(jnp.minimum(diff, 0.0)) * mask[:, :, None]
        scores = jnp.einsum('bhtk,bhsk,bhtsk->bhts', qc, kc, decay)
        o_intra = jnp.einsum('bhts,bhsv->bhtv', scores, vc)
        b_last = b[:, :, -1:, :]
        new_state = (jnp.exp(b_last[:, :, 0, :])[..., None] * state
                     + jnp.einsum('bhsk,bhsv->bhkv', kc * jnp.exp(b_last - b), vc))
        return new_state, o_inter + o_intra

    init = jnp.zeros((bsz, h, dk, dv), jnp.float32)
    _, o = lax.scan(step, init, (_to_chunks(q), _to_chunks(k), _to_chunks(v), _to_chunks(logf)))
    return o.transpose(1, 0, 3, 2, 4).reshape(bsz, s, h, dv)


def hgrn2_mixer(zq, zf_fwd, zf_bwd, zi, zg, lb_f, lb_b, norm_g):
    bsz, s, _ = zq.shape
    f32 = jnp.float32
    tiny = jnp.finfo(f32).tiny
    q = jax.nn.silu(zq.astype(f32)).reshape(bsz, s, HG_HEADS, HG_KEY_DIM)
    v = zi.astype(f32).reshape(bsz, s, HG_HEADS, HG_VAL_DIM)

    def gates(zf, lb):
        zf = zf.astype(f32)
        f = lb + (1.0 - lb) * jax.nn.sigmoid(zf)
        logf = jnp.log(jnp.maximum(f, tiny))
        k = (1.0 - lb) * jax.nn.sigmoid(-zf)
        return (k.reshape(bsz, s, HG_HEADS, HG_KEY_DIM), logf.reshape(bsz, s, HG_HEADS, HG_KEY_DIM))

    k_f, logf_f = gates(zf_fwd, lb_f)
    k_b, logf_b = gates(zf_bwd, lb_b)
    o_fwd = hgrn2_direction(q, k_f, v, logf_f)
    o_bwd = hgrn2_direction(q[:, ::-1], k_b[:, ::-1], v[:, ::-1], logf_b[:, ::-1])[:, ::-1]
    o = o_fwd + o_bwd
    o = rms_norm(o, norm_g.reshape(HG_HEADS, HG_VAL_DIM)).reshape(bsz, s, HG_VAL)
    return (o * jax.nn.silu(zg.astype(f32))).astype(zq.dtype)


def spatial_gating(zu, zv, w_s, b_s, ln_g, ln_b):
    bsz, s, _ = zu.shape
    u = jax.nn.gelu(zu, approximate=False)
    v = layer_norm(jax.nn.gelu(zv, approximate=False), ln_g, ln_b)
    vr = v.reshape(bsz, s // SG_CHUNK, SG_CHUNK, SG_GROUPS, SG_GROUP_DIM)
    sg = jnp.einsum('gts,bcsge->bctge', w_s, vr) + b_s.T[None, None, :, :, None]
    return u * sg.reshape(bsz, s, SG_WIDTH)


def _fwd_setup_inputs(seed: int = 0) -> dict:
    key = jax.random.key(seed)
    ks = jax.random.split(key, 24)
    f32 = jnp.float32

    def nrm(k, shape, scale):
        return jax.random.normal(k, shape, f32) * scale

    def gain(k, shape):
        return 1.0 + 0.05 * jax.random.normal(k, shape, f32)

    return {
        "x": nrm(ks[0], (BATCH, SEQ, D_MODEL), 1.0),
        "p": nrm(ks[1], (DEPTH, BATCH, SEQ, PLE_DIM), 1.0),
        "norm_mix_pre": gain(ks[2], (DEPTH, D_MODEL)),
        "w_in": nrm(ks[3], (DEPTH, D_MODEL, N_IN), D_MODEL ** -0.5),
        "lb_gamma_fwd": nrm(ks[4], (DEPTH, HG_KEY), 0.1),
        "lb_gamma_bwd": nrm(ks[5], (DEPTH, HG_KEY), 0.1),
        "hg_norm": gain(ks[6], (DEPTH, HG_VAL)),
        "sg_w": nrm(ks[7], (DEPTH, SG_GROUPS, SG_CHUNK, SG_CHUNK), SG_CHUNK ** -0.5),
        "sg_b": gain(ks[8], (DEPTH, SG_GROUPS, SG_CHUNK)),
        "sg_ln_g": gain(ks[9], (DEPTH, SG_WIDTH)),
        "sg_ln_b": nrm(ks[10], (DEPTH, SG_WIDTH), 0.02),
        "w_a": nrm(ks[11], (DEPTH, HG_VAL, D_MODEL), HG_VAL ** -0.5),
        "w_b": nrm(ks[12], (DEPTH, SG_WIDTH, D_MODEL), SG_WIDTH ** -0.5),
        "w_out": nrm(ks[13], (DEPTH, D_MODEL, D_MODEL), D_MODEL ** -0.5),
        "norm_mix_post": gain(ks[14], (DEPTH, D_MODEL)),
        "norm_ffn_pre": gain(ks[15], (DEPTH, D_MODEL)),
        "w_gate": nrm(ks[16], (DEPTH, D_MODEL, FFN_HIDDEN), D_MODEL ** -0.5),
        "w_up": nrm(ks[17], (DEPTH, D_MODEL, FFN_HIDDEN), D_MODEL ** -0.5),
        "w_down": nrm(ks[18], (DEPTH, FFN_HIDDEN, D_MODEL), FFN_HIDDEN ** -0.5),
        "norm_ffn_post": gain(ks[19], (DEPTH, D_MODEL)),
        "w_ple": nrm(ks[20], (DEPTH, PLE_DIM, D_MODEL), PLE_DIM ** -0.5),
        "w_ple_gate": nrm(ks[21], (DEPTH, D_MODEL, D_MODEL), D_MODEL ** -0.5),
    }


def _fwd_reference(x, p, norm_mix_pre, w_in, lb_gamma_fwd, lb_gamma_bwd, hg_norm, sg_w, sg_b,
              sg_ln_g, sg_ln_b, w_a, w_b, w_out, norm_mix_post, norm_ffn_pre, w_gate, w_up,
              w_down, norm_ffn_post, w_ple, w_ple_gate):
    lb_fwd_all = layer_lower_bounds(lb_gamma_fwd)
    lb_bwd_all = layer_lower_bounds(lb_gamma_bwd)
    offsets = [int(o) for o in np.cumsum(IN_SPLITS)[:-1]]
    for l in range(DEPTH):
        h = rms_norm(x, norm_mix_pre[l])
        z = jnp.einsum('bsd,dn->bsn', h, w_in[l])
        zq, zf_f, zf_b, zi, zg, zu, zv, ga, gb = jnp.split(z, offsets, axis=-1)
        a_out = hgrn2_mixer(zq, zf_f, zf_b, zi, zg, lb_fwd_all[l], lb_bwd_all[l], hg_norm[l])
        b_out = spatial_gating(zu, zv, sg_w[l], sg_b[l], sg_ln_g[l], sg_ln_b[l])
        merged = (jax.nn.sigmoid(ga) * jnp.einsum('bsv,vd->bsd', a_out, w_a[l])
                  + jax.nn.sigmoid(gb) * jnp.einsum('bsw,wd->bsd', b_out, w_b[l]))
        mix = jnp.einsum('bsd,de->bse', merged, w_out[l])
        x = x + rms_norm(mix, norm_mix_post[l])
        h2 = rms_norm(x, norm_ffn_pre[l])
        ff = jnp.einsum('bsf,fd->bsd',
                        jax.nn.silu(jnp.einsum('bsd,df->bsf', h2, w_gate[l]))
                        * jnp.einsum('bsd,df->bsf', h2, w_up[l]), w_down[l])
        x = x + rms_norm(ff, norm_ffn_post[l])
        x = x + (jnp.einsum('bse,ed->bsd', p[l], w_ple[l])
                 * jax.nn.sigmoid(jnp.einsum('bsd,de->bse', x, w_ple_gate[l])))
    return x


import jax as _jax
import jax.numpy as _jnp

TWIN_FORMAT = 'train_step'
FWD_PARAMS = ['x', 'p', 'norm_mix_pre', 'w_in', 'lb_gamma_fwd', 'lb_gamma_bwd', 'hg_norm', 'sg_w', 'sg_b', 'sg_ln_g', 'sg_ln_b', 'w_a', 'w_b', 'w_out', 'norm_mix_post', 'norm_ffn_pre', 'w_gate', 'w_up', 'w_down', 'norm_ffn_post', 'w_ple', 'w_ple_gate']
TWIN_WEIGHTS = ['norm_mix_pre', 'w_in', 'lb_gamma_fwd', 'lb_gamma_bwd', 'hg_norm', 'sg_w', 'sg_b', 'sg_ln_g', 'sg_ln_b', 'w_a', 'w_b', 'w_out', 'norm_mix_post', 'norm_ffn_pre', 'w_gate', 'w_up', 'w_down', 'norm_ffn_post', 'w_ple', 'w_ple_gate']
TWIN_DIFF_INPUT = 'x'
TWIN_INPUTS = ['x', 'p', 'norm_mix_pre', 'w_in', 'lb_gamma_fwd', 'lb_gamma_bwd', 'hg_norm', 'sg_w', 'sg_b', 'sg_ln_g', 'sg_ln_b', 'w_a', 'w_b', 'w_out', 'norm_mix_post', 'norm_ffn_pre', 'w_gate', 'w_up', 'w_down', 'norm_ffn_post', 'w_ple', 'w_ple_gate', 'loss_target', 'm_norm_mix_pre', 'm_w_in', 'm_lb_gamma_fwd', 'm_lb_gamma_bwd', 'm_hg_norm', 'm_sg_w', 'm_sg_b', 'm_sg_ln_g', 'm_sg_ln_b', 'm_w_a', 'm_w_b', 'm_w_out', 'm_norm_mix_post', 'm_norm_ffn_pre', 'm_w_gate', 'm_w_up', 'm_w_down', 'm_norm_ffn_post', 'm_w_ple', 'm_w_ple_gate', 'v_norm_mix_pre', 'v_w_in', 'v_lb_gamma_fwd', 'v_lb_gamma_bwd', 'v_hg_norm', 'v_sg_w', 'v_sg_b', 'v_sg_ln_g', 'v_sg_ln_b', 'v_w_a', 'v_w_b', 'v_w_out', 'v_norm_mix_post', 'v_norm_ffn_pre', 'v_w_gate', 'v_w_up', 'v_w_down', 'v_norm_ffn_post', 'v_w_ple', 'v_w_ple_gate']
TWIN_OUTPUTS = ['loss', 'grad_x', 'grad_norm_mix_pre', 'grad_w_in', 'grad_lb_gamma_fwd', 'grad_lb_gamma_bwd', 'grad_hg_norm', 'grad_sg_w', 'grad_sg_b', 'grad_sg_ln_g', 'grad_sg_ln_b', 'grad_w_a', 'grad_w_b', 'grad_w_out', 'grad_norm_mix_post', 'grad_norm_ffn_pre', 'grad_w_gate', 'grad_w_up', 'grad_w_down', 'grad_norm_ffn_post', 'grad_w_ple', 'grad_w_ple_gate', 'delta_norm_mix_pre', 'delta_w_in', 'delta_lb_gamma_fwd', 'delta_lb_gamma_bwd', 'delta_hg_norm', 'delta_sg_w', 'delta_sg_b', 'delta_sg_ln_g', 'delta_sg_ln_b', 'delta_w_a', 'delta_w_b', 'delta_w_out', 'delta_norm_mix_post', 'delta_norm_ffn_pre', 'delta_w_gate', 'delta_w_up', 'delta_w_down', 'delta_norm_ffn_post', 'delta_w_ple', 'delta_w_ple_gate', 'new_m_norm_mix_pre', 'new_m_w_in', 'new_m_lb_gamma_fwd', 'new_m_lb_gamma_bwd', 'new_m_hg_norm', 'new_m_sg_w', 'new_m_sg_b', 'new_m_sg_ln_g', 'new_m_sg_ln_b', 'new_m_w_a', 'new_m_w_b', 'new_m_w_out', 'new_m_norm_mix_post', 'new_m_norm_ffn_pre', 'new_m_w_gate', 'new_m_w_up', 'new_m_w_down', 'new_m_norm_ffn_post', 'new_m_w_ple', 'new_m_w_ple_gate', 'new_v_norm_mix_pre', 'new_v_w_in', 'new_v_lb_gamma_fwd', 'new_v_lb_gamma_bwd', 'new_v_hg_norm', 'new_v_sg_w', 'new_v_sg_b', 'new_v_sg_ln_g', 'new_v_sg_ln_b', 'new_v_w_a', 'new_v_w_b', 'new_v_w_out', 'new_v_norm_mix_post', 'new_v_norm_ffn_pre', 'new_v_w_gate', 'new_v_w_up', 'new_v_w_down', 'new_v_norm_ffn_post', 'new_v_w_ple', 'new_v_w_ple_gate']
TWIN_LEAF_KINDS = {'loss': 'loss', 'grad_x': 'grad_x', 'grad_norm_mix_pre': 'grad_w', 'grad_w_in': 'grad_w', 'grad_lb_gamma_fwd': 'grad_w', 'grad_lb_gamma_bwd': 'grad_w', 'grad_hg_norm': 'grad_w', 'grad_sg_w': 'grad_w', 'grad_sg_b': 'grad_w', 'grad_sg_ln_g': 'grad_w', 'grad_sg_ln_b': 'grad_w', 'grad_w_a': 'grad_w', 'grad_w_b': 'grad_w', 'grad_w_out': 'grad_w', 'grad_norm_mix_post': 'grad_w', 'grad_norm_ffn_pre': 'grad_w', 'grad_w_gate': 'grad_w', 'grad_w_up': 'grad_w', 'grad_w_down': 'grad_w', 'grad_norm_ffn_post': 'grad_w', 'grad_w_ple': 'grad_w', 'grad_w_ple_gate': 'grad_w', 'delta_norm_mix_pre': 'delta_w', 'delta_w_in': 'delta_w', 'delta_lb_gamma_fwd': 'delta_w', 'delta_lb_gamma_bwd': 'delta_w', 'delta_hg_norm': 'delta_w', 'delta_sg_w': 'delta_w', 'delta_sg_b': 'delta_w', 'delta_sg_ln_g': 'delta_w', 'delta_sg_ln_b': 'delta_w', 'delta_w_a': 'delta_w', 'delta_w_b': 'delta_w', 'delta_w_out': 'delta_w', 'delta_norm_mix_post': 'delta_w', 'delta_norm_ffn_pre': 'delta_w', 'delta_w_gate': 'delta_w', 'delta_w_up': 'delta_w', 'delta_w_down': 'delta_w', 'delta_norm_ffn_post': 'delta_w', 'delta_w_ple': 'delta_w', 'delta_w_ple_gate': 'delta_w', 'new_m_norm_mix_pre': 'new_m', 'new_m_w_in': 'new_m', 'new_m_lb_gamma_fwd': 'new_m', 'new_m_lb_gamma_bwd': 'new_m', 'new_m_hg_norm': 'new_m', 'new_m_sg_w': 'new_m', 'new_m_sg_b': 'new_m', 'new_m_sg_ln_g': 'new_m', 'new_m_sg_ln_b': 'new_m', 'new_m_w_a': 'new_m', 'new_m_w_b': 'new_m', 'new_m_w_out': 'new_m', 'new_m_norm_mix_post': 'new_m', 'new_m_norm_ffn_pre': 'new_m', 'new_m_w_gate': 'new_m', 'new_m_w_up': 'new_m', 'new_m_w_down': 'new_m', 'new_m_norm_ffn_post': 'new_m', 'new_m_w_ple': 'new_m', 'new_m_w_ple_gate': 'new_m', 'new_v_norm_mix_pre': 'new_v', 'new_v_w_in': 'new_v', 'new_v_lb_gamma_fwd': 'new_v', 'new_v_lb_gamma_bwd': 'new_v', 'new_v_hg_norm': 'new_v', 'new_v_sg_w': 'new_v', 'new_v_sg_b': 'new_v', 'new_v_sg_ln_g': 'new_v', 'new_v_sg_ln_b': 'new_v', 'new_v_w_a': 'new_v', 'new_v_w_b': 'new_v', 'new_v_w_out': 'new_v', 'new_v_norm_mix_post': 'new_v', 'new_v_norm_ffn_pre': 'new_v', 'new_v_w_gate': 'new_v', 'new_v_w_up': 'new_v', 'new_v_w_down': 'new_v', 'new_v_norm_ffn_post': 'new_v', 'new_v_w_ple': 'new_v', 'new_v_w_ple_gate': 'new_v'}


def _forward(args):
    return _fwd_reference(*[args[k] for k in FWD_PARAMS])


def _output_shape():
    def fwd():
        inp = _fwd_setup_inputs(0)
        return _fwd_reference(*[inp[k] for k in FWD_PARAMS])
    out = _jax.eval_shape(fwd)
    return out.shape, out.dtype

N_MICROBATCH = 1
ADAM_LR = 0.001
ADAM_B1 = 0.9
ADAM_B2 = 0.999
ADAM_EPS = 1e-08
ADAM_WD = 0.01
ADAM_STEP = 10
PER_EXAMPLE_BATCH_AXIS = {'x': 0, 'p': 1, 'loss_target': 0}
SHARED_INPUTS = []
_WEIGHT_DTYPES = {'norm_mix_pre': _jnp.float32, 'w_in': _jnp.float32, 'lb_gamma_fwd': _jnp.float32, 'lb_gamma_bwd': _jnp.float32, 'hg_norm': _jnp.float32, 'sg_w': _jnp.float32, 'sg_b': _jnp.float32, 'sg_ln_g': _jnp.float32, 'sg_ln_b': _jnp.float32, 'w_a': _jnp.float32, 'w_b': _jnp.float32, 'w_out': _jnp.float32, 'norm_mix_post': _jnp.float32, 'norm_ffn_pre': _jnp.float32, 'w_gate': _jnp.float32, 'w_up': _jnp.float32, 'w_down': _jnp.float32, 'norm_ffn_post': _jnp.float32, 'w_ple': _jnp.float32, 'w_ple_gate': _jnp.float32}
MOMENT_SCALE = {'norm_mix_pre': 2.844119e+00, 'w_in': 1.041526e+00, 'lb_gamma_fwd': 3.515823e-02, 'lb_gamma_bwd': 3.588482e-02, 'hg_norm': 1.434172e+00, 'sg_w': 9.228203e-01, 'sg_b': 9.256313e-01, 'sg_ln_g': 1.659750e+00, 'sg_ln_b': 1.441772e+00, 'w_a': 1.414854e+00, 'w_b': 7.761960e+00, 'w_out': 7.697429e+00, 'norm_mix_post': 1.314702e+02, 'norm_ffn_pre': 3.331926e+00, 'w_gate': 1.037850e+00, 'w_up': 1.821728e+00, 'w_down': 3.062937e+00, 'norm_ffn_post': 1.301685e+02, 'w_ple': 1.665300e+00, 'w_ple_gate': 2.072780e+00}


def _to_microbatches(a, axis):
    t = _jnp.moveaxis(a, axis, 0)
    t = t.reshape((N_MICROBATCH, t.shape[0] // N_MICROBATCH) + t.shape[1:])
    return _jnp.moveaxis(t, 1, axis + 1)


def setup_inputs(seed: int = 0) -> dict:
    inp = _fwd_setup_inputs(seed)
    key = _jax.random.fold_in(_jax.random.key(seed), 7919)
    shape, _ = _output_shape()
    out = dict(inp)
    out["loss_target"] = _jax.random.normal(_jax.random.fold_in(key, 0), shape, _jnp.float32)
    for i, name in enumerate(TWIN_WEIGHTS):
        w = inp[name].astype(_jnp.float32)
        if MOMENT_SCALE is None:
            s = _jnp.sqrt(_jnp.mean(_jnp.square(w)) + 1e-30)
        else:
            s = MOMENT_SCALE[name]
        km, kv = _jax.random.split(_jax.random.fold_in(key, i + 1))
        out[name] = w
        out["m_" + name] = s * _jax.random.normal(km, w.shape, _jnp.float32)
        out["v_" + name] = (s * s) * _jax.random.uniform(kv, w.shape, _jnp.float32, 0.5, 1.5)
    if N_MICROBATCH > 1:
        for name, axis in PER_EXAMPLE_BATCH_AXIS.items():
            out[name] = _to_microbatches(out[name], axis)
    return {'x': out['x'], 'p': out['p'], 'norm_mix_pre': out['norm_mix_pre'], 'w_in': out['w_in'], 'lb_gamma_fwd': out['lb_gamma_fwd'], 'lb_gamma_bwd': out['lb_gamma_bwd'], 'hg_norm': out['hg_norm'], 'sg_w': out['sg_w'], 'sg_b': out['sg_b'], 'sg_ln_g': out['sg_ln_g'], 'sg_ln_b': out['sg_ln_b'], 'w_a': out['w_a'], 'w_b': out['w_b'], 'w_out': out['w_out'], 'norm_mix_post': out['norm_mix_post'], 'norm_ffn_pre': out['norm_ffn_pre'], 'w_gate': out['w_gate'], 'w_up': out['w_up'], 'w_down': out['w_down'], 'norm_ffn_post': out['norm_ffn_post'], 'w_ple': out['w_ple'], 'w_ple_gate': out['w_ple_gate'], 'loss_target': out['loss_target'], 'm_norm_mix_pre': out['m_norm_mix_pre'], 'm_w_in': out['m_w_in'], 'm_lb_gamma_fwd': out['m_lb_gamma_fwd'], 'm_lb_gamma_bwd': out['m_lb_gamma_bwd'], 'm_hg_norm': out['m_hg_norm'], 'm_sg_w': out['m_sg_w'], 'm_sg_b': out['m_sg_b'], 'm_sg_ln_g': out['m_sg_ln_g'], 'm_sg_ln_b': out['m_sg_ln_b'], 'm_w_a': out['m_w_a'], 'm_w_b': out['m_w_b'], 'm_w_out': out['m_w_out'], 'm_norm_mix_post': out['m_norm_mix_post'], 'm_norm_ffn_pre': out['m_norm_ffn_pre'], 'm_w_gate': out['m_w_gate'], 'm_w_up': out['m_w_up'], 'm_w_down': out['m_w_down'], 'm_norm_ffn_post': out['m_norm_ffn_post'], 'm_w_ple': out['m_w_ple'], 'm_w_ple_gate': out['m_w_ple_gate'], 'v_norm_mix_pre': out['v_norm_mix_pre'], 'v_w_in': out['v_w_in'], 'v_lb_gamma_fwd': out['v_lb_gamma_fwd'], 'v_lb_gamma_bwd': out['v_lb_gamma_bwd'], 'v_hg_norm': out['v_hg_norm'], 'v_sg_w': out['v_sg_w'], 'v_sg_b': out['v_sg_b'], 'v_sg_ln_g': out['v_sg_ln_g'], 'v_sg_ln_b': out['v_sg_ln_b'], 'v_w_a': out['v_w_a'], 'v_w_b': out['v_w_b'], 'v_w_out': out['v_w_out'], 'v_norm_mix_post': out['v_norm_mix_post'], 'v_norm_ffn_pre': out['v_norm_ffn_pre'], 'v_w_gate': out['v_w_gate'], 'v_w_up': out['v_w_up'], 'v_w_down': out['v_w_down'], 'v_norm_ffn_post': out['v_norm_ffn_post'], 'v_w_ple': out['v_w_ple'], 'v_w_ple_gate': out['v_w_ple_gate']}


def _loss(weights, diff, rest, loss_target):
    with _jax.named_scope("forward"):
        args = {**rest, TWIN_DIFF_INPUT: diff, **{k: w.astype(_WEIGHT_DTYPES[k]) for k, w in weights.items()}}
        y = _forward(args)
    with _jax.named_scope("loss_head"):
        err = _jnp.square(y.astype(_jnp.float32) - loss_target)
        return 0.5 * _jnp.sum(_jnp.mean(err, axis=-1)) if err.ndim else 0.5 * err


def _adamw(w, g, m, v):
    m = ADAM_B1 * m + (1.0 - ADAM_B1) * g
    v = ADAM_B2 * v + (1.0 - ADAM_B2) * _jnp.square(g)
    m_hat = m / (1.0 - ADAM_B1 ** ADAM_STEP)
    v_hat = v / (1.0 - ADAM_B2 ** ADAM_STEP)
    delta = -ADAM_LR * (m_hat / (_jnp.sqrt(v_hat) + ADAM_EPS) + ADAM_WD * w)
    return delta, m, v


def reference(x, p, norm_mix_pre, w_in, lb_gamma_fwd, lb_gamma_bwd, hg_norm, sg_w, sg_b, sg_ln_g, sg_ln_b, w_a, w_b, w_out, norm_mix_post, norm_ffn_pre, w_gate, w_up, w_down, norm_ffn_post, w_ple, w_ple_gate, loss_target, m_norm_mix_pre, m_w_in, m_lb_gamma_fwd, m_lb_gamma_bwd, m_hg_norm, m_sg_w, m_sg_b, m_sg_ln_g, m_sg_ln_b, m_w_a, m_w_b, m_w_out, m_norm_mix_post, m_norm_ffn_pre, m_w_gate, m_w_up, m_w_down, m_norm_ffn_post, m_w_ple, m_w_ple_gate, v_norm_mix_pre, v_w_in, v_lb_gamma_fwd, v_lb_gamma_bwd, v_hg_norm, v_sg_w, v_sg_b, v_sg_ln_g, v_sg_ln_b, v_w_a, v_w_b, v_w_out, v_norm_mix_post, v_norm_ffn_pre, v_w_gate, v_w_up, v_w_down, v_norm_ffn_post, v_w_ple, v_w_ple_gate):
    given = dict(x=x, p=p, norm_mix_pre=norm_mix_pre, w_in=w_in, lb_gamma_fwd=lb_gamma_fwd, lb_gamma_bwd=lb_gamma_bwd, hg_norm=hg_norm, sg_w=sg_w, sg_b=sg_b, sg_ln_g=sg_ln_g, sg_ln_b=sg_ln_b, w_a=w_a, w_b=w_b, w_out=w_out, norm_mix_post=norm_mix_post, norm_ffn_pre=norm_ffn_pre, w_gate=w_gate, w_up=w_up, w_down=w_down, norm_ffn_post=norm_ffn_post, w_ple=w_ple, w_ple_gate=w_ple_gate, loss_target=loss_target, m_norm_mix_pre=m_norm_mix_pre, m_w_in=m_w_in, m_lb_gamma_fwd=m_lb_gamma_fwd, m_lb_gamma_bwd=m_lb_gamma_bwd, m_hg_norm=m_hg_norm, m_sg_w=m_sg_w, m_sg_b=m_sg_b, m_sg_ln_g=m_sg_ln_g, m_sg_ln_b=m_sg_ln_b, m_w_a=m_w_a, m_w_b=m_w_b, m_w_out=m_w_out, m_norm_mix_post=m_norm_mix_post, m_norm_ffn_pre=m_norm_ffn_pre, m_w_gate=m_w_gate, m_w_up=m_w_up, m_w_down=m_w_down, m_norm_ffn_post=m_norm_ffn_post, m_w_ple=m_w_ple, m_w_ple_gate=m_w_ple_gate, v_norm_mix_pre=v_norm_mix_pre, v_w_in=v_w_in, v_lb_gamma_fwd=v_lb_gamma_fwd, v_lb_gamma_bwd=v_lb_gamma_bwd, v_hg_norm=v_hg_norm, v_sg_w=v_sg_w, v_sg_b=v_sg_b, v_sg_ln_g=v_sg_ln_g, v_sg_ln_b=v_sg_ln_b, v_w_a=v_w_a, v_w_b=v_w_b, v_w_out=v_w_out, v_norm_mix_post=v_norm_mix_post, v_norm_ffn_pre=v_norm_ffn_pre, v_w_gate=v_w_gate, v_w_up=v_w_up, v_w_down=v_w_down, v_norm_ffn_post=v_norm_ffn_post, v_w_ple=v_w_ple, v_w_ple_gate=v_w_ple_gate)
    weights = {n: given[n] for n in TWIN_WEIGHTS}
    shared = {n: given[n] for n in SHARED_INPUTS}
    per_example = {n: given[n] for n in ['x', 'p']}
    grad_fn = _jax.value_and_grad(_loss, argnums=(0, 1))

    def one_microbatch(ex, loss_target):
        ex = dict(ex)
        diff = ex.pop(TWIN_DIFF_INPUT)
        return grad_fn(weights, diff, {**shared, **ex}, loss_target)

    if N_MICROBATCH == 1:
        loss, (grad_w, grad_x) = one_microbatch(per_example, given["loss_target"])
    else:
        def body(carry, xs):
            loss_sum, grad_sum = carry
            l_k, (gw_k, gx_k) = one_microbatch(xs[0], xs[1])
            with _jax.named_scope("update"):
                return (loss_sum + l_k, _jax.tree.map(_jnp.add, grad_sum, gw_k)), gx_k

        init = (_jnp.zeros((), _jnp.float32), _jax.tree.map(_jnp.zeros_like, weights))
        (loss, grad_w), grad_x = _jax.lax.scan(body, init, (per_example, given["loss_target"]))
    with _jax.named_scope("update"):
        delta_w, new_m, new_v = {}, {}, {}
        for n in TWIN_WEIGHTS:
            delta_w[n], new_m[n], new_v[n] = _adamw(weights[n], grad_w[n], given["m_" + n], given["v_" + n])
    return (loss, grad_x, *[grad_w[n] for n in TWIN_WEIGHTS], *[delta_w[n] for n in TWIN_WEIGHTS],
            *[new_m[n] for n in TWIN_WEIGHTS], *[new_v[n] for n in TWIN_WEIGHTS])
```

```python
import functools

import jax
import jax.numpy as jnp
from jax import lax
from jax.experimental import pallas as pl
from jax.experimental.pallas import tpu as pltpu

F32 = jnp.float32
MXU_DTYPE = jnp.bfloat16

D_MODEL = 1024
N_HEADS = 8
HEAD = 128
HG_CHUNK = 64
HG_SUB = 16
HG_BLOCK = 256
SG_CHUNK = 128
SG_WIDTH = 512
FFN_SHARD = 704
PLE_DIM = 256
N_IN = 8192
N_CHIPS = 4
N_DEV = 8
EPS = 1e-6
LANES = 128
VMEM_LIMIT_BYTES = 56 * 2 ** 20

ADAM_LR = 0.001
ADAM_B1 = 0.9
ADAM_B2 = 0.999
ADAM_EPS = 1e-08
ADAM_WD = 0.01
ADAM_STEP = 10
ADAM_TILE_ELEMS = 128 * 1024

MESH = pl.DeviceIdType.MESH
ANY = pl.BlockSpec(memory_space=pl.ANY)


def _pcall(body, **kw):
    return pl.pallas_call(body, **kw)


def _params(n_axes):
    return pltpu.CompilerParams(dimension_semantics=("arbitrary",) * n_axes, vmem_limit_bytes=VMEM_LIMIT_BYTES)


def _dot(a, b, ca, cb):
    return lax.dot_general(a.astype(MXU_DTYPE), b.astype(MXU_DTYPE), (((ca,), (cb,)), ((), ())),
                           preferred_element_type=F32)


def _nn(a, b):
    return _dot(a, b, 1, 0)


def _nt(a, b):
    return _dot(a, b, 1, 1)


def _tn(a, b):
    return _dot(a, b, 0, 0)


NN, NT, TN = (1, 0), (1, 1), (0, 0)


def _sigmoid(x):
    return jax.nn.sigmoid(x)


def _dsilu(x, s):
    return s * (1.0 + x * (1.0 - s))


_SQRT_HALF = 0.7071067811865476
_INV_SQRT_2PI = 0.3989422804014327


def _gelu(x):
    return 0.5 * x * (1.0 + lax.erf(x * _SQRT_HALF))


def _dgelu(x):
    return 0.5 * (1.0 + lax.erf(x * _SQRT_HALF)) + x * jnp.exp(-0.5 * x * x) * _INV_SQRT_2PI


def _mean_last(x):
    return jnp.mean(x, axis=-1, keepdims=True)


def _rowsum(x):
    return jnp.sum(x, axis=0, keepdims=True)


def _mm(name, pairs, kind, out_shape, grid, in_specs, out_spec, *, reduce_axis=None, add=None,
        add_spec=None, into=None, prep=None):
    n_pairs = len(pairs)
    has_add = add is not None
    has_into = into is not None

    def body(*refs):
        o_ref = refs[-1]
        acc = None
        for i in range(n_pairs):
            a = refs[2 * i][...]
            b = refs[2 * i + 1][...]
            if prep is not None:
                b = prep(b)
            prod = _dot(a, b, *kind)
            acc = prod if acc is None else acc + prod
        if has_add:
            acc = acc + refs[2 * n_pairs][...]
        if reduce_axis is None:
            o_ref[...] = acc.astype(o_ref.dtype)
        else:
            r = pl.program_id(reduce_axis)

            @pl.when(r == 0)
            def _():
                o_ref[...] = acc

            @pl.when(r > 0)
            def _():
                o_ref[...] += acc

    args = [t for pair in pairs for t in pair]
    specs = list(in_specs)
    if has_add:
        args.append(add)
        specs.append(add_spec)
    aliases = {}
    if has_into:
        aliases = {len(args): 0}
        args.append(into)
        specs.append(ANY)
    return _pcall(body, name=name, grid=grid, in_specs=specs, out_specs=out_spec, out_shape=out_shape,
                  input_output_aliases=aliases, compiler_params=_params(len(grid)))(*args)


def _sds(shape, dtype):
    return jax.ShapeDtypeStruct(tuple(shape), dtype)


def _bs(shape, fn):
    return pl.BlockSpec(tuple(shape), fn)


TM = 1024


def _merge_lead(b):
    return b.reshape(b.shape[0] * b.shape[1], b.shape[2])


def _proj_in(h, w_in_g, l):
    t = h.shape[0]
    return _mm(f"proj_in_l{l}", [(h, w_in_g)], NN, _sds((t, N_IN), F32), (8, t // TM),
               [_bs((TM, D_MODEL), lambda n, m: (m, 0)),
                _bs((None, None, D_MODEL, 1024), lambda n, m: (n // 2, l, 0, n % 2))],
               _bs((TM, 1024), lambda n, m: (m, n)))


def _proj_rows_sharded(name, a, w_g, l, out_dtype, add=None):
    t = a.shape[0]
    return _mm(name, [(a, w_g)], NN, _sds((t, D_MODEL), out_dtype), (t // TM,),
               [_bs((TM, D_MODEL), lambda m: (m, 0)),
                _bs((N_CHIPS, None, 256, D_MODEL), lambda m: (0, l, 0, 0))],
               _bs((TM, D_MODEL), lambda m: (m, 0)), prep=_merge_lead, add=add,
               add_spec=_bs((TM, D_MODEL), lambda m: (m, 0)))


def _proj_rows_sharded_t(name, g, w_g, l, out_dtype, add=None):
    t = g.shape[0]
    return _mm(name, [(g, w_g)], NT, _sds((t, D_MODEL), out_dtype), (t // TM,),
               [_bs((TM, D_MODEL), lambda m: (m, 0)),
                _bs((N_CHIPS, None, 256, D_MODEL), lambda m: (0, l, 0, 0))],
               _bs((TM, D_MODEL), lambda m: (m, 0)), prep=_merge_lead, add=add,
               add_spec=_bs((TM, D_MODEL), lambda m: (m, 0)))


def _proj_cols256(name, a, w_g, l):
    t, k = a.shape
    return _mm(name, [(a, w_g)], NN, _sds((t, D_MODEL), F32), (N_CHIPS, t // TM),
               [_bs((TM, k), lambda j, m: (m, 0)),
                _bs((None, None, k, 256), lambda j, m: (j, l, 0, 0))],
               _bs((TM, 256), lambda j, m: (m, j)))


def _proj_cols256_t(name, g, w_g, l):
    t = g.shape[0]
    k = w_g.shape[2]
    return _mm(name, [(g, w_g)], NT, _sds((t, k), F32), (t // TM, N_CHIPS),
               [_bs((TM, 256), lambda m, j: (m, j)),
                _bs((None, None, k, 256), lambda m, j: (j, l, 0, 0))],
               _bs((TM, k), lambda m, j: (m, 0)), reduce_axis=1)


def _proj_ffn_in(name, h2, w_g, l):
    t = h2.shape[0]
    return _mm(name, [(h2, w_g)], NN, _sds((N_CHIPS, t, FFN_SHARD), F32), (N_CHIPS, t // TM),
               [_bs((TM, D_MODEL), lambda j, m: (m, 0)),
                _bs((None, None, D_MODEL, FFN_SHARD), lambda j, m: (j, l, 0, 0))],
               _bs((None, TM, FFN_SHARD), lambda j, m: (j, m, 0)))


def _proj_ffn_in_t(name, pairs, l):
    t = pairs[0][0].shape[1]
    specs = []
    for _ in pairs:
        specs += [_bs((None, TM, FFN_SHARD), lambda m, j: (j, m, 0)),
                  _bs((None, None, D_MODEL, FFN_SHARD), lambda m, j: (j, l, 0, 0))]
    return _mm(name, pairs, NT, _sds((t, D_MODEL), F32), (t // TM, N_CHIPS), specs,
               _bs((TM, D_MODEL), lambda m, j: (m, 0)), reduce_axis=1)


def _proj_ffn_out(name, act, w_g, l):
    t = act.shape[1]
    return _mm(name, [(act, w_g)], NN, _sds((t, D_MODEL), F32), (t // TM, N_CHIPS),
               [_bs((None, TM, FFN_SHARD), lambda m, j: (j, m, 0)),
                _bs((None, None, FFN_SHARD, D_MODEL), lambda m, j: (j, l, 0, 0))],
               _bs((TM, D_MODEL), lambda m, j: (m, 0)), reduce_axis=1)


def _proj_ffn_out_t(name, g, w_g, l):
    t = g.shape[0]
    return _mm(name, [(g, w_g)], NT, _sds((N_CHIPS, t, FFN_SHARD), F32), (N_CHIPS, t // TM),
               [_bs((TM, D_MODEL), lambda j, m: (m, 0)),
                _bs((None, None, FFN_SHARD, D_MODEL), lambda j, m: (j, l, 0, 0))],
               _bs((None, TM, FFN_SHARD), lambda j, m: (j, m, 0)))


def _proj_in_t(dz, w_in_g, l):
    t = dz.shape[0]
    return _mm(f"proj_in_t_l{l}", [(dz, w_in_g)], NT, _sds((t, D_MODEL), F32), (t // TM, 8),
               [_bs((TM, 1024), lambda m, n: (m, n)),
                _bs((None, None, D_MODEL, 1024), lambda m, n: (n // 2, l, 0, n % 2))],
               _bs((TM, D_MODEL), lambda m, n: (m, 0)), reduce_axis=1)


TT = 1024


def _wgrad(name, a, a_spec, g, g_spec, shard_shape, o_map, n_outer, l, into):
    t = a.shape[-2]
    out = _sds((N_CHIPS, 2) + tuple(shard_shape), F32)
    return _mm(name, [(a, g)], TN, out, (n_outer, t // TT), [a_spec, g_spec],
               _bs((None, None) + tuple(o_map[0]), o_map[1]), reduce_axis=1, into=into)


def _rows(name, fn, n_tiles, ins, in_specs, out_shapes, out_specs, n_red=0, into=()):
    n_in = len(ins)
    n_out = len(out_shapes)

    def body(*refs):
        in_refs = refs[:n_in]
        out_refs = refs[len(refs) - n_out:]
        vals = fn(*[r[...] for r in in_refs])
        if not isinstance(vals, (tuple, list)):
            vals = (vals,)
        first = pl.program_id(0) == 0
        for j in range(n_out):
            o_ref = out_refs[j]
            val = vals[j]
            if j < n_out - n_red:
                o_ref[...] = val.astype(o_ref.dtype)
            else:
                @pl.when(first)
                def _(o_ref=o_ref, val=val):
                    o_ref[...] = val

                @pl.when(jnp.logical_not(first))
                def _(o_ref=o_ref, val=val):
                    o_ref[...] += val

    args = list(ins)
    specs = list(in_specs)
    aliases = {}
    for buf, out_idx in into:
        aliases[len(args)] = out_idx
        args.append(buf)
        specs.append(ANY)
    res = _pcall(body, name=name, grid=(n_tiles,), in_specs=specs, out_specs=list(out_specs),
                 out_shape=list(out_shapes), input_output_aliases=aliases, compiler_params=_params(1))(*args)
    return res


def _tile_spec(tile, width, blk=0):
    return pl.BlockSpec((tile, width), lambda i: (i, blk))


def _whole(shape):
    nd = len(shape)
    return pl.BlockSpec(tuple(shape), lambda i: (0,) * nd)


def _rms(x, g):
    r = lax.rsqrt(_mean_last(x * x) + EPS)
    return (x * r) * g


def _rms_bwd_math(u, g, dy):
    r = lax.rsqrt(_mean_last(u * u) + EPS)
    uh = u * r
    gdy = dy * g
    du = r * (gdy - uh * _mean_last(gdy * uh))
    return du, _rowsum(dy * uh)


def _rms_fwd(name, x, g):
    t, d = x.shape
    tile = 512
    return _rows(name, lambda xv, gv: (_rms(xv, gv),), t // tile, [x, g],
                 [_tile_spec(tile, d), _whole((1, d))], [_sds((t, d), MXU_DTYPE)], [_tile_spec(tile, d)])[0]


def _resid_rms_fwd(name, x, y, g):
    t, d = x.shape
    tile = 512
    return _rows(name, lambda xv, yv, gv: (xv + _rms(yv, gv),), t // tile, [x, y, g],
                 [_tile_spec(tile, d), _tile_spec(tile, d), _whole((1, d))], [_sds((t, d), F32)],
                 [_tile_spec(tile, d)])[0]


def _rms_bwd(name, u, g, dy, resid, out_dtype):
    t, d = u.shape
    tile = 256

    def fn(uv, gv, dyv, *rest):
        du, dg = _rms_bwd_math(uv, gv, dyv)
        if rest:
            du = du + rest[0]
        return du, dg

    ins = [u, g, dy] + ([resid] if resid is not None else [])
    specs = [_tile_spec(tile, d), _whole((1, d)), _tile_spec(tile, d)] + ([_tile_spec(tile, d)] if resid is not None else [])
    return _rows(name, fn, t // tile, ins, specs, [_sds((t, d), out_dtype), _sds((1, d), F32)],
                 [_tile_spec(tile, d), _whole((1, d))], n_red=1)


def _cumsum_rows(x, group, suffix):
    n = x.shape[0]
    pos = lax.broadcasted_iota(jnp.int32, x.shape, 0) % group
    d = 1
    while d < group:
        if suffix:
            x = x + jnp.where(pos < group - d, pltpu.roll(x, n - d, 0), 0.0)
        else:
            x = x + jnp.where(pos >= d, pltpu.roll(x, d, 0), 0.0)
        d *= 2
    return x


def _hg_gates(zq, zf, lb):
    q = zq * _sigmoid(zq)
    f = lb + (1.0 - lb) * _sigmoid(zf)
    logf = jnp.log(jnp.maximum(f, jnp.finfo(F32).tiny))
    k = (1.0 - lb) * _sigmoid(-zf)
    return q, k, logf, f


def _tri_mask(rev):
    t_i = lax.broadcasted_iota(jnp.int32, (HG_SUB, HG_SUB), 0)
    s_i = lax.broadcasted_iota(jnp.int32, (HG_SUB, HG_SUB), 1)
    return (s_i >= t_i) if rev else (s_i <= t_i)


def _sub_ranges(base, i_sub, rev):
    r0 = base + i_sub * HG_SUB
    r1 = r0 + HG_SUB
    if rev:
        e0, e1, anchor = r1, base + HG_CHUNK, r1
    else:
        e0, e1, anchor = base, r0, r0 - 1
    return r0, r1, e0, e1, anchor


def _hgrn_fwd(name, z, lb_row, rev):
    t = z.shape[0]
    nb = t // HG_BLOCK
    ncb = HG_BLOCK // HG_CHUNK
    nsb = HG_CHUNK // HG_SUB
    zf0 = 16 if rev else 8

    def tmap(i):
        return nb - 1 - i if rev else i

    def body(zq_ref, zf_ref, zi_ref, lb_ref, o_ref, st_ref, state, q_s, k_s, v_s, b_s):
        @pl.when(pl.program_id(1) == 0)
        def _():
            state[...] = jnp.zeros_like(state)

        q, k, logf, _ = _hg_gates(zq_ref[...], zf_ref[...], lb_ref[...])
        q_s[...] = q
        k_s[...] = k
        v_s[...] = zi_ref[...]
        b_s[...] = _cumsum_rows(logf, HG_CHUNK, rev)
        mask = _tri_mask(rev)
        order = range(ncb - 1, -1, -1) if rev else range(ncb)
        for c in order:
            base = c * HG_CHUNK
            rows = slice(base, base + HG_CHUNK)
            last = base if rev else base + HG_CHUNK - 1
            st = state[...]
            st_ref[c] = st
            b = b_s[rows, :]
            bl = b_s[last:last + 1, :]
            o_inter = _nt(q_s[rows, :] * jnp.exp(b), st)
            kd = k_s[rows, :] * jnp.exp(bl - b)
            state[...] = st * jnp.exp(bl) + _tn(v_s[rows, :], kd)
            parts = []
            for i_sub in range(nsb):
                r0, r1, e0, e1, anchor = _sub_ranges(base, i_sub, rev)
                q_i, k_i, b_i = q_s[r0:r1, :], k_s[r0:r1, :], b_s[r0:r1, :]
                decay = jnp.exp(jnp.minimum(b_i[:, None, :] - b_i[None, :, :], 0.0))
                a_d = jnp.where(mask, jnp.sum(q_i[:, None, :] * k_i[None, :, :] * decay, axis=-1), 0.0)
                o_i = _nn(a_d, v_s[r0:r1, :])
                if e1 > e0:
                    anc = b_s[anchor:anchor + 1, :]
                    q_t = q_i * jnp.exp(b_i - anc)
                    k_t = k_s[e0:e1, :] * jnp.exp(anc - b_s[e0:e1, :])
                    o_i = o_i + _nn(_nt(q_t, k_t), v_s[e0:e1, :])
                parts.append(o_i)
            o_ref[rows, :] = o_inter + jnp.concatenate(parts, axis=0)

    blk = lambda off: pl.BlockSpec((HG_BLOCK, HEAD), lambda h, i: (tmap(i), off + h))
    return _pcall(
        body, name=name, grid=(N_HEADS, nb),
        in_specs=[blk(0), blk(zf0), blk(24), pl.BlockSpec((1, HEAD), lambda h, i: (0, h))],
        out_specs=[blk(0), pl.BlockSpec((None, ncb, HEAD, HEAD), lambda h, i: (h, tmap(i), 0, 0))],
        out_shape=[_sds((t, D_MODEL), F32), _sds((N_HEADS, t // HG_CHUNK, HEAD, HEAD), F32)],
        scratch_shapes=[pltpu.VMEM((HEAD, HEAD), F32)] + [pltpu.VMEM((HG_BLOCK, HEAD), F32)] * 4,
        compiler_params=_params(2))(z, z, z, lb_row)


def _hgrn_bwd(name, z, d_o, states, lb_row, rev, dz):
    t = z.shape[0]
    nb = t // HG_BLOCK
    ncb = HG_BLOCK // HG_CHUNK
    nsb = HG_CHUNK // HG_SUB
    zf0 = 16 if rev else 8

    def tmap(i):
        return i if rev else nb - 1 - i

    def body(zq_ref, zf_ref, zi_ref, do_ref, st_ref, lb_ref, dz_in, dq_ref, dv_ref, dzf_ref, dlb_ref,
             dstate, q_s, k_s, v_s, b_s, dq_s, dk_s, dv_s, db_s):
        del dz_in
        first = pl.program_id(1) == 0

        @pl.when(first)
        def _():
            dstate[...] = jnp.zeros_like(dstate)

        lb = lb_ref[...]
        zf = zf_ref[...]
        q, k, logf, f = _hg_gates(zq_ref[...], zf, lb)
        q_s[...] = q
        k_s[...] = k
        v_s[...] = zi_ref[...]
        b_s[...] = _cumsum_rows(logf, HG_CHUNK, rev)
        mask = _tri_mask(rev)
        order = range(ncb) if rev else range(ncb - 1, -1, -1)
        for c in order:
            base = c * HG_CHUNK
            rows = slice(base, base + HG_CHUNK)
            last = base if rev else base + HG_CHUNK - 1
            st0 = st_ref[c]
            dst1 = dstate[...]
            b = b_s[rows, :]
            bl = b_s[last:last + 1, :]
            e = jnp.exp(b)
            el = jnp.exp(bl)
            ebl = jnp.exp(bl - b)
            q_c, k_c, v_c, do_c = q_s[rows, :], k_s[rows, :], v_s[rows, :], do_ref[rows, :]
            kd = k_c * ebl
            dq_s[rows, :] = _nn(do_c, st0) * e
            dk_inter = _nn(v_c, dst1) * ebl
            dk_s[rows, :] = dk_inter
            dv_s[rows, :] = _nt(kd, dst1)
            extra = el * _rowsum(dst1 * st0) + _rowsum(k_c * dk_inter)
            dstate[...] = _tn(do_c, q_c * e) + dst1 * el
            for i_sub in range(nsb):
                r0, r1, e0, e1, anchor = _sub_ranges(base, i_sub, rev)
                q_i, k_i, b_i, v_i, do_i = q_s[r0:r1, :], k_s[r0:r1, :], b_s[r0:r1, :], v_s[r0:r1, :], do_ref[r0:r1, :]
                decay = jnp.exp(jnp.minimum(b_i[:, None, :] - b_i[None, :, :], 0.0))
                a_d = jnp.where(mask, jnp.sum(q_i[:, None, :] * k_i[None, :, :] * decay, axis=-1), 0.0)
                da_d = jnp.where(mask, _nt(do_i, v_i), 0.0)
                wgt = da_d[:, :, None] * decay
                dq_s[r0:r1, :] += jnp.sum(wgt * k_i[None, :, :], axis=1)
                dk_s[r0:r1, :] += jnp.sum(wgt * q_i[:, None, :], axis=0)
                dv_s[r0:r1, :] += _tn(a_d, do_i)
                if e1 > e0:
                    anc = b_s[anchor:anchor + 1, :]
                    e_q = jnp.exp(b_i - anc)
                    e_k = jnp.exp(anc - b_s[e0:e1, :])
                    q_t = q_i * e_q
                    k_t = k_s[e0:e1, :] * e_k
                    a_o = _nt(q_t, k_t)
                    da_o = _nt(do_i, v_s[e0:e1, :])
                    dq_s[r0:r1, :] += e_q * _nn(da_o, k_t)
                    dk_s[e0:e1, :] += e_k * _tn(da_o, q_t)
                    dv_s[e0:e1, :] += _tn(a_o, do_i)
            db_s[rows, :] = q_c * dq_s[rows, :] - k_c * dk_s[rows, :]
            db_s[last:last + 1, :] += extra
        dlogf = _cumsum_rows(db_s[...], HG_CHUNK, not rev)
        s_neg = _sigmoid(-zf)
        df = jnp.where(f > jnp.finfo(F32).tiny, dlogf / f, 0.0)
        dfk = df - dk_s[...]
        dzf_ref[...] = ((1.0 - lb) * _sigmoid(zf) * s_neg * dfk).astype(dzf_ref.dtype)
        dlb = _rowsum(s_neg * dfk)

        @pl.when(first)
        def _():
            dlb_ref[...] = dlb

        @pl.when(jnp.logical_not(first))
        def _():
            dlb_ref[...] += dlb

        dq_ref[...] = dq_s[...]
        dv_ref[...] = dv_s[...]

    blk = lambda off: pl.BlockSpec((HG_BLOCK, HEAD), lambda h, i: (tmap(i), off + h))
    vec = pl.BlockSpec((1, HEAD), lambda h, i: (0, h))
    return _pcall(
        body, name=name, grid=(N_HEADS, nb),
        in_specs=[blk(0), blk(zf0), blk(24), blk(0),
                  pl.BlockSpec((None, ncb, HEAD, HEAD), lambda h, i: (h, tmap(i), 0, 0)), vec, ANY],
        out_specs=[blk(0), blk(0), blk(zf0), vec],
        out_shape=[_sds((t, D_MODEL), F32), _sds((t, D_MODEL), F32), _sds(dz.shape, dz.dtype), _sds((1, D_MODEL), F32)],
        input_output_aliases={6: 2},
        scratch_shapes=[pltpu.VMEM((HEAD, HEAD), F32)] + [pltpu.VMEM((HG_BLOCK, HEAD), F32)] * 8,
        compiler_params=_params(2))(z, z, z, d_o, states, lb_row, dz)


def _lower_bounds(name, gamma):
    def body(g_ref, o_ref):
        g0, g1 = g_ref[0:1, :], g_ref[1:2, :]
        m = jnp.maximum(g0, g1)
        e0, e1 = jnp.exp(g0 - m), jnp.exp(g1 - m)
        s0, s1 = e0 / (e0 + e1), e1 / (e0 + e1)
        o_ref[0:1, :] = s0 - s0
        o_ref[1:2, :] = (s0 + s1) - s0

    return _pcall(body, name=name, out_shape=_sds(gamma.shape, F32))(gamma)


def _lower_bounds_bwd(name, gamma, dlb):
    def body(g_ref, d_ref, o_ref):
        g0, g1 = g_ref[0:1, :], g_ref[1:2, :]
        m = jnp.maximum(g0, g1)
        e0, e1 = jnp.exp(g0 - m), jnp.exp(g1 - m)
        s0, s1 = e0 / (e0 + e1), e1 / (e0 + e1)
        d0, d1 = d_ref[0:1, :], d_ref[1:2, :]
        ds0 = (d0 + d1) - (d0 + d1)
        ds1 = d1
        inner = s0 * ds0 + s1 * ds1
        o_ref[0:1, :] = s0 * (ds0 - inner)
        o_ref[1:2, :] = s1 * (ds1 - inner)

    return _pcall(body, name=name, out_shape=_sds(gamma.shape, F32))(gamma, dlb)


def _heads(x):
    return [x[:, h * HEAD:(h + 1) * HEAD] for h in range(N_HEADS)]


def _hg_post_fwd(name, o_f, o_b, z, gain):
    t = z.shape[0]
    tile = 256

    def fn(of, ob, zg, g):
        outs = []
        for o_h, zg_h, g_h in zip(_heads(of + ob), _heads(zg), _heads(g)):
            outs.append(_rms(o_h, g_h) * (zg_h * _sigmoid(zg_h)))
        return (jnp.concatenate(outs, axis=1),)

    return _rows(name, fn, t // tile, [o_f, o_b, z, gain],
                 [_tile_spec(tile, D_MODEL), _tile_spec(tile, D_MODEL), _tile_spec(tile, D_MODEL, 4), _whole((1, D_MODEL))],
                 [_sds((t, D_MODEL), MXU_DTYPE)], [_tile_spec(tile, D_MODEL)])[0]


def _hg_post_bwd(name, da, o_f, o_b, z, gain, dz):
    t = z.shape[0]
    tile = 256

    def fn(dav, of, ob, zg, g):
        d_o, dzg, dgain = [], [], []
        for da_h, o_h, zg_h, g_h in zip(_heads(dav), _heads(of + ob), _heads(zg), _heads(g)):
            s = _sigmoid(zg_h)
            r = lax.rsqrt(_mean_last(o_h * o_h) + EPS)
            oh = o_h * r
            dy = da_h * (zg_h * s)
            dzg.append(da_h * (oh * g_h) * _dsilu(zg_h, s))
            gdy = dy * g_h
            d_o.append(r * (gdy - oh * _mean_last(gdy * oh)))
            dgain.append(_rowsum(dy * oh))
        return jnp.concatenate(d_o, axis=1), jnp.concatenate(dzg, axis=1), jnp.concatenate(dgain, axis=1)

    return _rows(name, fn, t // tile, [da, o_f, o_b, z, gain],
                 [_tile_spec(tile, D_MODEL)] * 3 + [_tile_spec(tile, D_MODEL, 4), _whole((1, D_MODEL))],
                 [_sds((t, D_MODEL), F32), _sds(dz.shape, dz.dtype), _sds((1, D_MODEL), F32)],
                 [_tile_spec(tile, D_MODEL), _tile_spec(tile, D_MODEL, 4), _whole((1, D_MODEL))],
                 n_red=1, into=[(dz, 1)])


def _hg_gate_bwd(name, dq_f, dq_b, dv_f, dv_b, z, dz):
    t = z.shape[0]
    tile = 512

    def fq(a, b, zq):
        return ((a + b) * _dsilu(zq, _sigmoid(zq)),)

    dz = _rows(name + "_q", fq, t // tile, [dq_f, dq_b, z], [_tile_spec(tile, D_MODEL)] * 3,
               [_sds(dz.shape, dz.dtype)], [_tile_spec(tile, D_MODEL, 0)], into=[(dz, 0)])[0]
    dz = _rows(name + "_v", lambda a, b: (a + b,), t // tile, [dv_f, dv_b], [_tile_spec(tile, D_MODEL)] * 2,
               [_sds(dz.shape, dz.dtype)], [_tile_spec(tile, D_MODEL, 3)], into=[(dz, 0)])[0]
    return dz


def _sg_norm(zv, ln_g, ln_b):
    gv = _gelu(zv)
    xc = gv - _mean_last(gv)
    rstd = lax.rsqrt(_mean_last(xc * xc) + EPS)
    xhat = xc * rstd
    return xhat * ln_g + ln_b, xhat, rstd


def _lane_lo():
    return lax.broadcasted_iota(jnp.int32, (SG_CHUNK, LANES), 1) < (LANES // 2)


SG_TILE = 512


def _sgu_fwd(name, z, w, bias_t, ln_g, ln_b):
    t = z.shape[0]

    def fn(zu, zv, wv, bt, lg, lb):
        u = _gelu(zu)
        vn, _, _ = _sg_norm(zv, lg, lb)
        lo = _lane_lo()
        out_rows = []
        for c in range(SG_TILE // SG_CHUNK):
            rs = slice(c * SG_CHUNK, (c + 1) * SG_CHUNK)
            cols = []
            for j in range(SG_WIDTH // LANES):
                cs = slice(j * LANES, (j + 1) * LANES)
                vb = vn[rs, cs]
                sg = jnp.where(lo, _nn(wv[2 * j], vb), _nn(wv[2 * j + 1], vb)) + bt[:, cs]
                cols.append(u[rs, cs] * sg)
            out_rows.append(jnp.concatenate(cols, axis=1))
        return (jnp.concatenate(out_rows, axis=0),)

    return _rows(name, fn, t // SG_TILE, [z, z, w, bias_t, ln_g, ln_b],
                 [_tile_spec(SG_TILE, SG_WIDTH, 10), _tile_spec(SG_TILE, SG_WIDTH, 11), _whole(w.shape),
                  _whole(bias_t.shape), _whole((1, SG_WIDTH)), _whole((1, SG_WIDTH))],
                 [_sds((t, SG_WIDTH), MXU_DTYPE)], [_tile_spec(SG_TILE, SG_WIDTH)])[0]


def _sgu_bwd(name, dbo, z, w, bias_t, ln_g, ln_b, dz):
    t = z.shape[0]
    n_grp = w.shape[0]

    def fn(dbov, zu, zv, wv, bt, lg, lb):
        u = _gelu(zu)
        vn, xhat, rstd = _sg_norm(zv, lg, lb)
        lo = _lane_lo()
        dw = [None] * n_grp
        dsg_sum = None
        du_rows, dvn_rows = [], []
        for c in range(SG_TILE // SG_CHUNK):
            rs = slice(c * SG_CHUNK, (c + 1) * SG_CHUNK)
            du_cols, dvn_cols, dsg_cols = [], [], []
            for j in range(SG_WIDTH // LANES):
                cs = slice(j * LANES, (j + 1) * LANES)
                vb = vn[rs, cs]
                sg = jnp.where(lo, _nn(wv[2 * j], vb), _nn(wv[2 * j + 1], vb)) + bt[:, cs]
                du_cols.append(dbov[rs, cs] * sg)
                dsg = dbov[rs, cs] * u[rs, cs]
                dsg_cols.append(dsg)
                d0 = _nt(jnp.where(lo, dsg, 0.0), vb)
                d1 = _nt(jnp.where(lo, 0.0, dsg), vb)
                dw[2 * j] = d0 if dw[2 * j] is None else dw[2 * j] + d0
                dw[2 * j + 1] = d1 if dw[2 * j + 1] is None else dw[2 * j + 1] + d1
                dvn_cols.append(jnp.where(lo, _tn(wv[2 * j], dsg), _tn(wv[2 * j + 1], dsg)))
            du_rows.append(jnp.concatenate(du_cols, axis=1))
            dvn_rows.append(jnp.concatenate(dvn_cols, axis=1))
            dsg_c = jnp.concatenate(dsg_cols, axis=1)
            dsg_sum = dsg_c if dsg_sum is None else dsg_sum + dsg_c
        du = jnp.concatenate(du_rows, axis=0)
        dvn = jnp.concatenate(dvn_rows, axis=0)
        dxhat = dvn * lg
        dgv = rstd * (dxhat - _mean_last(dxhat) - xhat * _mean_last(dxhat * xhat))
        dzuv = jnp.concatenate([du * _dgelu(zu), dgv * _dgelu(zv)], axis=1)
        return dzuv, jnp.stack(dw, axis=0), dsg_sum, _rowsum(dvn * xhat), _rowsum(dvn)

    return _rows(name, fn, t // SG_TILE, [dbo, z, z, w, bias_t, ln_g, ln_b],
                 [_tile_spec(SG_TILE, SG_WIDTH), _tile_spec(SG_TILE, SG_WIDTH, 10), _tile_spec(SG_TILE, SG_WIDTH, 11),
                  _whole(w.shape), _whole(bias_t.shape), _whole((1, SG_WIDTH)), _whole((1, SG_WIDTH))],
                 [_sds(dz.shape, dz.dtype), _sds(w.shape, F32), _sds((SG_CHUNK, SG_WIDTH), F32),
                  _sds((1, SG_WIDTH), F32), _sds((1, SG_WIDTH), F32)],
                 [_tile_spec(SG_TILE, 2 * SG_WIDTH, 5), _whole(w.shape), _whole((SG_CHUNK, SG_WIDTH)),
                  _whole((1, SG_WIDTH)), _whole((1, SG_WIDTH))],
                 n_red=4, into=[(dz, 0)])


def _merge_fwd(name, pa, pb, z):
    t = z.shape[0]
    tile = 256
    return _rows(name, lambda a, b, ga, gb: (_sigmoid(ga) * a + _sigmoid(gb) * b,), t // tile, [pa, pb, z, z],
                 [_tile_spec(tile, D_MODEL)] * 2 + [_tile_spec(tile, D_MODEL, 6), _tile_spec(tile, D_MODEL, 7)],
                 [_sds((t, D_MODEL), MXU_DTYPE)], [_tile_spec(tile, D_MODEL)])[0]


def _merge_bwd(name, dm, pa, pb, z):
    t = z.shape[0]
    tile = 256

    def fn(d, a, b, ga, gb):
        sa, sb = _sigmoid(ga), _sigmoid(gb)
        dgate = jnp.concatenate([d * a * sa * (1.0 - sa), d * b * sb * (1.0 - sb)], axis=1)
        return d * sa, d * sb, dgate

    return _rows(name, fn, t // tile, [dm, pa, pb, z, z],
                 [_tile_spec(tile, D_MODEL)] * 3 + [_tile_spec(tile, D_MODEL, 6), _tile_spec(tile, D_MODEL, 7)],
                 [_sds((t, D_MODEL), MXU_DTYPE), _sds((t, D_MODEL), MXU_DTYPE), _sds((t, N_IN), MXU_DTYPE)],
                 [_tile_spec(tile, D_MODEL), _tile_spec(tile, D_MODEL), _tile_spec(tile, 2 * D_MODEL, 3)])


def _swiglu_fwd(name, gt, up):
    n, t, f = gt.shape
    tile = 512
    g2, u2 = gt.reshape(n * t, f), up.reshape(n * t, f)
    act = _rows(name, lambda g, u: ((g * _sigmoid(g)) * u,), n * t // tile, [g2, u2], [_tile_spec(tile, f)] * 2,
                [_sds((n * t, f), MXU_DTYPE)], [_tile_spec(tile, f)])[0]
    return act.reshape(n, t, f)


def _swiglu_bwd(name, dact, gt, up):
    n, t, f = gt.shape
    tile = 512

    def fn(d, g, u):
        s = _sigmoid(g)
        return d * u * _dsilu(g, s), d * (g * s)

    dgt, dup = _rows(name, fn, n * t // tile, [dact.reshape(n * t, f), gt.reshape(n * t, f), up.reshape(n * t, f)],
                     [_tile_spec(tile, f)] * 3, [_sds((n * t, f), MXU_DTYPE)] * 2, [_tile_spec(tile, f)] * 2)
    return dgt.reshape(n, t, f), dup.reshape(n, t, f)


def _ple_fwd(name, x2, pe, gz):
    t, d = x2.shape
    tile = 512
    return _rows(name, lambda x, p, g: (x + p * _sigmoid(g),), t // tile, [x2, pe, gz], [_tile_spec(tile, d)] * 3,
                 [_sds((t, d), F32)], [_tile_spec(tile, d)])[0]


def _ple_bwd(name, dx, pe, gz):
    t, d = dx.shape
    tile = 512

    def fn(dv, p, g):
        s = _sigmoid(g)
        return dv * s, dv * p * s * (1.0 - s)

    return _rows(name, fn, t // tile, [dx, pe, gz], [_tile_spec(tile, d)] * 3, [_sds((t, d), MXU_DTYPE)] * 2,
                 [_tile_spec(tile, d)] * 2)


def _loss_bwd(name, y, target):
    t, d = y.shape
    tile = 512

    def fn(yv, tv):
        err = yv - tv
        return err * (1.0 / d), _rowsum(err * err)

    return _rows(name, fn, t // tile, [y, target], [_tile_spec(tile, d)] * 2, [_sds((t, d), F32), _sds((1, d), F32)],
                 [_tile_spec(tile, d), _whole((1, d))], n_red=1)


def _position():
    return lax.axis_index("x"), lax.axis_index("y"), lax.axis_index("c")


def _all_gather_weights(shards):
    n = len(shards)

    def body(*refs):
        w_refs, out_refs = refs[:n], refs[n:2 * n]
        send_sems, recv_sems, local_sems = refs[2 * n:]
        x, y, c = _position()
        me = 2 * x + y
        chips = [(1 - x, y), (x, 1 - y), (1 - x, 1 - y)]

        def copy(i, j, src_chip, to):
            return pltpu.make_async_remote_copy(
                src_ref=w_refs[i], dst_ref=out_refs[i].at[src_chip], send_sem=send_sems.at[3 * i + j],
                recv_sem=recv_sems.at[3 * i + j], device_id=to, device_id_type=MESH)

        local = [pltpu.make_async_copy(w_refs[i], out_refs[i].at[me], local_sems.at[i]) for i in range(n)]
        for cp in local:
            cp.start()
        sends = [copy(i, j, me, (px, py, c)) for i in range(n) for j, (px, py) in enumerate(chips)]
        for cp in sends:
            cp.start()
        for i in range(n):
            for j, (px, py) in enumerate(chips):
                copy(i, j, 2 * px + py, (px, py, c)).wait_recv()
        for cp in sends:
            cp.wait_send()
        for cp in local:
            cp.wait()

    return _pcall(
        body, name="all_gather_weights",
        in_specs=[ANY] * n, out_specs=[ANY] * n,
        out_shape=[_sds((N_CHIPS,) + s.shape, s.dtype) for s in shards],
        scratch_shapes=[pltpu.SemaphoreType.DMA((3 * n,)), pltpu.SemaphoreType.DMA((3 * n,)),
                        pltpu.SemaphoreType.DMA((n,))],
    )(*shards)


def _exchange_grads(grads):
    n = len(grads)

    def body(*refs):
        g_refs, out_refs = refs[:n], refs[n:2 * n]
        send_sems, recv_sems, local_sems = refs[2 * n:]
        x, y, c = _position()
        me = 2 * x + y
        sibling = (x, y, 1 - c)
        chips = [(1 - x, y), (x, 1 - y), (1 - x, 1 - y)]

        def copy(i, k, src, slot, to):
            return pltpu.make_async_remote_copy(
                src_ref=src, dst_ref=out_refs[i].at[slot], send_sem=send_sems.at[7 * i + k],
                recv_sem=recv_sems.at[7 * i + k], device_id=to, device_id_type=MESH)

        local = [pltpu.make_async_copy(g_refs[i].at[me], out_refs[i].at[2 * me + c], local_sems.at[i]) for i in range(n)]
        for cp in local:
            cp.start()
        first = []
        for i in range(n):
            first.append(copy(i, 0, g_refs[i].at[me], 2 * me + c, sibling))
            for j, (px, py) in enumerate(chips):
                first.append(copy(i, 1 + j, g_refs[i].at[2 * px + py], 2 * me + c, (px, py, c)))
        for cp in first:
            cp.start()
        passed = []
        for i in range(n):
            for j, (px, py) in enumerate(chips):
                slot = 2 * (2 * px + py) + c
                copy(i, 1 + j, g_refs[i].at[me], slot, (px, py, c)).wait_recv()
                fwd = copy(i, 4 + j, out_refs[i].at[slot], slot, sibling)
                fwd.start()
                passed.append(fwd)
        for i in range(n):
            copy(i, 0, g_refs[i].at[me], 2 * me + (1 - c), sibling).wait_recv()
            for j, (px, py) in enumerate(chips):
                slot = 2 * (2 * px + py) + (1 - c)
                copy(i, 4 + j, g_refs[i].at[me], slot, sibling).wait_recv()
        for cp in first + passed:
            cp.wait_send()
        for cp in local:
            cp.wait()

    return _pcall(
        body, name="exchange_grads",
        in_specs=[ANY] * n, out_specs=[ANY] * n,
        out_shape=[_sds((N_DEV,) + g.shape[1:], g.dtype) for g in grads],
        scratch_shapes=[pltpu.SemaphoreType.DMA((7 * n,)), pltpu.SemaphoreType.DMA((7 * n,)),
                        pltpu.SemaphoreType.DMA((n,))],
    )(*grads)


def _all_reduce_small(packed):
    rows = packed.shape[0]

    def body(x_ref, sum_ref, slots, send_sems, recv_sems, local_sem):
        x, y, c = _position()
        me = 4 * x + 2 * y + c
        mine = pltpu.make_async_copy(x_ref, slots.at[me], local_sem)
        mine.start()
        sends = []
        for k in range(1, N_DEV):
            peer = (x ^ (k >> 2), y ^ ((k >> 1) & 1), c ^ (k & 1))
            cp = pltpu.make_async_remote_copy(src_ref=x_ref, dst_ref=slots.at[me], send_sem=send_sems.at[k - 1],
                                              recv_sem=recv_sems.at[k - 1], device_id=peer, device_id_type=MESH)
            cp.start()
            sends.append(cp)
        for k in range(1, N_DEV):
            px, py, pc = x ^ (k >> 2), y ^ ((k >> 1) & 1), c ^ (k & 1)
            pltpu.make_async_remote_copy(src_ref=x_ref, dst_ref=slots.at[4 * px + 2 * py + pc], send_sem=send_sems.at[k - 1],
                                         recv_sem=recv_sems.at[k - 1], device_id=(px, py, pc), device_id_type=MESH).wait_recv()
        for cp in sends:
            cp.wait_send()
        mine.wait()
        total = slots[0]
        for d in range(1, N_DEV):
            total = total + slots[d]
        sum_ref[...] = total

    vmem = pl.BlockSpec(memory_space=pltpu.VMEM)
    return _pcall(
        body, name="all_reduce_small", in_specs=[vmem], out_specs=vmem, out_shape=_sds(packed.shape, F32),
        scratch_shapes=[pltpu.VMEM((N_DEV, rows, LANES), F32), pltpu.SemaphoreType.DMA((N_DEV - 1,)),
                        pltpu.SemaphoreType.DMA((N_DEV - 1,)), pltpu.SemaphoreType.DMA],
        compiler_params=pltpu.CompilerParams(vmem_limit_bytes=VMEM_LIMIT_BYTES),
    )(packed)


def _adamw(w, g, m, v):
    m = ADAM_B1 * m + (1.0 - ADAM_B1) * g
    v = ADAM_B2 * v + (1.0 - ADAM_B2) * (g * g)
    m_hat = m / (1.0 - ADAM_B1 ** ADAM_STEP)
    v_hat = v / (1.0 - ADAM_B2 ** ADAM_STEP)
    delta = -ADAM_LR * (m_hat / (jnp.sqrt(v_hat) + ADAM_EPS) + ADAM_WD * w)
    return delta, m, v


def _adam_sharded(name, parts, w, m, v):
    shape = w.shape
    cols = shape[-1]
    rows = w.size // cols
    tile = 8
    while tile * 2 * cols <= ADAM_TILE_ELEMS and rows % (tile * 2) == 0:
        tile *= 2

    def fn(p, wv, mv, vv):
        g = p[0]
        for d in range(1, N_DEV):
            g = g + p[d]
        return (g,) + _adamw(wv, g, mv, vv)

    two_d = lambda a: a.reshape(rows, cols)
    outs = _rows(name, fn, rows // tile, [parts.reshape(N_DEV, rows, cols), two_d(w), two_d(m), two_d(v)],
                 [pl.BlockSpec((N_DEV, tile, cols), lambda i: (0, i, 0))] + [_tile_spec(tile, cols)] * 3,
                 [_sds((rows, cols), F32)] * 4, [_tile_spec(tile, cols)] * 4)
    return [o.reshape(shape) for o in outs]


def _adam_small(name, g, w, m, v):
    rows = g.shape[0]
    tile = rows // 2
    return _rows(name, lambda gv, wv, mv, vv: _adamw(wv, gv, mv, vv), rows // tile, [g, w, m, v],
                 [_tile_spec(tile, LANES)] * 4, [_sds(g.shape, F32)] * 3, [_tile_spec(tile, LANES)] * 3)


BIG = ("w_in", "w_a", "w_b", "w_out", "w_gate", "w_up", "w_down", "w_ple", "w_ple_gate")
SMALL = ("norm_mix_pre", "lb_gamma_fwd", "lb_gamma_bwd", "hg_norm", "sg_w", "sg_b", "sg_ln_g", "sg_ln_b",
         "norm_mix_post", "norm_ffn_pre", "norm_ffn_post")


def _layer_fwd(l, x, p_l, wg, sm):
    sv = {"x": x}
    h = _rms_fwd(f"norm_mix_pre_l{l}", x, sm["norm_mix_pre"])
    z = _proj_in(h, wg["w_in"], l)
    o_f, st_f = _hgrn_fwd(f"hgrn_fwd_l{l}", z, sm["lb_fwd"], False)
    o_b, st_b = _hgrn_fwd(f"hgrn_rev_l{l}", z, sm["lb_bwd"], True)
    a_out = _hg_post_fwd(f"hg_post_l{l}", o_f, o_b, z, sm["hg_norm"])
    b_out = _sgu_fwd(f"sgu_l{l}", z, sm["sg_w"], sm["sg_bias_t"], sm["sg_ln_g"], sm["sg_ln_b"])
    pa = _proj_rows_sharded(f"proj_a_l{l}", a_out, wg["w_a"], l, F32)
    pb = _proj_cols256(f"proj_b_l{l}", b_out, wg["w_b"], l)
    merged = _merge_fwd(f"merge_l{l}", pa, pb, z)
    mix = _proj_rows_sharded(f"proj_out_l{l}", merged, wg["w_out"], l, F32)
    x1 = _resid_rms_fwd(f"norm_mix_post_l{l}", x, mix, sm["norm_mix_post"])
    h2 = _rms_fwd(f"norm_ffn_pre_l{l}", x1, sm["norm_ffn_pre"])
    gt = _proj_ffn_in(f"proj_gate_l{l}", h2, wg["w_gate"], l)
    up = _proj_ffn_in(f"proj_up_l{l}", h2, wg["w_up"], l)
    act = _swiglu_fwd(f"swiglu_l{l}", gt, up)
    ff = _proj_ffn_out(f"proj_down_l{l}", act, wg["w_down"], l)
    x2 = _resid_rms_fwd(f"norm_ffn_post_l{l}", x1, ff, sm["norm_ffn_post"])
    pe = _proj_cols256(f"proj_ple_l{l}", p_l, wg["w_ple"], l)
    gz = _proj_rows_sharded(f"proj_ple_gate_l{l}", x2, wg["w_ple_gate"], l, F32)
    x3 = _ple_fwd(f"ple_l{l}", x2, pe, gz)
    sv.update(h=h, z=z, o_f=o_f, o_b=o_b, st_f=st_f, st_b=st_b, a_out=a_out, b_out=b_out, pa=pa, pb=pb,
              merged=merged, mix=mix, x1=x1, h2=h2, gt=gt, up=up, act=act, ff=ff, x2=x2, pe=pe, gz=gz, p=p_l)
    return x3, sv


def _layer_bwd(l, dx3, sv, wg, sm, gw):
    t = dx3.shape[0]
    nt = t // TT
    sg = {}

    def wgrad(key, *a, **k):
        gw[key] = _wgrad(f"grad_{key}_l{l}", *a, l=l, into=gw.get(key), **k)

    row = lambda width: _bs((TT, width), lambda j, i: (i, 0))
    row_j = lambda width: _bs((TT, width), lambda j, i: (i, j))
    ffn_j = _bs((None, TT, FFN_SHARD), lambda j, i: (j, i, 0))
    blk_j = lambda shape: (tuple(shape), lambda j, i: (j, l, 0, 0))

    dpe, dgz = _ple_bwd(f"ple_bwd_l{l}", dx3, sv["pe"], sv["gz"])
    wgrad("w_ple", sv["p"], row(PLE_DIM), dpe, row_j(256), (PLE_DIM, 256), blk_j((PLE_DIM, 256)), N_CHIPS)
    wgrad("w_ple_gate", sv["x2"], row_j(256), dgz, row(D_MODEL), (256, D_MODEL), blk_j((256, D_MODEL)), N_CHIPS)
    dx2 = _proj_rows_sharded_t(f"proj_ple_gate_t_l{l}", dgz, wg["w_ple_gate"], l, F32, add=dx3)

    dff, sg["norm_ffn_post"] = _rms_bwd(f"norm_ffn_post_bwd_l{l}", sv["ff"], sm["norm_ffn_post"], dx2, None, MXU_DTYPE)
    dact = _proj_ffn_out_t(f"proj_down_t_l{l}", dff, wg["w_down"], l)
    wgrad("w_down", sv["act"], ffn_j, dff, row(D_MODEL), (FFN_SHARD, D_MODEL), blk_j((FFN_SHARD, D_MODEL)), N_CHIPS)
    dgt, dup = _swiglu_bwd(f"swiglu_bwd_l{l}", dact, sv["gt"], sv["up"])
    dh2 = _proj_ffn_in_t(f"proj_gate_up_t_l{l}", [(dgt, wg["w_gate"]), (dup, wg["w_up"])], l)
    wgrad("w_gate", sv["h2"], row(D_MODEL), dgt, ffn_j, (D_MODEL, FFN_SHARD), blk_j((D_MODEL, FFN_SHARD)), N_CHIPS)
    wgrad("w_up", sv["h2"], row(D_MODEL), dup, ffn_j, (D_MODEL, FFN_SHARD), blk_j((D_MODEL, FFN_SHARD)), N_CHIPS)
    dx1, sg["norm_ffn_pre"] = _rms_bwd(f"norm_ffn_pre_bwd_l{l}", sv["x1"], sm["norm_ffn_pre"], dh2, dx2, F32)

    dmix, sg["norm_mix_post"] = _rms_bwd(f"norm_mix_post_bwd_l{l}", sv["mix"], sm["norm_mix_post"], dx1, None, MXU_DTYPE)
    dmerged = _proj_rows_sharded_t(f"proj_out_t_l{l}", dmix, wg["w_out"], l, F32)
    wgrad("w_out", sv["merged"], row_j(256), dmix, row(D_MODEL), (256, D_MODEL), blk_j((256, D_MODEL)), N_CHIPS)
    dpa, dpb, dz = _merge_bwd(f"merge_bwd_l{l}", dmerged, sv["pa"], sv["pb"], sv["z"])
    da = _proj_rows_sharded_t(f"proj_a_t_l{l}", dpa, wg["w_a"], l, F32)
    wgrad("w_a", sv["a_out"], row_j(256), dpa, row(D_MODEL), (256, D_MODEL), blk_j((256, D_MODEL)), N_CHIPS)
    dbo = _proj_cols256_t(f"proj_b_t_l{l}", dpb, wg["w_b"], l)
    wgrad("w_b", sv["b_out"], row(SG_WIDTH), dpb, row_j(256), (SG_WIDTH, 256), blk_j((SG_WIDTH, 256)), N_CHIPS)

    dz, sg["sg_w"], dsg_sum, sg["sg_ln_g"], sg["sg_ln_b"] = _sgu_bwd(
        f"sgu_bwd_l{l}", dbo, sv["z"], sm["sg_w"], sm["sg_bias_t"], sm["sg_ln_g"], sm["sg_ln_b"], dz)
    sg["sg_b"] = dsg_sum.reshape(SG_CHUNK, N_HEADS, SG_WIDTH // N_HEADS).sum(axis=-1).T
    d_o, dz, sg["hg_norm"] = _hg_post_bwd(f"hg_post_bwd_l{l}", da, sv["o_f"], sv["o_b"], sv["z"], sm["hg_norm"], dz)
    dq_f, dv_f, dz, sg["lb_fwd"] = _hgrn_bwd(f"hgrn_fwd_bwd_l{l}", sv["z"], d_o, sv["st_f"], sm["lb_fwd"], False, dz)
    dq_b, dv_b, dz, sg["lb_bwd"] = _hgrn_bwd(f"hgrn_rev_bwd_l{l}", sv["z"], d_o, sv["st_b"], sm["lb_bwd"], True, dz)
    dz = _hg_gate_bwd(f"hg_gate_bwd_l{l}", dq_f, dq_b, dv_f, dv_b, sv["z"], dz)

    dh = _proj_in_t(dz, wg["w_in"], l)
    gw["w_in"] = _wgrad(f"grad_w_in_l{l}", sv["h"], _bs((TT, D_MODEL), lambda n, i: (i, 0)), dz,
                        _bs((TT, 1024), lambda n, i: (i, n)), (D_MODEL, 2048),
                        ((D_MODEL, 1024), lambda n, i: (n // 2, l, 0, n % 2)), 8, l, gw.get("w_in"))
    dx, sg["norm_mix_pre"] = _rms_bwd(f"norm_mix_pre_bwd_l{l}", sv["x"], sm["norm_mix_pre"], dh, dx1, F32)
    del nt
    return dx, gw, sg


def _pack(parts):
    return jnp.concatenate([a.reshape(-1, LANES) for a in parts], axis=0)


def _step(x, p, loss_target, w, m, v):
    x = x[0]
    target = loss_target[0]
    depth = w["w_in"].shape[0]

    gathered = _all_gather_weights([w[k].astype(MXU_DTYPE) for k in BIG])
    wg = dict(zip(BIG, gathered))
    lb_f = _lower_bounds("lower_bounds_fwd", w["lb_gamma_fwd"])
    lb_b = _lower_bounds("lower_bounds_bwd", w["lb_gamma_bwd"])

    def small_of(l):
        sm = {k: w[k][l:l + 1] for k in ("norm_mix_pre", "hg_norm", "sg_ln_g", "sg_ln_b", "norm_mix_post",
                                        "norm_ffn_pre", "norm_ffn_post")}
        sm["lb_fwd"], sm["lb_bwd"] = lb_f[l:l + 1], lb_b[l:l + 1]
        sm["sg_w"] = w["sg_w"][l]
        sm["sg_bias_t"] = jnp.repeat(w["sg_b"][l].T, SG_WIDTH // N_HEADS, axis=1)
        return sm

    saved = []
    h = x
    for l in range(depth):
        h, sv = _layer_fwd(l, h, p[l, 0], wg, small_of(l))
        saved.append(sv)

    dy, sq_err = _loss_bwd("loss", h, target)
    gw = {}
    small_grads = [None] * depth
    for l in reversed(range(depth)):
        dy, gw, small_grads[l] = _layer_bwd(l, dy, saved[l], wg, small_of(l), gw)

    def stack(key):
        return jnp.concatenate([small_grads[l][key].reshape((1,) + w_shape[1:]) for l in range(depth)], axis=0)

    g_small = {}
    for key in SMALL:
        w_shape = w[key].shape
        if key == "lb_gamma_fwd":
            dlb = jnp.concatenate([small_grads[l]["lb_fwd"] for l in range(depth)], axis=0)
            g_small[key] = _lower_bounds_bwd("lower_bounds_fwd_bwd", w[key], dlb)
        elif key == "lb_gamma_bwd":
            dlb = jnp.concatenate([small_grads[l]["lb_bwd"] for l in range(depth)], axis=0)
            g_small[key] = _lower_bounds_bwd("lower_bounds_bwd_bwd", w[key], dlb)
        else:
            g_small[key] = stack(key)

    packed = _pack([g_small[k] for k in SMALL] + [sq_err])
    summed = _all_reduce_small(packed)
    n_small_rows = sum(w[k].size for k in SMALL) // LANES
    loss = 0.5 * jnp.sum(summed[n_small_rows:]) / D_MODEL

    g_rows = summed[:n_small_rows]
    d_rows, m_rows, v_rows = _adam_small("adamw_small", g_rows, _pack([w[k] for k in SMALL]),
                                         _pack([m[k] for k in SMALL]), _pack([v[k] for k in SMALL]))
    out = {}
    off = 0
    for key in SMALL:
        n_rows = w[key].size // LANES
        sl = slice(off, off + n_rows)
        out[key] = tuple(a[sl].reshape(w[key].shape) for a in (g_rows, d_rows, m_rows, v_rows))
        off += n_rows

    parts = _exchange_grads([gw[k] for k in BIG])
    for key, part in zip(BIG, parts):
        out[key] = tuple(_adam_sharded(f"adamw_{key}", part, w[key], m[key], v[key]))
    return loss, dy[None], out


WEIGHTS = ("norm_mix_pre", "w_in", "lb_gamma_fwd", "lb_gamma_bwd", "hg_norm", "sg_w", "sg_b", "sg_ln_g", "sg_ln_b",
           "w_a", "w_b", "w_out", "norm_mix_post", "norm_ffn_pre", "w_gate", "w_up", "w_down", "norm_ffn_post",
           "w_ple", "w_ple_gate")


def kernel(x, p, norm_mix_pre, w_in, lb_gamma_fwd, lb_gamma_bwd, hg_norm, sg_w, sg_b, sg_ln_g, sg_ln_b, w_a, w_b, w_out, norm_mix_post, norm_ffn_pre, w_gate, w_up, w_down, norm_ffn_post, w_ple, w_ple_gate, loss_target, m_norm_mix_pre, m_w_in, m_lb_gamma_fwd, m_lb_gamma_bwd, m_hg_norm, m_sg_w, m_sg_b, m_sg_ln_g, m_sg_ln_b, m_w_a, m_w_b, m_w_out, m_norm_mix_post, m_norm_ffn_pre, m_w_gate, m_w_up, m_w_down, m_norm_ffn_post, m_w_ple, m_w_ple_gate, v_norm_mix_pre, v_w_in, v_lb_gamma_fwd, v_lb_gamma_bwd, v_hg_norm, v_sg_w, v_sg_b, v_sg_ln_g, v_sg_ln_b, v_w_a, v_w_b, v_w_out, v_norm_mix_post, v_norm_ffn_pre, v_w_gate, v_w_up, v_w_down, v_norm_ffn_post, v_w_ple, v_w_ple_gate):
    w = dict(zip(WEIGHTS, (norm_mix_pre, w_in, lb_gamma_fwd, lb_gamma_bwd, hg_norm, sg_w, sg_b, sg_ln_g, sg_ln_b, w_a, w_b, w_out, norm_mix_post, norm_ffn_pre, w_gate, w_up, w_down, norm_ffn_post, w_ple, w_ple_gate)))
    m = dict(zip(WEIGHTS, (m_norm_mix_pre, m_w_in, m_lb_gamma_fwd, m_lb_gamma_bwd, m_hg_norm, m_sg_w, m_sg_b, m_sg_ln_g, m_sg_ln_b, m_w_a, m_w_b, m_w_out, m_norm_mix_post, m_norm_ffn_pre, m_w_gate, m_w_up, m_w_down, m_norm_ffn_post, m_w_ple, m_w_ple_gate)))
    v = dict(zip(WEIGHTS, (v_norm_mix_pre, v_w_in, v_lb_gamma_fwd, v_lb_gamma_bwd, v_hg_norm, v_sg_w, v_sg_b, v_sg_ln_g, v_sg_ln_b, v_w_a, v_w_b, v_w_out, v_norm_mix_post, v_norm_ffn_pre, v_w_gate, v_w_up, v_w_down, v_norm_ffn_post, v_w_ple, v_w_ple_gate)))
    loss, grad_x, out = _step(x, p, loss_target, w, m, v)
    res = [loss, grad_x]
    for i in range(4):
        res += [out[k][i] for k in WEIGHTS]
    return tuple(res)
```

```python
import functools

import jax
import jax.numpy as jnp
from jax import lax
from jax.experimental import pallas as pl
from jax.experimental.pallas import tpu as pltpu

F32 = jnp.float32
MXU_DTYPE = jnp.bfloat16
GRAD_EXCHANGE_DTYPE = jnp.bfloat16

D_MODEL = 1024
N_HEADS = 8
HEAD = 128
HG_CHUNK = 64
HG_SUB = 16
HG_BLOCK = 256
HG_SAFE_EXP = 80.0
SG_CHUNK = 128
SG_WIDTH = 512
FFN_SHARD = 704
PLE_DIM = 256
N_IN = 8192
N_CHIPS = 4
N_DEV = 8
EPS = 1e-6
LANES = 128
VMEM_LIMIT_BYTES = 56 * 2 ** 20

ADAM_LR = 0.001
ADAM_B1 = 0.9
ADAM_B2 = 0.999
ADAM_EPS = 1e-08
ADAM_WD = 0.01
ADAM_STEP = 10
ADAM_TILE_ELEMS = 128 * 1024

MESH = pl.DeviceIdType.MESH
ANY = pl.BlockSpec(memory_space=pl.ANY)


def _pcall(body, **kw):
    return pl.pallas_call(body, **kw)


def _params(n_axes):
    return pltpu.CompilerParams(dimension_semantics=("arbitrary",) * n_axes, vmem_limit_bytes=VMEM_LIMIT_BYTES)


def _dot(a, b, ca, cb):
    return lax.dot_general(a.astype(MXU_DTYPE), b.astype(MXU_DTYPE), (((ca,), (cb,)), ((), ())),
                           preferred_element_type=F32)


def _nn(a, b):
    return _dot(a, b, 1, 0)


def _nt(a, b):
    return _dot(a, b, 1, 1)


def _tn(a, b):
    return _dot(a, b, 0, 0)


NN, NT, TN = (1, 0), (1, 1), (0, 0)


def _sigmoid(x):
    return jax.nn.sigmoid(x)


def _dsilu(x, s):
    return s * (1.0 + x * (1.0 - s))


_SQRT_HALF = 0.7071067811865476
_INV_SQRT_2PI = 0.3989422804014327


def _gelu(x):
    return 0.5 * x * (1.0 + lax.erf(x * _SQRT_HALF))


def _dgelu(x):
    return 0.5 * (1.0 + lax.erf(x * _SQRT_HALF)) + x * jnp.exp(-0.5 * x * x) * _INV_SQRT_2PI


def _mean_last(x):
    return jnp.mean(x, axis=-1, keepdims=True)


def _rowsum(x):
    return jnp.sum(x, axis=0, keepdims=True)


def _mm(name, pairs, kind, out_shape, grid, in_specs, out_spec, *, reduce_axis=None, add=None,
        add_spec=None, into=None, prep=None):
    n_pairs = len(pairs)
    has_add = add is not None
    has_into = into is not None
    staged = reduce_axis is not None and out_shape.dtype != F32
    n_args = 2 * n_pairs + has_add + has_into

    def body(*refs):
        o_ref = refs[n_args]
        acc = None
        for i in range(n_pairs):
            a = refs[2 * i][...]
            b = refs[2 * i + 1][...]
            if prep is not None:
                b = prep(b)
            prod = _dot(a, b, *kind)
            acc = prod if acc is None else acc + prod
        if has_add:
            acc = acc + refs[2 * n_pairs][...]
        if reduce_axis is None:
            o_ref[...] = acc.astype(o_ref.dtype)
        else:
            r = pl.program_id(reduce_axis)
            acc_ref = refs[n_args + 1] if staged else o_ref

            @pl.when(r == 0)
            def _():
                acc_ref[...] = acc

            @pl.when(r > 0)
            def _():
                acc_ref[...] += acc

            if staged:
                @pl.when(r == grid[reduce_axis] - 1)
                def _():
                    o_ref[...] = acc_ref[...].astype(o_ref.dtype)

    scratch = []
    if staged:
        scratch = [pltpu.VMEM(tuple(d for d in out_spec.block_shape if d is not None), F32)]
    args = [t for pair in pairs for t in pair]
    specs = list(in_specs)
    if has_add:
        args.append(add)
        specs.append(add_spec)
    aliases = {}
    if has_into:
        aliases = {len(args): 0}
        args.append(into)
        specs.append(ANY)
    return _pcall(body, name=name, grid=grid, in_specs=specs, out_specs=out_spec, out_shape=out_shape,
                  scratch_shapes=scratch, input_output_aliases=aliases, compiler_params=_params(len(grid)))(*args)


def _sds(shape, dtype):
    return jax.ShapeDtypeStruct(tuple(shape), dtype)


def _bs(shape, fn):
    return pl.BlockSpec(tuple(shape), fn)


TM = 1024


def _merge_lead(b):
    return b.reshape(b.shape[0] * b.shape[1], b.shape[2])


def _proj_in(h, w_in_g, l):
    t = h.shape[0]
    return _mm(f"proj_in_l{l}", [(h, w_in_g)], NN, _sds((t, N_IN), F32), (8, t // TM),
               [_bs((TM, D_MODEL), lambda n, m: (m, 0)),
                _bs((None, None, D_MODEL, 1024), lambda n, m: (n // 2, l, 0, n % 2))],
               _bs((TM, 1024), lambda n, m: (m, n)))


def _proj_rows_sharded(name, a, w_g, l, out_dtype, add=None):
    t = a.shape[0]
    return _mm(name, [(a, w_g)], NN, _sds((t, D_MODEL), out_dtype), (t // TM,),
               [_bs((TM, D_MODEL), lambda m: (m, 0)),
                _bs((N_CHIPS, None, 256, D_MODEL), lambda m: (0, l, 0, 0))],
               _bs((TM, D_MODEL), lambda m: (m, 0)), prep=_merge_lead, add=add,
               add_spec=_bs((TM, D_MODEL), lambda m: (m, 0)))


def _proj_rows_sharded_t(name, g, w_g, l, out_dtype, add=None):
    t = g.shape[0]
    return _mm(name, [(g, w_g)], NT, _sds((t, D_MODEL), out_dtype), (t // TM,),
               [_bs((TM, D_MODEL), lambda m: (m, 0)),
                _bs((N_CHIPS, None, 256, D_MODEL), lambda m: (0, l, 0, 0))],
               _bs((TM, D_MODEL), lambda m: (m, 0)), prep=_merge_lead, add=add,
               add_spec=_bs((TM, D_MODEL), lambda m: (m, 0)))


def _proj_cols256(name, a, w_g, l):
    t, k = a.shape
    return _mm(name, [(a, w_g)], NN, _sds((t, D_MODEL), F32), (N_CHIPS, t // TM),
               [_bs((TM, k), lambda j, m: (m, 0)),
                _bs((None, None, k, 256), lambda j, m: (j, l, 0, 0))],
               _bs((TM, 256), lambda j, m: (m, j)))


def _proj_cols256_t(name, g, w_g, l):
    t = g.shape[0]
    k = w_g.shape[2]
    return _mm(name, [(g, w_g)], NT, _sds((t, k), F32), (t // TM, N_CHIPS),
               [_bs((TM, 256), lambda m, j: (m, j)),
                _bs((None, None, k, 256), lambda m, j: (j, l, 0, 0))],
               _bs((TM, k), lambda m, j: (m, 0)), reduce_axis=1)


def _proj_ffn_in(name, h2, w_g, l):
    t = h2.shape[0]
    return _mm(name, [(h2, w_g)], NN, _sds((N_CHIPS, t, FFN_SHARD), F32), (N_CHIPS, t // TM),
               [_bs((TM, D_MODEL), lambda j, m: (m, 0)),
                _bs((None, None, D_MODEL, FFN_SHARD), lambda j, m: (j, l, 0, 0))],
               _bs((None, TM, FFN_SHARD), lambda j, m: (j, m, 0)))


def _proj_ffn_in_t(name, pairs, l):
    t = pairs[0][0].shape[1]
    specs = []
    for _ in pairs:
        specs += [_bs((None, TM, FFN_SHARD), lambda m, j: (j, m, 0)),
                  _bs((None, None, D_MODEL, FFN_SHARD), lambda m, j: (j, l, 0, 0))]
    return _mm(name, pairs, NT, _sds((t, D_MODEL), F32), (t // TM, N_CHIPS), specs,
               _bs((TM, D_MODEL), lambda m, j: (m, 0)), reduce_axis=1)


def _proj_ffn_out(name, act, w_g, l):
    t = act.shape[1]
    return _mm(name, [(act, w_g)], NN, _sds((t, D_MODEL), F32), (t // TM, N_CHIPS),
               [_bs((None, TM, FFN_SHARD), lambda m, j: (j, m, 0)),
                _bs((None, None, FFN_SHARD, D_MODEL), lambda m, j: (j, l, 0, 0))],
               _bs((TM, D_MODEL), lambda m, j: (m, 0)), reduce_axis=1)


def _proj_ffn_out_t(name, g, w_g, l):
    t = g.shape[0]
    return _mm(name, [(g, w_g)], NT, _sds((N_CHIPS, t, FFN_SHARD), F32), (N_CHIPS, t // TM),
               [_bs((TM, D_MODEL), lambda j, m: (m, 0)),
                _bs((None, None, FFN_SHARD, D_MODEL), lambda j, m: (j, l, 0, 0))],
               _bs((None, TM, FFN_SHARD), lambda j, m: (j, m, 0)))


def _proj_in_t(dz, w_in_g, l):
    t = dz.shape[0]
    return _mm(f"proj_in_t_l{l}", [(dz, w_in_g)], NT, _sds((t, D_MODEL), F32), (t // TM, 8),
               [_bs((TM, 1024), lambda m, n: (m, n)),
                _bs((None, None, D_MODEL, 1024), lambda m, n: (n // 2, l, 0, n % 2))],
               _bs((TM, D_MODEL), lambda m, n: (m, 0)), reduce_axis=1)


TT = 1024


def _wgrad(name, a, a_spec, g, g_spec, shard_shape, o_map, n_outer, l, into):
    t = a.shape[-2]
    out = _sds((N_CHIPS, 2) + tuple(shard_shape), GRAD_EXCHANGE_DTYPE)
    return _mm(name, [(a, g)], TN, out, (n_outer, t // TT), [a_spec, g_spec],
               _bs((None, None) + tuple(o_map[0]), o_map[1]), reduce_axis=1, into=into)


def _rows(name, fn, n_tiles, ins, in_specs, out_shapes, out_specs, n_red=0, into=()):
    n_in = len(ins)
    n_out = len(out_shapes)

    def body(*refs):
        in_refs = refs[:n_in]
        out_refs = refs[len(refs) - n_out:]
        vals = fn(*[r[...] for r in in_refs])
        if not isinstance(vals, (tuple, list)):
            vals = (vals,)
        first = pl.program_id(0) == 0
        for j in range(n_out):
            o_ref = out_refs[j]
            val = vals[j]
            if j < n_out - n_red:
                o_ref[...] = val.astype(o_ref.dtype)
            else:
                @pl.when(first)
                def _(o_ref=o_ref, val=val):
                    o_ref[...] = val

                @pl.when(jnp.logical_not(first))
                def _(o_ref=o_ref, val=val):
                    o_ref[...] += val

    args = list(ins)
    specs = list(in_specs)
    aliases = {}
    for buf, out_idx in into:
        aliases[len(args)] = out_idx
        args.append(buf)
        specs.append(ANY)
    res = _pcall(body, name=name, grid=(n_tiles,), in_specs=specs, out_specs=list(out_specs),
                 out_shape=list(out_shapes), input_output_aliases=aliases, compiler_params=_params(1))(*args)
    return res


def _tile_spec(tile, width, blk=0):
    return pl.BlockSpec((tile, width), lambda i: (i, blk))


def _whole(shape):
    nd = len(shape)
    return pl.BlockSpec(tuple(shape), lambda i: (0,) * nd)


def _rms(x, g):
    r = lax.rsqrt(_mean_last(x * x) + EPS)
    return (x * r) * g


def _rms_bwd_math(u, g, dy):
    r = lax.rsqrt(_mean_last(u * u) + EPS)
    uh = u * r
    gdy = dy * g
    du = r * (gdy - uh * _mean_last(gdy * uh))
    return du, _rowsum(dy * uh)


def _rms_fwd(name, x, g):
    t, d = x.shape
    tile = 512
    return _rows(name, lambda xv, gv: (_rms(xv, gv),), t // tile, [x, g],
                 [_tile_spec(tile, d), _whole((1, d))], [_sds((t, d), MXU_DTYPE)], [_tile_spec(tile, d)])[0]


def _resid_rms_fwd(name, x, y, g):
    t, d = x.shape
    tile = 512
    return _rows(name, lambda xv, yv, gv: (xv + _rms(yv, gv),), t // tile, [x, y, g],
                 [_tile_spec(tile, d), _tile_spec(tile, d), _whole((1, d))], [_sds((t, d), F32)],
                 [_tile_spec(tile, d)])[0]


def _rms_bwd(name, u, g, dy, resid, out_dtype):
    t, d = u.shape
    tile = 256

    def fn(uv, gv, dyv, *rest):
        du, dg = _rms_bwd_math(uv, gv, dyv)
        if rest:
            du = du + rest[0]
        return du, dg

    ins = [u, g, dy] + ([resid] if resid is not None else [])
    specs = [_tile_spec(tile, d), _whole((1, d)), _tile_spec(tile, d)] + ([_tile_spec(tile, d)] if resid is not None else [])
    return _rows(name, fn, t // tile, ins, specs, [_sds((t, d), out_dtype), _sds((1, d), F32)],
                 [_tile_spec(tile, d), _whole((1, d))], n_red=1)


def _cumsum_rows(x, group, suffix):
    n = x.shape[0]
    pos = lax.broadcasted_iota(jnp.int32, x.shape, 0) % group
    d = 1
    while d < group:
        if suffix:
            x = x + jnp.where(pos < group - d, pltpu.roll(x, n - d, 0), 0.0)
        else:
            x = x + jnp.where(pos >= d, pltpu.roll(x, d, 0), 0.0)
        d *= 2
    return x


def _hg_gates(zq, zf, lb):
    q = zq * _sigmoid(zq)
    f = lb + (1.0 - lb) * _sigmoid(zf)
    logf = jnp.log(jnp.maximum(f, jnp.finfo(F32).tiny))
    k = (1.0 - lb) * _sigmoid(-zf)
    return q, k, logf, f


def _tri_mask(n, rev):
    t_i = lax.broadcasted_iota(jnp.int32, (n, n), 0)
    s_i = lax.broadcasted_iota(jnp.int32, (n, n), 1)
    return (s_i >= t_i) if rev else (s_i <= t_i)


def _anchors_are_safe(b_s, n_chunks, rev):
    worst = None
    for c in range(n_chunks):
        base = c * HG_CHUNK
        first = base + HG_CHUNK - 1 if rev else base
        last = base if rev else base + HG_CHUNK - 1
        mid = base + HG_CHUNK // 2
        b0, bm, bl = b_s[first:first + 1, :], b_s[mid:mid + 1, :], b_s[last:last + 1, :]
        span = jnp.maximum(b0 - bm, bm - bl)
        worst = span if worst is None else jnp.maximum(worst, span)
    return jnp.max(worst) < HG_SAFE_EXP


def _sub_ranges(base, i_sub, rev):
    r0 = base + i_sub * HG_SUB
    r1 = r0 + HG_SUB
    if rev:
        e0, e1, anchor = r1, base + HG_CHUNK, r1
    else:
        e0, e1, anchor = base, r0, r0 - 1
    return r0, r1, e0, e1, anchor


def _hgrn_fwd(name, z, lb_row, rev):
    t = z.shape[0]
    nb = t // HG_BLOCK
    ncb = HG_BLOCK // HG_CHUNK
    nsb = HG_CHUNK // HG_SUB
    zf0 = 16 if rev else 8

    def tmap(i):
        return nb - 1 - i if rev else i

    def body(zq_ref, zf_ref, zi_ref, lb_ref, o_ref, st_ref, state, q_s, k_s, v_s, b_s):
        @pl.when(pl.program_id(1) == 0)
        def _():
            state[...] = jnp.zeros_like(state)

        q, k, logf, _ = _hg_gates(zq_ref[...], zf_ref[...], lb_ref[...])
        v = zi_ref[...]
        b_all = _cumsum_rows(logf, HG_CHUNK, rev)
        b_s[...] = b_all
        order = range(ncb - 1, -1, -1) if rev else range(ncb)
        safe = _anchors_are_safe(b_s, ncb, rev)

        @pl.when(safe)
        def _():
            cmask = _tri_mask(HG_CHUNK, rev)
            st = state[...]
            outs = [None] * ncb
            for c in order:
                base = c * HG_CHUNK
                rows = slice(base, base + HG_CHUNK)
                last = base if rev else base + HG_CHUNK - 1
                mid = base + HG_CHUNK // 2
                q_c, k_c, v_c, b = q[rows, :], k[rows, :], v[rows, :], b_all[rows, :]
                bl, bm = b_s[last:last + 1, :], b_s[mid:mid + 1, :]
                st_ref[c] = st
                a = jnp.where(cmask, _nt(q_c * jnp.exp(b - bm), k_c * jnp.exp(bm - b)), 0.0)
                outs[c] = _nt(q_c * jnp.exp(b), st) + _nn(a, v_c)
                st = st * jnp.exp(bl) + _tn(v_c, k_c * jnp.exp(bl - b))
            state[...] = st
            o_ref[...] = jnp.concatenate(outs, axis=0)

        @pl.when(jnp.logical_not(safe))
        def _():
            q_s[...] = q
            k_s[...] = k
            v_s[...] = v
            mask = _tri_mask(HG_SUB, rev)
            for c in order:
                base = c * HG_CHUNK
                rows = slice(base, base + HG_CHUNK)
                last = base if rev else base + HG_CHUNK - 1
                st = state[...]
                st_ref[c] = st
                b = b_s[rows, :]
                bl = b_s[last:last + 1, :]
                o_inter = _nt(q_s[rows, :] * jnp.exp(b), st)
                kd = k_s[rows, :] * jnp.exp(bl - b)
                state[...] = st * jnp.exp(bl) + _tn(v_s[rows, :], kd)
                parts = []
                for i_sub in range(nsb):
                    r0, r1, e0, e1, anchor = _sub_ranges(base, i_sub, rev)
                    q_i, k_i, b_i = q_s[r0:r1, :], k_s[r0:r1, :], b_s[r0:r1, :]
                    decay = jnp.exp(jnp.minimum(b_i[:, None, :] - b_i[None, :, :], 0.0))
                    a_d = jnp.where(mask, jnp.sum(q_i[:, None, :] * k_i[None, :, :] * decay, axis=-1), 0.0)
                    o_i = _nn(a_d, v_s[r0:r1, :])
                    if e1 > e0:
                        anc = b_s[anchor:anchor + 1, :]
                        q_t = q_i * jnp.exp(b_i - anc)
                        k_t = k_s[e0:e1, :] * jnp.exp(anc - b_s[e0:e1, :])
                        o_i = o_i + _nn(_nt(q_t, k_t), v_s[e0:e1, :])
                    parts.append(o_i)
                o_ref[rows, :] = o_inter + jnp.concatenate(parts, axis=0)

    blk = lambda off: pl.BlockSpec((HG_BLOCK, HEAD), lambda h, i: (tmap(i), off + h))
    return _pcall(
        body, name=name, grid=(N_HEADS, nb),
        in_specs=[blk(0), blk(zf0), blk(24), pl.BlockSpec((1, HEAD), lambda h, i: (0, h))],
        out_specs=[blk(0), pl.BlockSpec((None, ncb, HEAD, HEAD), lambda h, i: (h, tmap(i), 0, 0))],
        out_shape=[_sds((t, D_MODEL), F32), _sds((N_HEADS, t // HG_CHUNK, HEAD, HEAD), F32)],
        scratch_shapes=[pltpu.VMEM((HEAD, HEAD), F32)] + [pltpu.VMEM((HG_BLOCK, HEAD), F32)] * 4,
        compiler_params=_params(2))(z, z, z, lb_row)


def _hgrn_bwd(name, z, d_o, states, lb_row, rev, dz):
    t = z.shape[0]
    nb = t // HG_BLOCK
    ncb = HG_BLOCK // HG_CHUNK
    nsb = HG_CHUNK // HG_SUB
    zf0 = 16 if rev else 8

    def tmap(i):
        return i if rev else nb - 1 - i

    def body(zq_ref, zf_ref, zi_ref, do_ref, st_ref, lb_ref, dz_in, dq_ref, dv_ref, dzf_ref, dlb_ref,
             dstate, q_s, k_s, v_s, b_s, dq_s, dk_s, dv_s, db_s):
        del dz_in
        first = pl.program_id(1) == 0

        @pl.when(first)
        def _():
            dstate[...] = jnp.zeros_like(dstate)

        lb = lb_ref[...]
        zf = zf_ref[...]
        q, k, logf, f = _hg_gates(zq_ref[...], zf, lb)
        v = zi_ref[...]
        b_all = _cumsum_rows(logf, HG_CHUNK, rev)
        b_s[...] = b_all
        order = range(ncb) if rev else range(ncb - 1, -1, -1)
        safe = _anchors_are_safe(b_s, ncb, rev)

        @pl.when(safe)
        def _():
            cmask = _tri_mask(HG_CHUNK, rev)
            row_i = lax.broadcasted_iota(jnp.int32, (HG_CHUNK, HEAD), 0)
            dst = dstate[...]
            dq_l, dk_l, dv_l, db_l = [None] * ncb, [None] * ncb, [None] * ncb, [None] * ncb
            for c in order:
                base = c * HG_CHUNK
                rows = slice(base, base + HG_CHUNK)
                last = base if rev else base + HG_CHUNK - 1
                mid = base + HG_CHUNK // 2
                q_c, k_c, v_c, b, do_c = q[rows, :], k[rows, :], v[rows, :], b_all[rows, :], do_ref[rows, :]
                bl, bm = b_s[last:last + 1, :], b_s[mid:mid + 1, :]
                st0 = st_ref[c]
                e, el, ebl = jnp.exp(b), jnp.exp(bl), jnp.exp(bl - b)
                e_q, e_k = jnp.exp(b - bm), jnp.exp(bm - b)
                q_t, k_t = q_c * e_q, k_c * e_k
                a = jnp.where(cmask, _nt(q_t, k_t), 0.0)
                da = jnp.where(cmask, _nt(do_c, v_c), 0.0)
                dq = _nn(do_c, st0) * e + e_q * _nn(da, k_t)
                dk_inter = _nn(v_c, dst) * ebl
                dk = dk_inter + e_k * _tn(da, q_t)
                dv_l[c] = _nt(k_c * ebl, dst) + _tn(a, do_c)
                extra = el * _rowsum(dst * st0) + _rowsum(k_c * dk_inter)
                dst = _tn(do_c, q_c * e) + dst * el
                db_l[c] = q_c * dq - k_c * dk + jnp.where(row_i == last - base, extra, 0.0)
                dq_l[c], dk_l[c] = dq, dk
            dstate[...] = dst
            dq_s[...] = jnp.concatenate(dq_l, axis=0)
            dk_s[...] = jnp.concatenate(dk_l, axis=0)
            dv_s[...] = jnp.concatenate(dv_l, axis=0)
            db_s[...] = jnp.concatenate(db_l, axis=0)

        @pl.when(jnp.logical_not(safe))
        def _():
            q_s[...] = q
            k_s[...] = k
            v_s[...] = v
            mask = _tri_mask(HG_SUB, rev)
            for c in order:
                base = c * HG_CHUNK
                rows = slice(base, base + HG_CHUNK)
                last = base if rev else base + HG_CHUNK - 1
                st0 = st_ref[c]
                dst1 = dstate[...]
                b = b_s[rows, :]
                bl = b_s[last:last + 1, :]
                e = jnp.exp(b)
                el = jnp.exp(bl)
                ebl = jnp.exp(bl - b)
                q_c, k_c, v_c, do_c = q_s[rows, :], k_s[rows, :], v_s[rows, :], do_ref[rows, :]
                kd = k_c * ebl
                dq_s[rows, :] = _nn(do_c, st0) * e
                dk_inter = _nn(v_c, dst1) * ebl
                dk_s[rows, :] = dk_inter
                dv_s[rows, :] = _nt(kd, dst1)
                extra = el * _rowsum(dst1 * st0) + _rowsum(k_c * dk_inter)
                dstate[...] = _tn(do_c, q_c * e) + dst1 * el
                for i_sub in range(nsb):
                    r0, r1, e0, e1, anchor = _sub_ranges(base, i_sub, rev)
                    q_i, k_i, b_i, v_i, do_i = q_s[r0:r1, :], k_s[r0:r1, :], b_s[r0:r1, :], v_s[r0:r1, :], do_ref[r0:r1, :]
                    decay = jnp.exp(jnp.minimum(b_i[:, None, :] - b_i[None, :, :], 0.0))
                    a_d = jnp.where(mask, jnp.sum(q_i[:, None, :] * k_i[None, :, :] * decay, axis=-1), 0.0)
                    da_d = jnp.where(mask, _nt(do_i, v_i), 0.0)
                    wgt = da_d[:, :, None] * decay
                    dq_s[r0:r1, :] += jnp.sum(wgt * k_i[None, :, :], axis=1)
                    dk_s[r0:r1, :] += jnp.sum(wgt * q_i[:, None, :], axis=0)
                    dv_s[r0:r1, :] += _tn(a_d, do_i)
                    if e1 > e0:
                        anc = b_s[anchor:anchor + 1, :]
                        e_q = jnp.exp(b_i - anc)
                        e_k = jnp.exp(anc - b_s[e0:e1, :])
                        q_t = q_i * e_q
                        k_t = k_s[e0:e1, :] * e_k
                        a_o = _nt(q_t, k_t)
                        da_o = _nt(do_i, v_s[e0:e1, :])
                        dq_s[r0:r1, :] += e_q * _nn(da_o, k_t)
                        dk_s[e0:e1, :] += e_k * _tn(da_o, q_t)
                        dv_s[e0:e1, :] += _tn(a_o, do_i)
                db_s[rows, :] = q_c * dq_s[rows, :] - k_c * dk_s[rows, :]
                db_s[last:last + 1, :] += extra

        dlogf = _cumsum_rows(db_s[...], HG_CHUNK, not rev)
        s_neg = _sigmoid(-zf)
        df = jnp.where(f > jnp.finfo(F32).tiny, dlogf / f, 0.0)
        dfk = df - dk_s[...]
        dzf_ref[...] = ((1.0 - lb) * _sigmoid(zf) * s_neg * dfk).astype(dzf_ref.dtype)
        dlb = _rowsum(s_neg * dfk)

        @pl.when(first)
        def _():
            dlb_ref[...] = dlb

        @pl.when(jnp.logical_not(first))
        def _():
            dlb_ref[...] += dlb

        dq_ref[...] = dq_s[...]
        dv_ref[...] = dv_s[...]

    blk = lambda off: pl.BlockSpec((HG_BLOCK, HEAD), lambda h, i: (tmap(i), off + h))
    vec = pl.BlockSpec((1, HEAD), lambda h, i: (0, h))
    return _pcall(
        body, name=name, grid=(N_HEADS, nb),
        in_specs=[blk(0), blk(zf0), blk(24), blk(0),
                  pl.BlockSpec((None, ncb, HEAD, HEAD), lambda h, i: (h, tmap(i), 0, 0)), vec, ANY],
        out_specs=[blk(0), blk(0), blk(zf0), vec],
        out_shape=[_sds((t, D_MODEL), F32), _sds((t, D_MODEL), F32), _sds(dz.shape, dz.dtype), _sds((1, D_MODEL), F32)],
        input_output_aliases={6: 2},
        scratch_shapes=[pltpu.VMEM((HEAD, HEAD), F32)] + [pltpu.VMEM((HG_BLOCK, HEAD), F32)] * 8,
        compiler_params=_params(2))(z, z, z, d_o, states, lb_row, dz)


def _lower_bounds(name, gamma):
    def body(g_ref, o_ref):
        g0, g1 = g_ref[0:1, :], g_ref[1:2, :]
        m = jnp.maximum(g0, g1)
        e0, e1 = jnp.exp(g0 - m), jnp.exp(g1 - m)
        s0, s1 = e0 / (e0 + e1), e1 / (e0 + e1)
        o_ref[0:1, :] = s0 - s0
        o_ref[1:2, :] = (s0 + s1) - s0

    return _pcall(body, name=name, out_shape=_sds(gamma.shape, F32))(gamma)


def _lower_bounds_bwd(name, gamma, dlb):
    def body(g_ref, d_ref, o_ref):
        g0, g1 = g_ref[0:1, :], g_ref[1:2, :]
        m = jnp.maximum(g0, g1)
        e0, e1 = jnp.exp(g0 - m), jnp.exp(g1 - m)
        s0, s1 = e0 / (e0 + e1), e1 / (e0 + e1)
        d0, d1 = d_ref[0:1, :], d_ref[1:2, :]
        ds0 = (d0 + d1) - (d0 + d1)
        ds1 = d1
        inner = s0 * ds0 + s1 * ds1
        o_ref[0:1, :] = s0 * (ds0 - inner)
        o_ref[1:2, :] = s1 * (ds1 - inner)

    return _pcall(body, name=name, out_shape=_sds(gamma.shape, F32))(gamma, dlb)


def _heads(x):
    return [x[:, h * HEAD:(h + 1) * HEAD] for h in range(N_HEADS)]


def _hg_post_fwd(name, o_f, o_b, z, gain):
    t = z.shape[0]
    tile = 256

    def fn(of, ob, zg, g):
        outs = []
        for o_h, zg_h, g_h in zip(_heads(of + ob), _heads(zg), _heads(g)):
            outs.append(_rms(o_h, g_h) * (zg_h * _sigmoid(zg_h)))
        return (jnp.concatenate(outs, axis=1),)

    return _rows(name, fn, t // tile, [o_f, o_b, z, gain],
                 [_tile_spec(tile, D_MODEL), _tile_spec(tile, D_MODEL), _tile_spec(tile, D_MODEL, 4), _whole((1, D_MODEL))],
                 [_sds((t, D_MODEL), MXU_DTYPE)], [_tile_spec(tile, D_MODEL)])[0]


def _hg_post_bwd(name, da, o_f, o_b, z, gain, dz):
    t = z.shape[0]
    tile = 256

    def fn(dav, of, ob, zg, g):
        d_o, dzg, dgain = [], [], []
        for da_h, o_h, zg_h, g_h in zip(_heads(dav), _heads(of + ob), _heads(zg), _heads(g)):
            s = _sigmoid(zg_h)
            r = lax.rsqrt(_mean_last(o_h * o_h) + EPS)
            oh = o_h * r
            dy = da_h * (zg_h * s)
            dzg.append(da_h * (oh * g_h) * _dsilu(zg_h, s))
            gdy = dy * g_h
            d_o.append(r * (gdy - oh * _mean_last(gdy * oh)))
            dgain.append(_rowsum(dy * oh))
        return jnp.concatenate(d_o, axis=1), jnp.concatenate(dzg, axis=1), jnp.concatenate(dgain, axis=1)

    return _rows(name, fn, t // tile, [da, o_f, o_b, z, gain],
                 [_tile_spec(tile, D_MODEL)] * 3 + [_tile_spec(tile, D_MODEL, 4), _whole((1, D_MODEL))],
                 [_sds((t, D_MODEL), F32), _sds(dz.shape, dz.dtype), _sds((1, D_MODEL), F32)],
                 [_tile_spec(tile, D_MODEL), _tile_spec(tile, D_MODEL, 4), _whole((1, D_MODEL))],
                 n_red=1, into=[(dz, 1)])


def _hg_gate_bwd(name, dq_f, dq_b, dv_f, dv_b, z, dz):
    t = z.shape[0]
    tile = 512

    def fq(a, b, zq):
        return ((a + b) * _dsilu(zq, _sigmoid(zq)),)

    dz = _rows(name + "_q", fq, t // tile, [dq_f, dq_b, z], [_tile_spec(tile, D_MODEL)] * 3,
               [_sds(dz.shape, dz.dtype)], [_tile_spec(tile, D_MODEL, 0)], into=[(dz, 0)])[0]
    dz = _rows(name + "_v", lambda a, b: (a + b,), t // tile, [dv_f, dv_b], [_tile_spec(tile, D_MODEL)] * 2,
               [_sds(dz.shape, dz.dtype)], [_tile_spec(tile, D_MODEL, 3)], into=[(dz, 0)])[0]
    return dz


def _sg_norm(zv, ln_g, ln_b):
    gv = _gelu(zv)
    xc = gv - _mean_last(gv)
    rstd = lax.rsqrt(_mean_last(xc * xc) + EPS)
    xhat = xc * rstd
    return xhat * ln_g + ln_b, xhat, rstd


def _lane_lo():
    return lax.broadcasted_iota(jnp.int32, (SG_CHUNK, LANES), 1) < (LANES // 2)


SG_TILE = 512


def _sgu_fwd(name, z, w, bias_t, ln_g, ln_b):
    t = z.shape[0]

    def fn(zu, zv, wv, bt, lg, lb):
        u = _gelu(zu)
        vn, _, _ = _sg_norm(zv, lg, lb)
        lo = _lane_lo()
        out_rows = []
        for c in range(SG_TILE // SG_CHUNK):
            rs = slice(c * SG_CHUNK, (c + 1) * SG_CHUNK)
            cols = []
            for j in range(SG_WIDTH // LANES):
                cs = slice(j * LANES, (j + 1) * LANES)
                vb = vn[rs, cs]
                sg = jnp.where(lo, _nn(wv[2 * j], vb), _nn(wv[2 * j + 1], vb)) + bt[:, cs]
                cols.append(u[rs, cs] * sg)
            out_rows.append(jnp.concatenate(cols, axis=1))
        return (jnp.concatenate(out_rows, axis=0),)

    return _rows(name, fn, t // SG_TILE, [z, z, w, bias_t, ln_g, ln_b],
                 [_tile_spec(SG_TILE, SG_WIDTH, 10), _tile_spec(SG_TILE, SG_WIDTH, 11), _whole(w.shape),
                  _whole(bias_t.shape), _whole((1, SG_WIDTH)), _whole((1, SG_WIDTH))],
                 [_sds((t, SG_WIDTH), MXU_DTYPE)], [_tile_spec(SG_TILE, SG_WIDTH)])[0]


def _sgu_bwd(name, dbo, z, w, bias_t, ln_g, ln_b, dz):
    t = z.shape[0]
    n_grp = w.shape[0]

    def fn(dbov, zu, zv, wv, bt, lg, lb):
        u = _gelu(zu)
        vn, xhat, rstd = _sg_norm(zv, lg, lb)
        lo = _lane_lo()
        dw = [None] * n_grp
        dsg_sum = None
        du_rows, dvn_rows = [], []
        for c in range(SG_TILE // SG_CHUNK):
            rs = slice(c * SG_CHUNK, (c + 1) * SG_CHUNK)
            du_cols, dvn_cols, dsg_cols = [], [], []
            for j in range(SG_WIDTH // LANES):
                cs = slice(j * LANES, (j + 1) * LANES)
                vb = vn[rs, cs]
                sg = jnp.where(lo, _nn(wv[2 * j], vb), _nn(wv[2 * j + 1], vb)) + bt[:, cs]
                du_cols.append(dbov[rs, cs] * sg)
                dsg = dbov[rs, cs] * u[rs, cs]
                dsg_cols.append(dsg)
                d0 = _nt(jnp.where(lo, dsg, 0.0), vb)
                d1 = _nt(jnp.where(lo, 0.0, dsg), vb)
                dw[2 * j] = d0 if dw[2 * j] is None else dw[2 * j] + d0
                dw[2 * j + 1] = d1 if dw[2 * j + 1] is None else dw[2 * j + 1] + d1
                dvn_cols.append(jnp.where(lo, _tn(wv[2 * j], dsg), _tn(wv[2 * j + 1], dsg)))
            du_rows.append(jnp.concatenate(du_cols, axis=1))
            dvn_rows.append(jnp.concatenate(dvn_cols, axis=1))
            dsg_c = jnp.concatenate(dsg_cols, axis=1)
            dsg_sum = dsg_c if dsg_sum is None else dsg_sum + dsg_c
        du = jnp.concatenate(du_rows, axis=0)
        dvn = jnp.concatenate(dvn_rows, axis=0)
        dxhat = dvn * lg
        dgv = rstd * (dxhat - _mean_last(dxhat) - xhat * _mean_last(dxhat * xhat))
        dzuv = jnp.concatenate([du * _dgelu(zu), dgv * _dgelu(zv)], axis=1)
        return dzuv, jnp.stack(dw, axis=0), dsg_sum, _rowsum(dvn * xhat), _rowsum(dvn)

    return _rows(name, fn, t // SG_TILE, [dbo, z, z, w, bias_t, ln_g, ln_b],
                 [_tile_spec(SG_TILE, SG_WIDTH), _tile_spec(SG_TILE, SG_WIDTH, 10), _tile_spec(SG_TILE, SG_WIDTH, 11),
                  _whole(w.shape), _whole(bias_t.shape), _whole((1, SG_WIDTH)), _whole((1, SG_WIDTH))],
                 [_sds(dz.shape, dz.dtype), _sds(w.shape, F32), _sds((SG_CHUNK, SG_WIDTH), F32),
                  _sds((1, SG_WIDTH), F32), _sds((1, SG_WIDTH), F32)],
                 [_tile_spec(SG_TILE, 2 * SG_WIDTH, 5), _whole(w.shape), _whole((SG_CHUNK, SG_WIDTH)),
                  _whole((1, SG_WIDTH)), _whole((1, SG_WIDTH))],
                 n_red=4, into=[(dz, 0)])


def _merge_fwd(name, pa, pb, z):
    t = z.shape[0]
    tile = 256
    return _rows(name, lambda a, b, ga, gb: (_sigmoid(ga) * a + _sigmoid(gb) * b,), t // tile, [pa, pb, z, z],
                 [_tile_spec(tile, D_MODEL)] * 2 + [_tile_spec(tile, D_MODEL, 6), _tile_spec(tile, D_MODEL, 7)],
                 [_sds((t, D_MODEL), MXU_DTYPE)], [_tile_spec(tile, D_MODEL)])[0]


def _merge_bwd(name, dm, pa, pb, z):
    t = z.shape[0]
    tile = 256

    def fn(d, a, b, ga, gb):
        sa, sb = _sigmoid(ga), _sigmoid(gb)
        dgate = jnp.concatenate([d * a * sa * (1.0 - sa), d * b * sb * (1.0 - sb)], axis=1)
        return d * sa, d * sb, dgate

    return _rows(name, fn, t // tile, [dm, pa, pb, z, z],
                 [_tile_spec(tile, D_MODEL)] * 3 + [_tile_spec(tile, D_MODEL, 6), _tile_spec(tile, D_MODEL, 7)],
                 [_sds((t, D_MODEL), MXU_DTYPE), _sds((t, D_MODEL), MXU_DTYPE), _sds((t, N_IN), MXU_DTYPE)],
                 [_tile_spec(tile, D_MODEL), _tile_spec(tile, D_MODEL), _tile_spec(tile, 2 * D_MODEL, 3)])


def _swiglu_fwd(name, gt, up):
    n, t, f = gt.shape
    tile = 512
    g2, u2 = gt.reshape(n * t, f), up.reshape(n * t, f)
    act = _rows(name, lambda g, u: ((g * _sigmoid(g)) * u,), n * t // tile, [g2, u2], [_tile_spec(tile, f)] * 2,
                [_sds((n * t, f), MXU_DTYPE)], [_tile_spec(tile, f)])[0]
    return act.reshape(n, t, f)


def _swiglu_bwd(name, dact, gt, up):
    n, t, f = gt.shape
    tile = 512

    def fn(d, g, u):
        s = _sigmoid(g)
        return d * u * _dsilu(g, s), d * (g * s)

    dgt, dup = _rows(name, fn, n * t // tile, [dact.reshape(n * t, f), gt.reshape(n * t, f), up.reshape(n * t, f)],
                     [_tile_spec(tile, f)] * 3, [_sds((n * t, f), MXU_DTYPE)] * 2, [_tile_spec(tile, f)] * 2)
    return dgt.reshape(n, t, f), dup.reshape(n, t, f)


def _ple_fwd(name, x2, pe, gz):
    t, d = x2.shape
    tile = 512
    return _rows(name, lambda x, p, g: (x + p * _sigmoid(g),), t // tile, [x2, pe, gz], [_tile_spec(tile, d)] * 3,
                 [_sds((t, d), F32)], [_tile_spec(tile, d)])[0]


def _ple_bwd(name, dx, pe, gz):
    t, d = dx.shape
    tile = 512

    def fn(dv, p, g):
        s = _sigmoid(g)
        return dv * s, dv * p * s * (1.0 - s)

    return _rows(name, fn, t // tile, [dx, pe, gz], [_tile_spec(tile, d)] * 3, [_sds((t, d), MXU_DTYPE)] * 2,
                 [_tile_spec(tile, d)] * 2)


def _loss_bwd(name, y, target):
    t, d = y.shape
    tile = 512

    def fn(yv, tv):
        err = yv - tv
        return err * (1.0 / d), _rowsum(err * err)

    return _rows(name, fn, t // tile, [y, target], [_tile_spec(tile, d)] * 2, [_sds((t, d), F32), _sds((1, d), F32)],
                 [_tile_spec(tile, d), _whole((1, d))], n_red=1)


def _position():
    return lax.axis_index("x"), lax.axis_index("y"), lax.axis_index("c")


def _all_gather_weights(shards):
    n = len(shards)

    def body(*refs):
        w_refs, out_refs = refs[:n], refs[n:2 * n]
        send_sems, recv_sems, local_sems = refs[2 * n:]
        x, y, c = _position()
        me = 2 * x + y
        sibling = (x, y, 1 - c)
        chips = [(1 - x, y), (x, 1 - y), (1 - x, 1 - y)]

        def copy(i, k, src, chip, layer, to):
            return pltpu.make_async_remote_copy(
                src_ref=src, dst_ref=out_refs[i].at[chip, pl.ds(layer, 1)], send_sem=send_sems.at[6 * i + k],
                recv_sem=recv_sems.at[6 * i + k], device_id=to, device_id_type=MESH)

        local = [pltpu.make_async_copy(w_refs[i], out_refs[i].at[me], local_sems.at[i]) for i in range(n)]
        for cp in local:
            cp.start()
        sends = [copy(i, j, w_refs[i].at[pl.ds(c, 1)], me, c, (px, py, c))
                 for i in range(n) for j, (px, py) in enumerate(chips)]
        for cp in sends:
            cp.start()
        passed = []
        for i in range(n):
            for j, (px, py) in enumerate(chips):
                chip = 2 * px + py
                copy(i, j, w_refs[i].at[pl.ds(c, 1)], chip, c, (px, py, c)).wait_recv()
                fwd = copy(i, 3 + j, out_refs[i].at[chip, pl.ds(c, 1)], chip, c, sibling)
                fwd.start()
                passed.append(fwd)
        for i in range(n):
            for j, (px, py) in enumerate(chips):
                copy(i, 3 + j, w_refs[i].at[pl.ds(c, 1)], 2 * px + py, 1 - c, sibling).wait_recv()
        for cp in sends + passed:
            cp.wait_send()
        for cp in local:
            cp.wait()

    return _pcall(
        body, name="all_gather_weights",
        in_specs=[ANY] * n, out_specs=[ANY] * n,
        out_shape=[_sds((N_CHIPS,) + s.shape, s.dtype) for s in shards],
        scratch_shapes=[pltpu.SemaphoreType.DMA((6 * n,)), pltpu.SemaphoreType.DMA((6 * n,)),
                        pltpu.SemaphoreType.DMA((n,))],
    )(*shards)


def _exchange_grads(grads):
    n = len(grads)

    def body(*refs):
        g_refs, out_refs = refs[:n], refs[n:2 * n]
        send_sems, recv_sems, local_sems = refs[2 * n:]
        x, y, c = _position()
        me = 2 * x + y
        sibling = (x, y, 1 - c)
        chips = [(1 - x, y), (x, 1 - y), (1 - x, 1 - y)]

        def copy(i, k, src, slot, to):
            return pltpu.make_async_remote_copy(
                src_ref=src, dst_ref=out_refs[i].at[slot], send_sem=send_sems.at[7 * i + k],
                recv_sem=recv_sems.at[7 * i + k], device_id=to, device_id_type=MESH)

        local = [pltpu.make_async_copy(g_refs[i].at[me], out_refs[i].at[2 * me + c], local_sems.at[i]) for i in range(n)]
        for cp in local:
            cp.start()
        first = []
        for i in range(n):
            first.append(copy(i, 0, g_refs[i].at[me], 2 * me + c, sibling))
            for j, (px, py) in enumerate(chips):
                first.append(copy(i, 1 + j, g_refs[i].at[2 * px + py], 2 * me + c, (px, py, c)))
        for cp in first:
            cp.start()
        passed = []
        for i in range(n):
            for j, (px, py) in enumerate(chips):
                slot = 2 * (2 * px + py) + c
                copy(i, 1 + j, g_refs[i].at[me], slot, (px, py, c)).wait_recv()
                fwd = copy(i, 4 + j, out_refs[i].at[slot], slot, sibling)
                fwd.start()
                passed.append(fwd)
        for i in range(n):
            copy(i, 0, g_refs[i].at[me], 2 * me + (1 - c), sibling).wait_recv()
            for j, (px, py) in enumerate(chips):
                slot = 2 * (2 * px + py) + (1 - c)
                copy(i, 4 + j, g_refs[i].at[me], slot, sibling).wait_recv()
        for cp in first + passed:
            cp.wait_send()
        for cp in local:
            cp.wait()

    return _pcall(
        body, name="exchange_grads",
        in_specs=[ANY] * n, out_specs=[ANY] * n,
        out_shape=[_sds((N_DEV,) + g.shape[1:], g.dtype) for g in grads],
        scratch_shapes=[pltpu.SemaphoreType.DMA((7 * n,)), pltpu.SemaphoreType.DMA((7 * n,)),
                        pltpu.SemaphoreType.DMA((n,))],
    )(*grads)


def _all_reduce_small(packed):
    rows = packed.shape[0]

    def body(x_ref, sum_ref, slots, send_sems, recv_sems, local_sem):
        x, y, c = _position()
        me = 4 * x + 2 * y + c
        mine = pltpu.make_async_copy(x_ref, slots.at[me], local_sem)
        mine.start()
        sends = []
        for k in range(1, N_DEV):
            peer = (x ^ (k >> 2), y ^ ((k >> 1) & 1), c ^ (k & 1))
            cp = pltpu.make_async_remote_copy(src_ref=x_ref, dst_ref=slots.at[me], send_sem=send_sems.at[k - 1],
                                              recv_sem=recv_sems.at[k - 1], device_id=peer, device_id_type=MESH)
            cp.start()
            sends.append(cp)
        for k in range(1, N_DEV):
            px, py, pc = x ^ (k >> 2), y ^ ((k >> 1) & 1), c ^ (k & 1)
            pltpu.make_async_remote_copy(src_ref=x_ref, dst_ref=slots.at[4 * px + 2 * py + pc], send_sem=send_sems.at[k - 1],
                                         recv_sem=recv_sems.at[k - 1], device_id=(px, py, pc), device_id_type=MESH).wait_recv()
        for cp in sends:
            cp.wait_send()
        mine.wait()
        total = slots[0]
        for d in range(1, N_DEV):
            total = total + slots[d]
        sum_ref[...] = total

    vmem = pl.BlockSpec(memory_space=pltpu.VMEM)
    return _pcall(
        body, name="all_reduce_small", in_specs=[vmem], out_specs=vmem, out_shape=_sds(packed.shape, F32),
        scratch_shapes=[pltpu.VMEM((N_DEV, rows, LANES), F32), pltpu.SemaphoreType.DMA((N_DEV - 1,)),
                        pltpu.SemaphoreType.DMA((N_DEV - 1,)), pltpu.SemaphoreType.DMA],
        compiler_params=pltpu.CompilerParams(vmem_limit_bytes=VMEM_LIMIT_BYTES),
    )(packed)


def _adamw(w, g, m, v):
    m = ADAM_B1 * m + (1.0 - ADAM_B1) * g
    v = ADAM_B2 * v + (1.0 - ADAM_B2) * (g * g)
    m_hat = m / (1.0 - ADAM_B1 ** ADAM_STEP)
    v_hat = v / (1.0 - ADAM_B2 ** ADAM_STEP)
    delta = -ADAM_LR * (m_hat / (jnp.sqrt(v_hat) + ADAM_EPS) + ADAM_WD * w)
    return delta, m, v


def _adam_sharded(name, parts, w, m, v):
    shape = w.shape
    cols = shape[-1]
    rows = w.size // cols
    tile = 8
    while tile * 2 * cols <= ADAM_TILE_ELEMS and rows % (tile * 2) == 0:
        tile *= 2

    def fn(p, wv, mv, vv):
        g = p[0].astype(F32)
        for d in range(1, N_DEV):
            g = g + p[d].astype(F32)
        return (g,) + _adamw(wv, g, mv, vv)

    two_d = lambda a: a.reshape(rows, cols)
    outs = _rows(name, fn, rows // tile, [parts.reshape(N_DEV, rows, cols), two_d(w), two_d(m), two_d(v)],
                 [pl.BlockSpec((N_DEV, tile, cols), lambda i: (0, i, 0))] + [_tile_spec(tile, cols)] * 3,
                 [_sds((rows, cols), F32)] * 4, [_tile_spec(tile, cols)] * 4)
    return [o.reshape(shape) for o in outs]


def _adam_small(name, g, w, m, v):
    rows = g.shape[0]
    tile = rows // 2
    return _rows(name, lambda gv, wv, mv, vv: _adamw(wv, gv, mv, vv), rows // tile, [g, w, m, v],
                 [_tile_spec(tile, LANES)] * 4, [_sds(g.shape, F32)] * 3, [_tile_spec(tile, LANES)] * 3)


BIG = ("w_in", "w_a", "w_b", "w_out", "w_gate", "w_up", "w_down", "w_ple", "w_ple_gate")
SMALL = ("norm_mix_pre", "lb_gamma_fwd", "lb_gamma_bwd", "hg_norm", "sg_w", "sg_b", "sg_ln_g", "sg_ln_b",
         "norm_mix_post", "norm_ffn_pre", "norm_ffn_post")


def _layer_fwd(l, x, p_l, wg, sm):
    sv = {"x": x}
    h = _rms_fwd(f"norm_mix_pre_l{l}", x, sm["norm_mix_pre"])
    z = _proj_in(h, wg["w_in"], l)
    o_f, st_f = _hgrn_fwd(f"hgrn_fwd_l{l}", z, sm["lb_fwd"], False)
    o_b, st_b = _hgrn_fwd(f"hgrn_rev_l{l}", z, sm["lb_bwd"], True)
    a_out = _hg_post_fwd(f"hg_post_l{l}", o_f, o_b, z, sm["hg_norm"])
    b_out = _sgu_fwd(f"sgu_l{l}", z, sm["sg_w"], sm["sg_bias_t"], sm["sg_ln_g"], sm["sg_ln_b"])
    pa = _proj_rows_sharded(f"proj_a_l{l}", a_out, wg["w_a"], l, F32)
    pb = _proj_cols256(f"proj_b_l{l}", b_out, wg["w_b"], l)
    merged = _merge_fwd(f"merge_l{l}", pa, pb, z)
    mix = _proj_rows_sharded(f"proj_out_l{l}", merged, wg["w_out"], l, F32)
    x1 = _resid_rms_fwd(f"norm_mix_post_l{l}", x, mix, sm["norm_mix_post"])
    h2 = _rms_fwd(f"norm_ffn_pre_l{l}", x1, sm["norm_ffn_pre"])
    gt = _proj_ffn_in(f"proj_gate_l{l}", h2, wg["w_gate"], l)
    up = _proj_ffn_in(f"proj_up_l{l}", h2, wg["w_up"], l)
    act = _swiglu_fwd(f"swiglu_l{l}", gt, up)
    ff = _proj_ffn_out(f"proj_down_l{l}", act, wg["w_down"], l)
    x2 = _resid_rms_fwd(f"norm_ffn_post_l{l}", x1, ff, sm["norm_ffn_post"])
    pe = _proj_cols256(f"proj_ple_l{l}", p_l, wg["w_ple"], l)
    gz = _proj_rows_sharded(f"proj_ple_gate_l{l}", x2, wg["w_ple_gate"], l, F32)
    x3 = _ple_fwd(f"ple_l{l}", x2, pe, gz)
    sv.update(h=h, z=z, o_f=o_f, o_b=o_b, st_f=st_f, st_b=st_b, a_out=a_out, b_out=b_out, pa=pa, pb=pb,
              merged=merged, mix=mix, x1=x1, h2=h2, gt=gt, up=up, act=act, ff=ff, x2=x2, pe=pe, gz=gz, p=p_l)
    return x3, sv


def _layer_bwd(l, dx3, sv, wg, sm, gw):
    t = dx3.shape[0]
    nt = t // TT
    sg = {}

    def wgrad(key, *a, **k):
        gw[key] = _wgrad(f"grad_{key}_l{l}", *a, l=l, into=gw.get(key), **k)

    row = lambda width: _bs((TT, width), lambda j, i: (i, 0))
    row_j = lambda width: _bs((TT, width), lambda j, i: (i, j))
    ffn_j = _bs((None, TT, FFN_SHARD), lambda j, i: (j, i, 0))
    blk_j = lambda shape: (tuple(shape), lambda j, i: (j, l, 0, 0))

    dpe, dgz = _ple_bwd(f"ple_bwd_l{l}", dx3, sv["pe"], sv["gz"])
    wgrad("w_ple", sv["p"], row(PLE_DIM), dpe, row_j(256), (PLE_DIM, 256), blk_j((PLE_DIM, 256)), N_CHIPS)
    wgrad("w_ple_gate", sv["x2"], row_j(256), dgz, row(D_MODEL), (256, D_MODEL), blk_j((256, D_MODEL)), N_CHIPS)
    dx2 = _proj_rows_sharded_t(f"proj_ple_gate_t_l{l}", dgz, wg["w_ple_gate"], l, F32, add=dx3)

    dff, sg["norm_ffn_post"] = _rms_bwd(f"norm_ffn_post_bwd_l{l}", sv["ff"], sm["norm_ffn_post"], dx2, None, MXU_DTYPE)
    dact = _proj_ffn_out_t(f"proj_down_t_l{l}", dff, wg["w_down"], l)
    wgrad("w_down", sv["act"], ffn_j, dff, row(D_MODEL), (FFN_SHARD, D_MODEL), blk_j((FFN_SHARD, D_MODEL)), N_CHIPS)
    dgt, dup = _swiglu_bwd(f"swiglu_bwd_l{l}", dact, sv["gt"], sv["up"])
    dh2 = _proj_ffn_in_t(f"proj_gate_up_t_l{l}", [(dgt, wg["w_gate"]), (dup, wg["w_up"])], l)
    wgrad("w_gate", sv["h2"], row(D_MODEL), dgt, ffn_j, (D_MODEL, FFN_SHARD), blk_j((D_MODEL, FFN_SHARD)), N_CHIPS)
    wgrad("w_up", sv["h2"], row(D_MODEL), dup, ffn_j, (D_MODEL, FFN_SHARD), blk_j((D_MODEL, FFN_SHARD)), N_CHIPS)
    dx1, sg["norm_ffn_pre"] = _rms_bwd(f"norm_ffn_pre_bwd_l{l}", sv["x1"], sm["norm_ffn_pre"], dh2, dx2, F32)

    dmix, sg["norm_mix_post"] = _rms_bwd(f"norm_mix_post_bwd_l{l}", sv["mix"], sm["norm_mix_post"], dx1, None, MXU_DTYPE)
    dmerged = _proj_rows_sharded_t(f"proj_out_t_l{l}", dmix, wg["w_out"], l, F32)
    wgrad("w_out", sv["merged"], row_j(256), dmix, row(D_MODEL), (256, D_MODEL), blk_j((256, D_MODEL)), N_CHIPS)
    dpa, dpb, dz = _merge_bwd(f"merge_bwd_l{l}", dmerged, sv["pa"], sv["pb"], sv["z"])
    da = _proj_rows_sharded_t(f"proj_a_t_l{l}", dpa, wg["w_a"], l, F32)
    wgrad("w_a", sv["a_out"], row_j(256), dpa, row(D_MODEL), (256, D_MODEL), blk_j((256, D_MODEL)), N_CHIPS)
    dbo = _proj_cols256_t(f"proj_b_t_l{l}", dpb, wg["w_b"], l)
    wgrad("w_b", sv["b_out"], row(SG_WIDTH), dpb, row_j(256), (SG_WIDTH, 256), blk_j((SG_WIDTH, 256)), N_CHIPS)

    dz, sg["sg_w"], dsg_sum, sg["sg_ln_g"], sg["sg_ln_b"] = _sgu_bwd(
        f"sgu_bwd_l{l}", dbo, sv["z"], sm["sg_w"], sm["sg_bias_t"], sm["sg_ln_g"], sm["sg_ln_b"], dz)
    sg["sg_b"] = dsg_sum.reshape(SG_CHUNK, N_HEADS, SG_WIDTH // N_HEADS).sum(axis=-1).T
    d_o, dz, sg["hg_norm"] = _hg_post_bwd(f"hg_post_bwd_l{l}", da, sv["o_f"], sv["o_b"], sv["z"], sm["hg_norm"], dz)
    dq_f, dv_f, dz, sg["lb_fwd"] = _hgrn_bwd(f"hgrn_fwd_bwd_l{l}", sv["z"], d_o, sv["st_f"], sm["lb_fwd"], False, dz)
    dq_b, dv_b, dz, sg["lb_bwd"] = _hgrn_bwd(f"hgrn_rev_bwd_l{l}", sv["z"], d_o, sv["st_b"], sm["lb_bwd"], True, dz)
    dz = _hg_gate_bwd(f"hg_gate_bwd_l{l}", dq_f, dq_b, dv_f, dv_b, sv["z"], dz)

    dh = _proj_in_t(dz, wg["w_in"], l)
    gw["w_in"] = _wgrad(f"grad_w_in_l{l}", sv["h"], _bs((TT, D_MODEL), lambda n, i: (i, 0)), dz,
                        _bs((TT, 1024), lambda n, i: (i, n)), (D_MODEL, 2048),
                        ((D_MODEL, 1024), lambda n, i: (n // 2, l, 0, n % 2)), 8, l, gw.get("w_in"))
    dx, sg["norm_mix_pre"] = _rms_bwd(f"norm_mix_pre_bwd_l{l}", sv["x"], sm["norm_mix_pre"], dh, dx1, F32)
    del nt
    return dx, gw, sg


def _pack(parts):
    return jnp.concatenate([a.reshape(-1, LANES) for a in parts], axis=0)


def _step(x, p, loss_target, w, m, v):
    x = x[0]
    target = loss_target[0]
    depth = w["w_in"].shape[0]

    gathered = _all_gather_weights([w[k].astype(MXU_DTYPE) for k in BIG])
    wg = dict(zip(BIG, gathered))
    lb_f = _lower_bounds("lower_bounds_fwd", w["lb_gamma_fwd"])
    lb_b = _lower_bounds("lower_bounds_bwd", w["lb_gamma_bwd"])

    def small_of(l):
        sm = {k: w[k][l:l + 1] for k in ("norm_mix_pre", "hg_norm", "sg_ln_g", "sg_ln_b", "norm_mix_post",
                                        "norm_ffn_pre", "norm_ffn_post")}
        sm["lb_fwd"], sm["lb_bwd"] = lb_f[l:l + 1], lb_b[l:l + 1]
        sm["sg_w"] = w["sg_w"][l]
        sm["sg_bias_t"] = jnp.repeat(w["sg_b"][l].T, SG_WIDTH // N_HEADS, axis=1)
        return sm

    saved = []
    h = x
    for l in range(depth):
        h, sv = _layer_fwd(l, h, p[l, 0], wg, small_of(l))
        saved.append(sv)

    dy, sq_err = _loss_bwd("loss", h, target)
    gw = {}
    small_grads = [None] * depth
    for l in reversed(range(depth)):
        dy, gw, small_grads[l] = _layer_bwd(l, dy, saved[l], wg, small_of(l), gw)

    def stack(key):
        return jnp.concatenate([small_grads[l][key].reshape((1,) + w_shape[1:]) for l in range(depth)], axis=0)

    g_small = {}
    for key in SMALL:
        w_shape = w[key].shape
        if key == "lb_gamma_fwd":
            dlb = jnp.concatenate([small_grads[l]["lb_fwd"] for l in range(depth)], axis=0)
            g_small[key] = _lower_bounds_bwd("lower_bounds_fwd_bwd", w[key], dlb)
        elif key == "lb_gamma_bwd":
            dlb = jnp.concatenate([small_grads[l]["lb_bwd"] for l in range(depth)], axis=0)
            g_small[key] = _lower_bounds_bwd("lower_bounds_bwd_bwd", w[key], dlb)
        else:
            g_small[key] = stack(key)

    packed = _pack([g_small[k] for k in SMALL] + [sq_err])
    summed = _all_reduce_small(packed)
    n_small_rows = sum(w[k].size for k in SMALL) // LANES
    loss = 0.5 * jnp.sum(summed[n_small_rows:]) / D_MODEL

    g_rows = summed[:n_small_rows]
    d_rows, m_rows, v_rows = _adam_small("adamw_small", g_rows, _pack([w[k] for k in SMALL]),
                                         _pack([m[k] for k in SMALL]), _pack([v[k] for k in SMALL]))
    out = {}
    off = 0
    for key in SMALL:
        n_rows = w[key].size // LANES
        sl = slice(off, off + n_rows)
        out[key] = tuple(a[sl].reshape(w[key].shape) for a in (g_rows, d_rows, m_rows, v_rows))
        off += n_rows

    parts = _exchange_grads([gw[k] for k in BIG])
    for key, part in zip(BIG, parts):
        out[key] = tuple(_adam_sharded(f"adamw_{key}", part, w[key], m[key], v[key]))
    return loss, dy[None], out


WEIGHTS = ("norm_mix_pre", "w_in", "lb_gamma_fwd", "lb_gamma_bwd", "hg_norm", "sg_w", "sg_b", "sg_ln_g", "sg_ln_b",
           "w_a", "w_b", "w_out", "norm_mix_post", "norm_ffn_pre", "w_gate", "w_up", "w_down", "norm_ffn_post",
           "w_ple", "w_ple_gate")


def kernel(x, p, norm_mix_pre, w_in, lb_gamma_fwd, lb_gamma_bwd, hg_norm, sg_w, sg_b, sg_ln_g, sg_ln_b, w_a, w_b, w_out, norm_mix_post, norm_ffn_pre, w_gate, w_up, w_down, norm_ffn_post, w_ple, w_ple_gate, loss_target, m_norm_mix_pre, m_w_in, m_lb_gamma_fwd, m_lb_gamma_bwd, m_hg_norm, m_sg_w, m_sg_b, m_sg_ln_g, m_sg_ln_b, m_w_a, m_w_b, m_w_out, m_norm_mix_post, m_norm_ffn_pre, m_w_gate, m_w_up, m_w_down, m_norm_ffn_post, m_w_ple, m_w_ple_gate, v_norm_mix_pre, v_w_in, v_lb_gamma_fwd, v_lb_gamma_bwd, v_hg_norm, v_sg_w, v_sg_b, v_sg_ln_g, v_sg_ln_b, v_w_a, v_w_b, v_w_out, v_norm_mix_post, v_norm_ffn_pre, v_w_gate, v_w_up, v_w_down, v_norm_ffn_post, v_w_ple, v_w_ple_gate):
    w = dict(zip(WEIGHTS, (norm_mix_pre, w_in, lb_gamma_fwd, lb_gamma_bwd, hg_norm, sg_w, sg_b, sg_ln_g, sg_ln_b, w_a, w_b, w_out, norm_mix_post, norm_ffn_pre, w_gate, w_up, w_down, norm_ffn_post, w_ple, w_ple_gate)))
    m = dict(zip(WEIGHTS, (m_norm_mix_pre, m_w_in, m_lb_gamma_fwd, m_lb_gamma_bwd, m_hg_norm, m_sg_w, m_sg_b, m_sg_ln_g, m_sg_ln_b, m_w_a, m_w_b, m_w_out, m_norm_mix_post, m_norm_ffn_pre, m_w_gate, m_w_up, m_w_down, m_norm_ffn_post, m_w_ple, m_w_ple_gate)))
    v = dict(zip(WEIGHTS, (v_norm_mix_pre, v_w_in, v_lb_gamma_fwd, v_lb_gamma_bwd, v_hg_norm, v_sg_w, v_sg_b, v_sg_ln_g, v_sg_ln_b, v_w_a, v_w_b, v_w_out, v_norm_mix_post, v_norm_ffn_pre, v_w_gate, v_w_up, v_w_down, v_norm_ffn_post, v_w_ple, v_w_ple_gate)))
    loss, grad_x, out = _step(x, p, loss_target, w, m, v)
    res = [loss, grad_x]
    for i in range(4):
        res += [out[k][i] for k in WEIGHTS]
    return tuple(res)
```

```python
import functools

import jax
import jax.numpy as jnp
from jax import lax
from jax.experimental import pallas as pl
from jax.experimental.pallas import tpu as pltpu

F32 = jnp.float32
MXU_DTYPE = jnp.bfloat16
GRAD_EXCHANGE_DTYPE = jnp.bfloat16

D_MODEL = 1024
N_HEADS = 8
HEAD = 128
HG_CHUNK = 64
HG_SUB = 16
HG_BLOCK = 512
HG_SAFE_EXP = 80.0
SG_CHUNK = 128
SG_WIDTH = 512
FFN_SHARD = 704
PLE_DIM = 256
N_IN = 8192
N_CHIPS = 4
N_DEV = 8
EPS = 1e-6
LANES = 128
VMEM_LIMIT_BYTES = 56 * 2 ** 20

ADAM_LR = 0.001
ADAM_B1 = 0.9
ADAM_B2 = 0.999
ADAM_EPS = 1e-08
ADAM_WD = 0.01
ADAM_STEP = 10
ADAM_TILE_ELEMS = 128 * 1024

MESH = pl.DeviceIdType.MESH
ANY = pl.BlockSpec(memory_space=pl.ANY)


def _pcall(body, **kw):
    return pl.pallas_call(body, **kw)


def _params(n_axes):
    return pltpu.CompilerParams(dimension_semantics=("arbitrary",) * n_axes, vmem_limit_bytes=VMEM_LIMIT_BYTES)


def _dot(a, b, ca, cb):
    return lax.dot_general(a.astype(MXU_DTYPE), b.astype(MXU_DTYPE), (((ca,), (cb,)), ((), ())),
                           preferred_element_type=F32)


def _dot_f32(a, b, ca, cb):
    return lax.dot_general(a, b, (((ca,), (cb,)), ((), ())), precision=lax.Precision.HIGH,
                           preferred_element_type=F32)


def _nn(a, b):
    return _dot(a, b, 1, 0)


def _nt(a, b):
    return _dot(a, b, 1, 1)


def _tn(a, b):
    return _dot(a, b, 0, 0)


NN, NT, TN = (1, 0), (1, 1), (0, 0)


def _sigmoid(x):
    return jax.nn.sigmoid(x)


def _dsilu(x, s):
    return s * (1.0 + x * (1.0 - s))


_SQRT_HALF = 0.7071067811865476
_INV_SQRT_2PI = 0.3989422804014327


def _gelu(x):
    return 0.5 * x * (1.0 + lax.erf(x * _SQRT_HALF))


def _dgelu(x):
    return 0.5 * (1.0 + lax.erf(x * _SQRT_HALF)) + x * jnp.exp(-0.5 * x * x) * _INV_SQRT_2PI


def _mean_last(x):
    return jnp.mean(x, axis=-1, keepdims=True)


def _rowsum(x):
    return jnp.sum(x, axis=0, keepdims=True)


def _mm(name, pairs, kind, out_shape, grid, in_specs, out_spec, *, reduce_axis=None, add=None,
        add_spec=None, into=None, prep=None):
    n_pairs = len(pairs)
    has_add = add is not None
    has_into = into is not None
    staged = reduce_axis is not None and out_shape.dtype != F32
    n_args = 2 * n_pairs + has_add + has_into

    def body(*refs):
        o_ref = refs[n_args]
        acc = None
        for i in range(n_pairs):
            a = refs[2 * i][...]
            b = refs[2 * i + 1][...]
            if prep is not None:
                b = prep(b)
            prod = _dot(a, b, *kind)
            acc = prod if acc is None else acc + prod
        if has_add:
            acc = acc + refs[2 * n_pairs][...]
        if reduce_axis is None:
            o_ref[...] = acc.astype(o_ref.dtype)
        else:
            r = pl.program_id(reduce_axis)
            acc_ref = refs[n_args + 1] if staged else o_ref

            @pl.when(r == 0)
            def _():
                acc_ref[...] = acc

            @pl.when(r > 0)
            def _():
                acc_ref[...] += acc

            if staged:
                @pl.when(r == grid[reduce_axis] - 1)
                def _():
                    o_ref[...] = acc_ref[...].astype(o_ref.dtype)

    scratch = []
    if staged:
        scratch = [pltpu.VMEM(tuple(d for d in out_spec.block_shape if d is not None), F32)]
    args = [t for pair in pairs for t in pair]
    specs = list(in_specs)
    if has_add:
        args.append(add)
        specs.append(add_spec)
    aliases = {}
    if has_into:
        aliases = {len(args): 0}
        args.append(into)
        specs.append(ANY)
    return _pcall(body, name=name, grid=grid, in_specs=specs, out_specs=out_spec, out_shape=out_shape,
                  scratch_shapes=scratch, input_output_aliases=aliases, compiler_params=_params(len(grid)))(*args)


def _sds(shape, dtype):
    return jax.ShapeDtypeStruct(tuple(shape), dtype)


def _bs(shape, fn):
    return pl.BlockSpec(tuple(shape), fn)


TM = 1024
TM_WIDE = 2048


def _merge_lead(b):
    return b.reshape(b.shape[0] * b.shape[1], b.shape[2])


def _proj_in(h, w_in_g, l):
    t = h.shape[0]
    return _mm(f"proj_in_l{l}", [(h, w_in_g)], NN, _sds((t, N_IN), F32), (8, t // TM_WIDE),
               [_bs((TM_WIDE, D_MODEL), lambda n, m: (m, 0)),
                _bs((None, None, D_MODEL, 1024), lambda n, m: (n // 2, l, 0, n % 2))],
               _bs((TM_WIDE, 1024), lambda n, m: (m, n)))


def _proj_rows_sharded(name, a, w_g, l, out_dtype, add=None):
    t = a.shape[0]
    return _mm(name, [(a, w_g)], NN, _sds((t, D_MODEL), out_dtype), (t // TM,),
               [_bs((TM, D_MODEL), lambda m: (m, 0)),
                _bs((N_CHIPS, None, 256, D_MODEL), lambda m: (0, l, 0, 0))],
               _bs((TM, D_MODEL), lambda m: (m, 0)), prep=_merge_lead, add=add,
               add_spec=_bs((TM, D_MODEL), lambda m: (m, 0)))


def _proj_rows_sharded_t(name, g, w_g, l, out_dtype, add=None):
    t = g.shape[0]
    return _mm(name, [(g, w_g)], NT, _sds((t, D_MODEL), out_dtype), (t // TM,),
               [_bs((TM, D_MODEL), lambda m: (m, 0)),
                _bs((N_CHIPS, None, 256, D_MODEL), lambda m: (0, l, 0, 0))],
               _bs((TM, D_MODEL), lambda m: (m, 0)), prep=_merge_lead, add=add,
               add_spec=_bs((TM, D_MODEL), lambda m: (m, 0)))


def _proj_cols256(name, a, w_g, l):
    t, k = a.shape
    return _mm(name, [(a, w_g)], NN, _sds((t, D_MODEL), F32), (N_CHIPS, t // TM),
               [_bs((TM, k), lambda j, m: (m, 0)),
                _bs((None, None, k, 256), lambda j, m: (j, l, 0, 0))],
               _bs((TM, 256), lambda j, m: (m, j)))


def _proj_cols256_t(name, g, w_g, l):
    t = g.shape[0]
    k = w_g.shape[2]
    return _mm(name, [(g, w_g)], NT, _sds((t, k), F32), (t // TM, N_CHIPS),
               [_bs((TM, 256), lambda m, j: (m, j)),
                _bs((None, None, k, 256), lambda m, j: (j, l, 0, 0))],
               _bs((TM, k), lambda m, j: (m, 0)), reduce_axis=1)


def _proj_gate_up(name, h2, wg_g, wu_g, l):
    t = h2.shape[0]

    def body(h_ref, wg_ref, wu_ref, gt_ref, up_ref, act_ref):
        h = h_ref[...]
        g = _nn(h, wg_ref[...])
        u = _nn(h, wu_ref[...])
        gt_ref[...] = g
        up_ref[...] = u
        act_ref[...] = ((g * _sigmoid(g)) * u).astype(act_ref.dtype)

    w_spec = _bs((None, None, D_MODEL, FFN_SHARD), lambda j, m: (j, l, 0, 0))
    o_spec = _bs((None, TM, FFN_SHARD), lambda j, m: (j, m, 0))
    return _pcall(body, name=name, grid=(N_CHIPS, t // TM),
                  in_specs=[_bs((TM, D_MODEL), lambda j, m: (m, 0)), w_spec, w_spec],
                  out_specs=[o_spec, o_spec, o_spec],
                  out_shape=[_sds((N_CHIPS, t, FFN_SHARD), F32), _sds((N_CHIPS, t, FFN_SHARD), F32),
                             _sds((N_CHIPS, t, FFN_SHARD), MXU_DTYPE)],
                  compiler_params=_params(2))(h2, wg_g, wu_g)


def _proj_ffn_in_t(name, pairs, l):
    t = pairs[0][0].shape[1]
    specs = []
    for _ in pairs:
        specs += [_bs((None, TM, FFN_SHARD), lambda m, j: (j, m, 0)),
                  _bs((None, None, D_MODEL, FFN_SHARD), lambda m, j: (j, l, 0, 0))]
    return _mm(name, pairs, NT, _sds((t, D_MODEL), F32), (t // TM, N_CHIPS), specs,
               _bs((TM, D_MODEL), lambda m, j: (m, 0)), reduce_axis=1)


def _proj_ffn_out(name, act, w_g, l):
    t = act.shape[1]
    return _mm(name, [(act, w_g)], NN, _sds((t, D_MODEL), F32), (t // TM, N_CHIPS),
               [_bs((None, TM, FFN_SHARD), lambda m, j: (j, m, 0)),
                _bs((None, None, FFN_SHARD, D_MODEL), lambda m, j: (j, l, 0, 0))],
               _bs((TM, D_MODEL), lambda m, j: (m, 0)), reduce_axis=1)


def _proj_down_t_swiglu_bwd(name, dff, w_g, gt, up, l):
    t = dff.shape[0]

    def body(d_ref, w_ref, gt_ref, up_ref, dgt_ref, dup_ref):
        dact = _nt(d_ref[...], w_ref[...])
        g = gt_ref[...]
        s = _sigmoid(g)
        dgt_ref[...] = (dact * up_ref[...] * _dsilu(g, s)).astype(dgt_ref.dtype)
        dup_ref[...] = (dact * (g * s)).astype(dup_ref.dtype)

    o_spec = _bs((None, TM, FFN_SHARD), lambda j, m: (j, m, 0))
    return _pcall(body, name=name, grid=(N_CHIPS, t // TM),
                  in_specs=[_bs((TM, D_MODEL), lambda j, m: (m, 0)),
                            _bs((None, None, FFN_SHARD, D_MODEL), lambda j, m: (j, l, 0, 0)), o_spec, o_spec],
                  out_specs=[o_spec, o_spec],
                  out_shape=[_sds((N_CHIPS, t, FFN_SHARD), MXU_DTYPE)] * 2,
                  compiler_params=_params(2))(dff, w_g, gt, up)


def _proj_in_t(dz, w_in_g, l):
    t = dz.shape[0]
    return _mm(f"proj_in_t_l{l}", [(dz, w_in_g)], NT, _sds((t, D_MODEL), F32), (t // TM_WIDE, 8),
               [_bs((TM_WIDE, 1024), lambda m, n: (m, n)),
                _bs((None, None, D_MODEL, 1024), lambda m, n: (n // 2, l, 0, n % 2))],
               _bs((TM_WIDE, D_MODEL), lambda m, n: (m, 0)), reduce_axis=1)


TT = 1024


def _wgrad(name, a, a_spec, g, g_spec, shard_shape, o_map, n_outer, l, into, tt=TT):
    t = a.shape[-2]
    out = _sds((N_CHIPS, 2) + tuple(shard_shape), GRAD_EXCHANGE_DTYPE)
    return _mm(name, [(a, g)], TN, out, (n_outer, t // tt), [a_spec, g_spec],
               _bs((None, None) + tuple(o_map[0]), o_map[1]), reduce_axis=1, into=into)


def _rows(name, fn, n_tiles, ins, in_specs, out_shapes, out_specs, n_red=0, into=()):
    n_in = len(ins)
    n_out = len(out_shapes)

    def body(*refs):
        in_refs = refs[:n_in]
        out_refs = refs[len(refs) - n_out:]
        vals = fn(*[r[...] for r in in_refs])
        if not isinstance(vals, (tuple, list)):
            vals = (vals,)
        first = pl.program_id(0) == 0
        for j in range(n_out):
            o_ref = out_refs[j]
            val = vals[j]
            if j < n_out - n_red:
                o_ref[...] = val.astype(o_ref.dtype)
            else:
                @pl.when(first)
                def _(o_ref=o_ref, val=val):
                    o_ref[...] = val

                @pl.when(jnp.logical_not(first))
                def _(o_ref=o_ref, val=val):
                    o_ref[...] += val

    args = list(ins)
    specs = list(in_specs)
    aliases = {}
    for buf, out_idx in into:
        aliases[len(args)] = out_idx
        args.append(buf)
        specs.append(ANY)
    res = _pcall(body, name=name, grid=(n_tiles,), in_specs=specs, out_specs=list(out_specs),
                 out_shape=list(out_shapes), input_output_aliases=aliases, compiler_params=_params(1))(*args)
    return res


def _tile_spec(tile, width, blk=0):
    return pl.BlockSpec((tile, width), lambda i: (i, blk))


def _whole(shape):
    nd = len(shape)
    return pl.BlockSpec(tuple(shape), lambda i: (0,) * nd)


def _rms(x, g):
    r = lax.rsqrt(_mean_last(x * x) + EPS)
    return (x * r) * g


def _rms_bwd_math(u, g, dy):
    r = lax.rsqrt(_mean_last(u * u) + EPS)
    uh = u * r
    gdy = dy * g
    du = r * (gdy - uh * _mean_last(gdy * uh))
    return du, _rowsum(dy * uh)


def _rms_fwd(name, x, g):
    t, d = x.shape
    tile = 512
    return _rows(name, lambda xv, gv: (_rms(xv, gv),), t // tile, [x, g],
                 [_tile_spec(tile, d), _whole((1, d))], [_sds((t, d), MXU_DTYPE)], [_tile_spec(tile, d)])[0]


def _resid_rms_fwd(name, x, y, g):
    t, d = x.shape
    tile = 512
    return _rows(name, lambda xv, yv, gv: (xv + _rms(yv, gv),), t // tile, [x, y, g],
                 [_tile_spec(tile, d), _tile_spec(tile, d), _whole((1, d))], [_sds((t, d), F32)],
                 [_tile_spec(tile, d)])[0]


def _resid_rms_norm_fwd(name, x, y, g, g_next):
    t, d = x.shape
    tile = 512

    def fn(xv, yv, gv, gn):
        s = xv + _rms(yv, gv)
        return s, _rms(s, gn)

    return _rows(name, fn, t // tile, [x, y, g, g_next],
                 [_tile_spec(tile, d), _tile_spec(tile, d), _whole((1, d)), _whole((1, d))],
                 [_sds((t, d), F32), _sds((t, d), MXU_DTYPE)], [_tile_spec(tile, d)] * 2)


TF = 512


def _merge_proj_out(name, pa, pb, z, w_g, l):
    t = z.shape[0]

    def body(pa_ref, pb_ref, ga_ref, gb_ref, w_ref, mg_ref, mix_ref):
        merged = _sigmoid(ga_ref[...]) * pa_ref[...] + _sigmoid(gb_ref[...]) * pb_ref[...]
        mg_ref[...] = merged.astype(mg_ref.dtype)
        mix_ref[...] = _nn(merged, _merge_lead(w_ref[...]))

    row = lambda blk: _bs((TF, D_MODEL), lambda m: (m, blk))
    return _pcall(body, name=name, grid=(t // TF,),
                  in_specs=[row(0), row(0), row(6), row(7), _bs((N_CHIPS, None, 256, D_MODEL), lambda m: (0, l, 0, 0))],
                  out_specs=[row(0), row(0)],
                  out_shape=[_sds((t, D_MODEL), MXU_DTYPE), _sds((t, D_MODEL), F32)],
                  compiler_params=_params(1))(pa, pb, z, z, w_g)


def _proj_ple_gate_ple(name, x2, w_g, pe, l):
    t = x2.shape[0]

    def body(x_ref, w_ref, pe_ref, gz_ref, x3_ref):
        x = x_ref[...]
        gz = _nn(x, _merge_lead(w_ref[...]))
        gz_ref[...] = gz
        x3_ref[...] = x + pe_ref[...] * _sigmoid(gz)

    row = _bs((TF, D_MODEL), lambda m: (m, 0))
    return _pcall(body, name=name, grid=(t // TF,),
                  in_specs=[row, _bs((N_CHIPS, None, 256, D_MODEL), lambda m: (0, l, 0, 0)), row],
                  out_specs=[row, row], out_shape=[_sds((t, D_MODEL), F32)] * 2,
                  compiler_params=_params(1))(x2, w_g, pe)


def _rms_bwd(name, u, g, dy, resid, out_dtype):
    t, d = u.shape
    tile = 256

    def fn(uv, gv, dyv, *rest):
        du, dg = _rms_bwd_math(uv, gv, dyv)
        if rest:
            du = du + rest[0]
        return du, dg

    ins = [u, g, dy] + ([resid] if resid is not None else [])
    specs = [_tile_spec(tile, d), _whole((1, d)), _tile_spec(tile, d)] + ([_tile_spec(tile, d)] if resid is not None else [])
    return _rows(name, fn, t // tile, ins, specs, [_sds((t, d), out_dtype), _sds((1, d), F32)],
                 [_tile_spec(tile, d), _whole((1, d))], n_red=1)


def _cumsum_rows(x, group, suffix):
    n = x.shape[0]
    pos = lax.broadcasted_iota(jnp.int32, x.shape, 0) % group
    d = 1
    while d < group:
        if suffix:
            x = x + jnp.where(pos < group - d, pltpu.roll(x, n - d, 0), 0.0)
        else:
            x = x + jnp.where(pos >= d, pltpu.roll(x, d, 0), 0.0)
        d *= 2
    return x


def _hg_gates(zq, zf, lb):
    q = zq * _sigmoid(zq)
    f = lb + (1.0 - lb) * _sigmoid(zf)
    logf = jnp.log(jnp.maximum(f, jnp.finfo(F32).tiny))
    k = (1.0 - lb) * _sigmoid(-zf)
    return q, k, logf, f


def _tri_mask(n, rev):
    t_i = lax.broadcasted_iota(jnp.int32, (n, n), 0)
    s_i = lax.broadcasted_iota(jnp.int32, (n, n), 1)
    return (s_i >= t_i) if rev else (s_i <= t_i)


def _anchors_are_safe(b_s, n_chunks, rev):
    worst = None
    for c in range(n_chunks):
        base = c * HG_CHUNK
        first = base + HG_CHUNK - 1 if rev else base
        last = base if rev else base + HG_CHUNK - 1
        mid = base + HG_CHUNK // 2
        b0, bm, bl = b_s[first:first + 1, :], b_s[mid:mid + 1, :], b_s[last:last + 1, :]
        span = jnp.maximum(b0 - bm, bm - bl)
        worst = span if worst is None else jnp.maximum(worst, span)
    return jnp.max(worst) < HG_SAFE_EXP


def _sub_ranges(base, i_sub, rev):
    r0 = base + i_sub * HG_SUB
    r1 = r0 + HG_SUB
    if rev:
        e0, e1, anchor = r1, base + HG_CHUNK, r1
    else:
        e0, e1, anchor = base, r0, r0 - 1
    return r0, r1, e0, e1, anchor


def _hgrn_fwd(name, z, lb_row, rev):
    t = z.shape[0]
    nb = t // HG_BLOCK
    ncb = HG_BLOCK // HG_CHUNK
    nsb = HG_CHUNK // HG_SUB
    zf0 = 16 if rev else 8

    def tmap(i):
        return nb - 1 - i if rev else i

    def body(zq_ref, zf_ref, zi_ref, lb_ref, o_ref, st_ref, state, q_s, k_s, v_s, b_s):
        @pl.when(pl.program_id(1) == 0)
        def _():
            state[...] = jnp.zeros_like(state)

        q, k, logf, _ = _hg_gates(zq_ref[...], zf_ref[...], lb_ref[...])
        v = zi_ref[...]
        b_all = _cumsum_rows(logf, HG_CHUNK, rev)
        b_s[...] = b_all
        order = range(ncb - 1, -1, -1) if rev else range(ncb)
        safe = _anchors_are_safe(b_s, ncb, rev)

        @pl.when(safe)
        def _():
            cmask = _tri_mask(HG_CHUNK, rev)
            st = state[...]
            outs = [None] * ncb
            for c in order:
                base = c * HG_CHUNK
                rows = slice(base, base + HG_CHUNK)
                last = base if rev else base + HG_CHUNK - 1
                mid = base + HG_CHUNK // 2
                q_c, k_c, v_c, b = q[rows, :], k[rows, :], v[rows, :], b_all[rows, :]
                bl, bm = b_s[last:last + 1, :], b_s[mid:mid + 1, :]
                st_ref[c] = st
                a = jnp.where(cmask, _nt(q_c * jnp.exp(b - bm), k_c * jnp.exp(bm - b)), 0.0)
                outs[c] = _nt(q_c * jnp.exp(b), st) + _nn(a, v_c)
                st = st * jnp.exp(bl) + _tn(v_c, k_c * jnp.exp(bl - b))
            state[...] = st
            o_ref[...] = jnp.concatenate(outs, axis=0)

        @pl.when(jnp.logical_not(safe))
        def _():
            q_s[...] = q
            k_s[...] = k
            v_s[...] = v
            mask = _tri_mask(HG_SUB, rev)
            for c in order:
                base = c * HG_CHUNK
                rows = slice(base, base + HG_CHUNK)
                last = base if rev else base + HG_CHUNK - 1
                st = state[...]
                st_ref[c] = st
                b = b_s[rows, :]
                bl = b_s[last:last + 1, :]
                o_inter = _nt(q_s[rows, :] * jnp.exp(b), st)
                kd = k_s[rows, :] * jnp.exp(bl - b)
                state[...] = st * jnp.exp(bl) + _tn(v_s[rows, :], kd)
                parts = []
                for i_sub in range(nsb):
                    r0, r1, e0, e1, anchor = _sub_ranges(base, i_sub, rev)
                    q_i, k_i, b_i = q_s[r0:r1, :], k_s[r0:r1, :], b_s[r0:r1, :]
                    decay = jnp.exp(jnp.minimum(b_i[:, None, :] - b_i[None, :, :], 0.0))
                    a_d = jnp.where(mask, jnp.sum(q_i[:, None, :] * k_i[None, :, :] * decay, axis=-1), 0.0)
                    o_i = _nn(a_d, v_s[r0:r1, :])
                    if e1 > e0:
                        anc = b_s[anchor:anchor + 1, :]
                        q_t = q_i * jnp.exp(b_i - anc)
                        k_t = k_s[e0:e1, :] * jnp.exp(anc - b_s[e0:e1, :])
                        o_i = o_i + _nn(_nt(q_t, k_t), v_s[e0:e1, :])
                    parts.append(o_i)
                o_ref[rows, :] = o_inter + jnp.concatenate(parts, axis=0)

    blk = lambda off: pl.BlockSpec((HG_BLOCK, HEAD), lambda h, i: (tmap(i), off + h))
    return _pcall(
        body, name=name, grid=(N_HEADS, nb),
        in_specs=[blk(0), blk(zf0), blk(24), pl.BlockSpec((1, HEAD), lambda h, i: (0, h))],
        out_specs=[blk(0), pl.BlockSpec((None, ncb, HEAD, HEAD), lambda h, i: (h, tmap(i), 0, 0))],
        out_shape=[_sds((t, D_MODEL), F32), _sds((N_HEADS, t // HG_CHUNK, HEAD, HEAD), F32)],
        scratch_shapes=[pltpu.VMEM((HEAD, HEAD), F32)] + [pltpu.VMEM((HG_BLOCK, HEAD), F32)] * 4,
        compiler_params=_params(2))(z, z, z, lb_row)


def _hgrn_bwd(name, z, d_o, states, lb_row, rev, dz):
    t = z.shape[0]
    nb = t // HG_BLOCK
    ncb = HG_BLOCK // HG_CHUNK
    nsb = HG_CHUNK // HG_SUB
    zf0 = 16 if rev else 8

    def tmap(i):
        return i if rev else nb - 1 - i

    def body(zq_ref, zf_ref, zi_ref, do_ref, st_ref, lb_ref, dz_in, dq_ref, dv_ref, dzf_ref, dlb_ref,
             dstate, q_s, k_s, v_s, b_s, dq_s, dk_s, dv_s, db_s):
        del dz_in
        first = pl.program_id(1) == 0

        @pl.when(first)
        def _():
            dstate[...] = jnp.zeros_like(dstate)

        lb = lb_ref[...]
        zf = zf_ref[...]
        q, k, logf, f = _hg_gates(zq_ref[...], zf, lb)
        v = zi_ref[...]
        b_all = _cumsum_rows(logf, HG_CHUNK, rev)
        b_s[...] = b_all
        order = range(ncb) if rev else range(ncb - 1, -1, -1)
        safe = _anchors_are_safe(b_s, ncb, rev)

        @pl.when(safe)
        def _():
            cmask = _tri_mask(HG_CHUNK, rev)
            row_i = lax.broadcasted_iota(jnp.int32, (HG_CHUNK, HEAD), 0)
            dst = dstate[...]
            dq_l, dk_l, dv_l, db_l = [None] * ncb, [None] * ncb, [None] * ncb, [None] * ncb
            for c in order:
                base = c * HG_CHUNK
                rows = slice(base, base + HG_CHUNK)
                last = base if rev else base + HG_CHUNK - 1
                mid = base + HG_CHUNK // 2
                q_c, k_c, v_c, b, do_c = q[rows, :], k[rows, :], v[rows, :], b_all[rows, :], do_ref[rows, :]
                bl, bm = b_s[last:last + 1, :], b_s[mid:mid + 1, :]
                st0 = st_ref[c]
                e, el, ebl = jnp.exp(b), jnp.exp(bl), jnp.exp(bl - b)
                e_q, e_k = jnp.exp(b - bm), jnp.exp(bm - b)
                q_t, k_t = q_c * e_q, k_c * e_k
                a = jnp.where(cmask, _nt(q_t, k_t), 0.0)
                da = jnp.where(cmask, _nt(do_c, v_c), 0.0)
                dq = _nn(do_c, st0) * e + e_q * _dot_f32(da, k_t, 1, 0)
                dk_inter = _nn(v_c, dst) * ebl
                dk = dk_inter + e_k * _dot_f32(da, q_t, 0, 0)
                dv_l[c] = _nt(k_c * ebl, dst) + _tn(a, do_c)
                extra = el * _rowsum(dst * st0) + _rowsum(k_c * dk_inter)
                dst = _tn(do_c, q_c * e) + dst * el
                db_l[c] = q_c * dq - k_c * dk + jnp.where(row_i == last - base, extra, 0.0)
                dq_l[c], dk_l[c] = dq, dk
            dstate[...] = dst
            dq_s[...] = jnp.concatenate(dq_l, axis=0)
            dk_s[...] = jnp.concatenate(dk_l, axis=0)
            dv_s[...] = jnp.concatenate(dv_l, axis=0)
            db_s[...] = jnp.concatenate(db_l, axis=0)

        @pl.when(jnp.logical_not(safe))
        def _():
            q_s[...] = q
            k_s[...] = k
            v_s[...] = v
            mask = _tri_mask(HG_SUB, rev)
            for c in order:
                base = c * HG_CHUNK
                rows = slice(base, base + HG_CHUNK)
                last = base if rev else base + HG_CHUNK - 1
                st0 = st_ref[c]
                dst1 = dstate[...]
                b = b_s[rows, :]
                bl = b_s[last:last + 1, :]
                e = jnp.exp(b)
                el = jnp.exp(bl)
                ebl = jnp.exp(bl - b)
                q_c, k_c, v_c, do_c = q_s[rows, :], k_s[rows, :], v_s[rows, :], do_ref[rows, :]
                kd = k_c * ebl
                dq_s[rows, :] = _nn(do_c, st0) * e
                dk_inter = _nn(v_c, dst1) * ebl
                dk_s[rows, :] = dk_inter
                dv_s[rows, :] = _nt(kd, dst1)
                extra = el * _rowsum(dst1 * st0) + _rowsum(k_c * dk_inter)
                dstate[...] = _tn(do_c, q_c * e) + dst1 * el
                for i_sub in range(nsb):
                    r0, r1, e0, e1, anchor = _sub_ranges(base, i_sub, rev)
                    q_i, k_i, b_i, v_i, do_i = q_s[r0:r1, :], k_s[r0:r1, :], b_s[r0:r1, :], v_s[r0:r1, :], do_ref[r0:r1, :]
                    decay = jnp.exp(jnp.minimum(b_i[:, None, :] - b_i[None, :, :], 0.0))
                    a_d = jnp.where(mask, jnp.sum(q_i[:, None, :] * k_i[None, :, :] * decay, axis=-1), 0.0)
                    da_d = jnp.where(mask, _nt(do_i, v_i), 0.0)
                    wgt = da_d[:, :, None] * decay
                    dq_s[r0:r1, :] += jnp.sum(wgt * k_i[None, :, :], axis=1)
                    dk_s[r0:r1, :] += jnp.sum(wgt * q_i[:, None, :], axis=0)
                    dv_s[r0:r1, :] += _tn(a_d, do_i)
                    if e1 > e0:
                        anc = b_s[anchor:anchor + 1, :]
                        e_q = jnp.exp(b_i - anc)
                        e_k = jnp.exp(anc - b_s[e0:e1, :])
                        q_t = q_i * e_q
                        k_t = k_s[e0:e1, :] * e_k
                        a_o = _nt(q_t, k_t)
                        da_o = _nt(do_i, v_s[e0:e1, :])
                        dq_s[r0:r1, :] += e_q * _nn(da_o, k_t)
                        dk_s[e0:e1, :] += e_k * _tn(da_o, q_t)
                        dv_s[e0:e1, :] += _tn(a_o, do_i)
                db_s[rows, :] = q_c * dq_s[rows, :] - k_c * dk_s[rows, :]
                db_s[last:last + 1, :] += extra

        dlogf = _cumsum_rows(db_s[...], HG_CHUNK, not rev)
        s_neg = _sigmoid(-zf)
        df = jnp.where(f > jnp.finfo(F32).tiny, dlogf / f, 0.0)
        dfk = df - dk_s[...]
        dzf_ref[...] = ((1.0 - lb) * _sigmoid(zf) * s_neg * dfk).astype(dzf_ref.dtype)
        dlb = _rowsum(s_neg * dfk)

        @pl.when(first)
        def _():
            dlb_ref[...] = dlb

        @pl.when(jnp.logical_not(first))
        def _():
            dlb_ref[...] += dlb

        dq_ref[...] = dq_s[...]
        dv_ref[...] = dv_s[...]

    blk = lambda off: pl.BlockSpec((HG_BLOCK, HEAD), lambda h, i: (tmap(i), off + h))
    vec = pl.BlockSpec((1, HEAD), lambda h, i: (0, h))
    return _pcall(
        body, name=name, grid=(N_HEADS, nb),
        in_specs=[blk(0), blk(zf0), blk(24), blk(0),
                  pl.BlockSpec((None, ncb, HEAD, HEAD), lambda h, i: (h, tmap(i), 0, 0)), vec, ANY],
        out_specs=[blk(0), blk(0), blk(zf0), vec],
        out_shape=[_sds((t, D_MODEL), F32), _sds((t, D_MODEL), F32), _sds(dz.shape, dz.dtype), _sds((1, D_MODEL), F32)],
        input_output_aliases={6: 2},
        scratch_shapes=[pltpu.VMEM((HEAD, HEAD), F32)] + [pltpu.VMEM((HG_BLOCK, HEAD), F32)] * 8,
        compiler_params=_params(2))(z, z, z, d_o, states, lb_row, dz)


def _lower_bounds(name, gamma):
    def body(g_ref, o_ref):
        g0, g1 = g_ref[0:1, :], g_ref[1:2, :]
        m = jnp.maximum(g0, g1)
        e0, e1 = jnp.exp(g0 - m), jnp.exp(g1 - m)
        s0, s1 = e0 / (e0 + e1), e1 / (e0 + e1)
        o_ref[0:1, :] = s0 - s0
        o_ref[1:2, :] = (s0 + s1) - s0

    return _pcall(body, name=name, out_shape=_sds(gamma.shape, F32))(gamma)


def _lower_bounds_bwd(name, gamma, dlb):
    def body(g_ref, d_ref, o_ref):
        g0, g1 = g_ref[0:1, :], g_ref[1:2, :]
        m = jnp.maximum(g0, g1)
        e0, e1 = jnp.exp(g0 - m), jnp.exp(g1 - m)
        s0, s1 = e0 / (e0 + e1), e1 / (e0 + e1)
        d0, d1 = d_ref[0:1, :], d_ref[1:2, :]
        ds0 = (d0 + d1) - (d0 + d1)
        ds1 = d1
        inner = s0 * ds0 + s1 * ds1
        o_ref[0:1, :] = s0 * (ds0 - inner)
        o_ref[1:2, :] = s1 * (ds1 - inner)

    return _pcall(body, name=name, out_shape=_sds(gamma.shape, F32))(gamma, dlb)


def _heads(x):
    return [x[:, h * HEAD:(h + 1) * HEAD] for h in range(N_HEADS)]


def _hg_post_fwd(name, o_f, o_b, z, gain):
    t = z.shape[0]
    tile = 256

    def fn(of, ob, zg, g):
        outs = []
        for o_h, zg_h, g_h in zip(_heads(of + ob), _heads(zg), _heads(g)):
            outs.append(_rms(o_h, g_h) * (zg_h * _sigmoid(zg_h)))
        return (jnp.concatenate(outs, axis=1),)

    return _rows(name, fn, t // tile, [o_f, o_b, z, gain],
                 [_tile_spec(tile, D_MODEL), _tile_spec(tile, D_MODEL), _tile_spec(tile, D_MODEL, 4), _whole((1, D_MODEL))],
                 [_sds((t, D_MODEL), MXU_DTYPE)], [_tile_spec(tile, D_MODEL)])[0]


def _hg_post_bwd(name, da, o_f, o_b, z, gain, dz):
    t = z.shape[0]
    tile = 256

    def fn(dav, of, ob, zg, g):
        d_o, dzg, dgain = [], [], []
        for da_h, o_h, zg_h, g_h in zip(_heads(dav), _heads(of + ob), _heads(zg), _heads(g)):
            s = _sigmoid(zg_h)
            r = lax.rsqrt(_mean_last(o_h * o_h) + EPS)
            oh = o_h * r
            dy = da_h * (zg_h * s)
            dzg.append(da_h * (oh * g_h) * _dsilu(zg_h, s))
            gdy = dy * g_h
            d_o.append(r * (gdy - oh * _mean_last(gdy * oh)))
            dgain.append(_rowsum(dy * oh))
        return jnp.concatenate(d_o, axis=1), jnp.concatenate(dzg, axis=1), jnp.concatenate(dgain, axis=1)

    return _rows(name, fn, t // tile, [da, o_f, o_b, z, gain],
                 [_tile_spec(tile, D_MODEL)] * 3 + [_tile_spec(tile, D_MODEL, 4), _whole((1, D_MODEL))],
                 [_sds((t, D_MODEL), F32), _sds(dz.shape, dz.dtype), _sds((1, D_MODEL), F32)],
                 [_tile_spec(tile, D_MODEL), _tile_spec(tile, D_MODEL, 4), _whole((1, D_MODEL))],
                 n_red=1, into=[(dz, 1)])


def _hg_gate_bwd(name, dq_f, dq_b, dv_f, dv_b, z, dz):
    t = z.shape[0]
    tile = 512

    def fq(a, b, zq):
        return ((a + b) * _dsilu(zq, _sigmoid(zq)),)

    dz = _rows(name + "_q", fq, t // tile, [dq_f, dq_b, z], [_tile_spec(tile, D_MODEL)] * 3,
               [_sds(dz.shape, dz.dtype)], [_tile_spec(tile, D_MODEL, 0)], into=[(dz, 0)])[0]
    dz = _rows(name + "_v", lambda a, b: (a + b,), t // tile, [dv_f, dv_b], [_tile_spec(tile, D_MODEL)] * 2,
               [_sds(dz.shape, dz.dtype)], [_tile_spec(tile, D_MODEL, 3)], into=[(dz, 0)])[0]
    return dz


def _sg_norm(zv, ln_g, ln_b):
    gv = _gelu(zv)
    xc = gv - _mean_last(gv)
    rstd = lax.rsqrt(_mean_last(xc * xc) + EPS)
    xhat = xc * rstd
    return xhat * ln_g + ln_b, xhat, rstd


def _lane_lo():
    return lax.broadcasted_iota(jnp.int32, (SG_CHUNK, LANES), 1) < (LANES // 2)


SG_TILE = 512


def _sgu_fwd(name, z, w, bias_t, ln_g, ln_b):
    t = z.shape[0]

    def fn(zu, zv, wv, bt, lg, lb):
        u = _gelu(zu)
        vn, _, _ = _sg_norm(zv, lg, lb)
        lo = _lane_lo()
        out_rows = []
        for c in range(SG_TILE // SG_CHUNK):
            rs = slice(c * SG_CHUNK, (c + 1) * SG_CHUNK)
            cols = []
            for j in range(SG_WIDTH // LANES):
                cs = slice(j * LANES, (j + 1) * LANES)
                vb = vn[rs, cs]
                sg = jnp.where(lo, _nn(wv[2 * j], vb), _nn(wv[2 * j + 1], vb)) + bt[:, cs]
                cols.append(u[rs, cs] * sg)
            out_rows.append(jnp.concatenate(cols, axis=1))
        return (jnp.concatenate(out_rows, axis=0),)

    return _rows(name, fn, t // SG_TILE, [z, z, w, bias_t, ln_g, ln_b],
                 [_tile_spec(SG_TILE, SG_WIDTH, 10), _tile_spec(SG_TILE, SG_WIDTH, 11), _whole(w.shape),
                  _whole(bias_t.shape), _whole((1, SG_WIDTH)), _whole((1, SG_WIDTH))],
                 [_sds((t, SG_WIDTH), MXU_DTYPE)], [_tile_spec(SG_TILE, SG_WIDTH)])[0]


def _sgu_bwd(name, dbo, z, w, bias_t, ln_g, ln_b, dz):
    t = z.shape[0]
    n_grp = w.shape[0]

    def fn(dbov, zu, zv, wv, bt, lg, lb):
        u = _gelu(zu)
        vn, xhat, rstd = _sg_norm(zv, lg, lb)
        lo = _lane_lo()
        dw = [None] * n_grp
        dsg_sum = None
        du_rows, dvn_rows = [], []
        for c in range(SG_TILE // SG_CHUNK):
            rs = slice(c * SG_CHUNK, (c + 1) * SG_CHUNK)
            du_cols, dvn_cols, dsg_cols = [], [], []
            for j in range(SG_WIDTH // LANES):
                cs = slice(j * LANES, (j + 1) * LANES)
                vb = vn[rs, cs]
                sg = jnp.where(lo, _nn(wv[2 * j], vb), _nn(wv[2 * j + 1], vb)) + bt[:, cs]
                du_cols.append(dbov[rs, cs] * sg)
                dsg = dbov[rs, cs] * u[rs, cs]
                dsg_cols.append(dsg)
                d0 = _nt(jnp.where(lo, dsg, 0.0), vb)
                d1 = _nt(jnp.where(lo, 0.0, dsg), vb)
                dw[2 * j] = d0 if dw[2 * j] is None else dw[2 * j] + d0
                dw[2 * j + 1] = d1 if dw[2 * j + 1] is None else dw[2 * j + 1] + d1
                dvn_cols.append(jnp.where(lo, _tn(wv[2 * j], dsg), _tn(wv[2 * j + 1], dsg)))
            du_rows.append(jnp.concatenate(du_cols, axis=1))
            dvn_rows.append(jnp.concatenate(dvn_cols, axis=1))
            dsg_c = jnp.concatenate(dsg_cols, axis=1)
            dsg_sum = dsg_c if dsg_sum is None else dsg_sum + dsg_c
        du = jnp.concatenate(du_rows, axis=0)
        dvn = jnp.concatenate(dvn_rows, axis=0)
        dxhat = dvn * lg
        dgv = rstd * (dxhat - _mean_last(dxhat) - xhat * _mean_last(dxhat * xhat))
        dzuv = jnp.concatenate([du * _dgelu(zu), dgv * _dgelu(zv)], axis=1)
        return dzuv, jnp.stack(dw, axis=0), dsg_sum, _rowsum(dvn * xhat), _rowsum(dvn)

    return _rows(name, fn, t // SG_TILE, [dbo, z, z, w, bias_t, ln_g, ln_b],
                 [_tile_spec(SG_TILE, SG_WIDTH), _tile_spec(SG_TILE, SG_WIDTH, 10), _tile_spec(SG_TILE, SG_WIDTH, 11),
                  _whole(w.shape), _whole(bias_t.shape), _whole((1, SG_WIDTH)), _whole((1, SG_WIDTH))],
                 [_sds(dz.shape, dz.dtype), _sds(w.shape, F32), _sds((SG_CHUNK, SG_WIDTH), F32),
                  _sds((1, SG_WIDTH), F32), _sds((1, SG_WIDTH), F32)],
                 [_tile_spec(SG_TILE, 2 * SG_WIDTH, 5), _whole(w.shape), _whole((SG_CHUNK, SG_WIDTH)),
                  _whole((1, SG_WIDTH)), _whole((1, SG_WIDTH))],
                 n_red=4, into=[(dz, 0)])


def _merge_bwd(name, dm, pa, pb, z):
    t = z.shape[0]
    tile = 256

    def fn(d, a, b, ga, gb):
        sa, sb = _sigmoid(ga), _sigmoid(gb)
        dgate = jnp.concatenate([d * a * sa * (1.0 - sa), d * b * sb * (1.0 - sb)], axis=1)
        return d * sa, d * sb, dgate

    return _rows(name, fn, t // tile, [dm, pa, pb, z, z],
                 [_tile_spec(tile, D_MODEL)] * 3 + [_tile_spec(tile, D_MODEL, 6), _tile_spec(tile, D_MODEL, 7)],
                 [_sds((t, D_MODEL), MXU_DTYPE), _sds((t, D_MODEL), MXU_DTYPE), _sds((t, N_IN), MXU_DTYPE)],
                 [_tile_spec(tile, D_MODEL), _tile_spec(tile, D_MODEL), _tile_spec(tile, 2 * D_MODEL, 3)])


def _ple_bwd(name, dx, pe, gz):
    t, d = dx.shape
    tile = 512

    def fn(dv, p, g):
        s = _sigmoid(g)
        return dv * s, dv * p * s * (1.0 - s)

    return _rows(name, fn, t // tile, [dx, pe, gz], [_tile_spec(tile, d)] * 3, [_sds((t, d), MXU_DTYPE)] * 2,
                 [_tile_spec(tile, d)] * 2)


def _loss_bwd(name, y, target):
    t, d = y.shape
    tile = 512

    def fn(yv, tv):
        err = yv - tv
        return err * (1.0 / d), _rowsum(err * err)

    return _rows(name, fn, t // tile, [y, target], [_tile_spec(tile, d)] * 2, [_sds((t, d), F32), _sds((1, d), F32)],
                 [_tile_spec(tile, d), _whole((1, d))], n_red=1)


def _position():
    return lax.axis_index("x"), lax.axis_index("y"), lax.axis_index("c")


def _all_gather_weights(shards):
    n = len(shards)

    def body(*refs):
        w_refs, out_refs = refs[:n], refs[n:2 * n]
        send_sems, recv_sems, local_sems = refs[2 * n:]
        x, y, c = _position()
        me = 2 * x + y
        sibling = (x, y, 1 - c)
        chips = [(1 - x, y), (x, 1 - y), (1 - x, 1 - y)]

        def copy(i, k, src, chip, layer, to):
            return pltpu.make_async_remote_copy(
                src_ref=src, dst_ref=out_refs[i].at[chip, pl.ds(layer, 1)], send_sem=send_sems.at[6 * i + k],
                recv_sem=recv_sems.at[6 * i + k], device_id=to, device_id_type=MESH)

        local = [pltpu.make_async_copy(w_refs[i], out_refs[i].at[me], local_sems.at[i]) for i in range(n)]
        for cp in local:
            cp.start()
        sends = [copy(i, j, w_refs[i].at[pl.ds(c, 1)], me, c, (px, py, c))
                 for i in range(n) for j, (px, py) in enumerate(chips)]
        for cp in sends:
            cp.start()
        passed = []
        for i in range(n):
            for j, (px, py) in enumerate(chips):
                chip = 2 * px + py
                copy(i, j, w_refs[i].at[pl.ds(c, 1)], chip, c, (px, py, c)).wait_recv()
                fwd = copy(i, 3 + j, out_refs[i].at[chip, pl.ds(c, 1)], chip, c, sibling)
                fwd.start()
                passed.append(fwd)
        for i in range(n):
            for j, (px, py) in enumerate(chips):
                copy(i, 3 + j, w_refs[i].at[pl.ds(c, 1)], 2 * px + py, 1 - c, sibling).wait_recv()
        for cp in sends + passed:
            cp.wait_send()
        for cp in local:
            cp.wait()

    return _pcall(
        body, name="all_gather_weights",
        in_specs=[ANY] * n, out_specs=[ANY] * n,
        out_shape=[_sds((N_CHIPS,) + s.shape, s.dtype) for s in shards],
        scratch_shapes=[pltpu.SemaphoreType.DMA((6 * n,)), pltpu.SemaphoreType.DMA((6 * n,)),
                        pltpu.SemaphoreType.DMA((n,))],
    )(*shards)


def _exchange_grads(grads):
    n = len(grads)

    def body(*refs):
        g_refs, out_refs = refs[:n], refs[n:2 * n]
        send_sems, recv_sems, local_sems = refs[2 * n:]
        x, y, c = _position()
        me = 2 * x + y
        sibling = (x, y, 1 - c)
        chips = [(1 - x, y), (x, 1 - y), (1 - x, 1 - y)]

        def copy(i, k, src, slot, to):
            return pltpu.make_async_remote_copy(
                src_ref=src, dst_ref=out_refs[i].at[slot], send_sem=send_sems.at[7 * i + k],
                recv_sem=recv_sems.at[7 * i + k], device_id=to, device_id_type=MESH)

        local = [pltpu.make_async_copy(g_refs[i].at[me], out_refs[i].at[2 * me + c], local_sems.at[i]) for i in range(n)]
        for cp in local:
            cp.start()
        first = []
        for i in range(n):
            first.append(copy(i, 0, g_refs[i].at[me], 2 * me + c, sibling))
            for j, (px, py) in enumerate(chips):
                first.append(copy(i, 1 + j, g_refs[i].at[2 * px + py], 2 * me + c, (px, py, c)))
        for cp in first:
            cp.start()
        passed = []
        for i in range(n):
            for j, (px, py) in enumerate(chips):
                slot = 2 * (2 * px + py) + c
                copy(i, 1 + j, g_refs[i].at[me], slot, (px, py, c)).wait_recv()
                fwd = copy(i, 4 + j, out_refs[i].at[slot], slot, sibling)
                fwd.start()
                passed.append(fwd)
        for i in range(n):
            copy(i, 0, g_refs[i].at[me], 2 * me + (1 - c), sibling).wait_recv()
            for j, (px, py) in enumerate(chips):
                slot = 2 * (2 * px + py) + (1 - c)
                copy(i, 4 + j, g_refs[i].at[me], slot, sibling).wait_recv()
        for cp in first + passed:
            cp.wait_send()
        for cp in local:
            cp.wait()

    return _pcall(
        body, name="exchange_grads",
        in_specs=[ANY] * n, out_specs=[ANY] * n,
        out_shape=[_sds((N_DEV,) + g.shape[1:], g.dtype) for g in grads],
        scratch_shapes=[pltpu.SemaphoreType.DMA((7 * n,)), pltpu.SemaphoreType.DMA((7 * n,)),
                        pltpu.SemaphoreType.DMA((n,))],
    )(*grads)


def _all_reduce_small(packed):
    rows = packed.shape[0]

    def body(x_ref, sum_ref, slots, send_sems, recv_sems, local_sem):
        x, y, c = _position()
        me = 4 * x + 2 * y + c
        mine = pltpu.make_async_copy(x_ref, slots.at[me], local_sem)
        mine.start()
        sends = []
        for k in range(1, N_DEV):
            peer = (x ^ (k >> 2), y ^ ((k >> 1) & 1), c ^ (k & 1))
            cp = pltpu.make_async_remote_copy(src_ref=x_ref, dst_ref=slots.at[me], send_sem=send_sems.at[k - 1],
                                              recv_sem=recv_sems.at[k - 1], device_id=peer, device_id_type=MESH)
            cp.start()
            sends.append(cp)
        for k in range(1, N_DEV):
            px, py, pc = x ^ (k >> 2), y ^ ((k >> 1) & 1), c ^ (k & 1)
            pltpu.make_async_remote_copy(src_ref=x_ref, dst_ref=slots.at[4 * px + 2 * py + pc], send_sem=send_sems.at[k - 1],
                                         recv_sem=recv_sems.at[k - 1], device_id=(px, py, pc), device_id_type=MESH).wait_recv()
        for cp in sends:
            cp.wait_send()
        mine.wait()
        total = slots[0]
        for d in range(1, N_DEV):
            total = total + slots[d]
        sum_ref[...] = total

    vmem = pl.BlockSpec(memory_space=pltpu.VMEM)
    return _pcall(
        body, name="all_reduce_small", in_specs=[vmem], out_specs=vmem, out_shape=_sds(packed.shape, F32),
        scratch_shapes=[pltpu.VMEM((N_DEV, rows, LANES), F32), pltpu.SemaphoreType.DMA((N_DEV - 1,)),
                        pltpu.SemaphoreType.DMA((N_DEV - 1,)), pltpu.SemaphoreType.DMA],
        compiler_params=pltpu.CompilerParams(vmem_limit_bytes=VMEM_LIMIT_BYTES),
    )(packed)


def _adamw(w, g, m, v):
    m = ADAM_B1 * m + (1.0 - ADAM_B1) * g
    v = ADAM_B2 * v + (1.0 - ADAM_B2) * (g * g)
    m_hat = m / (1.0 - ADAM_B1 ** ADAM_STEP)
    v_hat = v / (1.0 - ADAM_B2 ** ADAM_STEP)
    delta = -ADAM_LR * (m_hat / (jnp.sqrt(v_hat) + ADAM_EPS) + ADAM_WD * w)
    return delta, m, v


def _adam_sharded(name, parts, w, m, v):
    shape = w.shape
    cols = shape[-1]
    rows = w.size // cols
    tile = 8
    while tile * 2 * cols <= ADAM_TILE_ELEMS and rows % (tile * 2) == 0:
        tile *= 2

    def fn(p, wv, mv, vv):
        g = p[0].astype(F32)
        for d in range(1, N_DEV):
            g = g + p[d].astype(F32)
        return (g,) + _adamw(wv, g, mv, vv)

    two_d = lambda a: a.reshape(rows, cols)
    outs = _rows(name, fn, rows // tile, [parts.reshape(N_DEV, rows, cols), two_d(w), two_d(m), two_d(v)],
                 [pl.BlockSpec((N_DEV, tile, cols), lambda i: (0, i, 0))] + [_tile_spec(tile, cols)] * 3,
                 [_sds((rows, cols), F32)] * 4, [_tile_spec(tile, cols)] * 4)
    return [o.reshape(shape) for o in outs]


def _adam_small(name, g, w, m, v):
    rows = g.shape[0]
    tile = rows // 2
    return _rows(name, lambda gv, wv, mv, vv: _adamw(wv, gv, mv, vv), rows // tile, [g, w, m, v],
                 [_tile_spec(tile, LANES)] * 4, [_sds(g.shape, F32)] * 3, [_tile_spec(tile, LANES)] * 3)


BIG = ("w_in", "w_a", "w_b", "w_out", "w_gate", "w_up", "w_down", "w_ple", "w_ple_gate")
SMALL = ("norm_mix_pre", "lb_gamma_fwd", "lb_gamma_bwd", "hg_norm", "sg_w", "sg_b", "sg_ln_g", "sg_ln_b",
         "norm_mix_post", "norm_ffn_pre", "norm_ffn_post")


def _layer_fwd(l, x, p_l, wg, sm):
    sv = {"x": x}
    h = _rms_fwd(f"norm_mix_pre_l{l}", x, sm["norm_mix_pre"])
    z = _proj_in(h, wg["w_in"], l)
    o_f, st_f = _hgrn_fwd(f"hgrn_fwd_l{l}", z, sm["lb_fwd"], False)
    o_b, st_b = _hgrn_fwd(f"hgrn_rev_l{l}", z, sm["lb_bwd"], True)
    a_out = _hg_post_fwd(f"hg_post_l{l}", o_f, o_b, z, sm["hg_norm"])
    b_out = _sgu_fwd(f"sgu_l{l}", z, sm["sg_w"], sm["sg_bias_t"], sm["sg_ln_g"], sm["sg_ln_b"])
    pa = _proj_rows_sharded(f"proj_a_l{l}", a_out, wg["w_a"], l, F32)
    pb = _proj_cols256(f"proj_b_l{l}", b_out, wg["w_b"], l)
    merged, mix = _merge_proj_out(f"merge_proj_out_l{l}", pa, pb, z, wg["w_out"], l)
    x1, h2 = _resid_rms_norm_fwd(f"norm_mix_post_ffn_pre_l{l}", x, mix, sm["norm_mix_post"], sm["norm_ffn_pre"])
    gt, up, act = _proj_gate_up(f"proj_gate_up_l{l}", h2, wg["w_gate"], wg["w_up"], l)
    ff = _proj_ffn_out(f"proj_down_l{l}", act, wg["w_down"], l)
    x2 = _resid_rms_fwd(f"norm_ffn_post_l{l}", x1, ff, sm["norm_ffn_post"])
    pe = _proj_cols256(f"proj_ple_l{l}", p_l, wg["w_ple"], l)
    gz, x3 = _proj_ple_gate_ple(f"proj_ple_gate_ple_l{l}", x2, wg["w_ple_gate"], pe, l)
    sv.update(h=h, z=z, o_f=o_f, o_b=o_b, st_f=st_f, st_b=st_b, a_out=a_out, b_out=b_out, pa=pa, pb=pb,
              merged=merged, mix=mix, x1=x1, h2=h2, gt=gt, up=up, act=act, ff=ff, x2=x2, pe=pe, gz=gz, p=p_l)
    return x3, sv


def _layer_bwd(l, dx3, sv, wg, sm, gw):
    t = dx3.shape[0]
    nt = t // TT
    sg = {}

    def wgrad(key, *a, **k):
        gw[key] = _wgrad(f"grad_{key}_l{l}", *a, l=l, into=gw.get(key), **k)

    row = lambda width: _bs((TT, width), lambda j, i: (i, 0))
    row_j = lambda width: _bs((TT, width), lambda j, i: (i, j))
    ffn_j = _bs((None, TT, FFN_SHARD), lambda j, i: (j, i, 0))
    blk_j = lambda shape: (tuple(shape), lambda j, i: (j, l, 0, 0))

    dpe, dgz = _ple_bwd(f"ple_bwd_l{l}", dx3, sv["pe"], sv["gz"])
    wgrad("w_ple", sv["p"], row(PLE_DIM), dpe, row_j(256), (PLE_DIM, 256), blk_j((PLE_DIM, 256)), N_CHIPS)
    wgrad("w_ple_gate", sv["x2"], row_j(256), dgz, row(D_MODEL), (256, D_MODEL), blk_j((256, D_MODEL)), N_CHIPS)
    dx2 = _proj_rows_sharded_t(f"proj_ple_gate_t_l{l}", dgz, wg["w_ple_gate"], l, F32, add=dx3)

    dff, sg["norm_ffn_post"] = _rms_bwd(f"norm_ffn_post_bwd_l{l}", sv["ff"], sm["norm_ffn_post"], dx2, None, MXU_DTYPE)
    dgt, dup = _proj_down_t_swiglu_bwd(f"proj_down_t_swiglu_bwd_l{l}", dff, wg["w_down"], sv["gt"], sv["up"], l)
    wgrad("w_down", sv["act"], ffn_j, dff, row(D_MODEL), (FFN_SHARD, D_MODEL), blk_j((FFN_SHARD, D_MODEL)), N_CHIPS)
    dh2 = _proj_ffn_in_t(f"proj_gate_up_t_l{l}", [(dgt, wg["w_gate"]), (dup, wg["w_up"])], l)
    wgrad("w_gate", sv["h2"], row(D_MODEL), dgt, ffn_j, (D_MODEL, FFN_SHARD), blk_j((D_MODEL, FFN_SHARD)), N_CHIPS)
    wgrad("w_up", sv["h2"], row(D_MODEL), dup, ffn_j, (D_MODEL, FFN_SHARD), blk_j((D_MODEL, FFN_SHARD)), N_CHIPS)
    dx1, sg["norm_ffn_pre"] = _rms_bwd(f"norm_ffn_pre_bwd_l{l}", sv["x1"], sm["norm_ffn_pre"], dh2, dx2, F32)

    dmix, sg["norm_mix_post"] = _rms_bwd(f"norm_mix_post_bwd_l{l}", sv["mix"], sm["norm_mix_post"], dx1, None, MXU_DTYPE)
    dmerged = _proj_rows_sharded_t(f"proj_out_t_l{l}", dmix, wg["w_out"], l, F32)
    wgrad("w_out", sv["merged"], row_j(256), dmix, row(D_MODEL), (256, D_MODEL), blk_j((256, D_MODEL)), N_CHIPS)
    dpa, dpb, dz = _merge_bwd(f"merge_bwd_l{l}", dmerged, sv["pa"], sv["pb"], sv["z"])
    da = _proj_rows_sharded_t(f"proj_a_t_l{l}", dpa, wg["w_a"], l, F32)
    wgrad("w_a", sv["a_out"], row_j(256), dpa, row(D_MODEL), (256, D_MODEL), blk_j((256, D_MODEL)), N_CHIPS)
    dbo = _proj_cols256_t(f"proj_b_t_l{l}", dpb, wg["w_b"], l)
    wgrad("w_b", sv["b_out"], row(SG_WIDTH), dpb, row_j(256), (SG_WIDTH, 256), blk_j((SG_WIDTH, 256)), N_CHIPS)

    dz, sg["sg_w"], dsg_sum, sg["sg_ln_g"], sg["sg_ln_b"] = _sgu_bwd(
        f"sgu_bwd_l{l}", dbo, sv["z"], sm["sg_w"], sm["sg_bias_t"], sm["sg_ln_g"], sm["sg_ln_b"], dz)
    sg["sg_b"] = dsg_sum.reshape(SG_CHUNK, N_HEADS, SG_WIDTH // N_HEADS).sum(axis=-1).T
    d_o, dz, sg["hg_norm"] = _hg_post_bwd(f"hg_post_bwd_l{l}", da, sv["o_f"], sv["o_b"], sv["z"], sm["hg_norm"], dz)
    dq_f, dv_f, dz, sg["lb_fwd"] = _hgrn_bwd(f"hgrn_fwd_bwd_l{l}", sv["z"], d_o, sv["st_f"], sm["lb_fwd"], False, dz)
    dq_b, dv_b, dz, sg["lb_bwd"] = _hgrn_bwd(f"hgrn_rev_bwd_l{l}", sv["z"], d_o, sv["st_b"], sm["lb_bwd"], True, dz)
    dz = _hg_gate_bwd(f"hg_gate_bwd_l{l}", dq_f, dq_b, dv_f, dv_b, sv["z"], dz)

    dh = _proj_in_t(dz, wg["w_in"], l)
    gw["w_in"] = _wgrad(f"grad_w_in_l{l}", sv["h"], _bs((TM_WIDE, D_MODEL), lambda n, i: (i, 0)), dz,
                        _bs((TM_WIDE, 1024), lambda n, i: (i, n)), (D_MODEL, 2048),
                        ((D_MODEL, 1024), lambda n, i: (n // 2, l, 0, n % 2)), 8, l, gw.get("w_in"), tt=TM_WIDE)
    dx, sg["norm_mix_pre"] = _rms_bwd(f"norm_mix_pre_bwd_l{l}", sv["x"], sm["norm_mix_pre"], dh, dx1, F32)
    del nt
    return dx, gw, sg


def _pack(parts):
    return jnp.concatenate([a.reshape(-1, LANES) for a in parts], axis=0)


def _step(x, p, loss_target, w, m, v):
    x = x[0]
    target = loss_target[0]
    depth = w["w_in"].shape[0]

    gathered = _all_gather_weights([w[k].astype(MXU_DTYPE) for k in BIG])
    wg = dict(zip(BIG, gathered))
    lb_f = _lower_bounds("lower_bounds_fwd", w["lb_gamma_fwd"])
    lb_b = _lower_bounds("lower_bounds_bwd", w["lb_gamma_bwd"])

    def small_of(l):
        sm = {k: w[k][l:l + 1] for k in ("norm_mix_pre", "hg_norm", "sg_ln_g", "sg_ln_b", "norm_mix_post",
                                        "norm_ffn_pre", "norm_ffn_post")}
        sm["lb_fwd"], sm["lb_bwd"] = lb_f[l:l + 1], lb_b[l:l + 1]
        sm["sg_w"] = w["sg_w"][l]
        sm["sg_bias_t"] = jnp.repeat(w["sg_b"][l].T, SG_WIDTH // N_HEADS, axis=1)
        return sm

    saved = []
    h = x
    for l in range(depth):
        h, sv = _layer_fwd(l, h, p[l, 0], wg, small_of(l))
        saved.append(sv)

    dy, sq_err = _loss_bwd("loss", h, target)
    gw = {}
    small_grads = [None] * depth
    for l in reversed(range(depth)):
        dy, gw, small_grads[l] = _layer_bwd(l, dy, saved[l], wg, small_of(l), gw)

    def stack(key):
        return jnp.concatenate([small_grads[l][key].reshape((1,) + w_shape[1:]) for l in range(depth)], axis=0)

    g_small = {}
    for key in SMALL:
        w_shape = w[key].shape
        if key == "lb_gamma_fwd":
            dlb = jnp.concatenate([small_grads[l]["lb_fwd"] for l in range(depth)], axis=0)
            g_small[key] = _lower_bounds_bwd("lower_bounds_fwd_bwd", w[key], dlb)
        elif key == "lb_gamma_bwd":
            dlb = jnp.concatenate([small_grads[l]["lb_bwd"] for l in range(depth)], axis=0)
            g_small[key] = _lower_bounds_bwd("lower_bounds_bwd_bwd", w[key], dlb)
        else:
            g_small[key] = stack(key)

    packed = _pack([g_small[k] for k in SMALL] + [sq_err])
    summed = _all_reduce_small(packed)
    n_small_rows = sum(w[k].size for k in SMALL) // LANES
    loss = 0.5 * jnp.sum(summed[n_small_rows:]) / D_MODEL

    g_rows = summed[:n_small_rows]
    d_rows, m_rows, v_rows = _adam_small("adamw_small", g_rows, _pack([w[k] for k in SMALL]),
                                         _pack([m[k] for k in SMALL]), _pack([v[k] for k in SMALL]))
    out = {}
    off = 0
    for key in SMALL:
        n_rows = w[key].size // LANES
        sl = slice(off, off + n_rows)
        out[key] = tuple(a[sl].reshape(w[key].shape) for a in (g_rows, d_rows, m_rows, v_rows))
        off += n_rows

    parts = _exchange_grads([gw[k] for k in BIG])
    for key, part in zip(BIG, parts):
        out[key] = tuple(_adam_sharded(f"adamw_{key}", part, w[key], m[key], v[key]))
    return loss, dy[None], out


WEIGHTS = ("norm_mix_pre", "w_in", "lb_gamma_fwd", "lb_gamma_bwd", "hg_norm", "sg_w", "sg_b", "sg_ln_g", "sg_ln_b",
           "w_a", "w_b", "w_out", "norm_mix_post", "norm_ffn_pre", "w_gate", "w_up", "w_down", "norm_ffn_post",
           "w_ple", "w_ple_gate")


def kernel(x, p, norm_mix_pre, w_in, lb_gamma_fwd, lb_gamma_bwd, hg_norm, sg_w, sg_b, sg_ln_g, sg_ln_b, w_a, w_b, w_out, norm_mix_post, norm_ffn_pre, w_gate, w_up, w_down, norm_ffn_post, w_ple, w_ple_gate, loss_target, m_norm_mix_pre, m_w_in, m_lb_gamma_fwd, m_lb_gamma_bwd, m_hg_norm, m_sg_w, m_sg_b, m_sg_ln_g, m_sg_ln_b, m_w_a, m_w_b, m_w_out, m_norm_mix_post, m_norm_ffn_pre, m_w_gate, m_w_up, m_w_down, m_norm_ffn_post, m_w_ple, m_w_ple_gate, v_norm_mix_pre, v_w_in, v_lb_gamma_fwd, v_lb_gamma_bwd, v_hg_norm, v_sg_w, v_sg_b, v_sg_ln_g, v_sg_ln_b, v_w_a, v_w_b, v_w_out, v_norm_mix_post, v_norm_ffn_pre, v_w_gate, v_w_up, v_w_down, v_norm_ffn_post, v_w_ple, v_w_ple_gate):
    w = dict(zip(WEIGHTS, (norm_mix_pre, w_in, lb_gamma_fwd, lb_gamma_bwd, hg_norm, sg_w, sg_b, sg_ln_g, sg_ln_b, w_a, w_b, w_out, norm_mix_post, norm_ffn_pre, w_gate, w_up, w_down, norm_ffn_post, w_ple, w_ple_gate)))
    m = dict(zip(WEIGHTS, (m_norm_mix_pre, m_w_in, m_lb_gamma_fwd, m_lb_gamma_bwd, m_hg_norm, m_sg_w, m_sg_b, m_sg_ln_g, m_sg_ln_b, m_w_a, m_w_b, m_w_out, m_norm_mix_post, m_norm_ffn_pre, m_w_gate, m_w_up, m_w_down, m_norm_ffn_post, m_w_ple, m_w_ple_gate)))
    v = dict(zip(WEIGHTS, (v_norm_mix_pre, v_w_in, v_lb_gamma_fwd, v_lb_gamma_bwd, v_hg_norm, v_sg_w, v_sg_b, v_sg_ln_g, v_sg_ln_b, v_w_a, v_w_b, v_w_out, v_norm_mix_post, v_norm_ffn_pre, v_w_gate, v_w_up, v_w_down, v_norm_ffn_post, v_w_ple, v_w_ple_gate)))
    loss, grad_x, out = _step(x, p, loss_target, w, m, v)
    res = [loss, grad_x]
    for i in range(4):
        res += [out[k][i] for k in WEIGHTS]
    return tuple(res)
```

```python
import functools

import jax
import jax.numpy as jnp
from jax import lax
from jax.experimental import pallas as pl
from jax.experimental.pallas import tpu as pltpu

F32 = jnp.float32
MXU_DTYPE = jnp.bfloat16
GRAD_EXCHANGE_DTYPE = jnp.bfloat16

D_MODEL = 1024
N_HEADS = 8
HEAD = 128
HG_CHUNK = 64
HG_SUB = 16
HG_BLOCK = 512
HG_SAFE_EXP = 80.0
SG_CHUNK = 128
SG_WIDTH = 512
FFN_SHARD = 704
PLE_DIM = 256
N_IN = 8192
N_CHIPS = 4
N_DEV = 8
EPS = 1e-6
LANES = 128
VMEM_LIMIT_BYTES = 56 * 2 ** 20

ADAM_LR = 0.001
ADAM_B1 = 0.9
ADAM_B2 = 0.999
ADAM_EPS = 1e-08
ADAM_WD = 0.01
ADAM_STEP = 10
ADAM_TILE_ELEMS = 128 * 1024

MESH = pl.DeviceIdType.MESH
ANY = pl.BlockSpec(memory_space=pl.ANY)


def _pcall(body, **kw):
    return pl.pallas_call(body, **kw)


def _params(n_axes):
    return pltpu.CompilerParams(dimension_semantics=("arbitrary",) * n_axes, vmem_limit_bytes=VMEM_LIMIT_BYTES)


def _dot(a, b, ca, cb):
    return lax.dot_general(a.astype(MXU_DTYPE), b.astype(MXU_DTYPE), (((ca,), (cb,)), ((), ())),
                           preferred_element_type=F32)


def _dot_f32(a, b, ca, cb):
    return lax.dot_general(a, b, (((ca,), (cb,)), ((), ())), precision=lax.Precision.HIGH,
                           preferred_element_type=F32)


def _nn(a, b):
    return _dot(a, b, 1, 0)


def _nt(a, b):
    return _dot(a, b, 1, 1)


def _tn(a, b):
    return _dot(a, b, 0, 0)


NN, NT, TN = (1, 0), (1, 1), (0, 0)


def _sigmoid(x):
    return jax.nn.sigmoid(x)


def _dsilu(x, s):
    return s * (1.0 + x * (1.0 - s))


_SQRT_HALF = 0.7071067811865476
_INV_SQRT_2PI = 0.3989422804014327


def _gelu(x):
    return 0.5 * x * (1.0 + lax.erf(x * _SQRT_HALF))


def _dgelu(x):
    return 0.5 * (1.0 + lax.erf(x * _SQRT_HALF)) + x * jnp.exp(-0.5 * x * x) * _INV_SQRT_2PI


def _mean_last(x):
    return jnp.mean(x, axis=-1, keepdims=True)


def _rowsum(x):
    return jnp.sum(x, axis=0, keepdims=True)


class _Comm:
    def __init__(self, ins, bufs, sem_shapes, make):
        self.ins, self.bufs, self.sem_shapes, self.make = list(ins), list(bufs), list(sem_shapes), make
        self.extends = not isinstance(self.bufs[0], jax.ShapeDtypeStruct)


def _hosted_call(name, compute, grid, args, in_specs, out_shapes, out_specs, scratch, aliases, comm):
    n_in, n_out, n_scr = len(args), len(out_shapes), len(scratch)
    if comm is None:
        def plain(*refs):
            compute(refs[:n_in], refs[n_in:n_in + n_out], refs[n_in + n_out:])

        res = _pcall(plain, name=name, grid=grid, in_specs=list(in_specs), out_specs=list(out_specs),
                     out_shape=list(out_shapes), scratch_shapes=list(scratch), input_output_aliases=dict(aliases),
                     compiler_params=_params(len(grid)))(*args)
        return list(res), []

    n_cin, n_buf = len(comm.ins), len(comm.bufs)
    all_args = list(args) + comm.ins + (comm.bufs if comm.extends else [])
    n_all = len(all_args)
    all_aliases = dict(aliases)
    if comm.extends:
        for j in range(n_buf):
            all_aliases[n_in + n_cin + j] = n_out + j
    buf_shapes = [_sds(b.shape, b.dtype) for b in comm.bufs]

    def body(*refs):
        outs = refs[n_all:n_all + n_out + n_buf]
        scr = refs[n_all + n_out + n_buf:]
        start, finish = comm.make(refs[n_in:n_in + n_cin], outs[n_out:], scr[n_scr:])
        first, last = None, None
        for axis, size in enumerate(grid):
            i = pl.program_id(axis)
            first = (i == 0) if first is None else jnp.logical_and(first, i == 0)
            last = (i == size - 1) if last is None else jnp.logical_and(last, i == size - 1)
        pl.when(first)(start)
        compute(refs[:n_in], outs[:n_out], scr[:n_scr])
        pl.when(last)(finish)

    res = _pcall(body, name=name, grid=grid, in_specs=list(in_specs) + [ANY] * (n_all - n_in),
                 out_specs=list(out_specs) + [ANY] * n_buf, out_shape=list(out_shapes) + buf_shapes,
                 scratch_shapes=list(scratch) + comm.sem_shapes, input_output_aliases=all_aliases,
                 compiler_params=_params(len(grid)))(*all_args)
    return list(res[:n_out]), list(res[n_out:])


def _run_comm(name, comm):
    n_cin, n_buf = len(comm.ins), len(comm.bufs)
    all_args = comm.ins + (comm.bufs if comm.extends else [])
    n_all = len(all_args)

    def body(*refs):
        start, finish = comm.make(refs[:n_cin], refs[n_all:n_all + n_buf], refs[n_all + n_buf:])
        start()
        finish()

    res = _pcall(body, name=name, in_specs=[ANY] * n_all, out_specs=[ANY] * n_buf,
                 out_shape=[_sds(b.shape, b.dtype) for b in comm.bufs], scratch_shapes=comm.sem_shapes,
                 input_output_aliases={n_cin + j: j for j in range(n_buf)} if comm.extends else {})(*all_args)
    return list(res)


def _mm(name, pairs, kind, out_shape, grid, in_specs, out_spec, *, reduce_axis=None, add=None,
        add_spec=None, into=None, prep=None, comm=None):
    n_pairs = len(pairs)
    has_add = add is not None
    staged = reduce_axis is not None and out_shape.dtype != F32

    def compute(in_refs, out_refs, scr):
        o_ref = out_refs[0]
        acc = None
        for i in range(n_pairs):
            a = in_refs[2 * i][...]
            b = in_refs[2 * i + 1][...]
            if prep is not None:
                b = prep(b)
            prod = _dot(a, b, *kind)
            acc = prod if acc is None else acc + prod
        if has_add:
            acc = acc + in_refs[2 * n_pairs][...]
        if reduce_axis is None:
            o_ref[...] = acc.astype(o_ref.dtype)
        else:
            r = pl.program_id(reduce_axis)
            acc_ref = scr[0] if staged else o_ref

            @pl.when(r == 0)
            def _():
                acc_ref[...] = acc

            @pl.when(r > 0)
            def _():
                acc_ref[...] += acc

            if staged:
                @pl.when(r == grid[reduce_axis] - 1)
                def _():
                    o_ref[...] = acc_ref[...].astype(o_ref.dtype)

    scratch = []
    if staged:
        scratch = [pltpu.VMEM(tuple(d for d in out_spec.block_shape if d is not None), F32)]
    args = [t for pair in pairs for t in pair]
    specs = list(in_specs)
    if has_add:
        args.append(add)
        specs.append(add_spec)
    aliases = {}
    if into is not None:
        aliases = {len(args): 0}
        args.append(into)
        specs.append(ANY)
    outs, bufs = _hosted_call(name, compute, grid, args, specs, [out_shape], [out_spec], scratch, aliases, comm)
    return outs[0] if comm is None else (outs[0], bufs)


def _sds(shape, dtype):
    return jax.ShapeDtypeStruct(tuple(shape), dtype)


def _bs(shape, fn):
    return pl.BlockSpec(tuple(shape), fn)


TM = 1024
TM_WIDE = 2048


def _merge_lead(b):
    return b.reshape(b.shape[0] * b.shape[1], b.shape[2])


def _proj_in(h, w_in_g, l, comm=None):
    t = h.shape[0]
    return _mm(f"proj_in_l{l}", [(h, w_in_g)], NN, _sds((t, N_IN), F32), (8, t // TM_WIDE),
               [_bs((TM_WIDE, D_MODEL), lambda n, m: (m, 0)),
                _bs((None, None, D_MODEL, 1024), lambda n, m: (n // 2, l, 0, n % 2))],
               _bs((TM_WIDE, 1024), lambda n, m: (m, n)), comm=comm)


def _proj_rows_sharded(name, a, w_g, l, out_dtype, add=None):
    t = a.shape[0]
    return _mm(name, [(a, w_g)], NN, _sds((t, D_MODEL), out_dtype), (t // TM,),
               [_bs((TM, D_MODEL), lambda m: (m, 0)),
                _bs((N_CHIPS, None, 256, D_MODEL), lambda m: (0, l, 0, 0))],
               _bs((TM, D_MODEL), lambda m: (m, 0)), prep=_merge_lead, add=add,
               add_spec=_bs((TM, D_MODEL), lambda m: (m, 0)))


def _proj_rows_sharded_t(name, g, w_g, l, out_dtype, add=None):
    t = g.shape[0]
    return _mm(name, [(g, w_g)], NT, _sds((t, D_MODEL), out_dtype), (t // TM,),
               [_bs((TM, D_MODEL), lambda m: (m, 0)),
                _bs((N_CHIPS, None, 256, D_MODEL), lambda m: (0, l, 0, 0))],
               _bs((TM, D_MODEL), lambda m: (m, 0)), prep=_merge_lead, add=add,
               add_spec=_bs((TM, D_MODEL), lambda m: (m, 0)))


def _proj_cols256(name, a, w_g, l):
    t, k = a.shape
    return _mm(name, [(a, w_g)], NN, _sds((t, D_MODEL), F32), (N_CHIPS, t // TM),
               [_bs((TM, k), lambda j, m: (m, 0)),
                _bs((None, None, k, 256), lambda j, m: (j, l, 0, 0))],
               _bs((TM, 256), lambda j, m: (m, j)))


def _proj_cols256_t(name, g, w_g, l):
    t = g.shape[0]
    k = w_g.shape[2]
    return _mm(name, [(g, w_g)], NT, _sds((t, k), F32), (t // TM, N_CHIPS),
               [_bs((TM, 256), lambda m, j: (m, j)),
                _bs((None, None, k, 256), lambda m, j: (j, l, 0, 0))],
               _bs((TM, k), lambda m, j: (m, 0)), reduce_axis=1)


def _proj_gate_up(name, h2, wg_g, wu_g, l, comm=None):
    t = h2.shape[0]

    def compute(in_refs, out_refs, scr):
        h_ref, wg_ref, wu_ref = in_refs
        gt_ref, up_ref, act_ref = out_refs
        h = h_ref[...]
        g = _nn(h, wg_ref[...])
        u = _nn(h, wu_ref[...])
        gt_ref[...] = g
        up_ref[...] = u
        act_ref[...] = ((g * _sigmoid(g)) * u).astype(act_ref.dtype)

    w_spec = _bs((None, None, D_MODEL, FFN_SHARD), lambda j, m: (j, l, 0, 0))
    o_spec = _bs((None, TM, FFN_SHARD), lambda j, m: (j, m, 0))
    outs, bufs = _hosted_call(name, compute, (N_CHIPS, t // TM), [h2, wg_g, wu_g],
                        [_bs((TM, D_MODEL), lambda j, m: (m, 0)), w_spec, w_spec],
                        [_sds((N_CHIPS, t, FFN_SHARD), F32), _sds((N_CHIPS, t, FFN_SHARD), F32),
                         _sds((N_CHIPS, t, FFN_SHARD), MXU_DTYPE)], [o_spec, o_spec, o_spec], [], {}, comm)
    return outs if comm is None else (outs, bufs)


def _proj_ffn_in_t(name, pairs, l, comm=None):
    t = pairs[0][0].shape[1]
    specs = []
    for _ in pairs:
        specs += [_bs((None, TM, FFN_SHARD), lambda m, j: (j, m, 0)),
                  _bs((None, None, D_MODEL, FFN_SHARD), lambda m, j: (j, l, 0, 0))]
    return _mm(name, pairs, NT, _sds((t, D_MODEL), F32), (t // TM, N_CHIPS), specs,
               _bs((TM, D_MODEL), lambda m, j: (m, 0)), reduce_axis=1, comm=comm)


def _proj_ffn_out(name, act, w_g, l, comm=None):
    t = act.shape[1]
    return _mm(name, [(act, w_g)], NN, _sds((t, D_MODEL), F32), (t // TM, N_CHIPS),
               [_bs((None, TM, FFN_SHARD), lambda m, j: (j, m, 0)),
                _bs((None, None, FFN_SHARD, D_MODEL), lambda m, j: (j, l, 0, 0))],
               _bs((TM, D_MODEL), lambda m, j: (m, 0)), reduce_axis=1, comm=comm)


def _proj_down_t_swiglu_bwd(name, dff, w_g, gt, up, l, comm=None):
    t = dff.shape[0]

    def compute(in_refs, out_refs, scr):
        d_ref, w_ref, gt_ref, up_ref = in_refs
        dgt_ref, dup_ref = out_refs
        dact = _nt(d_ref[...], w_ref[...])
        g = gt_ref[...]
        s = _sigmoid(g)
        dgt_ref[...] = (dact * up_ref[...] * _dsilu(g, s)).astype(dgt_ref.dtype)
        dup_ref[...] = (dact * (g * s)).astype(dup_ref.dtype)

    o_spec = _bs((None, TM, FFN_SHARD), lambda j, m: (j, m, 0))
    outs, bufs = _hosted_call(name, compute, (N_CHIPS, t // TM), [dff, w_g, gt, up],
                        [_bs((TM, D_MODEL), lambda j, m: (m, 0)),
                         _bs((None, None, FFN_SHARD, D_MODEL), lambda j, m: (j, l, 0, 0)), o_spec, o_spec],
                        [_sds((N_CHIPS, t, FFN_SHARD), MXU_DTYPE)] * 2, [o_spec, o_spec], [], {}, comm)
    return outs if comm is None else (outs, bufs)


def _proj_in_t(dz, w_in_g, l, comm=None):
    t = dz.shape[0]
    return _mm(f"proj_in_t_l{l}", [(dz, w_in_g)], NT, _sds((t, D_MODEL), F32), (t // TM_WIDE, 8),
               [_bs((TM_WIDE, 1024), lambda m, n: (m, n)),
                _bs((None, None, D_MODEL, 1024), lambda m, n: (n // 2, l, 0, n % 2))],
               _bs((TM_WIDE, D_MODEL), lambda m, n: (m, 0)), reduce_axis=1, comm=comm)


TT = 1024


def _wgrad(name, a, a_spec, g, g_spec, shard_shape, o_map, n_outer, l, into, tt=TT):
    t = a.shape[-2]
    out = _sds((N_CHIPS, 2) + tuple(shard_shape), GRAD_EXCHANGE_DTYPE)
    return _mm(name, [(a, g)], TN, out, (n_outer, t // tt), [a_spec, g_spec],
               _bs((None, None) + tuple(o_map[0]), o_map[1]), reduce_axis=1, into=into)


def _rows(name, fn, n_tiles, ins, in_specs, out_shapes, out_specs, n_red=0, into=()):
    n_in = len(ins)
    n_out = len(out_shapes)

    def body(*refs):
        in_refs = refs[:n_in]
        out_refs = refs[len(refs) - n_out:]
        vals = fn(*[r[...] for r in in_refs])
        if not isinstance(vals, (tuple, list)):
            vals = (vals,)
        first = pl.program_id(0) == 0
        for j in range(n_out):
            o_ref = out_refs[j]
            val = vals[j]
            if j < n_out - n_red:
                o_ref[...] = val.astype(o_ref.dtype)
            else:
                @pl.when(first)
                def _(o_ref=o_ref, val=val):
                    o_ref[...] = val

                @pl.when(jnp.logical_not(first))
                def _(o_ref=o_ref, val=val):
                    o_ref[...] += val

    args = list(ins)
    specs = list(in_specs)
    aliases = {}
    for buf, out_idx in into:
        aliases[len(args)] = out_idx
        args.append(buf)
        specs.append(ANY)
    res = _pcall(body, name=name, grid=(n_tiles,), in_specs=specs, out_specs=list(out_specs),
                 out_shape=list(out_shapes), input_output_aliases=aliases, compiler_params=_params(1))(*args)
    return res


def _tile_spec(tile, width, blk=0):
    return pl.BlockSpec((tile, width), lambda i: (i, blk))


def _whole(shape):
    nd = len(shape)
    return pl.BlockSpec(tuple(shape), lambda i: (0,) * nd)


def _rms(x, g):
    r = lax.rsqrt(_mean_last(x * x) + EPS)
    return (x * r) * g


def _rms_bwd_math(u, g, dy):
    r = lax.rsqrt(_mean_last(u * u) + EPS)
    uh = u * r
    gdy = dy * g
    du = r * (gdy - uh * _mean_last(gdy * uh))
    return du, _rowsum(dy * uh)


def _rms_fwd(name, x, g):
    t, d = x.shape
    tile = 512
    return _rows(name, lambda xv, gv: (_rms(xv, gv),), t // tile, [x, g],
                 [_tile_spec(tile, d), _whole((1, d))], [_sds((t, d), MXU_DTYPE)], [_tile_spec(tile, d)])[0]


def _resid_rms_fwd(name, x, y, g):
    t, d = x.shape
    tile = 512
    return _rows(name, lambda xv, yv, gv: (xv + _rms(yv, gv),), t // tile, [x, y, g],
                 [_tile_spec(tile, d), _tile_spec(tile, d), _whole((1, d))], [_sds((t, d), F32)],
                 [_tile_spec(tile, d)])[0]


def _resid_rms_norm_fwd(name, x, y, g, g_next):
    t, d = x.shape
    tile = 512

    def fn(xv, yv, gv, gn):
        s = xv + _rms(yv, gv)
        return s, _rms(s, gn)

    return _rows(name, fn, t // tile, [x, y, g, g_next],
                 [_tile_spec(tile, d), _tile_spec(tile, d), _whole((1, d)), _whole((1, d))],
                 [_sds((t, d), F32), _sds((t, d), MXU_DTYPE)], [_tile_spec(tile, d)] * 2)


TF = 512


def _merge_proj_out(name, pa, pb, z, w_g, l, comm=None):
    t = z.shape[0]

    def compute(in_refs, out_refs, scr):
        pa_ref, pb_ref, ga_ref, gb_ref, w_ref = in_refs
        mg_ref, mix_ref = out_refs
        merged = _sigmoid(ga_ref[...]) * pa_ref[...] + _sigmoid(gb_ref[...]) * pb_ref[...]
        mg_ref[...] = merged.astype(mg_ref.dtype)
        mix_ref[...] = _nn(merged, _merge_lead(w_ref[...]))

    row = lambda blk: _bs((TF, D_MODEL), lambda m: (m, blk))
    outs, bufs = _hosted_call(name, compute, (t // TF,), [pa, pb, z, z, w_g],
                        [row(0), row(0), row(6), row(7), _bs((N_CHIPS, None, 256, D_MODEL), lambda m: (0, l, 0, 0))],
                        [_sds((t, D_MODEL), MXU_DTYPE), _sds((t, D_MODEL), F32)], [row(0), row(0)], [], {}, comm)
    return outs if comm is None else (outs, bufs)


def _proj_ple_gate_ple(name, x2, w_g, pe, l):
    t = x2.shape[0]

    def body(x_ref, w_ref, pe_ref, gz_ref, x3_ref):
        x = x_ref[...]
        gz = _nn(x, _merge_lead(w_ref[...]))
        gz_ref[...] = gz
        x3_ref[...] = x + pe_ref[...] * _sigmoid(gz)

    row = _bs((TF, D_MODEL), lambda m: (m, 0))
    return _pcall(body, name=name, grid=(t // TF,),
                  in_specs=[row, _bs((N_CHIPS, None, 256, D_MODEL), lambda m: (0, l, 0, 0)), row],
                  out_specs=[row, row], out_shape=[_sds((t, D_MODEL), F32)] * 2,
                  compiler_params=_params(1))(x2, w_g, pe)


def _rms_bwd(name, u, g, dy, resid, out_dtype):
    t, d = u.shape
    tile = 256

    def fn(uv, gv, dyv, *rest):
        du, dg = _rms_bwd_math(uv, gv, dyv)
        if rest:
            du = du + rest[0]
        return du, dg

    ins = [u, g, dy] + ([resid] if resid is not None else [])
    specs = [_tile_spec(tile, d), _whole((1, d)), _tile_spec(tile, d)] + ([_tile_spec(tile, d)] if resid is not None else [])
    return _rows(name, fn, t // tile, ins, specs, [_sds((t, d), out_dtype), _sds((1, d), F32)],
                 [_tile_spec(tile, d), _whole((1, d))], n_red=1)


def _cumsum_rows(x, group, suffix):
    n = x.shape[0]
    pos = lax.broadcasted_iota(jnp.int32, x.shape, 0) % group
    d = 1
    while d < group:
        if suffix:
            x = x + jnp.where(pos < group - d, pltpu.roll(x, n - d, 0), 0.0)
        else:
            x = x + jnp.where(pos >= d, pltpu.roll(x, d, 0), 0.0)
        d *= 2
    return x


def _hg_gates(zq, zf, lb):
    q = zq * _sigmoid(zq)
    f = lb + (1.0 - lb) * _sigmoid(zf)
    logf = jnp.log(jnp.maximum(f, jnp.finfo(F32).tiny))
    k = (1.0 - lb) * _sigmoid(-zf)
    return q, k, logf, f


def _tri_mask(n, rev):
    t_i = lax.broadcasted_iota(jnp.int32, (n, n), 0)
    s_i = lax.broadcasted_iota(jnp.int32, (n, n), 1)
    return (s_i >= t_i) if rev else (s_i <= t_i)


def _anchors_are_safe(b_s, n_chunks, rev):
    worst = None
    for c in range(n_chunks):
        base = c * HG_CHUNK
        first = base + HG_CHUNK - 1 if rev else base
        last = base if rev else base + HG_CHUNK - 1
        mid = base + HG_CHUNK // 2
        b0, bm, bl = b_s[first:first + 1, :], b_s[mid:mid + 1, :], b_s[last:last + 1, :]
        span = jnp.maximum(b0 - bm, bm - bl)
        worst = span if worst is None else jnp.maximum(worst, span)
    return jnp.max(worst) < HG_SAFE_EXP


def _sub_ranges(base, i_sub, rev):
    r0 = base + i_sub * HG_SUB
    r1 = r0 + HG_SUB
    if rev:
        e0, e1, anchor = r1, base + HG_CHUNK, r1
    else:
        e0, e1, anchor = base, r0, r0 - 1
    return r0, r1, e0, e1, anchor


def _hgrn_fwd(name, z, lb_row, rev):
    t = z.shape[0]
    nb = t // HG_BLOCK
    ncb = HG_BLOCK // HG_CHUNK
    nsb = HG_CHUNK // HG_SUB
    zf0 = 16 if rev else 8

    def tmap(i):
        return nb - 1 - i if rev else i

    def body(zq_ref, zf_ref, zi_ref, lb_ref, o_ref, st_ref, state, q_s, k_s, v_s, b_s):
        @pl.when(pl.program_id(1) == 0)
        def _():
            state[...] = jnp.zeros_like(state)

        q, k, logf, _ = _hg_gates(zq_ref[...], zf_ref[...], lb_ref[...])
        v = zi_ref[...]
        b_all = _cumsum_rows(logf, HG_CHUNK, rev)
        b_s[...] = b_all
        order = range(ncb - 1, -1, -1) if rev else range(ncb)
        safe = _anchors_are_safe(b_s, ncb, rev)

        @pl.when(safe)
        def _():
            cmask = _tri_mask(HG_CHUNK, rev)
            st = state[...]
            outs = [None] * ncb
            for c in order:
                base = c * HG_CHUNK
                rows = slice(base, base + HG_CHUNK)
                last = base if rev else base + HG_CHUNK - 1
                mid = base + HG_CHUNK // 2
                q_c, k_c, v_c, b = q[rows, :], k[rows, :], v[rows, :], b_all[rows, :]
                bl, bm = b_s[last:last + 1, :], b_s[mid:mid + 1, :]
                st_ref[c] = st
                a = jnp.where(cmask, _nt(q_c * jnp.exp(b - bm), k_c * jnp.exp(bm - b)), 0.0)
                outs[c] = _nt(q_c * jnp.exp(b), st) + _nn(a, v_c)
                st = st * jnp.exp(bl) + _tn(v_c, k_c * jnp.exp(bl - b))
            state[...] = st
            o_ref[...] = jnp.concatenate(outs, axis=0)

        @pl.when(jnp.logical_not(safe))
        def _():
            q_s[...] = q
            k_s[...] = k
            v_s[...] = v
            mask = _tri_mask(HG_SUB, rev)
            for c in order:
                base = c * HG_CHUNK
                rows = slice(base, base + HG_CHUNK)
                last = base if rev else base + HG_CHUNK - 1
                st = state[...]
                st_ref[c] = st
                b = b_s[rows, :]
                bl = b_s[last:last + 1, :]
                o_inter = _nt(q_s[rows, :] * jnp.exp(b), st)
                kd = k_s[rows, :] * jnp.exp(bl - b)
                state[...] = st * jnp.exp(bl) + _tn(v_s[rows, :], kd)
                parts = []
                for i_sub in range(nsb):
                    r0, r1, e0, e1, anchor = _sub_ranges(base, i_sub, rev)
                    q_i, k_i, b_i = q_s[r0:r1, :], k_s[r0:r1, :], b_s[r0:r1, :]
                    decay = jnp.exp(jnp.minimum(b_i[:, None, :] - b_i[None, :, :], 0.0))
                    a_d = jnp.where(mask, jnp.sum(q_i[:, None, :] * k_i[None, :, :] * decay, axis=-1), 0.0)
                    o_i = _nn(a_d, v_s[r0:r1, :])
                    if e1 > e0:
                        anc = b_s[anchor:anchor + 1, :]
                        q_t = q_i * jnp.exp(b_i - anc)
                        k_t = k_s[e0:e1, :] * jnp.exp(anc - b_s[e0:e1, :])
                        o_i = o_i + _nn(_nt(q_t, k_t), v_s[e0:e1, :])
                    parts.append(o_i)
                o_ref[rows, :] = o_inter + jnp.concatenate(parts, axis=0)

    blk = lambda off: pl.BlockSpec((HG_BLOCK, HEAD), lambda h, i: (tmap(i), off + h))
    return _pcall(
        body, name=name, grid=(N_HEADS, nb),
        in_specs=[blk(0), blk(zf0), blk(24), pl.BlockSpec((1, HEAD), lambda h, i: (0, h))],
        out_specs=[blk(0), pl.BlockSpec((None, ncb, HEAD, HEAD), lambda h, i: (h, tmap(i), 0, 0))],
        out_shape=[_sds((t, D_MODEL), F32), _sds((N_HEADS, t // HG_CHUNK, HEAD, HEAD), F32)],
        scratch_shapes=[pltpu.VMEM((HEAD, HEAD), F32)] + [pltpu.VMEM((HG_BLOCK, HEAD), F32)] * 4,
        compiler_params=_params(2))(z, z, z, lb_row)


def _hgrn_bwd(name, z, d_o, states, lb_row, rev, dz):
    t = z.shape[0]
    nb = t // HG_BLOCK
    ncb = HG_BLOCK // HG_CHUNK
    nsb = HG_CHUNK // HG_SUB
    zf0 = 16 if rev else 8

    def tmap(i):
        return i if rev else nb - 1 - i

    def body(zq_ref, zf_ref, zi_ref, do_ref, st_ref, lb_ref, dz_in, dq_ref, dv_ref, dzf_ref, dlb_ref,
             dstate, q_s, k_s, v_s, b_s, dq_s, dk_s, dv_s, db_s):
        del dz_in
        first = pl.program_id(1) == 0

        @pl.when(first)
        def _():
            dstate[...] = jnp.zeros_like(dstate)

        lb = lb_ref[...]
        zf = zf_ref[...]
        q, k, logf, f = _hg_gates(zq_ref[...], zf, lb)
        v = zi_ref[...]
        b_all = _cumsum_rows(logf, HG_CHUNK, rev)
        b_s[...] = b_all
        order = range(ncb) if rev else range(ncb - 1, -1, -1)
        safe = _anchors_are_safe(b_s, ncb, rev)

        @pl.when(safe)
        def _():
            cmask = _tri_mask(HG_CHUNK, rev)
            row_i = lax.broadcasted_iota(jnp.int32, (HG_CHUNK, HEAD), 0)
            dst = dstate[...]
            dq_l, dk_l, dv_l, db_l = [None] * ncb, [None] * ncb, [None] * ncb, [None] * ncb
            for c in order:
                base = c * HG_CHUNK
                rows = slice(base, base + HG_CHUNK)
                last = base if rev else base + HG_CHUNK - 1
                mid = base + HG_CHUNK // 2
                q_c, k_c, v_c, b, do_c = q[rows, :], k[rows, :], v[rows, :], b_all[rows, :], do_ref[rows, :]
                bl, bm = b_s[last:last + 1, :], b_s[mid:mid + 1, :]
                st0 = st_ref[c]
                e, el, ebl = jnp.exp(b), jnp.exp(bl), jnp.exp(bl - b)
                e_q, e_k = jnp.exp(b - bm), jnp.exp(bm - b)
                q_t, k_t = q_c * e_q, k_c * e_k
                a = jnp.where(cmask, _nt(q_t, k_t), 0.0)
                da = jnp.where(cmask, _nt(do_c, v_c), 0.0)
                dq = _nn(do_c, st0) * e + e_q * _dot_f32(da, k_t, 1, 0)
                dk_inter = _nn(v_c, dst) * ebl
                dk = dk_inter + e_k * _dot_f32(da, q_t, 0, 0)
                dv_l[c] = _nt(k_c * ebl, dst) + _tn(a, do_c)
                extra = el * _rowsum(dst * st0) + _rowsum(k_c * dk_inter)
                dst = _tn(do_c, q_c * e) + dst * el
                db_l[c] = q_c * dq - k_c * dk + jnp.where(row_i == last - base, extra, 0.0)
                dq_l[c], dk_l[c] = dq, dk
            dstate[...] = dst
            dq_s[...] = jnp.concatenate(dq_l, axis=0)
            dk_s[...] = jnp.concatenate(dk_l, axis=0)
            dv_s[...] = jnp.concatenate(dv_l, axis=0)
            db_s[...] = jnp.concatenate(db_l, axis=0)

        @pl.when(jnp.logical_not(safe))
        def _():
            q_s[...] = q
            k_s[...] = k
            v_s[...] = v
            mask = _tri_mask(HG_SUB, rev)
            for c in order:
                base = c * HG_CHUNK
                rows = slice(base, base + HG_CHUNK)
                last = base if rev else base + HG_CHUNK - 1
                st0 = st_ref[c]
                dst1 = dstate[...]
                b = b_s[rows, :]
                bl = b_s[last:last + 1, :]
                e = jnp.exp(b)
                el = jnp.exp(bl)
                ebl = jnp.exp(bl - b)
                q_c, k_c, v_c, do_c = q_s[rows, :], k_s[rows, :], v_s[rows, :], do_ref[rows, :]
                kd = k_c * ebl
                dq_s[rows, :] = _nn(do_c, st0) * e
                dk_inter = _nn(v_c, dst1) * ebl
                dk_s[rows, :] = dk_inter
                dv_s[rows, :] = _nt(kd, dst1)
                extra = el * _rowsum(dst1 * st0) + _rowsum(k_c * dk_inter)
                dstate[...] = _tn(do_c, q_c * e) + dst1 * el
                for i_sub in range(nsb):
                    r0, r1, e0, e1, anchor = _sub_ranges(base, i_sub, rev)
                    q_i, k_i, b_i, v_i, do_i = q_s[r0:r1, :], k_s[r0:r1, :], b_s[r0:r1, :], v_s[r0:r1, :], do_ref[r0:r1, :]
                    decay = jnp.exp(jnp.minimum(b_i[:, None, :] - b_i[None, :, :], 0.0))
                    a_d = jnp.where(mask, jnp.sum(q_i[:, None, :] * k_i[None, :, :] * decay, axis=-1), 0.0)
                    da_d = jnp.where(mask, _nt(do_i, v_i), 0.0)
                    wgt = da_d[:, :, None] * decay
                    dq_s[r0:r1, :] += jnp.sum(wgt * k_i[None, :, :], axis=1)
                    dk_s[r0:r1, :] += jnp.sum(wgt * q_i[:, None, :], axis=0)
                    dv_s[r0:r1, :] += _tn(a_d, do_i)
                    if e1 > e0:
                        anc = b_s[anchor:anchor + 1, :]
                        e_q = jnp.exp(b_i - anc)
                        e_k = jnp.exp(anc - b_s[e0:e1, :])
                        q_t = q_i * e_q
                        k_t = k_s[e0:e1, :] * e_k
                        a_o = _nt(q_t, k_t)
                        da_o = _nt(do_i, v_s[e0:e1, :])
                        dq_s[r0:r1, :] += e_q * _nn(da_o, k_t)
                        dk_s[e0:e1, :] += e_k * _tn(da_o, q_t)
                        dv_s[e0:e1, :] += _tn(a_o, do_i)
                db_s[rows, :] = q_c * dq_s[rows, :] - k_c * dk_s[rows, :]
                db_s[last:last + 1, :] += extra

        dlogf = _cumsum_rows(db_s[...], HG_CHUNK, not rev)
        s_neg = _sigmoid(-zf)
        df = jnp.where(f > jnp.finfo(F32).tiny, dlogf / f, 0.0)
        dfk = df - dk_s[...]
        dzf_ref[...] = ((1.0 - lb) * _sigmoid(zf) * s_neg * dfk).astype(dzf_ref.dtype)
        dlb = _rowsum(s_neg * dfk)

        @pl.when(first)
        def _():
            dlb_ref[...] = dlb

        @pl.when(jnp.logical_not(first))
        def _():
            dlb_ref[...] += dlb

        dq_ref[...] = dq_s[...]
        dv_ref[...] = dv_s[...]

    blk = lambda off: pl.BlockSpec((HG_BLOCK, HEAD), lambda h, i: (tmap(i), off + h))
    vec = pl.BlockSpec((1, HEAD), lambda h, i: (0, h))
    return _pcall(
        body, name=name, grid=(N_HEADS, nb),
        in_specs=[blk(0), blk(zf0), blk(24), blk(0),
                  pl.BlockSpec((None, ncb, HEAD, HEAD), lambda h, i: (h, tmap(i), 0, 0)), vec, ANY],
        out_specs=[blk(0), blk(0), blk(zf0), vec],
        out_shape=[_sds((t, D_MODEL), F32), _sds((t, D_MODEL), F32), _sds(dz.shape, dz.dtype), _sds((1, D_MODEL), F32)],
        input_output_aliases={6: 2},
        scratch_shapes=[pltpu.VMEM((HEAD, HEAD), F32)] + [pltpu.VMEM((HG_BLOCK, HEAD), F32)] * 8,
        compiler_params=_params(2))(z, z, z, d_o, states, lb_row, dz)


def _lower_bounds(name, gamma):
    def body(g_ref, o_ref):
        g0, g1 = g_ref[0:1, :], g_ref[1:2, :]
        m = jnp.maximum(g0, g1)
        e0, e1 = jnp.exp(g0 - m), jnp.exp(g1 - m)
        s0, s1 = e0 / (e0 + e1), e1 / (e0 + e1)
        o_ref[0:1, :] = s0 - s0
        o_ref[1:2, :] = (s0 + s1) - s0

    return _pcall(body, name=name, out_shape=_sds(gamma.shape, F32))(gamma)


def _lower_bounds_bwd(name, gamma, dlb):
    def body(g_ref, d_ref, o_ref):
        g0, g1 = g_ref[0:1, :], g_ref[1:2, :]
        m = jnp.maximum(g0, g1)
        e0, e1 = jnp.exp(g0 - m), jnp.exp(g1 - m)
        s0, s1 = e0 / (e0 + e1), e1 / (e0 + e1)
        d0, d1 = d_ref[0:1, :], d_ref[1:2, :]
        ds0 = (d0 + d1) - (d0 + d1)
        ds1 = d1
        inner = s0 * ds0 + s1 * ds1
        o_ref[0:1, :] = s0 * (ds0 - inner)
        o_ref[1:2, :] = s1 * (ds1 - inner)

    return _pcall(body, name=name, out_shape=_sds(gamma.shape, F32))(gamma, dlb)


def _heads(x):
    return [x[:, h * HEAD:(h + 1) * HEAD] for h in range(N_HEADS)]


def _hg_post_fwd(name, o_f, o_b, z, gain):
    t = z.shape[0]
    tile = 256

    def fn(of, ob, zg, g):
        outs = []
        for o_h, zg_h, g_h in zip(_heads(of + ob), _heads(zg), _heads(g)):
            outs.append(_rms(o_h, g_h) * (zg_h * _sigmoid(zg_h)))
        return (jnp.concatenate(outs, axis=1),)

    return _rows(name, fn, t // tile, [o_f, o_b, z, gain],
                 [_tile_spec(tile, D_MODEL), _tile_spec(tile, D_MODEL), _tile_spec(tile, D_MODEL, 4), _whole((1, D_MODEL))],
                 [_sds((t, D_MODEL), MXU_DTYPE)], [_tile_spec(tile, D_MODEL)])[0]


def _hg_post_bwd(name, da, o_f, o_b, z, gain, dz):
    t = z.shape[0]
    tile = 256

    def fn(dav, of, ob, zg, g):
        d_o, dzg, dgain = [], [], []
        for da_h, o_h, zg_h, g_h in zip(_heads(dav), _heads(of + ob), _heads(zg), _heads(g)):
            s = _sigmoid(zg_h)
            r = lax.rsqrt(_mean_last(o_h * o_h) + EPS)
            oh = o_h * r
            dy = da_h * (zg_h * s)
            dzg.append(da_h * (oh * g_h) * _dsilu(zg_h, s))
            gdy = dy * g_h
            d_o.append(r * (gdy - oh * _mean_last(gdy * oh)))
            dgain.append(_rowsum(dy * oh))
        return jnp.concatenate(d_o, axis=1), jnp.concatenate(dzg, axis=1), jnp.concatenate(dgain, axis=1)

    return _rows(name, fn, t // tile, [da, o_f, o_b, z, gain],
                 [_tile_spec(tile, D_MODEL)] * 3 + [_tile_spec(tile, D_MODEL, 4), _whole((1, D_MODEL))],
                 [_sds((t, D_MODEL), F32), _sds(dz.shape, dz.dtype), _sds((1, D_MODEL), F32)],
                 [_tile_spec(tile, D_MODEL), _tile_spec(tile, D_MODEL, 4), _whole((1, D_MODEL))],
                 n_red=1, into=[(dz, 1)])


def _hg_gate_bwd(name, dq_f, dq_b, dv_f, dv_b, z, dz):
    t = z.shape[0]
    tile = 512

    def fq(a, b, zq):
        return ((a + b) * _dsilu(zq, _sigmoid(zq)),)

    dz = _rows(name + "_q", fq, t // tile, [dq_f, dq_b, z], [_tile_spec(tile, D_MODEL)] * 3,
               [_sds(dz.shape, dz.dtype)], [_tile_spec(tile, D_MODEL, 0)], into=[(dz, 0)])[0]
    dz = _rows(name + "_v", lambda a, b: (a + b,), t // tile, [dv_f, dv_b], [_tile_spec(tile, D_MODEL)] * 2,
               [_sds(dz.shape, dz.dtype)], [_tile_spec(tile, D_MODEL, 3)], into=[(dz, 0)])[0]
    return dz


def _sg_norm(zv, ln_g, ln_b):
    gv = _gelu(zv)
    xc = gv - _mean_last(gv)
    rstd = lax.rsqrt(_mean_last(xc * xc) + EPS)
    xhat = xc * rstd
    return xhat * ln_g + ln_b, xhat, rstd


def _lane_lo():
    return lax.broadcasted_iota(jnp.int32, (SG_CHUNK, LANES), 1) < (LANES // 2)


SG_TILE = 512


def _sgu_fwd(name, z, w, bias_t, ln_g, ln_b):
    t = z.shape[0]

    def fn(zu, zv, wv, bt, lg, lb):
        u = _gelu(zu)
        vn, _, _ = _sg_norm(zv, lg, lb)
        lo = _lane_lo()
        out_rows = []
        for c in range(SG_TILE // SG_CHUNK):
            rs = slice(c * SG_CHUNK, (c + 1) * SG_CHUNK)
            cols = []
            for j in range(SG_WIDTH // LANES):
                cs = slice(j * LANES, (j + 1) * LANES)
                vb = vn[rs, cs]
                sg = jnp.where(lo, _nn(wv[2 * j], vb), _nn(wv[2 * j + 1], vb)) + bt[:, cs]
                cols.append(u[rs, cs] * sg)
            out_rows.append(jnp.concatenate(cols, axis=1))
        return (jnp.concatenate(out_rows, axis=0),)

    return _rows(name, fn, t // SG_TILE, [z, z, w, bias_t, ln_g, ln_b],
                 [_tile_spec(SG_TILE, SG_WIDTH, 10), _tile_spec(SG_TILE, SG_WIDTH, 11), _whole(w.shape),
                  _whole(bias_t.shape), _whole((1, SG_WIDTH)), _whole((1, SG_WIDTH))],
                 [_sds((t, SG_WIDTH), MXU_DTYPE)], [_tile_spec(SG_TILE, SG_WIDTH)])[0]


def _sgu_bwd(name, dbo, z, w, bias_t, ln_g, ln_b, dz):
    t = z.shape[0]
    n_grp = w.shape[0]

    def fn(dbov, zu, zv, wv, bt, lg, lb):
        u = _gelu(zu)
        vn, xhat, rstd = _sg_norm(zv, lg, lb)
        lo = _lane_lo()
        dw = [None] * n_grp
        dsg_sum = None
        du_rows, dvn_rows = [], []
        for c in range(SG_TILE // SG_CHUNK):
            rs = slice(c * SG_CHUNK, (c + 1) * SG_CHUNK)
            du_cols, dvn_cols, dsg_cols = [], [], []
            for j in range(SG_WIDTH // LANES):
                cs = slice(j * LANES, (j + 1) * LANES)
                vb = vn[rs, cs]
                sg = jnp.where(lo, _nn(wv[2 * j], vb), _nn(wv[2 * j + 1], vb)) + bt[:, cs]
                du_cols.append(dbov[rs, cs] * sg)
                dsg = dbov[rs, cs] * u[rs, cs]
                dsg_cols.append(dsg)
                d0 = _nt(jnp.where(lo, dsg, 0.0), vb)
                d1 = _nt(jnp.where(lo, 0.0, dsg), vb)
                dw[2 * j] = d0 if dw[2 * j] is None else dw[2 * j] + d0
                dw[2 * j + 1] = d1 if dw[2 * j + 1] is None else dw[2 * j + 1] + d1
                dvn_cols.append(jnp.where(lo, _tn(wv[2 * j], dsg), _tn(wv[2 * j + 1], dsg)))
            du_rows.append(jnp.concatenate(du_cols, axis=1))
            dvn_rows.append(jnp.concatenate(dvn_cols, axis=1))
            dsg_c = jnp.concatenate(dsg_cols, axis=1)
            dsg_sum = dsg_c if dsg_sum is None else dsg_sum + dsg_c
        du = jnp.concatenate(du_rows, axis=0)
        dvn = jnp.concatenate(dvn_rows, axis=0)
        dxhat = dvn * lg
        dgv = rstd * (dxhat - _mean_last(dxhat) - xhat * _mean_last(dxhat * xhat))
        dzuv = jnp.concatenate([du * _dgelu(zu), dgv * _dgelu(zv)], axis=1)
        return dzuv, jnp.stack(dw, axis=0), dsg_sum, _rowsum(dvn * xhat), _rowsum(dvn)

    return _rows(name, fn, t // SG_TILE, [dbo, z, z, w, bias_t, ln_g, ln_b],
                 [_tile_spec(SG_TILE, SG_WIDTH), _tile_spec(SG_TILE, SG_WIDTH, 10), _tile_spec(SG_TILE, SG_WIDTH, 11),
                  _whole(w.shape), _whole(bias_t.shape), _whole((1, SG_WIDTH)), _whole((1, SG_WIDTH))],
                 [_sds(dz.shape, dz.dtype), _sds(w.shape, F32), _sds((SG_CHUNK, SG_WIDTH), F32),
                  _sds((1, SG_WIDTH), F32), _sds((1, SG_WIDTH), F32)],
                 [_tile_spec(SG_TILE, 2 * SG_WIDTH, 5), _whole(w.shape), _whole((SG_CHUNK, SG_WIDTH)),
                  _whole((1, SG_WIDTH)), _whole((1, SG_WIDTH))],
                 n_red=4, into=[(dz, 0)])


def _merge_bwd(name, dm, pa, pb, z):
    t = z.shape[0]
    tile = 256

    def fn(d, a, b, ga, gb):
        sa, sb = _sigmoid(ga), _sigmoid(gb)
        dgate = jnp.concatenate([d * a * sa * (1.0 - sa), d * b * sb * (1.0 - sb)], axis=1)
        return d * sa, d * sb, dgate

    return _rows(name, fn, t // tile, [dm, pa, pb, z, z],
                 [_tile_spec(tile, D_MODEL)] * 3 + [_tile_spec(tile, D_MODEL, 6), _tile_spec(tile, D_MODEL, 7)],
                 [_sds((t, D_MODEL), MXU_DTYPE), _sds((t, D_MODEL), MXU_DTYPE), _sds((t, N_IN), MXU_DTYPE)],
                 [_tile_spec(tile, D_MODEL), _tile_spec(tile, D_MODEL), _tile_spec(tile, 2 * D_MODEL, 3)])


def _ple_bwd(name, dx, pe, gz):
    t, d = dx.shape
    tile = 512

    def fn(dv, p, g):
        s = _sigmoid(g)
        return dv * s, dv * p * s * (1.0 - s)

    return _rows(name, fn, t // tile, [dx, pe, gz], [_tile_spec(tile, d)] * 3, [_sds((t, d), MXU_DTYPE)] * 2,
                 [_tile_spec(tile, d)] * 2)


def _loss_bwd(name, y, target):
    t, d = y.shape
    tile = 512

    def fn(yv, tv):
        err = yv - tv
        return err * (1.0 / d), _rowsum(err * err)

    return _rows(name, fn, t // tile, [y, target], [_tile_spec(tile, d)] * 2, [_sds((t, d), F32), _sds((1, d), F32)],
                 [_tile_spec(tile, d), _whole((1, d))], n_red=1)


def _position():
    return lax.axis_index("x"), lax.axis_index("y"), lax.axis_index("c")


def _gather_comm(shards, bufs, l0, nl):
    n = len(shards)
    if bufs is None:
        bufs = [_sds((N_CHIPS,) + s.shape, s.dtype) for s in shards]

    def make(w_refs, out_refs, sems):
        send_sems, recv_sems, local_sems = sems
        x, y, c = _position()
        me = 2 * x + y
        sibling = (x, y, 1 - c)
        chips = [(1 - x, y), (x, 1 - y), (1 - x, 1 - y)]

        def half(ref, cc):
            rows = ref.shape[1] // 2
            return ref.at[pl.ds(l0, nl), pl.ds(cc * rows, rows)]

        def copy(i, k, src, chip, cc, to):
            return pltpu.make_async_remote_copy(
                src_ref=src, dst_ref=half(out_refs[i].at[chip], cc), send_sem=send_sems.at[6 * i + k],
                recv_sem=recv_sems.at[6 * i + k], device_id=to, device_id_type=MESH)

        def local(i):
            return pltpu.make_async_copy(w_refs[i].at[pl.ds(l0, nl)], out_refs[i].at[me, pl.ds(l0, nl)], local_sems.at[i])

        def sends():
            return [copy(i, j, half(w_refs[i], c), me, c, (px, py, c))
                    for i in range(n) for j, (px, py) in enumerate(chips)]

        def start():
            for i in range(n):
                local(i).start()
            for cp in sends():
                cp.start()

        def finish():
            passed = []
            for i in range(n):
                for j, (px, py) in enumerate(chips):
                    chip = 2 * px + py
                    copy(i, j, half(w_refs[i], c), chip, c, (px, py, c)).wait_recv()
                    fwd = copy(i, 3 + j, half(out_refs[i].at[chip], c), chip, c, sibling)
                    fwd.start()
                    passed.append(fwd)
            for i in range(n):
                for j, (px, py) in enumerate(chips):
                    copy(i, 3 + j, half(w_refs[i], c), 2 * px + py, 1 - c, sibling).wait_recv()
            for cp in sends() + passed:
                cp.wait_send()
            for i in range(n):
                local(i).wait()

        return start, finish

    sems = [pltpu.SemaphoreType.DMA((6 * n,)), pltpu.SemaphoreType.DMA((6 * n,)), pltpu.SemaphoreType.DMA((n,))]
    return _Comm(shards, bufs, sems, make)


def _exchange_comm(grads, bufs, l0, nl):
    n = len(grads)
    if bufs is None:
        bufs = [_sds((N_DEV,) + g.shape[1:], g.dtype) for g in grads]

    def make(g_refs, out_refs, sems):
        send_sems, recv_sems, local_sems = sems
        x, y, c = _position()
        me = 2 * x + y
        sibling = (x, y, 1 - c)
        chips = [(1 - x, y), (x, 1 - y), (1 - x, 1 - y)]

        def lay(ref):
            return ref.at[pl.ds(l0, nl)]

        def copy(i, k, src, slot, to):
            return pltpu.make_async_remote_copy(
                src_ref=src, dst_ref=lay(out_refs[i].at[slot]), send_sem=send_sems.at[7 * i + k],
                recv_sem=recv_sems.at[7 * i + k], device_id=to, device_id_type=MESH)

        def local(i):
            return pltpu.make_async_copy(lay(g_refs[i].at[me]), lay(out_refs[i].at[2 * me + c]), local_sems.at[i])

        def first():
            cps = []
            for i in range(n):
                cps.append(copy(i, 0, lay(g_refs[i].at[me]), 2 * me + c, sibling))
                for j, (px, py) in enumerate(chips):
                    cps.append(copy(i, 1 + j, lay(g_refs[i].at[2 * px + py]), 2 * me + c, (px, py, c)))
            return cps

        def start():
            for i in range(n):
                local(i).start()
            for cp in first():
                cp.start()

        def finish():
            passed = []
            for i in range(n):
                for j, (px, py) in enumerate(chips):
                    slot = 2 * (2 * px + py) + c
                    copy(i, 1 + j, lay(g_refs[i].at[me]), slot, (px, py, c)).wait_recv()
                    fwd = copy(i, 4 + j, lay(out_refs[i].at[slot]), slot, sibling)
                    fwd.start()
                    passed.append(fwd)
            for i in range(n):
                copy(i, 0, lay(g_refs[i].at[me]), 2 * me + (1 - c), sibling).wait_recv()
                for j, (px, py) in enumerate(chips):
                    copy(i, 4 + j, lay(g_refs[i].at[me]), 2 * (2 * px + py) + (1 - c), sibling).wait_recv()
            for cp in first() + passed:
                cp.wait_send()
            for i in range(n):
                local(i).wait()

        return start, finish

    sems = [pltpu.SemaphoreType.DMA((7 * n,)), pltpu.SemaphoreType.DMA((7 * n,)), pltpu.SemaphoreType.DMA((n,))]
    return _Comm(grads, bufs, sems, make)


def _all_reduce_small(packed):
    rows = packed.shape[0]

    def body(x_ref, sum_ref, slots, send_sems, recv_sems, local_sem):
        x, y, c = _position()
        me = 4 * x + 2 * y + c
        mine = pltpu.make_async_copy(x_ref, slots.at[me], local_sem)
        mine.start()
        sends = []
        for k in range(1, N_DEV):
            peer = (x ^ (k >> 2), y ^ ((k >> 1) & 1), c ^ (k & 1))
            cp = pltpu.make_async_remote_copy(src_ref=x_ref, dst_ref=slots.at[me], send_sem=send_sems.at[k - 1],
                                              recv_sem=recv_sems.at[k - 1], device_id=peer, device_id_type=MESH)
            cp.start()
            sends.append(cp)
        for k in range(1, N_DEV):
            px, py, pc = x ^ (k >> 2), y ^ ((k >> 1) & 1), c ^ (k & 1)
            pltpu.make_async_remote_copy(src_ref=x_ref, dst_ref=slots.at[4 * px + 2 * py + pc], send_sem=send_sems.at[k - 1],
                                         recv_sem=recv_sems.at[k - 1], device_id=(px, py, pc), device_id_type=MESH).wait_recv()
        for cp in sends:
            cp.wait_send()
        mine.wait()
        total = slots[0]
        for d in range(1, N_DEV):
            total = total + slots[d]
        sum_ref[...] = total

    vmem = pl.BlockSpec(memory_space=pltpu.VMEM)
    return _pcall(
        body, name="all_reduce_small", in_specs=[vmem], out_specs=vmem, out_shape=_sds(packed.shape, F32),
        scratch_shapes=[pltpu.VMEM((N_DEV, rows, LANES), F32), pltpu.SemaphoreType.DMA((N_DEV - 1,)),
                        pltpu.SemaphoreType.DMA((N_DEV - 1,)), pltpu.SemaphoreType.DMA],
        compiler_params=pltpu.CompilerParams(vmem_limit_bytes=VMEM_LIMIT_BYTES),
    )(packed)


def _adamw(w, g, m, v):
    m = ADAM_B1 * m + (1.0 - ADAM_B1) * g
    v = ADAM_B2 * v + (1.0 - ADAM_B2) * (g * g)
    m_hat = m / (1.0 - ADAM_B1 ** ADAM_STEP)
    v_hat = v / (1.0 - ADAM_B2 ** ADAM_STEP)
    delta = -ADAM_LR * (m_hat / (jnp.sqrt(v_hat) + ADAM_EPS) + ADAM_WD * w)
    return delta, m, v


def _adam_sharded(name, parts, w, m, v):
    shape = w.shape
    cols = shape[-1]
    rows = w.size // cols
    tile = 8
    while tile * 2 * cols <= ADAM_TILE_ELEMS and rows % (tile * 2) == 0:
        tile *= 2

    def fn(p, wv, mv, vv):
        g = p[0].astype(F32)
        for d in range(1, N_DEV):
            g = g + p[d].astype(F32)
        return (g,) + _adamw(wv, g, mv, vv)

    two_d = lambda a: a.reshape(rows, cols)
    outs = _rows(name, fn, rows // tile, [parts.reshape(N_DEV, rows, cols), two_d(w), two_d(m), two_d(v)],
                 [pl.BlockSpec((N_DEV, tile, cols), lambda i: (0, i, 0))] + [_tile_spec(tile, cols)] * 3,
                 [_sds((rows, cols), F32)] * 4, [_tile_spec(tile, cols)] * 4)
    return [o.reshape(shape) for o in outs]


def _adam_small(name, g, w, m, v):
    rows = g.shape[0]
    tile = rows // 2
    return _rows(name, lambda gv, wv, mv, vv: _adamw(wv, gv, mv, vv), rows // tile, [g, w, m, v],
                 [_tile_spec(tile, LANES)] * 4, [_sds(g.shape, F32)] * 3, [_tile_spec(tile, LANES)] * 3)


BIG = ("w_in", "w_a", "w_b", "w_out", "w_gate", "w_up", "w_down", "w_ple", "w_ple_gate")
SMALL = ("norm_mix_pre", "lb_gamma_fwd", "lb_gamma_bwd", "hg_norm", "sg_w", "sg_b", "sg_ln_g", "sg_ln_b",
         "norm_mix_post", "norm_ffn_pre", "norm_ffn_post")


def _with_comm(plan, tag, state, call):
    if tag not in plan:
        return call(None)
    keys, comm = plan[tag](state)
    res, bufs = call(comm)
    state.update(zip(keys, bufs))
    return res


def _layer_fwd(l, x, p_l, wg, sm, plan):
    sv = {"x": x}
    h = _rms_fwd(f"norm_mix_pre_l{l}", x, sm["norm_mix_pre"])
    z = _with_comm(plan, "proj_in", wg, lambda comm: _proj_in(h, wg["w_in"], l, comm=comm))
    o_f, st_f = _hgrn_fwd(f"hgrn_fwd_l{l}", z, sm["lb_fwd"], False)
    o_b, st_b = _hgrn_fwd(f"hgrn_rev_l{l}", z, sm["lb_bwd"], True)
    a_out = _hg_post_fwd(f"hg_post_l{l}", o_f, o_b, z, sm["hg_norm"])
    b_out = _sgu_fwd(f"sgu_l{l}", z, sm["sg_w"], sm["sg_bias_t"], sm["sg_ln_g"], sm["sg_ln_b"])
    pa = _proj_rows_sharded(f"proj_a_l{l}", a_out, wg["w_a"], l, F32)
    pb = _proj_cols256(f"proj_b_l{l}", b_out, wg["w_b"], l)
    merged, mix = _with_comm(plan, "merge_proj_out", wg, lambda comm: _merge_proj_out(
        f"merge_proj_out_l{l}", pa, pb, z, wg["w_out"], l, comm=comm))
    x1, h2 = _resid_rms_norm_fwd(f"norm_mix_post_ffn_pre_l{l}", x, mix, sm["norm_mix_post"], sm["norm_ffn_pre"])
    gt, up, act = _with_comm(plan, "proj_gate_up", wg, lambda comm: _proj_gate_up(
        f"proj_gate_up_l{l}", h2, wg["w_gate"], wg["w_up"], l, comm=comm))
    ff = _with_comm(plan, "proj_down", wg, lambda comm: _proj_ffn_out(f"proj_down_l{l}", act, wg["w_down"], l, comm=comm))
    x2 = _resid_rms_fwd(f"norm_ffn_post_l{l}", x1, ff, sm["norm_ffn_post"])
    pe = _proj_cols256(f"proj_ple_l{l}", p_l, wg["w_ple"], l)
    gz, x3 = _proj_ple_gate_ple(f"proj_ple_gate_ple_l{l}", x2, wg["w_ple_gate"], pe, l)
    sv.update(h=h, z=z, o_f=o_f, o_b=o_b, st_f=st_f, st_b=st_b, a_out=a_out, b_out=b_out, pa=pa, pb=pb,
              merged=merged, mix=mix, x1=x1, h2=h2, gt=gt, up=up, act=act, ff=ff, x2=x2, pe=pe, gz=gz, p=p_l)
    return x3, sv


def _layer_bwd(l, dx3, sv, wg, sm, gw, parts, plan):
    t = dx3.shape[0]
    nt = t // TT
    sg = {}

    def wgrad(key, *a, **k):
        gw[key] = _wgrad(f"grad_{key}_l{l}", *a, l=l, into=gw.get(key), **k)

    row = lambda width: _bs((TT, width), lambda j, i: (i, 0))
    row_j = lambda width: _bs((TT, width), lambda j, i: (i, j))
    ffn_j = _bs((None, TT, FFN_SHARD), lambda j, i: (j, i, 0))
    blk_j = lambda shape: (tuple(shape), lambda j, i: (j, l, 0, 0))

    dpe, dgz = _ple_bwd(f"ple_bwd_l{l}", dx3, sv["pe"], sv["gz"])
    wgrad("w_ple", sv["p"], row(PLE_DIM), dpe, row_j(256), (PLE_DIM, 256), blk_j((PLE_DIM, 256)), N_CHIPS)
    wgrad("w_ple_gate", sv["x2"], row_j(256), dgz, row(D_MODEL), (256, D_MODEL), blk_j((256, D_MODEL)), N_CHIPS)
    dx2 = _proj_rows_sharded_t(f"proj_ple_gate_t_l{l}", dgz, wg["w_ple_gate"], l, F32, add=dx3)

    dff, sg["norm_ffn_post"] = _rms_bwd(f"norm_ffn_post_bwd_l{l}", sv["ff"], sm["norm_ffn_post"], dx2, None, MXU_DTYPE)
    dgt, dup = _with_comm(plan, "proj_down_t", parts, lambda comm: _proj_down_t_swiglu_bwd(
        f"proj_down_t_swiglu_bwd_l{l}", dff, wg["w_down"], sv["gt"], sv["up"], l, comm=comm))
    wgrad("w_down", sv["act"], ffn_j, dff, row(D_MODEL), (FFN_SHARD, D_MODEL), blk_j((FFN_SHARD, D_MODEL)), N_CHIPS)
    dh2 = _with_comm(plan, "proj_gate_up_t", parts, lambda comm: _proj_ffn_in_t(
        f"proj_gate_up_t_l{l}", [(dgt, wg["w_gate"]), (dup, wg["w_up"])], l, comm=comm))
    wgrad("w_gate", sv["h2"], row(D_MODEL), dgt, ffn_j, (D_MODEL, FFN_SHARD), blk_j((D_MODEL, FFN_SHARD)), N_CHIPS)
    wgrad("w_up", sv["h2"], row(D_MODEL), dup, ffn_j, (D_MODEL, FFN_SHARD), blk_j((D_MODEL, FFN_SHARD)), N_CHIPS)
    dx1, sg["norm_ffn_pre"] = _rms_bwd(f"norm_ffn_pre_bwd_l{l}", sv["x1"], sm["norm_ffn_pre"], dh2, dx2, F32)

    dmix, sg["norm_mix_post"] = _rms_bwd(f"norm_mix_post_bwd_l{l}", sv["mix"], sm["norm_mix_post"], dx1, None, MXU_DTYPE)
    dmerged = _proj_rows_sharded_t(f"proj_out_t_l{l}", dmix, wg["w_out"], l, F32)
    wgrad("w_out", sv["merged"], row_j(256), dmix, row(D_MODEL), (256, D_MODEL), blk_j((256, D_MODEL)), N_CHIPS)
    dpa, dpb, dz = _merge_bwd(f"merge_bwd_l{l}", dmerged, sv["pa"], sv["pb"], sv["z"])
    da = _proj_rows_sharded_t(f"proj_a_t_l{l}", dpa, wg["w_a"], l, F32)
    wgrad("w_a", sv["a_out"], row_j(256), dpa, row(D_MODEL), (256, D_MODEL), blk_j((256, D_MODEL)), N_CHIPS)
    dbo = _proj_cols256_t(f"proj_b_t_l{l}", dpb, wg["w_b"], l)
    wgrad("w_b", sv["b_out"], row(SG_WIDTH), dpb, row_j(256), (SG_WIDTH, 256), blk_j((SG_WIDTH, 256)), N_CHIPS)

    dz, sg["sg_w"], dsg_sum, sg["sg_ln_g"], sg["sg_ln_b"] = _sgu_bwd(
        f"sgu_bwd_l{l}", dbo, sv["z"], sm["sg_w"], sm["sg_bias_t"], sm["sg_ln_g"], sm["sg_ln_b"], dz)
    sg["sg_b"] = dsg_sum.reshape(SG_CHUNK, N_HEADS, SG_WIDTH // N_HEADS).sum(axis=-1).T
    d_o, dz, sg["hg_norm"] = _hg_post_bwd(f"hg_post_bwd_l{l}", da, sv["o_f"], sv["o_b"], sv["z"], sm["hg_norm"], dz)
    dq_f, dv_f, dz, sg["lb_fwd"] = _hgrn_bwd(f"hgrn_fwd_bwd_l{l}", sv["z"], d_o, sv["st_f"], sm["lb_fwd"], False, dz)
    dq_b, dv_b, dz, sg["lb_bwd"] = _hgrn_bwd(f"hgrn_rev_bwd_l{l}", sv["z"], d_o, sv["st_b"], sm["lb_bwd"], True, dz)
    dz = _hg_gate_bwd(f"hg_gate_bwd_l{l}", dq_f, dq_b, dv_f, dv_b, sv["z"], dz)

    dh = _with_comm(plan, "proj_in_t", parts, lambda comm: _proj_in_t(dz, wg["w_in"], l, comm=comm))
    gw["w_in"] = _wgrad(f"grad_w_in_l{l}", sv["h"], _bs((TM_WIDE, D_MODEL), lambda n, i: (i, 0)), dz,
                        _bs((TM_WIDE, 1024), lambda n, i: (i, n)), (D_MODEL, 2048),
                        ((D_MODEL, 1024), lambda n, i: (n // 2, l, 0, n % 2)), 8, l, gw.get("w_in"), tt=TM_WIDE)
    dx, sg["norm_mix_pre"] = _rms_bwd(f"norm_mix_pre_bwd_l{l}", sv["x"], sm["norm_mix_pre"], dh, dx1, F32)
    del nt
    return dx, gw, sg


def _pack(parts):
    return jnp.concatenate([a.reshape(-1, LANES) for a in parts], axis=0)


def _step(x, p, loss_target, w, m, v):
    x = x[0]
    target = loss_target[0]
    depth = w["w_in"].shape[0]

    assert depth == 2, "the exchanges below ride in layer 0's kernels and carry layer 1's data"
    shards = {k: w[k].astype(MXU_DTYPE) for k in BIG}
    rest_a = [k for k in BIG if k not in ("w_in", "w_gate", "w_up")]
    rest_b = ["w_gate", "w_up"]
    rest = rest_a + rest_b

    def gather(keys, l0, extend):
        return lambda wg: (keys, _gather_comm([shards[k] for k in keys], [wg[k] for k in keys] if extend else None, l0, 1))

    wg = {"w_in": _run_comm("gather_w_in_l0", gather(["w_in"], 0, False)(None)[1])[0]}
    fwd_plans = [{"proj_in": gather(rest, 0, False), "merge_proj_out": gather(["w_in"], 1, True),
                  "proj_gate_up": gather(rest_a, 1, True), "proj_down": gather(rest_b, 1, True)}, {}]
    lb_f = _lower_bounds("lower_bounds_fwd", w["lb_gamma_fwd"])
    lb_b = _lower_bounds("lower_bounds_bwd", w["lb_gamma_bwd"])

    def small_of(l):
        sm = {k: w[k][l:l + 1] for k in ("norm_mix_pre", "hg_norm", "sg_ln_g", "sg_ln_b", "norm_mix_post",
                                        "norm_ffn_pre", "norm_ffn_post")}
        sm["lb_fwd"], sm["lb_bwd"] = lb_f[l:l + 1], lb_b[l:l + 1]
        sm["sg_w"] = w["sg_w"][l]
        sm["sg_bias_t"] = jnp.repeat(w["sg_b"][l].T, SG_WIDTH // N_HEADS, axis=1)
        return sm

    saved = []
    h = x
    for l in range(depth):
        h, sv = _layer_fwd(l, h, p[l, 0], wg, small_of(l), fwd_plans[l])
        saved.append(sv)

    dy, sq_err = _loss_bwd("loss", h, target)
    gw, parts = {}, {}

    def exchange(keys, l0, extend):
        return lambda parts: (keys, _exchange_comm([gw[k] for k in keys], [parts[k] for k in keys] if extend else None, l0, 1))

    bwd_plans = [{"proj_down_t": exchange(["w_in"], 1, False), "proj_gate_up_t": exchange(rest, 1, False),
                  "proj_in_t": exchange(rest, 0, True)}, {}]
    small_grads = [None] * depth
    for l in reversed(range(depth)):
        dy, gw, small_grads[l] = _layer_bwd(l, dy, saved[l], wg, small_of(l), gw, parts, bwd_plans[l])
    parts["w_in"] = _run_comm("exchange_w_in_l0", exchange(["w_in"], 0, True)(parts)[1])[0]

    def stack(key):
        return jnp.concatenate([small_grads[l][key].reshape((1,) + w_shape[1:]) for l in range(depth)], axis=0)

    g_small = {}
    for key in SMALL:
        w_shape = w[key].shape
        if key == "lb_gamma_fwd":
            dlb = jnp.concatenate([small_grads[l]["lb_fwd"] for l in range(depth)], axis=0)
            g_small[key] = _lower_bounds_bwd("lower_bounds_fwd_bwd", w[key], dlb)
        elif key == "lb_gamma_bwd":
            dlb = jnp.concatenate([small_grads[l]["lb_bwd"] for l in range(depth)], axis=0)
            g_small[key] = _lower_bounds_bwd("lower_bounds_bwd_bwd", w[key], dlb)
        else:
            g_small[key] = stack(key)

    packed = _pack([g_small[k] for k in SMALL] + [sq_err])
    summed = _all_reduce_small(packed)
    n_small_rows = sum(w[k].size for k in SMALL) // LANES
    loss = 0.5 * jnp.sum(summed[n_small_rows:]) / D_MODEL

    g_rows = summed[:n_small_rows]
    d_rows, m_rows, v_rows = _adam_small("adamw_small", g_rows, _pack([w[k] for k in SMALL]),
                                         _pack([m[k] for k in SMALL]), _pack([v[k] for k in SMALL]))
    out = {}
    off = 0
    for key in SMALL:
        n_rows = w[key].size // LANES
        sl = slice(off, off + n_rows)
        out[key] = tuple(a[sl].reshape(w[key].shape) for a in (g_rows, d_rows, m_rows, v_rows))
        off += n_rows

    for key in BIG:
        out[key] = tuple(_adam_sharded(f"adamw_{key}", parts[key], w[key], m[key], v[key]))
    return loss, dy[None], out


WEIGHTS = ("norm_mix_pre", "w_in", "lb_gamma_fwd", "lb_gamma_bwd", "hg_norm", "sg_w", "sg_b", "sg_ln_g", "sg_ln_b",
           "w_a", "w_b", "w_out", "norm_mix_post", "norm_ffn_pre", "w_gate", "w_up", "w_down", "norm_ffn_post",
           "w_ple", "w_ple_gate")


def kernel(x, p, norm_mix_pre, w_in, lb_gamma_fwd, lb_gamma_bwd, hg_norm, sg_w, sg_b, sg_ln_g, sg_ln_b, w_a, w_b, w_out, norm_mix_post, norm_ffn_pre, w_gate, w_up, w_down, norm_ffn_post, w_ple, w_ple_gate, loss_target, m_norm_mix_pre, m_w_in, m_lb_gamma_fwd, m_lb_gamma_bwd, m_hg_norm, m_sg_w, m_sg_b, m_sg_ln_g, m_sg_ln_b, m_w_a, m_w_b, m_w_out, m_norm_mix_post, m_norm_ffn_pre, m_w_gate, m_w_up, m_w_down, m_norm_ffn_post, m_w_ple, m_w_ple_gate, v_norm_mix_pre, v_w_in, v_lb_gamma_fwd, v_lb_gamma_bwd, v_hg_norm, v_sg_w, v_sg_b, v_sg_ln_g, v_sg_ln_b, v_w_a, v_w_b, v_w_out, v_norm_mix_post, v_norm_ffn_pre, v_w_gate, v_w_up, v_w_down, v_norm_ffn_post, v_w_ple, v_w_ple_gate):
    w = dict(zip(WEIGHTS, (norm_mix_pre, w_in, lb_gamma_fwd, lb_gamma_bwd, hg_norm, sg_w, sg_b, sg_ln_g, sg_ln_b, w_a, w_b, w_out, norm_mix_post, norm_ffn_pre, w_gate, w_up, w_down, norm_ffn_post, w_ple, w_ple_gate)))
    m = dict(zip(WEIGHTS, (m_norm_mix_pre, m_w_in, m_lb_gamma_fwd, m_lb_gamma_bwd, m_hg_norm, m_sg_w, m_sg_b, m_sg_ln_g, m_sg_ln_b, m_w_a, m_w_b, m_w_out, m_norm_mix_post, m_norm_ffn_pre, m_w_gate, m_w_up, m_w_down, m_norm_ffn_post, m_w_ple, m_w_ple_gate)))
    v = dict(zip(WEIGHTS, (v_norm_mix_pre, v_w_in, v_lb_gamma_fwd, v_lb_gamma_bwd, v_hg_norm, v_sg_w, v_sg_b, v_sg_ln_g, v_sg_ln_b, v_w_a, v_w_b, v_w_out, v_norm_mix_post, v_norm_ffn_pre, v_w_gate, v_w_up, v_w_down, v_norm_ffn_post, v_w_ple, v_w_ple_gate)))
    loss, grad_x, out = _step(x, p, loss_target, w, m, v)
    res = [loss, grad_x]
    for i in range(4):
        res += [out[k][i] for k in WEIGHTS]
    return tuple(res)
```

```python
import functools

import jax
import jax.numpy as jnp
from jax import lax
from jax.experimental import pallas as pl
from jax.experimental.pallas import tpu as pltpu

F32 = jnp.float32
MXU_DTYPE = jnp.bfloat16
GRAD_EXCHANGE_DTYPE = jnp.bfloat16

D_MODEL = 1024
N_HEADS = 8
HEAD = 128
HG_CHUNK = 64
HG_SUB = 16
HG_BLOCK = 512
HG_SAFE_EXP = 80.0
SG_CHUNK = 128
SG_WIDTH = 512
FFN_SHARD = 704
PLE_DIM = 256
N_IN = 8192
N_CHIPS = 4
N_DEV = 8
EPS = 1e-6
LANES = 128
VMEM_LIMIT_BYTES = 56 * 2 ** 20

ADAM_LR = 0.001
ADAM_B1 = 0.9
ADAM_B2 = 0.999
ADAM_EPS = 1e-08
ADAM_WD = 0.01
ADAM_STEP = 10
ADAM_TILE_ELEMS = 128 * 1024

MESH = pl.DeviceIdType.MESH
ANY = pl.BlockSpec(memory_space=pl.ANY)


def _pcall(body, **kw):
    return pl.pallas_call(body, **kw)


def _params(n_axes):
    return pltpu.CompilerParams(dimension_semantics=("arbitrary",) * n_axes, vmem_limit_bytes=VMEM_LIMIT_BYTES)


def _dot(a, b, ca, cb):
    return lax.dot_general(a.astype(MXU_DTYPE), b.astype(MXU_DTYPE), (((ca,), (cb,)), ((), ())),
                           preferred_element_type=F32)


def _dot_f32(a, b, ca, cb):
    return lax.dot_general(a, b, (((ca,), (cb,)), ((), ())), precision=lax.Precision.HIGH,
                           preferred_element_type=F32)


def _nn(a, b):
    return _dot(a, b, 1, 0)


def _nt(a, b):
    return _dot(a, b, 1, 1)


def _tn(a, b):
    return _dot(a, b, 0, 0)


NN, NT, TN = (1, 0), (1, 1), (0, 0)


def _sigmoid(x):
    return jax.nn.sigmoid(x)


def _dsilu(x, s):
    return s * (1.0 + x * (1.0 - s))


_SQRT_HALF = 0.7071067811865476
_INV_SQRT_2PI = 0.3989422804014327


def _gelu(x):
    return 0.5 * x * (1.0 + lax.erf(x * _SQRT_HALF))


def _dgelu(x):
    return 0.5 * (1.0 + lax.erf(x * _SQRT_HALF)) + x * jnp.exp(-0.5 * x * x) * _INV_SQRT_2PI


def _mean_last(x):
    return jnp.mean(x, axis=-1, keepdims=True)


def _rowsum(x):
    return jnp.sum(x, axis=0, keepdims=True)


class _Comm:
    def __init__(self, ins, bufs, sem_shapes, make):
        self.ins, self.bufs, self.sem_shapes, self.make = list(ins), list(bufs), list(sem_shapes), make
        self.extends = not isinstance(self.bufs[0], jax.ShapeDtypeStruct)


def _hosted_call(name, compute, grid, args, in_specs, out_shapes, out_specs, scratch, aliases, comm):
    n_in, n_out, n_scr = len(args), len(out_shapes), len(scratch)
    if comm is None:
        def plain(*refs):
            compute(refs[:n_in], refs[n_in:n_in + n_out], refs[n_in + n_out:])

        res = _pcall(plain, name=name, grid=grid, in_specs=list(in_specs), out_specs=list(out_specs),
                     out_shape=list(out_shapes), scratch_shapes=list(scratch), input_output_aliases=dict(aliases),
                     compiler_params=_params(len(grid)))(*args)
        return list(res), []

    n_cin, n_buf = len(comm.ins), len(comm.bufs)
    all_args = list(args) + comm.ins + (comm.bufs if comm.extends else [])
    n_all = len(all_args)
    all_aliases = dict(aliases)
    if comm.extends:
        for j in range(n_buf):
            all_aliases[n_in + n_cin + j] = n_out + j
    buf_shapes = [_sds(b.shape, b.dtype) for b in comm.bufs]

    def body(*refs):
        outs = refs[n_all:n_all + n_out + n_buf]
        scr = refs[n_all + n_out + n_buf:]
        start, finish = comm.make(refs[n_in:n_in + n_cin], outs[n_out:], scr[n_scr:])
        first, last = None, None
        for axis, size in enumerate(grid):
            i = pl.program_id(axis)
            first = (i == 0) if first is None else jnp.logical_and(first, i == 0)
            last = (i == size - 1) if last is None else jnp.logical_and(last, i == size - 1)
        pl.when(first)(start)
        compute(refs[:n_in], outs[:n_out], scr[:n_scr])
        pl.when(last)(finish)

    res = _pcall(body, name=name, grid=grid, in_specs=list(in_specs) + [ANY] * (n_all - n_in),
                 out_specs=list(out_specs) + [ANY] * n_buf, out_shape=list(out_shapes) + buf_shapes,
                 scratch_shapes=list(scratch) + comm.sem_shapes, input_output_aliases=all_aliases,
                 compiler_params=_params(len(grid)))(*all_args)
    return list(res[:n_out]), list(res[n_out:])


def _run_comm(name, comm):
    n_cin, n_buf = len(comm.ins), len(comm.bufs)
    all_args = comm.ins + (comm.bufs if comm.extends else [])
    n_all = len(all_args)

    def body(*refs):
        start, finish = comm.make(refs[:n_cin], refs[n_all:n_all + n_buf], refs[n_all + n_buf:])
        start()
        finish()

    res = _pcall(body, name=name, in_specs=[ANY] * n_all, out_specs=[ANY] * n_buf,
                 out_shape=[_sds(b.shape, b.dtype) for b in comm.bufs], scratch_shapes=comm.sem_shapes,
                 input_output_aliases={n_cin + j: j for j in range(n_buf)} if comm.extends else {})(*all_args)
    return list(res)


def _mm(name, pairs, kind, out_shape, grid, in_specs, out_spec, *, reduce_axis=None, add=None,
        add_spec=None, into=None, prep=None, comm=None):
    n_pairs = len(pairs)
    has_add = add is not None
    staged = reduce_axis is not None and out_shape.dtype != F32

    def compute(in_refs, out_refs, scr):
        o_ref = out_refs[0]
        acc = None
        for i in range(n_pairs):
            a = in_refs[2 * i][...]
            b = in_refs[2 * i + 1][...]
            if prep is not None:
                b = prep(b)
            prod = _dot(a, b, *kind)
            acc = prod if acc is None else acc + prod
        if has_add:
            acc = acc + in_refs[2 * n_pairs][...]
        if reduce_axis is None:
            o_ref[...] = acc.astype(o_ref.dtype)
        else:
            r = pl.program_id(reduce_axis)
            acc_ref = scr[0] if staged else o_ref

            @pl.when(r == 0)
            def _():
                acc_ref[...] = acc

            @pl.when(r > 0)
            def _():
                acc_ref[...] += acc

            if staged:
                @pl.when(r == grid[reduce_axis] - 1)
                def _():
                    o_ref[...] = acc_ref[...].astype(o_ref.dtype)

    scratch = []
    if staged:
        scratch = [pltpu.VMEM(tuple(d for d in out_spec.block_shape if d is not None), F32)]
    args = [t for pair in pairs for t in pair]
    specs = list(in_specs)
    if has_add:
        args.append(add)
        specs.append(add_spec)
    aliases = {}
    if into is not None:
        aliases = {len(args): 0}
        args.append(into)
        specs.append(ANY)
    outs, bufs = _hosted_call(name, compute, grid, args, specs, [out_shape], [out_spec], scratch, aliases, comm)
    return outs[0] if comm is None else (outs[0], bufs)


def _sds(shape, dtype):
    return jax.ShapeDtypeStruct(tuple(shape), dtype)


def _bs(shape, fn):
    return pl.BlockSpec(tuple(shape), fn)


TM = 1024
TM_WIDE = 2048


def _merge_lead(b):
    return b.reshape(b.shape[0] * b.shape[1], b.shape[2])


def _proj_in(h, w_in_g, l, comm=None):
    t = h.shape[0]
    return _mm(f"proj_in_l{l}", [(h, w_in_g)], NN, _sds((t, N_IN), F32), (8, t // TM_WIDE),
               [_bs((TM_WIDE, D_MODEL), lambda n, m: (m, 0)),
                _bs((None, None, D_MODEL, 1024), lambda n, m: (n // 2, l, 0, n % 2))],
               _bs((TM_WIDE, 1024), lambda n, m: (m, n)), comm=comm)


def _proj_rows_sharded(name, a, w_g, l, out_dtype, add=None):
    t = a.shape[0]
    return _mm(name, [(a, w_g)], NN, _sds((t, D_MODEL), out_dtype), (t // TM,),
               [_bs((TM, D_MODEL), lambda m: (m, 0)),
                _bs((N_CHIPS, None, 256, D_MODEL), lambda m: (0, l, 0, 0))],
               _bs((TM, D_MODEL), lambda m: (m, 0)), prep=_merge_lead, add=add,
               add_spec=_bs((TM, D_MODEL), lambda m: (m, 0)))


def _proj_rows_sharded_t(name, g, w_g, l, out_dtype, add=None):
    t = g.shape[0]
    return _mm(name, [(g, w_g)], NT, _sds((t, D_MODEL), out_dtype), (t // TM,),
               [_bs((TM, D_MODEL), lambda m: (m, 0)),
                _bs((N_CHIPS, None, 256, D_MODEL), lambda m: (0, l, 0, 0))],
               _bs((TM, D_MODEL), lambda m: (m, 0)), prep=_merge_lead, add=add,
               add_spec=_bs((TM, D_MODEL), lambda m: (m, 0)))


def _proj_cols256(name, a, w_g, l):
    t, k = a.shape
    return _mm(name, [(a, w_g)], NN, _sds((t, D_MODEL), F32), (N_CHIPS, t // TM),
               [_bs((TM, k), lambda j, m: (m, 0)),
                _bs((None, None, k, 256), lambda j, m: (j, l, 0, 0))],
               _bs((TM, 256), lambda j, m: (m, j)))


def _proj_cols256_t(name, g, w_g, l):
    t = g.shape[0]
    k = w_g.shape[2]
    return _mm(name, [(g, w_g)], NT, _sds((t, k), F32), (t // TM, N_CHIPS),
               [_bs((TM, 256), lambda m, j: (m, j)),
                _bs((None, None, k, 256), lambda m, j: (j, l, 0, 0))],
               _bs((TM, k), lambda m, j: (m, 0)), reduce_axis=1)


def _proj_gate_up(name, h2, wg_g, wu_g, l, comm=None):
    t = h2.shape[0]

    def compute(in_refs, out_refs, scr):
        h_ref, wg_ref, wu_ref = in_refs
        gt_ref, up_ref, act_ref = out_refs
        h = h_ref[...]
        g = _nn(h, wg_ref[...])
        u = _nn(h, wu_ref[...])
        gt_ref[...] = g
        up_ref[...] = u
        act_ref[...] = ((g * _sigmoid(g)) * u).astype(act_ref.dtype)

    w_spec = _bs((None, None, D_MODEL, FFN_SHARD), lambda j, m: (j, l, 0, 0))
    o_spec = _bs((None, TM, FFN_SHARD), lambda j, m: (j, m, 0))
    outs, bufs = _hosted_call(name, compute, (N_CHIPS, t // TM), [h2, wg_g, wu_g],
                        [_bs((TM, D_MODEL), lambda j, m: (m, 0)), w_spec, w_spec],
                        [_sds((N_CHIPS, t, FFN_SHARD), F32), _sds((N_CHIPS, t, FFN_SHARD), F32),
                         _sds((N_CHIPS, t, FFN_SHARD), MXU_DTYPE)], [o_spec, o_spec, o_spec], [], {}, comm)
    return outs if comm is None else (outs, bufs)


def _proj_ffn_in_t(name, pairs, l, comm=None):
    t = pairs[0][0].shape[1]
    specs = []
    for _ in pairs:
        specs += [_bs((None, TM, FFN_SHARD), lambda m, j: (j, m, 0)),
                  _bs((None, None, D_MODEL, FFN_SHARD), lambda m, j: (j, l, 0, 0))]
    return _mm(name, pairs, NT, _sds((t, D_MODEL), F32), (t // TM, N_CHIPS), specs,
               _bs((TM, D_MODEL), lambda m, j: (m, 0)), reduce_axis=1, comm=comm)


def _proj_ffn_out(name, act, w_g, l, comm=None):
    t = act.shape[1]
    return _mm(name, [(act, w_g)], NN, _sds((t, D_MODEL), F32), (t // TM, N_CHIPS),
               [_bs((None, TM, FFN_SHARD), lambda m, j: (j, m, 0)),
                _bs((None, None, FFN_SHARD, D_MODEL), lambda m, j: (j, l, 0, 0))],
               _bs((TM, D_MODEL), lambda m, j: (m, 0)), reduce_axis=1, comm=comm)


def _proj_down_t_swiglu_bwd(name, dff, w_g, gt, up, l, comm=None):
    t = dff.shape[0]

    def compute(in_refs, out_refs, scr):
        d_ref, w_ref, gt_ref, up_ref = in_refs
        dgt_ref, dup_ref = out_refs
        dact = _nt(d_ref[...], w_ref[...])
        g = gt_ref[...]
        s = _sigmoid(g)
        dgt_ref[...] = (dact * up_ref[...] * _dsilu(g, s)).astype(dgt_ref.dtype)
        dup_ref[...] = (dact * (g * s)).astype(dup_ref.dtype)

    o_spec = _bs((None, TM, FFN_SHARD), lambda j, m: (j, m, 0))
    outs, bufs = _hosted_call(name, compute, (N_CHIPS, t // TM), [dff, w_g, gt, up],
                        [_bs((TM, D_MODEL), lambda j, m: (m, 0)),
                         _bs((None, None, FFN_SHARD, D_MODEL), lambda j, m: (j, l, 0, 0)), o_spec, o_spec],
                        [_sds((N_CHIPS, t, FFN_SHARD), MXU_DTYPE)] * 2, [o_spec, o_spec], [], {}, comm)
    return outs if comm is None else (outs, bufs)


def _proj_in_t(dz, w_in_g, l, comm=None):
    t = dz.shape[0]
    return _mm(f"proj_in_t_l{l}", [(dz, w_in_g)], NT, _sds((t, D_MODEL), F32), (t // TM_WIDE, 8),
               [_bs((TM_WIDE, 1024), lambda m, n: (m, n)),
                _bs((None, None, D_MODEL, 1024), lambda m, n: (n // 2, l, 0, n % 2))],
               _bs((TM_WIDE, D_MODEL), lambda m, n: (m, 0)), reduce_axis=1, comm=comm)


TT = 1024


def _wgrad(name, a, a_spec, g, g_spec, shard_shape, o_map, n_outer, l, into, tt=TT):
    t = a.shape[-2]
    out = _sds((N_CHIPS, 2) + tuple(shard_shape), GRAD_EXCHANGE_DTYPE)
    return _mm(name, [(a, g)], TN, out, (n_outer, t // tt), [a_spec, g_spec],
               _bs((None, None) + tuple(o_map[0]), o_map[1]), reduce_axis=1, into=into)


def _rows(name, fn, n_tiles, ins, in_specs, out_shapes, out_specs, n_red=0, into=()):
    n_in = len(ins)
    n_out = len(out_shapes)

    def body(*refs):
        in_refs = refs[:n_in]
        out_refs = refs[len(refs) - n_out:]
        vals = fn(*[r[...] for r in in_refs])
        if not isinstance(vals, (tuple, list)):
            vals = (vals,)
        first = pl.program_id(0) == 0
        for j in range(n_out):
            o_ref = out_refs[j]
            val = vals[j]
            if j < n_out - n_red:
                o_ref[...] = val.astype(o_ref.dtype)
            else:
                @pl.when(first)
                def _(o_ref=o_ref, val=val):
                    o_ref[...] = val

                @pl.when(jnp.logical_not(first))
                def _(o_ref=o_ref, val=val):
                    o_ref[...] += val

    args = list(ins)
    specs = list(in_specs)
    aliases = {}
    for buf, out_idx in into:
        aliases[len(args)] = out_idx
        args.append(buf)
        specs.append(ANY)
    res = _pcall(body, name=name, grid=(n_tiles,), in_specs=specs, out_specs=list(out_specs),
                 out_shape=list(out_shapes), input_output_aliases=aliases, compiler_params=_params(1))(*args)
    return res


def _tile_spec(tile, width, blk=0):
    return pl.BlockSpec((tile, width), lambda i: (i, blk))


def _whole(shape):
    nd = len(shape)
    return pl.BlockSpec(tuple(shape), lambda i: (0,) * nd)


def _rms(x, g):
    r = lax.rsqrt(_mean_last(x * x) + EPS)
    return (x * r) * g


def _rms_bwd_math(u, g, dy):
    r = lax.rsqrt(_mean_last(u * u) + EPS)
    uh = u * r
    gdy = dy * g
    du = r * (gdy - uh * _mean_last(gdy * uh))
    return du, _rowsum(dy * uh)


def _rms_fwd(name, x, g):
    t, d = x.shape
    tile = 512
    return _rows(name, lambda xv, gv: (_rms(xv, gv),), t // tile, [x, g],
                 [_tile_spec(tile, d), _whole((1, d))], [_sds((t, d), MXU_DTYPE)], [_tile_spec(tile, d)])[0]


def _resid_rms_fwd(name, x, y, g):
    t, d = x.shape
    tile = 512
    return _rows(name, lambda xv, yv, gv: (xv + _rms(yv, gv),), t // tile, [x, y, g],
                 [_tile_spec(tile, d), _tile_spec(tile, d), _whole((1, d))], [_sds((t, d), F32)],
                 [_tile_spec(tile, d)])[0]


def _resid_rms_norm_fwd(name, x, y, g, g_next):
    t, d = x.shape
    tile = 512

    def fn(xv, yv, gv, gn):
        s = xv + _rms(yv, gv)
        return s, _rms(s, gn)

    return _rows(name, fn, t // tile, [x, y, g, g_next],
                 [_tile_spec(tile, d), _tile_spec(tile, d), _whole((1, d)), _whole((1, d))],
                 [_sds((t, d), F32), _sds((t, d), MXU_DTYPE)], [_tile_spec(tile, d)] * 2)


TF = 512


def _merge_proj_out(name, pa, pb, z, w_g, l, comm=None):
    t = z.shape[0]

    def compute(in_refs, out_refs, scr):
        pa_ref, pb_ref, ga_ref, gb_ref, w_ref = in_refs
        mg_ref, mix_ref = out_refs
        merged = _sigmoid(ga_ref[...]) * pa_ref[...] + _sigmoid(gb_ref[...]) * pb_ref[...]
        mg_ref[...] = merged.astype(mg_ref.dtype)
        mix_ref[...] = _nn(merged, _merge_lead(w_ref[...]))

    row = lambda blk: _bs((TF, D_MODEL), lambda m: (m, blk))
    outs, bufs = _hosted_call(name, compute, (t // TF,), [pa, pb, z, z, w_g],
                        [row(0), row(0), row(6), row(7), _bs((N_CHIPS, None, 256, D_MODEL), lambda m: (0, l, 0, 0))],
                        [_sds((t, D_MODEL), MXU_DTYPE), _sds((t, D_MODEL), F32)], [row(0), row(0)], [], {}, comm)
    return outs if comm is None else (outs, bufs)


def _proj_ple_gate_ple(name, x2, w_g, pe, l):
    t = x2.shape[0]

    def body(x_ref, w_ref, pe_ref, gz_ref, x3_ref):
        x = x_ref[...]
        gz = _nn(x, _merge_lead(w_ref[...]))
        gz_ref[...] = gz
        x3_ref[...] = x + pe_ref[...] * _sigmoid(gz)

    row = _bs((TF, D_MODEL), lambda m: (m, 0))
    return _pcall(body, name=name, grid=(t // TF,),
                  in_specs=[row, _bs((N_CHIPS, None, 256, D_MODEL), lambda m: (0, l, 0, 0)), row],
                  out_specs=[row, row], out_shape=[_sds((t, D_MODEL), F32)] * 2,
                  compiler_params=_params(1))(x2, w_g, pe)


def _rms_bwd(name, u, g, dy, resid, out_dtype):
    t, d = u.shape
    tile = 256

    def fn(uv, gv, dyv, *rest):
        du, dg = _rms_bwd_math(uv, gv, dyv)
        if rest:
            du = du + rest[0]
        return du, dg

    ins = [u, g, dy] + ([resid] if resid is not None else [])
    specs = [_tile_spec(tile, d), _whole((1, d)), _tile_spec(tile, d)] + ([_tile_spec(tile, d)] if resid is not None else [])
    return _rows(name, fn, t // tile, ins, specs, [_sds((t, d), out_dtype), _sds((1, d), F32)],
                 [_tile_spec(tile, d), _whole((1, d))], n_red=1)


def _cumsum_rows(x, group, suffix):
    n = x.shape[0]
    pos = lax.broadcasted_iota(jnp.int32, x.shape, 0) % group
    d = 1
    while d < group:
        if suffix:
            x = x + jnp.where(pos < group - d, pltpu.roll(x, n - d, 0), 0.0)
        else:
            x = x + jnp.where(pos >= d, pltpu.roll(x, d, 0), 0.0)
        d *= 2
    return x


def _hg_gates(zq, zf, lb):
    q = zq * _sigmoid(zq)
    f = lb + (1.0 - lb) * _sigmoid(zf)
    logf = jnp.log(jnp.maximum(f, jnp.finfo(F32).tiny))
    k = (1.0 - lb) * _sigmoid(-zf)
    return q, k, logf, f


def _tri_mask(n, rev):
    t_i = lax.broadcasted_iota(jnp.int32, (n, n), 0)
    s_i = lax.broadcasted_iota(jnp.int32, (n, n), 1)
    return (s_i >= t_i) if rev else (s_i <= t_i)


def _anchors_are_safe(b_s, n_chunks, rev):
    worst = None
    for c in range(n_chunks):
        base = c * HG_CHUNK
        first = base + HG_CHUNK - 1 if rev else base
        last = base if rev else base + HG_CHUNK - 1
        mid = base + HG_CHUNK // 2
        b0, bm, bl = b_s[first:first + 1, :], b_s[mid:mid + 1, :], b_s[last:last + 1, :]
        span = jnp.maximum(b0 - bm, bm - bl)
        worst = span if worst is None else jnp.maximum(worst, span)
    return jnp.max(worst) < HG_SAFE_EXP


def _sub_ranges(base, i_sub, rev):
    r0 = base + i_sub * HG_SUB
    r1 = r0 + HG_SUB
    if rev:
        e0, e1, anchor = r1, base + HG_CHUNK, r1
    else:
        e0, e1, anchor = base, r0, r0 - 1
    return r0, r1, e0, e1, anchor


def _hgrn_fwd(name, z, lb_row, rev):
    t = z.shape[0]
    nb = t // HG_BLOCK
    ncb = HG_BLOCK // HG_CHUNK
    nsb = HG_CHUNK // HG_SUB
    zf0 = 16 if rev else 8

    def tmap(i):
        return nb - 1 - i if rev else i

    def body(zq_ref, zf_ref, zi_ref, lb_ref, o_ref, st_ref, state, q_s, k_s, v_s, b_s):
        @pl.when(pl.program_id(1) == 0)
        def _():
            state[...] = jnp.zeros_like(state)

        q, k, logf, _ = _hg_gates(zq_ref[...], zf_ref[...], lb_ref[...])
        v = zi_ref[...]
        b_all = _cumsum_rows(logf, HG_CHUNK, rev)
        b_s[...] = b_all
        order = range(ncb - 1, -1, -1) if rev else range(ncb)
        safe = _anchors_are_safe(b_s, ncb, rev)

        @pl.when(safe)
        def _():
            cmask = _tri_mask(HG_CHUNK, rev)
            ch = []
            for c in range(ncb):
                base = c * HG_CHUNK
                rows = slice(base, base + HG_CHUNK)
                last = base if rev else base + HG_CHUNK - 1
                mid = base + HG_CHUNK // 2
                q_c, k_c, v_c, b = q[rows, :], k[rows, :], v[rows, :], b_all[rows, :]
                bl, bm = b_s[last:last + 1, :], b_s[mid:mid + 1, :]
                ch.append(dict(v=v_c, qe=q_c * jnp.exp(b), el=jnp.exp(bl), q_t=q_c * jnp.exp(b - bm),
                               k_t=k_c * jnp.exp(bm - b), kd=k_c * jnp.exp(bl - b)))
            for d in ch:
                d["a"] = _nt(d["q_t"], d["k_t"])
                d["inc"] = _tn(d["v"], d["kd"])
            for d in ch:
                d["o"] = _nn(jnp.where(cmask, d["a"], 0.0), d["v"])
            st = state[...]
            for c in order:
                d = ch[c]
                st_ref[c] = st
                d["o"] = d["o"] + _nt(d["qe"], st)
                st = st * d["el"] + d["inc"]
            state[...] = st
            o_ref[...] = jnp.concatenate([d["o"] for d in ch], axis=0)

        @pl.when(jnp.logical_not(safe))
        def _():
            q_s[...] = q
            k_s[...] = k
            v_s[...] = v
            mask = _tri_mask(HG_SUB, rev)
            for c in order:
                base = c * HG_CHUNK
                rows = slice(base, base + HG_CHUNK)
                last = base if rev else base + HG_CHUNK - 1
                st = state[...]
                st_ref[c] = st
                b = b_s[rows, :]
                bl = b_s[last:last + 1, :]
                o_inter = _nt(q_s[rows, :] * jnp.exp(b), st)
                kd = k_s[rows, :] * jnp.exp(bl - b)
                state[...] = st * jnp.exp(bl) + _tn(v_s[rows, :], kd)
                parts = []
                for i_sub in range(nsb):
                    r0, r1, e0, e1, anchor = _sub_ranges(base, i_sub, rev)
                    q_i, k_i, b_i = q_s[r0:r1, :], k_s[r0:r1, :], b_s[r0:r1, :]
                    decay = jnp.exp(jnp.minimum(b_i[:, None, :] - b_i[None, :, :], 0.0))
                    a_d = jnp.where(mask, jnp.sum(q_i[:, None, :] * k_i[None, :, :] * decay, axis=-1), 0.0)
                    o_i = _nn(a_d, v_s[r0:r1, :])
                    if e1 > e0:
                        anc = b_s[anchor:anchor + 1, :]
                        q_t = q_i * jnp.exp(b_i - anc)
                        k_t = k_s[e0:e1, :] * jnp.exp(anc - b_s[e0:e1, :])
                        o_i = o_i + _nn(_nt(q_t, k_t), v_s[e0:e1, :])
                    parts.append(o_i)
                o_ref[rows, :] = o_inter + jnp.concatenate(parts, axis=0)

    blk = lambda off: pl.BlockSpec((HG_BLOCK, HEAD), lambda h, i: (tmap(i), off + h))
    return _pcall(
        body, name=name, grid=(N_HEADS, nb),
        in_specs=[blk(0), blk(zf0), blk(24), pl.BlockSpec((1, HEAD), lambda h, i: (0, h))],
        out_specs=[blk(0), pl.BlockSpec((None, ncb, HEAD, HEAD), lambda h, i: (h, tmap(i), 0, 0))],
        out_shape=[_sds((t, D_MODEL), F32), _sds((N_HEADS, t // HG_CHUNK, HEAD, HEAD), F32)],
        scratch_shapes=[pltpu.VMEM((HEAD, HEAD), F32)] + [pltpu.VMEM((HG_BLOCK, HEAD), F32)] * 4,
        compiler_params=_params(2))(z, z, z, lb_row)


def _hgrn_bwd(name, z, d_o, states, lb_row, rev, dz, comm=None):
    t = z.shape[0]
    nb = t // HG_BLOCK
    ncb = HG_BLOCK // HG_CHUNK
    nsb = HG_CHUNK // HG_SUB
    zf0 = 16 if rev else 8

    def tmap(i):
        return i if rev else nb - 1 - i

    def compute(in_refs, out_refs, scr):
        zq_ref, zf_ref, zi_ref, do_ref, st_ref, lb_ref, _ = in_refs
        dq_ref, dv_ref, dzf_ref, dlb_ref = out_refs
        dstate, q_s, k_s, v_s, b_s, dq_s, dk_s, dv_s, db_s = scr
        first = pl.program_id(1) == 0

        @pl.when(first)
        def _():
            dstate[...] = jnp.zeros_like(dstate)

        lb = lb_ref[...]
        zf = zf_ref[...]
        q, k, logf, f = _hg_gates(zq_ref[...], zf, lb)
        v = zi_ref[...]
        b_all = _cumsum_rows(logf, HG_CHUNK, rev)
        b_s[...] = b_all
        order = range(ncb) if rev else range(ncb - 1, -1, -1)
        safe = _anchors_are_safe(b_s, ncb, rev)

        @pl.when(safe)
        def _():
            cmask = _tri_mask(HG_CHUNK, rev)
            row_i = lax.broadcasted_iota(jnp.int32, (HG_CHUNK, HEAD), 0)
            ch = []
            for c in range(ncb):
                base = c * HG_CHUNK
                rows = slice(base, base + HG_CHUNK)
                last = base if rev else base + HG_CHUNK - 1
                mid = base + HG_CHUNK // 2
                q_c, k_c, v_c, b, do_c = q[rows, :], k[rows, :], v[rows, :], b_all[rows, :], do_ref[rows, :]
                bl, bm = b_s[last:last + 1, :], b_s[mid:mid + 1, :]
                e, ebl, e_q, e_k = jnp.exp(b), jnp.exp(bl - b), jnp.exp(b - bm), jnp.exp(bm - b)
                ch.append(dict(q=q_c, k=k_c, v=v_c, do=do_c, e=e, el=jnp.exp(bl), ebl=ebl, e_q=e_q, e_k=e_k,
                               q_t=q_c * e_q, k_t=k_c * e_k, kd=k_c * ebl, last=last - base))
            for c, d in enumerate(ch):
                d["a"] = _nt(d["q_t"], d["k_t"])
                d["da"] = _nt(d["do"], d["v"])
                d["dq"] = _nn(d["do"], st_ref[c]) * d["e"]
                d["inc"] = _tn(d["do"], d["q"] * d["e"])
            for d in ch:
                da = jnp.where(cmask, d["da"], 0.0)
                d["dq"] = d["dq"] + d["e_q"] * _dot_f32(da, d["k_t"], 1, 0)
                d["dk_intra"] = d["e_k"] * _dot_f32(da, d["q_t"], 0, 0)
                d["dv"] = _tn(jnp.where(cmask, d["a"], 0.0), d["do"])
            dst = dstate[...]
            for c in order:
                d = ch[c]
                dk_inter = _nn(d["v"], dst) * d["ebl"]
                d["dk"] = dk_inter + d["dk_intra"]
                d["dv"] = _nt(d["kd"], dst) + d["dv"]
                d["extra"] = d["el"] * _rowsum(dst * st_ref[c]) + _rowsum(d["k"] * dk_inter)
                dst = d["inc"] + dst * d["el"]
            dstate[...] = dst
            for d in ch:
                d["db"] = d["q"] * d["dq"] - d["k"] * d["dk"] + jnp.where(row_i == d["last"], d["extra"], 0.0)
            dq_s[...] = jnp.concatenate([d["dq"] for d in ch], axis=0)
            dk_s[...] = jnp.concatenate([d["dk"] for d in ch], axis=0)
            dv_s[...] = jnp.concatenate([d["dv"] for d in ch], axis=0)
            db_s[...] = jnp.concatenate([d["db"] for d in ch], axis=0)

        @pl.when(jnp.logical_not(safe))
        def _():
            q_s[...] = q
            k_s[...] = k
            v_s[...] = v
            mask = _tri_mask(HG_SUB, rev)
            for c in order:
                base = c * HG_CHUNK
                rows = slice(base, base + HG_CHUNK)
                last = base if rev else base + HG_CHUNK - 1
                st0 = st_ref[c]
                dst1 = dstate[...]
                b = b_s[rows, :]
                bl = b_s[last:last + 1, :]
                e = jnp.exp(b)
                el = jnp.exp(bl)
                ebl = jnp.exp(bl - b)
                q_c, k_c, v_c, do_c = q_s[rows, :], k_s[rows, :], v_s[rows, :], do_ref[rows, :]
                kd = k_c * ebl
                dq_s[rows, :] = _nn(do_c, st0) * e
                dk_inter = _nn(v_c, dst1) * ebl
                dk_s[rows, :] = dk_inter
                dv_s[rows, :] = _nt(kd, dst1)
                extra = el * _rowsum(dst1 * st0) + _rowsum(k_c * dk_inter)
                dstate[...] = _tn(do_c, q_c * e) + dst1 * el
                for i_sub in range(nsb):
                    r0, r1, e0, e1, anchor = _sub_ranges(base, i_sub, rev)
                    q_i, k_i, b_i, v_i, do_i = q_s[r0:r1, :], k_s[r0:r1, :], b_s[r0:r1, :], v_s[r0:r1, :], do_ref[r0:r1, :]
                    decay = jnp.exp(jnp.minimum(b_i[:, None, :] - b_i[None, :, :], 0.0))
                    a_d = jnp.where(mask, jnp.sum(q_i[:, None, :] * k_i[None, :, :] * decay, axis=-1), 0.0)
                    da_d = jnp.where(mask, _nt(do_i, v_i), 0.0)
                    wgt = da_d[:, :, None] * decay
                    dq_s[r0:r1, :] += jnp.sum(wgt * k_i[None, :, :], axis=1)
                    dk_s[r0:r1, :] += jnp.sum(wgt * q_i[:, None, :], axis=0)
                    dv_s[r0:r1, :] += _tn(a_d, do_i)
                    if e1 > e0:
                        anc = b_s[anchor:anchor + 1, :]
                        e_q = jnp.exp(b_i - anc)
                        e_k = jnp.exp(anc - b_s[e0:e1, :])
                        q_t = q_i * e_q
                        k_t = k_s[e0:e1, :] * e_k
                        a_o = _nt(q_t, k_t)
                        da_o = _nt(do_i, v_s[e0:e1, :])
                        dq_s[r0:r1, :] += e_q * _nn(da_o, k_t)
                        dk_s[e0:e1, :] += e_k * _tn(da_o, q_t)
                        dv_s[e0:e1, :] += _tn(a_o, do_i)
                db_s[rows, :] = q_c * dq_s[rows, :] - k_c * dk_s[rows, :]
                db_s[last:last + 1, :] += extra

        dlogf = _cumsum_rows(db_s[...], HG_CHUNK, not rev)
        s_neg = _sigmoid(-zf)
        df = jnp.where(f > jnp.finfo(F32).tiny, dlogf / f, 0.0)
        dfk = df - dk_s[...]
        dzf_ref[...] = ((1.0 - lb) * _sigmoid(zf) * s_neg * dfk).astype(dzf_ref.dtype)
        dlb = _rowsum(s_neg * dfk)

        @pl.when(first)
        def _():
            dlb_ref[...] = dlb

        @pl.when(jnp.logical_not(first))
        def _():
            dlb_ref[...] += dlb

        dq_ref[...] = dq_s[...]
        dv_ref[...] = dv_s[...]

    blk = lambda off: pl.BlockSpec((HG_BLOCK, HEAD), lambda h, i: (tmap(i), off + h))
    vec = pl.BlockSpec((1, HEAD), lambda h, i: (0, h))
    outs, bufs = _hosted_call(
        name, compute, (N_HEADS, nb), [z, z, z, d_o, states, lb_row, dz],
        [blk(0), blk(zf0), blk(24), blk(0),
         pl.BlockSpec((None, ncb, HEAD, HEAD), lambda h, i: (h, tmap(i), 0, 0)), vec, ANY],
        [_sds((t, D_MODEL), F32), _sds((t, D_MODEL), F32), _sds(dz.shape, dz.dtype), _sds((1, D_MODEL), F32)],
        [blk(0), blk(0), blk(zf0), vec],
        [pltpu.VMEM((HEAD, HEAD), F32)] + [pltpu.VMEM((HG_BLOCK, HEAD), F32)] * 8, {6: 2}, comm)
    return outs if comm is None else (outs, bufs)


def _lower_bounds(name, gamma):
    def body(g_ref, o_ref):
        g0, g1 = g_ref[0:1, :], g_ref[1:2, :]
        m = jnp.maximum(g0, g1)
        e0, e1 = jnp.exp(g0 - m), jnp.exp(g1 - m)
        s0, s1 = e0 / (e0 + e1), e1 / (e0 + e1)
        o_ref[0:1, :] = s0 - s0
        o_ref[1:2, :] = (s0 + s1) - s0

    return _pcall(body, name=name, out_shape=_sds(gamma.shape, F32))(gamma)


def _lower_bounds_bwd(name, gamma, dlb):
    def body(g_ref, d_ref, o_ref):
        g0, g1 = g_ref[0:1, :], g_ref[1:2, :]
        m = jnp.maximum(g0, g1)
        e0, e1 = jnp.exp(g0 - m), jnp.exp(g1 - m)
        s0, s1 = e0 / (e0 + e1), e1 / (e0 + e1)
        d0, d1 = d_ref[0:1, :], d_ref[1:2, :]
        ds0 = (d0 + d1) - (d0 + d1)
        ds1 = d1
        inner = s0 * ds0 + s1 * ds1
        o_ref[0:1, :] = s0 * (ds0 - inner)
        o_ref[1:2, :] = s1 * (ds1 - inner)

    return _pcall(body, name=name, out_shape=_sds(gamma.shape, F32))(gamma, dlb)


def _heads(x):
    return [x[:, h * HEAD:(h + 1) * HEAD] for h in range(N_HEADS)]


def _hg_post_fwd(name, o_f, o_b, z, gain):
    t = z.shape[0]
    tile = 256

    def fn(of, ob, zg, g):
        outs = []
        for o_h, zg_h, g_h in zip(_heads(of + ob), _heads(zg), _heads(g)):
            outs.append(_rms(o_h, g_h) * (zg_h * _sigmoid(zg_h)))
        return (jnp.concatenate(outs, axis=1),)

    return _rows(name, fn, t // tile, [o_f, o_b, z, gain],
                 [_tile_spec(tile, D_MODEL), _tile_spec(tile, D_MODEL), _tile_spec(tile, D_MODEL, 4), _whole((1, D_MODEL))],
                 [_sds((t, D_MODEL), MXU_DTYPE)], [_tile_spec(tile, D_MODEL)])[0]


def _hg_post_bwd(name, da, o_f, o_b, z, gain, dz):
    t = z.shape[0]
    tile = 256

    def fn(dav, of, ob, zg, g):
        d_o, dzg, dgain = [], [], []
        for da_h, o_h, zg_h, g_h in zip(_heads(dav), _heads(of + ob), _heads(zg), _heads(g)):
            s = _sigmoid(zg_h)
            r = lax.rsqrt(_mean_last(o_h * o_h) + EPS)
            oh = o_h * r
            dy = da_h * (zg_h * s)
            dzg.append(da_h * (oh * g_h) * _dsilu(zg_h, s))
            gdy = dy * g_h
            d_o.append(r * (gdy - oh * _mean_last(gdy * oh)))
            dgain.append(_rowsum(dy * oh))
        return jnp.concatenate(d_o, axis=1), jnp.concatenate(dzg, axis=1), jnp.concatenate(dgain, axis=1)

    return _rows(name, fn, t // tile, [da, o_f, o_b, z, gain],
                 [_tile_spec(tile, D_MODEL)] * 3 + [_tile_spec(tile, D_MODEL, 4), _whole((1, D_MODEL))],
                 [_sds((t, D_MODEL), F32), _sds(dz.shape, dz.dtype), _sds((1, D_MODEL), F32)],
                 [_tile_spec(tile, D_MODEL), _tile_spec(tile, D_MODEL, 4), _whole((1, D_MODEL))],
                 n_red=1, into=[(dz, 1)])


def _hg_gate_bwd(name, dq_f, dq_b, dv_f, dv_b, z, dz):
    t = z.shape[0]
    tile = 512

    def fq(a, b, zq):
        return ((a + b) * _dsilu(zq, _sigmoid(zq)),)

    dz = _rows(name + "_q", fq, t // tile, [dq_f, dq_b, z], [_tile_spec(tile, D_MODEL)] * 3,
               [_sds(dz.shape, dz.dtype)], [_tile_spec(tile, D_MODEL, 0)], into=[(dz, 0)])[0]
    dz = _rows(name + "_v", lambda a, b: (a + b,), t // tile, [dv_f, dv_b], [_tile_spec(tile, D_MODEL)] * 2,
               [_sds(dz.shape, dz.dtype)], [_tile_spec(tile, D_MODEL, 3)], into=[(dz, 0)])[0]
    return dz


def _sg_norm(zv, ln_g, ln_b):
    gv = _gelu(zv)
    xc = gv - _mean_last(gv)
    rstd = lax.rsqrt(_mean_last(xc * xc) + EPS)
    xhat = xc * rstd
    return xhat * ln_g + ln_b, xhat, rstd


def _lane_lo():
    return lax.broadcasted_iota(jnp.int32, (SG_CHUNK, LANES), 1) < (LANES // 2)


SG_TILE = 512


def _sgu_fwd(name, z, w, bias_t, ln_g, ln_b):
    t = z.shape[0]

    def fn(zu, zv, wv, bt, lg, lb):
        u = _gelu(zu)
        vn, _, _ = _sg_norm(zv, lg, lb)
        lo = _lane_lo()
        out_rows = []
        for c in range(SG_TILE // SG_CHUNK):
            rs = slice(c * SG_CHUNK, (c + 1) * SG_CHUNK)
            cols = []
            for j in range(SG_WIDTH // LANES):
                cs = slice(j * LANES, (j + 1) * LANES)
                vb = vn[rs, cs]
                sg = jnp.where(lo, _nn(wv[2 * j], vb), _nn(wv[2 * j + 1], vb)) + bt[:, cs]
                cols.append(u[rs, cs] * sg)
            out_rows.append(jnp.concatenate(cols, axis=1))
        return (jnp.concatenate(out_rows, axis=0),)

    return _rows(name, fn, t // SG_TILE, [z, z, w, bias_t, ln_g, ln_b],
                 [_tile_spec(SG_TILE, SG_WIDTH, 10), _tile_spec(SG_TILE, SG_WIDTH, 11), _whole(w.shape),
                  _whole(bias_t.shape), _whole((1, SG_WIDTH)), _whole((1, SG_WIDTH))],
                 [_sds((t, SG_WIDTH), MXU_DTYPE)], [_tile_spec(SG_TILE, SG_WIDTH)])[0]


def _sgu_bwd(name, dbo, z, w, bias_t, ln_g, ln_b, dz):
    t = z.shape[0]
    n_grp = w.shape[0]

    def fn(dbov, zu, zv, wv, bt, lg, lb):
        u = _gelu(zu)
        vn, xhat, rstd = _sg_norm(zv, lg, lb)
        lo = _lane_lo()
        dw = [None] * n_grp
        dsg_sum = None
        du_rows, dvn_rows = [], []
        for c in range(SG_TILE // SG_CHUNK):
            rs = slice(c * SG_CHUNK, (c + 1) * SG_CHUNK)
            du_cols, dvn_cols, dsg_cols = [], [], []
            for j in range(SG_WIDTH // LANES):
                cs = slice(j * LANES, (j + 1) * LANES)
                vb = vn[rs, cs]
                sg = jnp.where(lo, _nn(wv[2 * j], vb), _nn(wv[2 * j + 1], vb)) + bt[:, cs]
                du_cols.append(dbov[rs, cs] * sg)
                dsg = dbov[rs, cs] * u[rs, cs]
                dsg_cols.append(dsg)
                d0 = _nt(jnp.where(lo, dsg, 0.0), vb)
                d1 = _nt(jnp.where(lo, 0.0, dsg), vb)
                dw[2 * j] = d0 if dw[2 * j] is None else dw[2 * j] + d0
                dw[2 * j + 1] = d1 if dw[2 * j + 1] is None else dw[2 * j + 1] + d1
                dvn_cols.append(jnp.where(lo, _tn(wv[2 * j], dsg), _tn(wv[2 * j + 1], dsg)))
            du_rows.append(jnp.concatenate(du_cols, axis=1))
            dvn_rows.append(jnp.concatenate(dvn_cols, axis=1))
            dsg_c = jnp.concatenate(dsg_cols, axis=1)
            dsg_sum = dsg_c if dsg_sum is None else dsg_sum + dsg_c
        du = jnp.concatenate(du_rows, axis=0)
        dvn = jnp.concatenate(dvn_rows, axis=0)
        dxhat = dvn * lg
        dgv = rstd * (dxhat - _mean_last(dxhat) - xhat * _mean_last(dxhat * xhat))
        dzuv = jnp.concatenate([du * _dgelu(zu), dgv * _dgelu(zv)], axis=1)
        return dzuv, jnp.stack(dw, axis=0), dsg_sum, _rowsum(dvn * xhat), _rowsum(dvn)

    return _rows(name, fn, t // SG_TILE, [dbo, z, z, w, bias_t, ln_g, ln_b],
                 [_tile_spec(SG_TILE, SG_WIDTH), _tile_spec(SG_TILE, SG_WIDTH, 10), _tile_spec(SG_TILE, SG_WIDTH, 11),
                  _whole(w.shape), _whole(bias_t.shape), _whole((1, SG_WIDTH)), _whole((1, SG_WIDTH))],
                 [_sds(dz.shape, dz.dtype), _sds(w.shape, F32), _sds((SG_CHUNK, SG_WIDTH), F32),
                  _sds((1, SG_WIDTH), F32), _sds((1, SG_WIDTH), F32)],
                 [_tile_spec(SG_TILE, 2 * SG_WIDTH, 5), _whole(w.shape), _whole((SG_CHUNK, SG_WIDTH)),
                  _whole((1, SG_WIDTH)), _whole((1, SG_WIDTH))],
                 n_red=4, into=[(dz, 0)])


def _merge_bwd(name, dm, pa, pb, z):
    t = z.shape[0]
    tile = 256

    def fn(d, a, b, ga, gb):
        sa, sb = _sigmoid(ga), _sigmoid(gb)
        dgate = jnp.concatenate([d * a * sa * (1.0 - sa), d * b * sb * (1.0 - sb)], axis=1)
        return d * sa, d * sb, dgate

    return _rows(name, fn, t // tile, [dm, pa, pb, z, z],
                 [_tile_spec(tile, D_MODEL)] * 3 + [_tile_spec(tile, D_MODEL, 6), _tile_spec(tile, D_MODEL, 7)],
                 [_sds((t, D_MODEL), MXU_DTYPE), _sds((t, D_MODEL), MXU_DTYPE), _sds((t, N_IN), MXU_DTYPE)],
                 [_tile_spec(tile, D_MODEL), _tile_spec(tile, D_MODEL), _tile_spec(tile, 2 * D_MODEL, 3)])


def _ple_bwd(name, dx, pe, gz):
    t, d = dx.shape
    tile = 512

    def fn(dv, p, g):
        s = _sigmoid(g)
        return dv * s, dv * p * s * (1.0 - s)

    return _rows(name, fn, t // tile, [dx, pe, gz], [_tile_spec(tile, d)] * 3, [_sds((t, d), MXU_DTYPE)] * 2,
                 [_tile_spec(tile, d)] * 2)


def _loss_bwd(name, y, target):
    t, d = y.shape
    tile = 512

    def fn(yv, tv):
        err = yv - tv
        return err * (1.0 / d), _rowsum(err * err)

    return _rows(name, fn, t // tile, [y, target], [_tile_spec(tile, d)] * 2, [_sds((t, d), F32), _sds((1, d), F32)],
                 [_tile_spec(tile, d), _whole((1, d))], n_red=1)


def _position():
    return lax.axis_index("x"), lax.axis_index("y"), lax.axis_index("c")


def _gather_comm(shards, bufs, l0, nl):
    n = len(shards)
    if bufs is None:
        bufs = [_sds((N_CHIPS,) + s.shape, s.dtype) for s in shards]

    def make(w_refs, out_refs, sems):
        send_sems, recv_sems, local_sems = sems
        x, y, c = _position()
        me = 2 * x + y
        sibling = (x, y, 1 - c)
        chips = [(1 - x, y), (x, 1 - y), (1 - x, 1 - y)]

        def half(ref, cc):
            rows = ref.shape[1] // 2
            return ref.at[pl.ds(l0, nl), pl.ds(cc * rows, rows)]

        def copy(i, k, src, chip, cc, to):
            return pltpu.make_async_remote_copy(
                src_ref=src, dst_ref=half(out_refs[i].at[chip], cc), send_sem=send_sems.at[6 * i + k],
                recv_sem=recv_sems.at[6 * i + k], device_id=to, device_id_type=MESH)

        def local(i):
            return pltpu.make_async_copy(w_refs[i].at[pl.ds(l0, nl)], out_refs[i].at[me, pl.ds(l0, nl)], local_sems.at[i])

        def sends():
            return [copy(i, j, half(w_refs[i], c), me, c, (px, py, c))
                    for i in range(n) for j, (px, py) in enumerate(chips)]

        def start():
            for i in range(n):
                local(i).start()
            for cp in sends():
                cp.start()

        def finish():
            passed = []
            for i in range(n):
                for j, (px, py) in enumerate(chips):
                    chip = 2 * px + py
                    copy(i, j, half(w_refs[i], c), chip, c, (px, py, c)).wait_recv()
                    fwd = copy(i, 3 + j, half(out_refs[i].at[chip], c), chip, c, sibling)
                    fwd.start()
                    passed.append(fwd)
            for i in range(n):
                for j, (px, py) in enumerate(chips):
                    copy(i, 3 + j, half(w_refs[i], c), 2 * px + py, 1 - c, sibling).wait_recv()
            for cp in sends() + passed:
                cp.wait_send()
            for i in range(n):
                local(i).wait()

        return start, finish

    sems = [pltpu.SemaphoreType.DMA((6 * n,)), pltpu.SemaphoreType.DMA((6 * n,)), pltpu.SemaphoreType.DMA((n,))]
    return _Comm(shards, bufs, sems, make)


def _exchange_comm(grads, bufs, l0, nl):
    n = len(grads)
    if bufs is None:
        bufs = [_sds((N_DEV,) + g.shape[1:], g.dtype) for g in grads]

    def make(g_refs, out_refs, sems):
        send_sems, recv_sems, local_sems = sems
        x, y, c = _position()
        me = 2 * x + y
        sibling = (x, y, 1 - c)
        chips = [(1 - x, y), (x, 1 - y), (1 - x, 1 - y)]

        def lay(ref):
            return ref.at[pl.ds(l0, nl)]

        def copy(i, k, src, slot, to):
            return pltpu.make_async_remote_copy(
                src_ref=src, dst_ref=lay(out_refs[i].at[slot]), send_sem=send_sems.at[7 * i + k],
                recv_sem=recv_sems.at[7 * i + k], device_id=to, device_id_type=MESH)

        def local(i):
            return pltpu.make_async_copy(lay(g_refs[i].at[me]), lay(out_refs[i].at[2 * me + c]), local_sems.at[i])

        def first():
            cps = []
            for i in range(n):
                cps.append(copy(i, 0, lay(g_refs[i].at[me]), 2 * me + c, sibling))
                for j, (px, py) in enumerate(chips):
                    cps.append(copy(i, 1 + j, lay(g_refs[i].at[2 * px + py]), 2 * me + c, (px, py, c)))
            return cps

        def start():
            for i in range(n):
                local(i).start()
            for cp in first():
                cp.start()

        def finish():
            passed = []
            for i in range(n):
                for j, (px, py) in enumerate(chips):
                    slot = 2 * (2 * px + py) + c
                    copy(i, 1 + j, lay(g_refs[i].at[me]), slot, (px, py, c)).wait_recv()
                    fwd = copy(i, 4 + j, lay(out_refs[i].at[slot]), slot, sibling)
                    fwd.start()
                    passed.append(fwd)
            for i in range(n):
                copy(i, 0, lay(g_refs[i].at[me]), 2 * me + (1 - c), sibling).wait_recv()
                for j, (px, py) in enumerate(chips):
                    copy(i, 4 + j, lay(g_refs[i].at[me]), 2 * (2 * px + py) + (1 - c), sibling).wait_recv()
            for cp in first() + passed:
                cp.wait_send()
            for i in range(n):
                local(i).wait()

        return start, finish

    sems = [pltpu.SemaphoreType.DMA((7 * n,)), pltpu.SemaphoreType.DMA((7 * n,)), pltpu.SemaphoreType.DMA((n,))]
    return _Comm(grads, bufs, sems, make)


def _all_reduce_small(packed):
    rows = packed.shape[0]

    def body(x_ref, sum_ref, slots, send_sems, recv_sems, local_sem):
        x, y, c = _position()
        me = 4 * x + 2 * y + c
        mine = pltpu.make_async_copy(x_ref, slots.at[me], local_sem)
        mine.start()
        sends = []
        for k in range(1, N_DEV):
            peer = (x ^ (k >> 2), y ^ ((k >> 1) & 1), c ^ (k & 1))
            cp = pltpu.make_async_remote_copy(src_ref=x_ref, dst_ref=slots.at[me], send_sem=send_sems.at[k - 1],
                                              recv_sem=recv_sems.at[k - 1], device_id=peer, device_id_type=MESH)
            cp.start()
            sends.append(cp)
        for k in range(1, N_DEV):
            px, py, pc = x ^ (k >> 2), y ^ ((k >> 1) & 1), c ^ (k & 1)
            pltpu.make_async_remote_copy(src_ref=x_ref, dst_ref=slots.at[4 * px + 2 * py + pc], send_sem=send_sems.at[k - 1],
                                         recv_sem=recv_sems.at[k - 1], device_id=(px, py, pc), device_id_type=MESH).wait_recv()
        for cp in sends:
            cp.wait_send()
        mine.wait()
        total = slots[0]
        for d in range(1, N_DEV):
            total = total + slots[d]
        sum_ref[...] = total

    vmem = pl.BlockSpec(memory_space=pltpu.VMEM)
    return _pcall(
        body, name="all_reduce_small", in_specs=[vmem], out_specs=vmem, out_shape=_sds(packed.shape, F32),
        scratch_shapes=[pltpu.VMEM((N_DEV, rows, LANES), F32), pltpu.SemaphoreType.DMA((N_DEV - 1,)),
                        pltpu.SemaphoreType.DMA((N_DEV - 1,)), pltpu.SemaphoreType.DMA],
        compiler_params=pltpu.CompilerParams(vmem_limit_bytes=VMEM_LIMIT_BYTES),
    )(packed)


def _adamw(w, g, m, v):
    m = ADAM_B1 * m + (1.0 - ADAM_B1) * g
    v = ADAM_B2 * v + (1.0 - ADAM_B2) * (g * g)
    m_hat = m / (1.0 - ADAM_B1 ** ADAM_STEP)
    v_hat = v / (1.0 - ADAM_B2 ** ADAM_STEP)
    delta = -ADAM_LR * (m_hat / (jnp.sqrt(v_hat) + ADAM_EPS) + ADAM_WD * w)
    return delta, m, v


def _adam_sharded(name, parts, w, m, v):
    shape = w.shape
    cols = shape[-1]
    rows = w.size // cols
    tile = 8
    while tile * 2 * cols <= ADAM_TILE_ELEMS and rows % (tile * 2) == 0:
        tile *= 2

    def fn(p, wv, mv, vv):
        g = p[0].astype(F32)
        for d in range(1, N_DEV):
            g = g + p[d].astype(F32)
        return (g,) + _adamw(wv, g, mv, vv)

    two_d = lambda a: a.reshape(rows, cols)
    outs = _rows(name, fn, rows // tile, [parts.reshape(N_DEV, rows, cols), two_d(w), two_d(m), two_d(v)],
                 [pl.BlockSpec((N_DEV, tile, cols), lambda i: (0, i, 0))] + [_tile_spec(tile, cols)] * 3,
                 [_sds((rows, cols), F32)] * 4, [_tile_spec(tile, cols)] * 4)
    return [o.reshape(shape) for o in outs]


def _adam_small(name, g, w, m, v):
    rows = g.shape[0]
    tile = rows // 2
    return _rows(name, lambda gv, wv, mv, vv: _adamw(wv, gv, mv, vv), rows // tile, [g, w, m, v],
                 [_tile_spec(tile, LANES)] * 4, [_sds(g.shape, F32)] * 3, [_tile_spec(tile, LANES)] * 3)


BIG = ("w_in", "w_a", "w_b", "w_out", "w_gate", "w_up", "w_down", "w_ple", "w_ple_gate")
SMALL = ("norm_mix_pre", "lb_gamma_fwd", "lb_gamma_bwd", "hg_norm", "sg_w", "sg_b", "sg_ln_g", "sg_ln_b",
         "norm_mix_post", "norm_ffn_pre", "norm_ffn_post")


def _with_comm(plan, tag, state, call):
    if tag not in plan:
        return call(None)
    keys, comm = plan[tag](state)
    res, bufs = call(comm)
    state.update(zip(keys, bufs))
    return res


def _layer_fwd(l, x, p_l, wg, sm, plan):
    sv = {"x": x}
    h = _rms_fwd(f"norm_mix_pre_l{l}", x, sm["norm_mix_pre"])
    z = _with_comm(plan, "proj_in", wg, lambda comm: _proj_in(h, wg["w_in"], l, comm=comm))
    o_f, st_f = _hgrn_fwd(f"hgrn_fwd_l{l}", z, sm["lb_fwd"], False)
    o_b, st_b = _hgrn_fwd(f"hgrn_rev_l{l}", z, sm["lb_bwd"], True)
    a_out = _hg_post_fwd(f"hg_post_l{l}", o_f, o_b, z, sm["hg_norm"])
    b_out = _sgu_fwd(f"sgu_l{l}", z, sm["sg_w"], sm["sg_bias_t"], sm["sg_ln_g"], sm["sg_ln_b"])
    pa = _proj_rows_sharded(f"proj_a_l{l}", a_out, wg["w_a"], l, F32)
    pb = _proj_cols256(f"proj_b_l{l}", b_out, wg["w_b"], l)
    merged, mix = _with_comm(plan, "merge_proj_out", wg, lambda comm: _merge_proj_out(
        f"merge_proj_out_l{l}", pa, pb, z, wg["w_out"], l, comm=comm))
    x1, h2 = _resid_rms_norm_fwd(f"norm_mix_post_ffn_pre_l{l}", x, mix, sm["norm_mix_post"], sm["norm_ffn_pre"])
    gt, up, act = _with_comm(plan, "proj_gate_up", wg, lambda comm: _proj_gate_up(
        f"proj_gate_up_l{l}", h2, wg["w_gate"], wg["w_up"], l, comm=comm))
    ff = _with_comm(plan, "proj_down", wg, lambda comm: _proj_ffn_out(f"proj_down_l{l}", act, wg["w_down"], l, comm=comm))
    x2 = _resid_rms_fwd(f"norm_ffn_post_l{l}", x1, ff, sm["norm_ffn_post"])
    pe = _proj_cols256(f"proj_ple_l{l}", p_l, wg["w_ple"], l)
    gz, x3 = _proj_ple_gate_ple(f"proj_ple_gate_ple_l{l}", x2, wg["w_ple_gate"], pe, l)
    sv.update(h=h, z=z, o_f=o_f, o_b=o_b, st_f=st_f, st_b=st_b, a_out=a_out, b_out=b_out, pa=pa, pb=pb,
              merged=merged, mix=mix, x1=x1, h2=h2, gt=gt, up=up, act=act, ff=ff, x2=x2, pe=pe, gz=gz, p=p_l)
    return x3, sv


def _layer_bwd(l, dx3, sv, wg, sm, gw, parts, plan):
    t = dx3.shape[0]
    nt = t // TT
    sg = {}

    def wgrad(key, *a, **k):
        gw[key] = _wgrad(f"grad_{key}_l{l}", *a, l=l, into=gw.get(key), **k)

    row = lambda width: _bs((TT, width), lambda j, i: (i, 0))
    row_j = lambda width: _bs((TT, width), lambda j, i: (i, j))
    ffn_j = _bs((None, TT, FFN_SHARD), lambda j, i: (j, i, 0))
    blk_j = lambda shape: (tuple(shape), lambda j, i: (j, l, 0, 0))

    dpe, dgz = _ple_bwd(f"ple_bwd_l{l}", dx3, sv["pe"], sv["gz"])
    wgrad("w_ple", sv["p"], row(PLE_DIM), dpe, row_j(256), (PLE_DIM, 256), blk_j((PLE_DIM, 256)), N_CHIPS)
    wgrad("w_ple_gate", sv["x2"], row_j(256), dgz, row(D_MODEL), (256, D_MODEL), blk_j((256, D_MODEL)), N_CHIPS)
    dx2 = _proj_rows_sharded_t(f"proj_ple_gate_t_l{l}", dgz, wg["w_ple_gate"], l, F32, add=dx3)

    dff, sg["norm_ffn_post"] = _rms_bwd(f"norm_ffn_post_bwd_l{l}", sv["ff"], sm["norm_ffn_post"], dx2, None, MXU_DTYPE)
    dgt, dup = _with_comm(plan, "proj_down_t", parts, lambda comm: _proj_down_t_swiglu_bwd(
        f"proj_down_t_swiglu_bwd_l{l}", dff, wg["w_down"], sv["gt"], sv["up"], l, comm=comm))
    wgrad("w_down", sv["act"], ffn_j, dff, row(D_MODEL), (FFN_SHARD, D_MODEL), blk_j((FFN_SHARD, D_MODEL)), N_CHIPS)
    dh2 = _with_comm(plan, "proj_gate_up_t", parts, lambda comm: _proj_ffn_in_t(
        f"proj_gate_up_t_l{l}", [(dgt, wg["w_gate"]), (dup, wg["w_up"])], l, comm=comm))
    wgrad("w_gate", sv["h2"], row(D_MODEL), dgt, ffn_j, (D_MODEL, FFN_SHARD), blk_j((D_MODEL, FFN_SHARD)), N_CHIPS)
    wgrad("w_up", sv["h2"], row(D_MODEL), dup, ffn_j, (D_MODEL, FFN_SHARD), blk_j((D_MODEL, FFN_SHARD)), N_CHIPS)
    dx1, sg["norm_ffn_pre"] = _rms_bwd(f"norm_ffn_pre_bwd_l{l}", sv["x1"], sm["norm_ffn_pre"], dh2, dx2, F32)

    dmix, sg["norm_mix_post"] = _rms_bwd(f"norm_mix_post_bwd_l{l}", sv["mix"], sm["norm_mix_post"], dx1, None, MXU_DTYPE)
    dmerged = _proj_rows_sharded_t(f"proj_out_t_l{l}", dmix, wg["w_out"], l, F32)
    wgrad("w_out", sv["merged"], row_j(256), dmix, row(D_MODEL), (256, D_MODEL), blk_j((256, D_MODEL)), N_CHIPS)
    dpa, dpb, dz = _merge_bwd(f"merge_bwd_l{l}", dmerged, sv["pa"], sv["pb"], sv["z"])
    da = _proj_rows_sharded_t(f"proj_a_t_l{l}", dpa, wg["w_a"], l, F32)
    wgrad("w_a", sv["a_out"], row_j(256), dpa, row(D_MODEL), (256, D_MODEL), blk_j((256, D_MODEL)), N_CHIPS)
    dbo = _proj_cols256_t(f"proj_b_t_l{l}", dpb, wg["w_b"], l)
    wgrad("w_b", sv["b_out"], row(SG_WIDTH), dpb, row_j(256), (SG_WIDTH, 256), blk_j((SG_WIDTH, 256)), N_CHIPS)

    dz, sg["sg_w"], dsg_sum, sg["sg_ln_g"], sg["sg_ln_b"] = _sgu_bwd(
        f"sgu_bwd_l{l}", dbo, sv["z"], sm["sg_w"], sm["sg_bias_t"], sm["sg_ln_g"], sm["sg_ln_b"], dz)
    sg["sg_b"] = dsg_sum.reshape(SG_CHUNK, N_HEADS, SG_WIDTH // N_HEADS).sum(axis=-1).T
    d_o, dz, sg["hg_norm"] = _hg_post_bwd(f"hg_post_bwd_l{l}", da, sv["o_f"], sv["o_b"], sv["z"], sm["hg_norm"], dz)
    dq_f, dv_f, dz, sg["lb_fwd"] = _with_comm(plan, "hgrn_fwd_bwd", parts, lambda comm: _hgrn_bwd(
        f"hgrn_fwd_bwd_l{l}", sv["z"], d_o, sv["st_f"], sm["lb_fwd"], False, dz, comm=comm))
    dq_b, dv_b, dz, sg["lb_bwd"] = _hgrn_bwd(f"hgrn_rev_bwd_l{l}", sv["z"], d_o, sv["st_b"], sm["lb_bwd"], True, dz)
    dz = _hg_gate_bwd(f"hg_gate_bwd_l{l}", dq_f, dq_b, dv_f, dv_b, sv["z"], dz)

    gw["w_in"] = _wgrad(f"grad_w_in_l{l}", sv["h"], _bs((TM_WIDE, D_MODEL), lambda n, i: (i, 0)), dz,
                        _bs((TM_WIDE, 1024), lambda n, i: (i, n)), (D_MODEL, 2048),
                        ((D_MODEL, 1024), lambda n, i: (n // 2, l, 0, n % 2)), 8, l, gw.get("w_in"), tt=TM_WIDE)
    dh = _with_comm(plan, "proj_in_t", parts, lambda comm: _proj_in_t(dz, wg["w_in"], l, comm=comm))
    dx, sg["norm_mix_pre"] = _rms_bwd(f"norm_mix_pre_bwd_l{l}", sv["x"], sm["norm_mix_pre"], dh, dx1, F32)
    del nt
    return dx, gw, sg


def _pack(parts):
    return jnp.concatenate([a.reshape(-1, LANES) for a in parts], axis=0)


def _step(x, p, loss_target, w, m, v):
    x = x[0]
    target = loss_target[0]
    depth = w["w_in"].shape[0]

    assert depth == 2, "the exchanges below ride in layer 0's kernels and carry layer 1's data"
    shards = {k: w[k].astype(MXU_DTYPE) for k in BIG}
    rest_a = [k for k in BIG if k not in ("w_in", "w_gate", "w_up")]
    rest_b = ["w_gate", "w_up"]
    rest = rest_a + rest_b

    def gather(keys, l0, extend):
        return lambda wg: (keys, _gather_comm([shards[k] for k in keys], [wg[k] for k in keys] if extend else None, l0, 1))

    wg = {"w_in": _run_comm("gather_w_in_l0", gather(["w_in"], 0, False)(None)[1])[0]}
    fwd_plans = [{"proj_in": gather(rest, 0, False), "merge_proj_out": gather(["w_in"], 1, True),
                  "proj_gate_up": gather(rest_a, 1, True), "proj_down": gather(rest_b, 1, True)}, {}]
    lb_f = _lower_bounds("lower_bounds_fwd", w["lb_gamma_fwd"])
    lb_b = _lower_bounds("lower_bounds_bwd", w["lb_gamma_bwd"])

    def small_of(l):
        sm = {k: w[k][l:l + 1] for k in ("norm_mix_pre", "hg_norm", "sg_ln_g", "sg_ln_b", "norm_mix_post",
                                        "norm_ffn_pre", "norm_ffn_post")}
        sm["lb_fwd"], sm["lb_bwd"] = lb_f[l:l + 1], lb_b[l:l + 1]
        sm["sg_w"] = w["sg_w"][l]
        sm["sg_bias_t"] = jnp.repeat(w["sg_b"][l].T, SG_WIDTH // N_HEADS, axis=1)
        return sm

    saved = []
    h = x
    for l in range(depth):
        h, sv = _layer_fwd(l, h, p[l, 0], wg, small_of(l), fwd_plans[l])
        saved.append(sv)

    dy, sq_err = _loss_bwd("loss", h, target)
    gw, parts = {}, {}

    def exchange(keys, l0, extend):
        return lambda parts: (keys, _exchange_comm([gw[k] for k in keys], [parts[k] for k in keys] if extend else None, l0, 1))

    bwd_plans = [{"proj_down_t": exchange(["w_in"], 1, False), "proj_gate_up_t": exchange(rest, 1, False),
                  "hgrn_fwd_bwd": exchange(rest, 0, True), "proj_in_t": exchange(["w_in"], 0, True)}, {}]
    small_grads = [None] * depth
    for l in reversed(range(depth)):
        dy, gw, small_grads[l] = _layer_bwd(l, dy, saved[l], wg, small_of(l), gw, parts, bwd_plans[l])

    def stack(key):
        return jnp.concatenate([small_grads[l][key].reshape((1,) + w_shape[1:]) for l in range(depth)], axis=0)

    g_small = {}
    for key in SMALL:
        w_shape = w[key].shape
        if key == "lb_gamma_fwd":
            dlb = jnp.concatenate([small_grads[l]["lb_fwd"] for l in range(depth)], axis=0)
            g_small[key] = _lower_bounds_bwd("lower_bounds_fwd_bwd", w[key], dlb)
        elif key == "lb_gamma_bwd":
            dlb = jnp.concatenate([small_grads[l]["lb_bwd"] for l in range(depth)], axis=0)
            g_small[key] = _lower_bounds_bwd("lower_bounds_bwd_bwd", w[key], dlb)
        else:
            g_small[key] = stack(key)

    packed = _pack([g_small[k] for k in SMALL] + [sq_err])
    summed = _all_reduce_small(packed)
    n_small_rows = sum(w[k].size for k in SMALL) // LANES
    loss = 0.5 * jnp.sum(summed[n_small_rows:]) / D_MODEL

    g_rows = summed[:n_small_rows]
    d_rows, m_rows, v_rows = _adam_small("adamw_small", g_rows, _pack([w[k] for k in SMALL]),
                                         _pack([m[k] for k in SMALL]), _pack([v[k] for k in SMALL]))
    out = {}
    off = 0
    for key in SMALL:
        n_rows = w[key].size // LANES
        sl = slice(off, off + n_rows)
        out[key] = tuple(a[sl].reshape(w[key].shape) for a in (g_rows, d_rows, m_rows, v_rows))
        off += n_rows

    for key in BIG:
        out[key] = tuple(_adam_sharded(f"adamw_{key}", parts[key], w[key], m[key], v[key]))
    return loss, dy[None], out


WEIGHTS = ("norm_mix_pre", "w_in", "lb_gamma_fwd", "lb_gamma_bwd", "hg_norm", "sg_w", "sg_b", "sg_ln_g", "sg_ln_b",
           "w_a", "w_b", "w_out", "norm_mix_post", "norm_ffn_pre", "w_gate", "w_up", "w_down", "norm_ffn_post",
           "w_ple", "w_ple_gate")


def kernel(x, p, norm_mix_pre, w_in, lb_gamma_fwd, lb_gamma_bwd, hg_norm, sg_w, sg_b, sg_ln_g, sg_ln_b, w_a, w_b, w_out, norm_mix_post, norm_ffn_pre, w_gate, w_up, w_down, norm_ffn_post, w_ple, w_ple_gate, loss_target, m_norm_mix_pre, m_w_in, m_lb_gamma_fwd, m_lb_gamma_bwd, m_hg_norm, m_sg_w, m_sg_b, m_sg_ln_g, m_sg_ln_b, m_w_a, m_w_b, m_w_out, m_norm_mix_post, m_norm_ffn_pre, m_w_gate, m_w_up, m_w_down, m_norm_ffn_post, m_w_ple, m_w_ple_gate, v_norm_mix_pre, v_w_in, v_lb_gamma_fwd, v_lb_gamma_bwd, v_hg_norm, v_sg_w, v_sg_b, v_sg_ln_g, v_sg_ln_b, v_w_a, v_w_b, v_w_out, v_norm_mix_post, v_norm_ffn_pre, v_w_gate, v_w_up, v_w_down, v_norm_ffn_post, v_w_ple, v_w_ple_gate):
    w = dict(zip(WEIGHTS, (norm_mix_pre, w_in, lb_gamma_fwd, lb_gamma_bwd, hg_norm, sg_w, sg_b, sg_ln_g, sg_ln_b, w_a, w_b, w_out, norm_mix_post, norm_ffn_pre, w_gate, w_up, w_down, norm_ffn_post, w_ple, w_ple_gate)))
    m = dict(zip(WEIGHTS, (m_norm_mix_pre, m_w_in, m_lb_gamma_fwd, m_lb_gamma_bwd, m_hg_norm, m_sg_w, m_sg_b, m_sg_ln_g, m_sg_ln_b, m_w_a, m_w_b, m_w_out, m_norm_mix_post, m_norm_ffn_pre, m_w_gate, m_w_up, m_w_down, m_norm_ffn_post, m_w_ple, m_w_ple_gate)))
    v = dict(zip(WEIGHTS, (v_norm_mix_pre, v_w_in, v_lb_gamma_fwd, v_lb_gamma_bwd, v_hg_norm, v_sg_w, v_sg_b, v_sg_ln_g, v_sg_ln_b, v_w_a, v_w_b, v_w_out, v_norm_mix_post, v_norm_ffn_pre, v_w_gate, v_w_up, v_w_down, v_norm_ffn_post, v_w_ple, v_w_ple_gate)))
    loss, grad_x, out = _step(x, p, loss_target, w, m, v)
    res = [loss, grad_x]
    for i in range(4):
        res += [out[k][i] for k in WEIGHTS]
    return tuple(res)
```

```python
import functools

import jax
import jax.numpy as jnp
from jax import lax
from jax.experimental import pallas as pl
from jax.experimental.pallas import tpu as pltpu

F32 = jnp.float32
MXU_DTYPE = jnp.bfloat16
GRAD_EXCHANGE_DTYPE = jnp.bfloat16

D_MODEL = 1024
N_HEADS = 8
HEAD = 128
HG_CHUNK = 64
HG_SUB = 16
HG_BLOCK = 512
HG_SAFE_EXP = 80.0
SG_CHUNK = 128
SG_WIDTH = 512
FFN_SHARD = 704
PLE_DIM = 256
N_IN = 8192
N_CHIPS = 4
N_DEV = 8
EPS = 1e-6
LANES = 128
VMEM_LIMIT_BYTES = 56 * 2 ** 20

ADAM_LR = 0.001
ADAM_B1 = 0.9
ADAM_B2 = 0.999
ADAM_EPS = 1e-08
ADAM_WD = 0.01
ADAM_STEP = 10
ADAM_TILE_ELEMS = 128 * 1024

MESH = pl.DeviceIdType.MESH
ANY = pl.BlockSpec(memory_space=pl.ANY)


def _pcall(body, **kw):
    return pl.pallas_call(body, **kw)


def _params(n_axes):
    return pltpu.CompilerParams(dimension_semantics=("arbitrary",) * n_axes, vmem_limit_bytes=VMEM_LIMIT_BYTES)


def _dot(a, b, ca, cb):
    return lax.dot_general(a.astype(MXU_DTYPE), b.astype(MXU_DTYPE), (((ca,), (cb,)), ((), ())),
                           preferred_element_type=F32)


def _dot_f32(a, b, ca, cb):
    return lax.dot_general(a, b, (((ca,), (cb,)), ((), ())), precision=lax.Precision.HIGH,
                           preferred_element_type=F32)


def _nn(a, b):
    return _dot(a, b, 1, 0)


def _nt(a, b):
    return _dot(a, b, 1, 1)


def _tn(a, b):
    return _dot(a, b, 0, 0)


NN, NT, TN = (1, 0), (1, 1), (0, 0)


def _sigmoid(x):
    return jax.nn.sigmoid(x)


def _dsilu(x, s):
    return s * (1.0 + x * (1.0 - s))


_SQRT_HALF = 0.7071067811865476
_INV_SQRT_2PI = 0.3989422804014327


def _gelu(x):
    return 0.5 * x * (1.0 + lax.erf(x * _SQRT_HALF))


def _dgelu(x):
    return 0.5 * (1.0 + lax.erf(x * _SQRT_HALF)) + x * jnp.exp(-0.5 * x * x) * _INV_SQRT_2PI


def _mean_last(x):
    return jnp.mean(x, axis=-1, keepdims=True)


def _rowsum(x):
    return jnp.sum(x, axis=0, keepdims=True)


class _Comm:
    def __init__(self, ins, bufs, sem_shapes, make):
        self.ins, self.bufs, self.sem_shapes, self.make = list(ins), list(bufs), list(sem_shapes), make
        self.extends = not isinstance(self.bufs[0], jax.ShapeDtypeStruct)


def _hosted_call(name, compute, grid, args, in_specs, out_shapes, out_specs, scratch, aliases, comm):
    n_in, n_out, n_scr = len(args), len(out_shapes), len(scratch)
    if comm is None:
        def plain(*refs):
            compute(refs[:n_in], refs[n_in:n_in + n_out], refs[n_in + n_out:])

        res = _pcall(plain, name=name, grid=grid, in_specs=list(in_specs), out_specs=list(out_specs),
                     out_shape=list(out_shapes), scratch_shapes=list(scratch), input_output_aliases=dict(aliases),
                     compiler_params=_params(len(grid)))(*args)
        return list(res), []

    n_cin, n_buf = len(comm.ins), len(comm.bufs)
    all_args = list(args) + comm.ins + (comm.bufs if comm.extends else [])
    n_all = len(all_args)
    all_aliases = dict(aliases)
    if comm.extends:
        for j in range(n_buf):
            all_aliases[n_in + n_cin + j] = n_out + j
    buf_shapes = [_sds(b.shape, b.dtype) for b in comm.bufs]

    def body(*refs):
        outs = refs[n_all:n_all + n_out + n_buf]
        scr = refs[n_all + n_out + n_buf:]
        start, finish = comm.make(refs[n_in:n_in + n_cin], outs[n_out:], scr[n_scr:])
        first, last = None, None
        for axis, size in enumerate(grid):
            i = pl.program_id(axis)
            first = (i == 0) if first is None else jnp.logical_and(first, i == 0)
            last = (i == size - 1) if last is None else jnp.logical_and(last, i == size - 1)
        pl.when(first)(start)
        compute(refs[:n_in], outs[:n_out], scr[:n_scr])
        pl.when(last)(finish)

    res = _pcall(body, name=name, grid=grid, in_specs=list(in_specs) + [ANY] * (n_all - n_in),
                 out_specs=list(out_specs) + [ANY] * n_buf, out_shape=list(out_shapes) + buf_shapes,
                 scratch_shapes=list(scratch) + comm.sem_shapes, input_output_aliases=all_aliases,
                 compiler_params=_params(len(grid)))(*all_args)
    return list(res[:n_out]), list(res[n_out:])


def _run_comm(name, comm):
    n_cin, n_buf = len(comm.ins), len(comm.bufs)
    all_args = comm.ins + (comm.bufs if comm.extends else [])
    n_all = len(all_args)

    def body(*refs):
        start, finish = comm.make(refs[:n_cin], refs[n_all:n_all + n_buf], refs[n_all + n_buf:])
        start()
        finish()

    res = _pcall(body, name=name, in_specs=[ANY] * n_all, out_specs=[ANY] * n_buf,
                 out_shape=[_sds(b.shape, b.dtype) for b in comm.bufs], scratch_shapes=comm.sem_shapes,
                 input_output_aliases={n_cin + j: j for j in range(n_buf)} if comm.extends else {})(*all_args)
    return list(res)


def _mm(name, pairs, kind, out_shape, grid, in_specs, out_spec, *, reduce_axis=None, add=None,
        add_spec=None, into=None, prep=None, comm=None):
    n_pairs = len(pairs)
    has_add = add is not None
    staged = reduce_axis is not None and out_shape.dtype != F32

    def compute(in_refs, out_refs, scr):
        o_ref = out_refs[0]
        acc = None
        for i in range(n_pairs):
            a = in_refs[2 * i][...]
            b = in_refs[2 * i + 1][...]
            if prep is not None:
                b = prep(b)
            prod = _dot(a, b, *kind)
            acc = prod if acc is None else acc + prod
        if has_add:
            acc = acc + in_refs[2 * n_pairs][...]
        if reduce_axis is None:
            o_ref[...] = acc.astype(o_ref.dtype)
        else:
            r = pl.program_id(reduce_axis)
            acc_ref = scr[0] if staged else o_ref

            @pl.when(r == 0)
            def _():
                acc_ref[...] = acc

            @pl.when(r > 0)
            def _():
                acc_ref[...] += acc

            if staged:
                @pl.when(r == grid[reduce_axis] - 1)
                def _():
                    o_ref[...] = acc_ref[...].astype(o_ref.dtype)

    scratch = []
    if staged:
        scratch = [pltpu.VMEM(tuple(d for d in out_spec.block_shape if d is not None), F32)]
    args = [t for pair in pairs for t in pair]
    specs = list(in_specs)
    if has_add:
        args.append(add)
        specs.append(add_spec)
    aliases = {}
    if into is not None:
        aliases = {len(args): 0}
        args.append(into)
        specs.append(ANY)
    outs, bufs = _hosted_call(name, compute, grid, args, specs, [out_shape], [out_spec], scratch, aliases, comm)
    return outs[0] if comm is None else (outs[0], bufs)


def _sds(shape, dtype):
    return jax.ShapeDtypeStruct(tuple(shape), dtype)


def _bs(shape, fn):
    return pl.BlockSpec(tuple(shape), fn)


TM = 1024
TM_WIDE = 2048


def _merge_lead(b):
    return b.reshape(b.shape[0] * b.shape[1], b.shape[2])


def _proj_in(h, w_in_g, l, comm=None):
    t = h.shape[0]
    return _mm(f"proj_in_l{l}", [(h, w_in_g)], NN, _sds((t, N_IN), F32), (8, t // TM_WIDE),
               [_bs((TM_WIDE, D_MODEL), lambda n, m: (m, 0)),
                _bs((None, None, D_MODEL, 1024), lambda n, m: (n // 2, l, 0, n % 2))],
               _bs((TM_WIDE, 1024), lambda n, m: (m, n)), comm=comm)


def _proj_rows_sharded(name, a, w_g, l, out_dtype, add=None):
    t = a.shape[0]
    return _mm(name, [(a, w_g)], NN, _sds((t, D_MODEL), out_dtype), (t // TM,),
               [_bs((TM, D_MODEL), lambda m: (m, 0)),
                _bs((N_CHIPS, None, 256, D_MODEL), lambda m: (0, l, 0, 0))],
               _bs((TM, D_MODEL), lambda m: (m, 0)), prep=_merge_lead, add=add,
               add_spec=_bs((TM, D_MODEL), lambda m: (m, 0)))


def _proj_rows_sharded_t(name, g, w_g, l, out_dtype, add=None):
    t = g.shape[0]
    return _mm(name, [(g, w_g)], NT, _sds((t, D_MODEL), out_dtype), (t // TM,),
               [_bs((TM, D_MODEL), lambda m: (m, 0)),
                _bs((N_CHIPS, None, 256, D_MODEL), lambda m: (0, l, 0, 0))],
               _bs((TM, D_MODEL), lambda m: (m, 0)), prep=_merge_lead, add=add,
               add_spec=_bs((TM, D_MODEL), lambda m: (m, 0)))


def _proj_cols256(name, a, w_g, l):
    t, k = a.shape
    return _mm(name, [(a, w_g)], NN, _sds((t, D_MODEL), F32), (N_CHIPS, t // TM),
               [_bs((TM, k), lambda j, m: (m, 0)),
                _bs((None, None, k, 256), lambda j, m: (j, l, 0, 0))],
               _bs((TM, 256), lambda j, m: (m, j)))


def _proj_cols256_t(name, g, w_g, l):
    t = g.shape[0]
    k = w_g.shape[2]
    return _mm(name, [(g, w_g)], NT, _sds((t, k), F32), (t // TM, N_CHIPS),
               [_bs((TM, 256), lambda m, j: (m, j)),
                _bs((None, None, k, 256), lambda m, j: (j, l, 0, 0))],
               _bs((TM, k), lambda m, j: (m, 0)), reduce_axis=1)


def _proj_gate_up(name, h2, wg_g, wu_g, l, comm=None):
    t = h2.shape[0]

    def compute(in_refs, out_refs, scr):
        h_ref, wg_ref, wu_ref = in_refs
        gt_ref, up_ref, act_ref = out_refs
        h = h_ref[...]
        g = _nn(h, wg_ref[...])
        u = _nn(h, wu_ref[...])
        gt_ref[...] = g
        up_ref[...] = u
        act_ref[...] = ((g * _sigmoid(g)) * u).astype(act_ref.dtype)

    w_spec = _bs((None, None, D_MODEL, FFN_SHARD), lambda j, m: (j, l, 0, 0))
    o_spec = _bs((None, TM, FFN_SHARD), lambda j, m: (j, m, 0))
    outs, bufs = _hosted_call(name, compute, (N_CHIPS, t // TM), [h2, wg_g, wu_g],
                        [_bs((TM, D_MODEL), lambda j, m: (m, 0)), w_spec, w_spec],
                        [_sds((N_CHIPS, t, FFN_SHARD), F32), _sds((N_CHIPS, t, FFN_SHARD), F32),
                         _sds((N_CHIPS, t, FFN_SHARD), MXU_DTYPE)], [o_spec, o_spec, o_spec], [], {}, comm)
    return outs if comm is None else (outs, bufs)


def _proj_ffn_in_t(name, pairs, l, comm=None):
    t = pairs[0][0].shape[1]
    specs = []
    for _ in pairs:
        specs += [_bs((None, TM, FFN_SHARD), lambda m, j: (j, m, 0)),
                  _bs((None, None, D_MODEL, FFN_SHARD), lambda m, j: (j, l, 0, 0))]
    return _mm(name, pairs, NT, _sds((t, D_MODEL), F32), (t // TM, N_CHIPS), specs,
               _bs((TM, D_MODEL), lambda m, j: (m, 0)), reduce_axis=1, comm=comm)


def _proj_ffn_out(name, act, w_g, l, comm=None):
    t = act.shape[1]
    specs = []
    for j in range(N_CHIPS):
        specs += [_bs((None, TM, FFN_SHARD), lambda m, j=j: (j, m, 0)),
                  _bs((None, None, FFN_SHARD, D_MODEL), lambda m, j=j: (j, l, 0, 0))]
    return _mm(name, [(act, w_g)] * N_CHIPS, NN, _sds((t, D_MODEL), F32), (t // TM,), specs,
               _bs((TM, D_MODEL), lambda m: (m, 0)), comm=comm)


def _proj_down_t_swiglu_bwd(name, dff, w_g, gt, up, l, comm=None):
    t = dff.shape[0]

    def compute(in_refs, out_refs, scr):
        d_ref, w_ref, gt_ref, up_ref = in_refs
        dgt_ref, dup_ref = out_refs
        dact = _nt(d_ref[...], w_ref[...])
        g = gt_ref[...]
        s = _sigmoid(g)
        dgt_ref[...] = (dact * up_ref[...] * _dsilu(g, s)).astype(dgt_ref.dtype)
        dup_ref[...] = (dact * (g * s)).astype(dup_ref.dtype)

    o_spec = _bs((None, TM, FFN_SHARD), lambda j, m: (j, m, 0))
    outs, bufs = _hosted_call(name, compute, (N_CHIPS, t // TM), [dff, w_g, gt, up],
                        [_bs((TM, D_MODEL), lambda j, m: (m, 0)),
                         _bs((None, None, FFN_SHARD, D_MODEL), lambda j, m: (j, l, 0, 0)), o_spec, o_spec],
                        [_sds((N_CHIPS, t, FFN_SHARD), MXU_DTYPE)] * 2, [o_spec, o_spec], [], {}, comm)
    return outs if comm is None else (outs, bufs)


def _proj_in_t(dz, w_in_g, l, comm=None):
    t = dz.shape[0]
    return _mm(f"proj_in_t_l{l}", [(dz, w_in_g)], NT, _sds((t, D_MODEL), F32), (t // TM_WIDE, 8),
               [_bs((TM_WIDE, 1024), lambda m, n: (m, n)),
                _bs((None, None, D_MODEL, 1024), lambda m, n: (n // 2, l, 0, n % 2))],
               _bs((TM_WIDE, D_MODEL), lambda m, n: (m, 0)), reduce_axis=1, comm=comm)


TT = 1024


def _wgrad(name, a, a_spec, g, g_spec, shard_shape, o_map, n_outer, l, into, tt=TT):
    t = a.shape[-2]
    out = _sds((N_CHIPS, 2) + tuple(shard_shape), GRAD_EXCHANGE_DTYPE)
    return _mm(name, [(a, g)], TN, out, (n_outer, t // tt), [a_spec, g_spec],
               _bs((None, None) + tuple(o_map[0]), o_map[1]), reduce_axis=1, into=into)


def _rows(name, fn, n_tiles, ins, in_specs, out_shapes, out_specs, n_red=0, into=()):
    n_in = len(ins)
    n_out = len(out_shapes)

    def body(*refs):
        in_refs = refs[:n_in]
        out_refs = refs[len(refs) - n_out:]
        vals = fn(*[r[...] for r in in_refs])
        if not isinstance(vals, (tuple, list)):
            vals = (vals,)
        first = pl.program_id(0) == 0
        for j in range(n_out):
            o_ref = out_refs[j]
            val = vals[j]
            if j < n_out - n_red:
                o_ref[...] = val.astype(o_ref.dtype)
            else:
                @pl.when(first)
                def _(o_ref=o_ref, val=val):
                    o_ref[...] = val

                @pl.when(jnp.logical_not(first))
                def _(o_ref=o_ref, val=val):
                    o_ref[...] += val

    args = list(ins)
    specs = list(in_specs)
    aliases = {}
    for buf, out_idx in into:
        aliases[len(args)] = out_idx
        args.append(buf)
        specs.append(ANY)
    res = _pcall(body, name=name, grid=(n_tiles,), in_specs=specs, out_specs=list(out_specs),
                 out_shape=list(out_shapes), input_output_aliases=aliases, compiler_params=_params(1))(*args)
    return res


def _tile_spec(tile, width, blk=0):
    return pl.BlockSpec((tile, width), lambda i: (i, blk))


def _whole(shape):
    nd = len(shape)
    return pl.BlockSpec(tuple(shape), lambda i: (0,) * nd)


def _rms(x, g):
    r = lax.rsqrt(_mean_last(x * x) + EPS)
    return (x * r) * g


def _rms_bwd_math(u, g, dy):
    r = lax.rsqrt(_mean_last(u * u) + EPS)
    uh = u * r
    gdy = dy * g
    du = r * (gdy - uh * _mean_last(gdy * uh))
    return du, _rowsum(dy * uh)


def _rms_fwd(name, x, g):
    t, d = x.shape
    tile = 512
    return _rows(name, lambda xv, gv: (_rms(xv, gv),), t // tile, [x, g],
                 [_tile_spec(tile, d), _whole((1, d))], [_sds((t, d), MXU_DTYPE)], [_tile_spec(tile, d)])[0]


def _resid_rms_fwd(name, x, y, g):
    t, d = x.shape
    tile = 512
    return _rows(name, lambda xv, yv, gv: (xv + _rms(yv, gv),), t // tile, [x, y, g],
                 [_tile_spec(tile, d), _tile_spec(tile, d), _whole((1, d))], [_sds((t, d), F32)],
                 [_tile_spec(tile, d)])[0]


def _resid_rms_norm_fwd(name, x, y, g, g_next):
    t, d = x.shape
    tile = 512

    def fn(xv, yv, gv, gn):
        s = xv + _rms(yv, gv)
        return s, _rms(s, gn)

    return _rows(name, fn, t // tile, [x, y, g, g_next],
                 [_tile_spec(tile, d), _tile_spec(tile, d), _whole((1, d)), _whole((1, d))],
                 [_sds((t, d), F32), _sds((t, d), MXU_DTYPE)], [_tile_spec(tile, d)] * 2)


TF = 512


def _merge_proj_out(name, pa, pb, z, w_g, l, comm=None):
    t = z.shape[0]

    def compute(in_refs, out_refs, scr):
        pa_ref, pb_ref, ga_ref, gb_ref, w_ref = in_refs
        mg_ref, mix_ref = out_refs
        merged = _sigmoid(ga_ref[...]) * pa_ref[...] + _sigmoid(gb_ref[...]) * pb_ref[...]
        mg_ref[...] = merged.astype(mg_ref.dtype)
        mix_ref[...] = _nn(merged, _merge_lead(w_ref[...]))

    row = lambda blk: _bs((TF, D_MODEL), lambda m: (m, blk))
    outs, bufs = _hosted_call(name, compute, (t // TF,), [pa, pb, z, z, w_g],
                        [row(0), row(0), row(6), row(7), _bs((N_CHIPS, None, 256, D_MODEL), lambda m: (0, l, 0, 0))],
                        [_sds((t, D_MODEL), MXU_DTYPE), _sds((t, D_MODEL), F32)], [row(0), row(0)], [], {}, comm)
    return outs if comm is None else (outs, bufs)


def _proj_ple_gate_ple(name, x2, w_g, pe, l):
    t = x2.shape[0]

    def body(x_ref, w_ref, pe_ref, gz_ref, x3_ref):
        x = x_ref[...]
        gz = _nn(x, _merge_lead(w_ref[...]))
        gz_ref[...] = gz
        x3_ref[...] = x + pe_ref[...] * _sigmoid(gz)

    row = _bs((TF, D_MODEL), lambda m: (m, 0))
    return _pcall(body, name=name, grid=(t // TF,),
                  in_specs=[row, _bs((N_CHIPS, None, 256, D_MODEL), lambda m: (0, l, 0, 0)), row],
                  out_specs=[row, row], out_shape=[_sds((t, D_MODEL), F32)] * 2,
                  compiler_params=_params(1))(x2, w_g, pe)


def _rms_bwd(name, u, g, dy, resid, out_dtype):
    t, d = u.shape
    tile = 256

    def fn(uv, gv, dyv, *rest):
        du, dg = _rms_bwd_math(uv, gv, dyv)
        if rest:
            du = du + rest[0]
        return du, dg

    ins = [u, g, dy] + ([resid] if resid is not None else [])
    specs = [_tile_spec(tile, d), _whole((1, d)), _tile_spec(tile, d)] + ([_tile_spec(tile, d)] if resid is not None else [])
    return _rows(name, fn, t // tile, ins, specs, [_sds((t, d), out_dtype), _sds((1, d), F32)],
                 [_tile_spec(tile, d), _whole((1, d))], n_red=1)


def _rms_bwd_pair(name, u1, g1, dy1, resid, u2, g2, out2_dtype):
    t, d = u1.shape
    tile = 256

    def fn(u1v, g1v, dy1v, rv, u2v, g2v):
        d1, dg1 = _rms_bwd_math(u1v, g1v, dy1v)
        d1 = d1 + rv
        d2, dg2 = _rms_bwd_math(u2v, g2v, d1)
        return d1, d2, dg1, dg2

    row, vec = _tile_spec(tile, d), _whole((1, d))
    return _rows(name, fn, t // tile, [u1, g1, dy1, resid, u2, g2], [row, vec, row, row, row, vec],
                 [_sds((t, d), F32), _sds((t, d), out2_dtype), _sds((1, d), F32), _sds((1, d), F32)],
                 [row, row, vec, vec], n_red=2)


def _cumsum_rows(x, group, suffix):
    n = x.shape[0]
    pos = lax.broadcasted_iota(jnp.int32, x.shape, 0) % group
    d = 1
    while d < group:
        if suffix:
            x = x + jnp.where(pos < group - d, pltpu.roll(x, n - d, 0), 0.0)
        else:
            x = x + jnp.where(pos >= d, pltpu.roll(x, d, 0), 0.0)
        d *= 2
    return x


def _hg_gates(zq, zf, lb):
    q = zq * _sigmoid(zq)
    f = lb + (1.0 - lb) * _sigmoid(zf)
    logf = jnp.log(jnp.maximum(f, jnp.finfo(F32).tiny))
    k = (1.0 - lb) * _sigmoid(-zf)
    return q, k, logf, f


def _tri_mask(n, rev):
    t_i = lax.broadcasted_iota(jnp.int32, (n, n), 0)
    s_i = lax.broadcasted_iota(jnp.int32, (n, n), 1)
    return (s_i >= t_i) if rev else (s_i <= t_i)


def _anchors_are_safe(b_s, n_chunks, rev):
    worst = None
    for c in range(n_chunks):
        base = c * HG_CHUNK
        first = base + HG_CHUNK - 1 if rev else base
        last = base if rev else base + HG_CHUNK - 1
        mid = base + HG_CHUNK // 2
        b0, bm, bl = b_s[first:first + 1, :], b_s[mid:mid + 1, :], b_s[last:last + 1, :]
        span = jnp.maximum(b0 - bm, bm - bl)
        worst = span if worst is None else jnp.maximum(worst, span)
    return jnp.max(worst) < HG_SAFE_EXP


def _sub_ranges(base, i_sub, rev):
    r0 = base + i_sub * HG_SUB
    r1 = r0 + HG_SUB
    if rev:
        e0, e1, anchor = r1, base + HG_CHUNK, r1
    else:
        e0, e1, anchor = base, r0, r0 - 1
    return r0, r1, e0, e1, anchor


def _hgrn_fwd(name, z, lb_row, rev, post=None):
    t = z.shape[0]
    nb = t // HG_BLOCK
    ncb = HG_BLOCK // HG_CHUNK
    nsb = HG_CHUNK // HG_SUB
    zf0 = 16 if rev else 8
    n_in = 4 if post is None else 7

    def tmap(i):
        return nb - 1 - i if rev else i

    def body(*refs):
        zq_ref, zf_ref, zi_ref, lb_ref = refs[:4]
        o_ref, st_ref = refs[n_in:n_in + 2]
        state, q_s, k_s, v_s, b_s = refs[len(refs) - 5:]

        @pl.when(pl.program_id(1) == 0)
        def _():
            state[...] = jnp.zeros_like(state)

        q, k, logf, _ = _hg_gates(zq_ref[...], zf_ref[...], lb_ref[...])
        v = zi_ref[...]
        b_all = _cumsum_rows(logf, HG_CHUNK, rev)
        b_s[...] = b_all
        order = range(ncb - 1, -1, -1) if rev else range(ncb)
        safe = _anchors_are_safe(b_s, ncb, rev)

        @pl.when(safe)
        def _():
            cmask = _tri_mask(HG_CHUNK, rev)
            ch = []
            for c in range(ncb):
                base = c * HG_CHUNK
                rows = slice(base, base + HG_CHUNK)
                last = base if rev else base + HG_CHUNK - 1
                mid = base + HG_CHUNK // 2
                q_c, k_c, v_c, b = q[rows, :], k[rows, :], v[rows, :], b_all[rows, :]
                bl, bm = b_s[last:last + 1, :], b_s[mid:mid + 1, :]
                ch.append(dict(v=v_c, qe=q_c * jnp.exp(b), el=jnp.exp(bl), q_t=q_c * jnp.exp(b - bm),
                               k_t=k_c * jnp.exp(bm - b), kd=k_c * jnp.exp(bl - b)))
            for d in ch:
                d["a"] = _nt(d["q_t"], d["k_t"])
                d["inc"] = _tn(d["v"], d["kd"])
            for d in ch:
                d["o"] = _nn(jnp.where(cmask, d["a"], 0.0), d["v"])
            st = state[...]
            for c in order:
                d = ch[c]
                st_ref[c] = st
                d["o"] = d["o"] + _nt(d["qe"], st)
                st = st * d["el"] + d["inc"]
            state[...] = st
            o_ref[...] = jnp.concatenate([d["o"] for d in ch], axis=0)

        @pl.when(jnp.logical_not(safe))
        def _():
            q_s[...] = q
            k_s[...] = k
            v_s[...] = v
            mask = _tri_mask(HG_SUB, rev)
            for c in order:
                base = c * HG_CHUNK
                rows = slice(base, base + HG_CHUNK)
                last = base if rev else base + HG_CHUNK - 1
                st = state[...]
                st_ref[c] = st
                b = b_s[rows, :]
                bl = b_s[last:last + 1, :]
                o_inter = _nt(q_s[rows, :] * jnp.exp(b), st)
                kd = k_s[rows, :] * jnp.exp(bl - b)
                state[...] = st * jnp.exp(bl) + _tn(v_s[rows, :], kd)
                parts = []
                for i_sub in range(nsb):
                    r0, r1, e0, e1, anchor = _sub_ranges(base, i_sub, rev)
                    q_i, k_i, b_i = q_s[r0:r1, :], k_s[r0:r1, :], b_s[r0:r1, :]
                    decay = jnp.exp(jnp.minimum(b_i[:, None, :] - b_i[None, :, :], 0.0))
                    a_d = jnp.where(mask, jnp.sum(q_i[:, None, :] * k_i[None, :, :] * decay, axis=-1), 0.0)
                    o_i = _nn(a_d, v_s[r0:r1, :])
                    if e1 > e0:
                        anc = b_s[anchor:anchor + 1, :]
                        q_t = q_i * jnp.exp(b_i - anc)
                        k_t = k_s[e0:e1, :] * jnp.exp(anc - b_s[e0:e1, :])
                        o_i = o_i + _nn(_nt(q_t, k_t), v_s[e0:e1, :])
                    parts.append(o_i)
                o_ref[rows, :] = o_inter + jnp.concatenate(parts, axis=0)

        if post is not None:
            other_ref, zg_ref, gain_ref = refs[4:7]
            o = o_ref[...] + other_ref[...]
            zg = zg_ref[...]
            o_ref[...] = o
            refs[n_in + 2][...] = (_rms(o, gain_ref[...]) * (zg * _sigmoid(zg))).astype(MXU_DTYPE)

    blk = lambda off: pl.BlockSpec((HG_BLOCK, HEAD), lambda h, i: (tmap(i), off + h))
    vec = pl.BlockSpec((1, HEAD), lambda h, i: (0, h))
    args, in_specs = [z, z, z, lb_row], [blk(0), blk(zf0), blk(24), vec]
    out_specs = [blk(0), pl.BlockSpec((None, ncb, HEAD, HEAD), lambda h, i: (h, tmap(i), 0, 0))]
    out_shape = [_sds((t, D_MODEL), F32), _sds((N_HEADS, t // HG_CHUNK, HEAD, HEAD), F32)]
    if post is not None:
        args += [post[0], z, post[1]]
        in_specs += [blk(0), blk(32), vec]
        out_specs.append(blk(0))
        out_shape.append(_sds((t, D_MODEL), MXU_DTYPE))
    return _pcall(
        body, name=name, grid=(N_HEADS, nb), in_specs=in_specs, out_specs=out_specs, out_shape=out_shape,
        scratch_shapes=[pltpu.VMEM((HEAD, HEAD), F32)] + [pltpu.VMEM((HG_BLOCK, HEAD), F32)] * 4,
        compiler_params=_params(2))(*args)


def _hgrn_bwd(name, z, d_o, states, lb_row, rev, dz, comm=None, prev=None):
    t = z.shape[0]
    nb = t // HG_BLOCK
    ncb = HG_BLOCK // HG_CHUNK
    nsb = HG_CHUNK // HG_SUB
    zf0 = 16 if rev else 8

    def tmap(i):
        return i if rev else nb - 1 - i

    def compute(in_refs, out_refs, scr):
        zq_ref, zf_ref, zi_ref, do_ref, st_ref, lb_ref = in_refs[:6]
        dq_ref, dv_ref, dzf_ref, dlb_ref = out_refs
        dstate, q_s, k_s, v_s, b_s, dq_s, dk_s, dv_s, db_s = scr
        first = pl.program_id(1) == 0

        @pl.when(first)
        def _():
            dstate[...] = jnp.zeros_like(dstate)

        lb = lb_ref[...]
        zf = zf_ref[...]
        q, k, logf, f = _hg_gates(zq_ref[...], zf, lb)
        v = zi_ref[...]
        b_all = _cumsum_rows(logf, HG_CHUNK, rev)
        b_s[...] = b_all
        order = range(ncb) if rev else range(ncb - 1, -1, -1)
        safe = _anchors_are_safe(b_s, ncb, rev)

        @pl.when(safe)
        def _():
            cmask = _tri_mask(HG_CHUNK, rev)
            row_i = lax.broadcasted_iota(jnp.int32, (HG_CHUNK, HEAD), 0)
            ch = []
            for c in range(ncb):
                base = c * HG_CHUNK
                rows = slice(base, base + HG_CHUNK)
                last = base if rev else base + HG_CHUNK - 1
                mid = base + HG_CHUNK // 2
                q_c, k_c, v_c, b, do_c = q[rows, :], k[rows, :], v[rows, :], b_all[rows, :], do_ref[rows, :]
                bl, bm = b_s[last:last + 1, :], b_s[mid:mid + 1, :]
                e, ebl, e_q, e_k = jnp.exp(b), jnp.exp(bl - b), jnp.exp(b - bm), jnp.exp(bm - b)
                ch.append(dict(q=q_c, k=k_c, v=v_c, do=do_c, e=e, el=jnp.exp(bl), ebl=ebl, e_q=e_q, e_k=e_k,
                               q_t=q_c * e_q, k_t=k_c * e_k, kd=k_c * ebl, last=last - base))
            for c, d in enumerate(ch):
                d["a"] = _nt(d["q_t"], d["k_t"])
                d["da"] = _nt(d["do"], d["v"])
                d["dq"] = _nn(d["do"], st_ref[c]) * d["e"]
                d["inc"] = _tn(d["do"], d["q"] * d["e"])
            for d in ch:
                da = jnp.where(cmask, d["da"], 0.0)
                d["dq"] = d["dq"] + d["e_q"] * _dot_f32(da, d["k_t"], 1, 0)
                d["dk_intra"] = d["e_k"] * _dot_f32(da, d["q_t"], 0, 0)
                d["dv"] = _tn(jnp.where(cmask, d["a"], 0.0), d["do"])
            dst = dstate[...]
            for c in order:
                d = ch[c]
                dk_inter = _nn(d["v"], dst) * d["ebl"]
                d["dk"] = dk_inter + d["dk_intra"]
                d["dv"] = _nt(d["kd"], dst) + d["dv"]
                d["extra"] = d["el"] * _rowsum(dst * st_ref[c]) + _rowsum(d["k"] * dk_inter)
                dst = d["inc"] + dst * d["el"]
            dstate[...] = dst
            for d in ch:
                d["db"] = d["q"] * d["dq"] - d["k"] * d["dk"] + jnp.where(row_i == d["last"], d["extra"], 0.0)
            dq_s[...] = jnp.concatenate([d["dq"] for d in ch], axis=0)
            dk_s[...] = jnp.concatenate([d["dk"] for d in ch], axis=0)
            dv_s[...] = jnp.concatenate([d["dv"] for d in ch], axis=0)
            db_s[...] = jnp.concatenate([d["db"] for d in ch], axis=0)

        @pl.when(jnp.logical_not(safe))
        def _():
            q_s[...] = q
            k_s[...] = k
            v_s[...] = v
            mask = _tri_mask(HG_SUB, rev)
            for c in order:
                base = c * HG_CHUNK
                rows = slice(base, base + HG_CHUNK)
                last = base if rev else base + HG_CHUNK - 1
                st0 = st_ref[c]
                dst1 = dstate[...]
                b = b_s[rows, :]
                bl = b_s[last:last + 1, :]
                e = jnp.exp(b)
                el = jnp.exp(bl)
                ebl = jnp.exp(bl - b)
                q_c, k_c, v_c, do_c = q_s[rows, :], k_s[rows, :], v_s[rows, :], do_ref[rows, :]
                kd = k_c * ebl
                dq_s[rows, :] = _nn(do_c, st0) * e
                dk_inter = _nn(v_c, dst1) * ebl
                dk_s[rows, :] = dk_inter
                dv_s[rows, :] = _nt(kd, dst1)
                extra = el * _rowsum(dst1 * st0) + _rowsum(k_c * dk_inter)
                dstate[...] = _tn(do_c, q_c * e) + dst1 * el
                for i_sub in range(nsb):
                    r0, r1, e0, e1, anchor = _sub_ranges(base, i_sub, rev)
                    q_i, k_i, b_i, v_i, do_i = q_s[r0:r1, :], k_s[r0:r1, :], b_s[r0:r1, :], v_s[r0:r1, :], do_ref[r0:r1, :]
                    decay = jnp.exp(jnp.minimum(b_i[:, None, :] - b_i[None, :, :], 0.0))
                    a_d = jnp.where(mask, jnp.sum(q_i[:, None, :] * k_i[None, :, :] * decay, axis=-1), 0.0)
                    da_d = jnp.where(mask, _nt(do_i, v_i), 0.0)
                    wgt = da_d[:, :, None] * decay
                    dq_s[r0:r1, :] += jnp.sum(wgt * k_i[None, :, :], axis=1)
                    dk_s[r0:r1, :] += jnp.sum(wgt * q_i[:, None, :], axis=0)
                    dv_s[r0:r1, :] += _tn(a_d, do_i)
                    if e1 > e0:
                        anc = b_s[anchor:anchor + 1, :]
                        e_q = jnp.exp(b_i - anc)
                        e_k = jnp.exp(anc - b_s[e0:e1, :])
                        q_t = q_i * e_q
                        k_t = k_s[e0:e1, :] * e_k
                        a_o = _nt(q_t, k_t)
                        da_o = _nt(do_i, v_s[e0:e1, :])
                        dq_s[r0:r1, :] += e_q * _nn(da_o, k_t)
                        dk_s[e0:e1, :] += e_k * _tn(da_o, q_t)
                        dv_s[e0:e1, :] += _tn(a_o, do_i)
                db_s[rows, :] = q_c * dq_s[rows, :] - k_c * dk_s[rows, :]
                db_s[last:last + 1, :] += extra

        dlogf = _cumsum_rows(db_s[...], HG_CHUNK, not rev)
        s_neg = _sigmoid(-zf)
        df = jnp.where(f > jnp.finfo(F32).tiny, dlogf / f, 0.0)
        dfk = df - dk_s[...]
        dzf_ref[...] = ((1.0 - lb) * _sigmoid(zf) * s_neg * dfk).astype(dzf_ref.dtype)
        dlb = _rowsum(s_neg * dfk)

        @pl.when(first)
        def _():
            dlb_ref[...] = dlb

        @pl.when(jnp.logical_not(first))
        def _():
            dlb_ref[...] += dlb

        if prev is None:
            dq_ref[...] = dq_s[...]
            dv_ref[...] = dv_s[...]
        else:
            zq = zq_ref[...]
            dq_ref[...] = ((dq_s[...] + in_refs[7][...]) * _dsilu(zq, _sigmoid(zq))).astype(dq_ref.dtype)
            dv_ref[...] = (dv_s[...] + in_refs[8][...]).astype(dv_ref.dtype)

    blk = lambda off: pl.BlockSpec((HG_BLOCK, HEAD), lambda h, i: (tmap(i), off + h))
    vec = pl.BlockSpec((1, HEAD), lambda h, i: (0, h))
    qv_dtype = F32 if prev is None else dz.dtype
    outs, bufs = _hosted_call(
        name, compute, (N_HEADS, nb), [z, z, z, d_o, states, lb_row, dz] + list(prev or ()),
        [blk(0), blk(zf0), blk(24), blk(0),
         pl.BlockSpec((None, ncb, HEAD, HEAD), lambda h, i: (h, tmap(i), 0, 0)), vec, ANY] + [blk(0)] * len(prev or ()),
        [_sds((t, D_MODEL), qv_dtype), _sds((t, D_MODEL), qv_dtype), _sds(dz.shape, dz.dtype), _sds((1, D_MODEL), F32)],
        [blk(0), blk(0), blk(zf0), vec],
        [pltpu.VMEM((HEAD, HEAD), F32)] + [pltpu.VMEM((HG_BLOCK, HEAD), F32)] * 8, {6: 2}, comm)
    return outs if comm is None else (outs, bufs)


def _lower_bounds(name, gamma):
    def body(g_ref, o_ref):
        g0, g1 = g_ref[0:1, :], g_ref[1:2, :]
        m = jnp.maximum(g0, g1)
        e0, e1 = jnp.exp(g0 - m), jnp.exp(g1 - m)
        s0, s1 = e0 / (e0 + e1), e1 / (e0 + e1)
        o_ref[0:1, :] = s0 - s0
        o_ref[1:2, :] = (s0 + s1) - s0

    return _pcall(body, name=name, out_shape=_sds(gamma.shape, F32))(gamma)


def _lower_bounds_bwd(name, gamma, dlb):
    def body(g_ref, d_ref, o_ref):
        g0, g1 = g_ref[0:1, :], g_ref[1:2, :]
        m = jnp.maximum(g0, g1)
        e0, e1 = jnp.exp(g0 - m), jnp.exp(g1 - m)
        s0, s1 = e0 / (e0 + e1), e1 / (e0 + e1)
        d0, d1 = d_ref[0:1, :], d_ref[1:2, :]
        ds0 = (d0 + d1) - (d0 + d1)
        ds1 = d1
        inner = s0 * ds0 + s1 * ds1
        o_ref[0:1, :] = s0 * (ds0 - inner)
        o_ref[1:2, :] = s1 * (ds1 - inner)

    return _pcall(body, name=name, out_shape=_sds(gamma.shape, F32))(gamma, dlb)


def _heads(x):
    return [x[:, h * HEAD:(h + 1) * HEAD] for h in range(N_HEADS)]


def _hg_post_bwd(name, da, o, z, gain, dz):
    t = z.shape[0]
    tile = 256

    def fn(dav, ov, zg, g):
        d_o, dzg, dgain = [], [], []
        for da_h, o_h, zg_h, g_h in zip(_heads(dav), _heads(ov), _heads(zg), _heads(g)):
            s = _sigmoid(zg_h)
            r = lax.rsqrt(_mean_last(o_h * o_h) + EPS)
            oh = o_h * r
            dy = da_h * (zg_h * s)
            dzg.append(da_h * (oh * g_h) * _dsilu(zg_h, s))
            gdy = dy * g_h
            d_o.append(r * (gdy - oh * _mean_last(gdy * oh)))
            dgain.append(_rowsum(dy * oh))
        return jnp.concatenate(d_o, axis=1), jnp.concatenate(dzg, axis=1), jnp.concatenate(dgain, axis=1)

    return _rows(name, fn, t // tile, [da, o, z, gain],
                 [_tile_spec(tile, D_MODEL)] * 2 + [_tile_spec(tile, D_MODEL, 4), _whole((1, D_MODEL))],
                 [_sds((t, D_MODEL), F32), _sds(dz.shape, dz.dtype), _sds((1, D_MODEL), F32)],
                 [_tile_spec(tile, D_MODEL), _tile_spec(tile, D_MODEL, 4), _whole((1, D_MODEL))],
                 n_red=1, into=[(dz, 1)])


def _place_in_dz(name, piece, blk, dz):
    t = piece.shape[0]
    tile = 1024
    return _rows(name, lambda a: (a,), t // tile, [piece], [_tile_spec(tile, D_MODEL)],
                 [_sds(dz.shape, dz.dtype)], [_tile_spec(tile, D_MODEL, blk)], into=[(dz, 0)])[0]


def _sg_norm(zv, ln_g, ln_b):
    gv = _gelu(zv)
    xc = gv - _mean_last(gv)
    rstd = lax.rsqrt(_mean_last(xc * xc) + EPS)
    xhat = xc * rstd
    return xhat * ln_g + ln_b, xhat, rstd


def _lane_lo():
    return lax.broadcasted_iota(jnp.int32, (SG_CHUNK, LANES), 1) < (LANES // 2)


SG_TILE = 512


def _sgu_fwd(name, z, w, bias_t, ln_g, ln_b):
    t = z.shape[0]

    def fn(zu, zv, wv, bt, lg, lb):
        u = _gelu(zu)
        vn, _, _ = _sg_norm(zv, lg, lb)
        lo = _lane_lo()
        out_rows = []
        for c in range(SG_TILE // SG_CHUNK):
            rs = slice(c * SG_CHUNK, (c + 1) * SG_CHUNK)
            cols = []
            for j in range(SG_WIDTH // LANES):
                cs = slice(j * LANES, (j + 1) * LANES)
                vb = vn[rs, cs]
                sg = jnp.where(lo, _nn(wv[2 * j], vb), _nn(wv[2 * j + 1], vb)) + bt[:, cs]
                cols.append(u[rs, cs] * sg)
            out_rows.append(jnp.concatenate(cols, axis=1))
        return (jnp.concatenate(out_rows, axis=0),)

    return _rows(name, fn, t // SG_TILE, [z, z, w, bias_t, ln_g, ln_b],
                 [_tile_spec(SG_TILE, SG_WIDTH, 10), _tile_spec(SG_TILE, SG_WIDTH, 11), _whole(w.shape),
                  _whole(bias_t.shape), _whole((1, SG_WIDTH)), _whole((1, SG_WIDTH))],
                 [_sds((t, SG_WIDTH), MXU_DTYPE)], [_tile_spec(SG_TILE, SG_WIDTH)])[0]


def _sgu_bwd(name, dbo, z, w, bias_t, ln_g, ln_b, dz):
    t = z.shape[0]
    n_grp = w.shape[0]

    def fn(dbov, zu, zv, wv, bt, lg, lb):
        u = _gelu(zu)
        vn, xhat, rstd = _sg_norm(zv, lg, lb)
        lo = _lane_lo()
        dw = [None] * n_grp
        dsg_sum = None
        du_rows, dvn_rows = [], []
        for c in range(SG_TILE // SG_CHUNK):
            rs = slice(c * SG_CHUNK, (c + 1) * SG_CHUNK)
            du_cols, dvn_cols, dsg_cols = [], [], []
            for j in range(SG_WIDTH // LANES):
                cs = slice(j * LANES, (j + 1) * LANES)
                vb = vn[rs, cs]
                sg = jnp.where(lo, _nn(wv[2 * j], vb), _nn(wv[2 * j + 1], vb)) + bt[:, cs]
                du_cols.append(dbov[rs, cs] * sg)
                dsg = dbov[rs, cs] * u[rs, cs]
                dsg_cols.append(dsg)
                d0 = _nt(jnp.where(lo, dsg, 0.0), vb)
                d1 = _nt(jnp.where(lo, 0.0, dsg), vb)
                dw[2 * j] = d0 if dw[2 * j] is None else dw[2 * j] + d0
                dw[2 * j + 1] = d1 if dw[2 * j + 1] is None else dw[2 * j + 1] + d1
                dvn_cols.append(jnp.where(lo, _tn(wv[2 * j], dsg), _tn(wv[2 * j + 1], dsg)))
            du_rows.append(jnp.concatenate(du_cols, axis=1))
            dvn_rows.append(jnp.concatenate(dvn_cols, axis=1))
            dsg_c = jnp.concatenate(dsg_cols, axis=1)
            dsg_sum = dsg_c if dsg_sum is None else dsg_sum + dsg_c
        du = jnp.concatenate(du_rows, axis=0)
        dvn = jnp.concatenate(dvn_rows, axis=0)
        dxhat = dvn * lg
        dgv = rstd * (dxhat - _mean_last(dxhat) - xhat * _mean_last(dxhat * xhat))
        dzuv = jnp.concatenate([du * _dgelu(zu), dgv * _dgelu(zv)], axis=1)
        return dzuv, jnp.stack(dw, axis=0), dsg_sum, _rowsum(dvn * xhat), _rowsum(dvn)

    return _rows(name, fn, t // SG_TILE, [dbo, z, z, w, bias_t, ln_g, ln_b],
                 [_tile_spec(SG_TILE, SG_WIDTH), _tile_spec(SG_TILE, SG_WIDTH, 10), _tile_spec(SG_TILE, SG_WIDTH, 11),
                  _whole(w.shape), _whole(bias_t.shape), _whole((1, SG_WIDTH)), _whole((1, SG_WIDTH))],
                 [_sds(dz.shape, dz.dtype), _sds(w.shape, F32), _sds((SG_CHUNK, SG_WIDTH), F32),
                  _sds((1, SG_WIDTH), F32), _sds((1, SG_WIDTH), F32)],
                 [_tile_spec(SG_TILE, 2 * SG_WIDTH, 5), _whole(w.shape), _whole((SG_CHUNK, SG_WIDTH)),
                  _whole((1, SG_WIDTH)), _whole((1, SG_WIDTH))],
                 n_red=4, into=[(dz, 0)])


def _merge_bwd(name, dm, pa, pb, z):
    t = z.shape[0]
    tile = 256

    def fn(d, a, b, ga, gb):
        sa, sb = _sigmoid(ga), _sigmoid(gb)
        dgate = jnp.concatenate([d * a * sa * (1.0 - sa), d * b * sb * (1.0 - sb)], axis=1)
        return d * sa, d * sb, dgate

    return _rows(name, fn, t // tile, [dm, pa, pb, z, z],
                 [_tile_spec(tile, D_MODEL)] * 3 + [_tile_spec(tile, D_MODEL, 6), _tile_spec(tile, D_MODEL, 7)],
                 [_sds((t, D_MODEL), MXU_DTYPE), _sds((t, D_MODEL), MXU_DTYPE), _sds((t, N_IN), MXU_DTYPE)],
                 [_tile_spec(tile, D_MODEL), _tile_spec(tile, D_MODEL), _tile_spec(tile, 2 * D_MODEL, 3)])


def _ple_bwd(name, dx, pe, gz):
    t, d = dx.shape
    tile = 512

    def fn(dv, p, g):
        s = _sigmoid(g)
        return dv * s, dv * p * s * (1.0 - s)

    return _rows(name, fn, t // tile, [dx, pe, gz], [_tile_spec(tile, d)] * 3, [_sds((t, d), MXU_DTYPE)] * 2,
                 [_tile_spec(tile, d)] * 2)


def _loss_bwd(name, y, target):
    t, d = y.shape
    tile = 512

    def fn(yv, tv):
        err = yv - tv
        return err * (1.0 / d), _rowsum(err * err)

    return _rows(name, fn, t // tile, [y, target], [_tile_spec(tile, d)] * 2, [_sds((t, d), F32), _sds((1, d), F32)],
                 [_tile_spec(tile, d), _whole((1, d))], n_red=1)


def _position():
    return lax.axis_index("x"), lax.axis_index("y"), lax.axis_index("c")


def _gather_comm(shards, bufs, l0, nl):
    n = len(shards)
    if bufs is None:
        bufs = [_sds((N_CHIPS,) + s.shape, s.dtype) for s in shards]

    def make(w_refs, out_refs, sems):
        send_sems, recv_sems, local_sems = sems
        x, y, c = _position()
        me = 2 * x + y
        sibling = (x, y, 1 - c)
        chips = [(1 - x, y), (x, 1 - y), (1 - x, 1 - y)]

        def half(ref, cc):
            rows = ref.shape[1] // 2
            return ref.at[pl.ds(l0, nl), pl.ds(cc * rows, rows)]

        def copy(i, k, src, chip, cc, to):
            return pltpu.make_async_remote_copy(
                src_ref=src, dst_ref=half(out_refs[i].at[chip], cc), send_sem=send_sems.at[6 * i + k],
                recv_sem=recv_sems.at[6 * i + k], device_id=to, device_id_type=MESH)

        def local(i):
            return pltpu.make_async_copy(w_refs[i].at[pl.ds(l0, nl)], out_refs[i].at[me, pl.ds(l0, nl)], local_sems.at[i])

        def sends():
            return [copy(i, j, half(w_refs[i], c), me, c, (px, py, c))
                    for i in range(n) for j, (px, py) in enumerate(chips)]

        def start():
            for i in range(n):
                local(i).start()
            for cp in sends():
                cp.start()

        def finish():
            passed = []
            for i in range(n):
                for j, (px, py) in enumerate(chips):
                    chip = 2 * px + py
                    copy(i, j, half(w_refs[i], c), chip, c, (px, py, c)).wait_recv()
                    fwd = copy(i, 3 + j, half(out_refs[i].at[chip], c), chip, c, sibling)
                    fwd.start()
                    passed.append(fwd)
            for i in range(n):
                for j, (px, py) in enumerate(chips):
                    copy(i, 3 + j, half(w_refs[i], c), 2 * px + py, 1 - c, sibling).wait_recv()
            for cp in sends() + passed:
                cp.wait_send()
            for i in range(n):
                local(i).wait()

        return start, finish

    sems = [pltpu.SemaphoreType.DMA((6 * n,)), pltpu.SemaphoreType.DMA((6 * n,)), pltpu.SemaphoreType.DMA((n,))]
    return _Comm(shards, bufs, sems, make)


def _exchange_comm(grads, bufs, l0, nl):
    n = len(grads)
    if bufs is None:
        bufs = [_sds((N_DEV,) + g.shape[1:], g.dtype) for g in grads]

    def make(g_refs, out_refs, sems):
        send_sems, recv_sems, local_sems = sems
        x, y, c = _position()
        me = 2 * x + y
        sibling = (x, y, 1 - c)
        chips = [(1 - x, y), (x, 1 - y), (1 - x, 1 - y)]

        def lay(ref):
            return ref.at[pl.ds(l0, nl)]

        def copy(i, k, src, slot, to):
            return pltpu.make_async_remote_copy(
                src_ref=src, dst_ref=lay(out_refs[i].at[slot]), send_sem=send_sems.at[7 * i + k],
                recv_sem=recv_sems.at[7 * i + k], device_id=to, device_id_type=MESH)

        def local(i):
            return pltpu.make_async_copy(lay(g_refs[i].at[me]), lay(out_refs[i].at[2 * me + c]), local_sems.at[i])

        def first():
            cps = []
            for i in range(n):
                cps.append(copy(i, 0, lay(g_refs[i].at[me]), 2 * me + c, sibling))
                for j, (px, py) in enumerate(chips):
                    cps.append(copy(i, 1 + j, lay(g_refs[i].at[2 * px + py]), 2 * me + c, (px, py, c)))
            return cps

        def start():
            for i in range(n):
                local(i).start()
            for cp in first():
                cp.start()

        def finish():
            passed = []
            for i in range(n):
                for j, (px, py) in enumerate(chips):
                    slot = 2 * (2 * px + py) + c
                    copy(i, 1 + j, lay(g_refs[i].at[me]), slot, (px, py, c)).wait_recv()
                    fwd = copy(i, 4 + j, lay(out_refs[i].at[slot]), slot, sibling)
                    fwd.start()
                    passed.append(fwd)
            for i in range(n):
                copy(i, 0, lay(g_refs[i].at[me]), 2 * me + (1 - c), sibling).wait_recv()
                for j, (px, py) in enumerate(chips):
                    copy(i, 4 + j, lay(g_refs[i].at[me]), 2 * (2 * px + py) + (1 - c), sibling).wait_recv()
            for cp in first() + passed:
                cp.wait_send()
            for i in range(n):
                local(i).wait()

        return start, finish

    sems = [pltpu.SemaphoreType.DMA((7 * n,)), pltpu.SemaphoreType.DMA((7 * n,)), pltpu.SemaphoreType.DMA((n,))]
    return _Comm(grads, bufs, sems, make)


def _all_reduce_small(packed):
    rows = packed.shape[0]

    def body(x_ref, sum_ref, slots, send_sems, recv_sems, local_sem):
        x, y, c = _position()
        me = 4 * x + 2 * y + c
        mine = pltpu.make_async_copy(x_ref, slots.at[me], local_sem)
        mine.start()
        sends = []
        for k in range(1, N_DEV):
            peer = (x ^ (k >> 2), y ^ ((k >> 1) & 1), c ^ (k & 1))
            cp = pltpu.make_async_remote_copy(src_ref=x_ref, dst_ref=slots.at[me], send_sem=send_sems.at[k - 1],
                                              recv_sem=recv_sems.at[k - 1], device_id=peer, device_id_type=MESH)
            cp.start()
            sends.append(cp)
        for k in range(1, N_DEV):
            px, py, pc = x ^ (k >> 2), y ^ ((k >> 1) & 1), c ^ (k & 1)
            pltpu.make_async_remote_copy(src_ref=x_ref, dst_ref=slots.at[4 * px + 2 * py + pc], send_sem=send_sems.at[k - 1],
                                         recv_sem=recv_sems.at[k - 1], device_id=(px, py, pc), device_id_type=MESH).wait_recv()
        for cp in sends:
            cp.wait_send()
        mine.wait()
        total = slots[0]
        for d in range(1, N_DEV):
            total = total + slots[d]
        sum_ref[...] = total

    vmem = pl.BlockSpec(memory_space=pltpu.VMEM)
    return _pcall(
        body, name="all_reduce_small", in_specs=[vmem], out_specs=vmem, out_shape=_sds(packed.shape, F32),
        scratch_shapes=[pltpu.VMEM((N_DEV, rows, LANES), F32), pltpu.SemaphoreType.DMA((N_DEV - 1,)),
                        pltpu.SemaphoreType.DMA((N_DEV - 1,)), pltpu.SemaphoreType.DMA],
        compiler_params=pltpu.CompilerParams(vmem_limit_bytes=VMEM_LIMIT_BYTES),
    )(packed)


def _adamw(w, g, m, v):
    m = ADAM_B1 * m + (1.0 - ADAM_B1) * g
    v = ADAM_B2 * v + (1.0 - ADAM_B2) * (g * g)
    m_hat = m / (1.0 - ADAM_B1 ** ADAM_STEP)
    v_hat = v / (1.0 - ADAM_B2 ** ADAM_STEP)
    delta = -ADAM_LR * (m_hat / (jnp.sqrt(v_hat) + ADAM_EPS) + ADAM_WD * w)
    return delta, m, v


def _adam_sharded(name, parts, w, m, v):
    shape = w.shape
    cols = shape[-1]
    rows = w.size // cols
    tile = 8
    while tile * 2 * cols <= ADAM_TILE_ELEMS and rows % (tile * 2) == 0:
        tile *= 2

    def fn(p, wv, mv, vv):
        g = p[0].astype(F32)
        for d in range(1, N_DEV):
            g = g + p[d].astype(F32)
        return (g,) + _adamw(wv, g, mv, vv)

    two_d = lambda a: a.reshape(rows, cols)
    outs = _rows(name, fn, rows // tile, [parts.reshape(N_DEV, rows, cols), two_d(w), two_d(m), two_d(v)],
                 [pl.BlockSpec((N_DEV, tile, cols), lambda i: (0, i, 0))] + [_tile_spec(tile, cols)] * 3,
                 [_sds((rows, cols), F32)] * 4, [_tile_spec(tile, cols)] * 4)
    return [o.reshape(shape) for o in outs]


def _adam_small(name, g, w, m, v):
    rows = g.shape[0]
    tile = rows // 2
    return _rows(name, lambda gv, wv, mv, vv: _adamw(wv, gv, mv, vv), rows // tile, [g, w, m, v],
                 [_tile_spec(tile, LANES)] * 4, [_sds(g.shape, F32)] * 3, [_tile_spec(tile, LANES)] * 3)


BIG = ("w_in", "w_a", "w_b", "w_out", "w_gate", "w_up", "w_down", "w_ple", "w_ple_gate")
SMALL = ("norm_mix_pre", "lb_gamma_fwd", "lb_gamma_bwd", "hg_norm", "sg_w", "sg_b", "sg_ln_g", "sg_ln_b",
         "norm_mix_post", "norm_ffn_pre", "norm_ffn_post")


def _with_comm(plan, tag, state, call):
    if tag not in plan:
        return call(None)
    keys, comm = plan[tag](state)
    res, bufs = call(comm)
    state.update(zip(keys, bufs))
    return res


def _layer_fwd(l, x, p_l, wg, sm, plan):
    sv = {"x": x}
    h = _rms_fwd(f"norm_mix_pre_l{l}", x, sm["norm_mix_pre"])
    z = _with_comm(plan, "proj_in", wg, lambda comm: _proj_in(h, wg["w_in"], l, comm=comm))
    o_f, st_f = _hgrn_fwd(f"hgrn_fwd_l{l}", z, sm["lb_fwd"], False)
    o_sum, st_b, a_out = _hgrn_fwd(f"hgrn_rev_l{l}", z, sm["lb_bwd"], True, post=(o_f, sm["hg_norm"]))
    b_out = _sgu_fwd(f"sgu_l{l}", z, sm["sg_w"], sm["sg_bias_t"], sm["sg_ln_g"], sm["sg_ln_b"])
    pa = _proj_rows_sharded(f"proj_a_l{l}", a_out, wg["w_a"], l, F32)
    pb = _proj_cols256(f"proj_b_l{l}", b_out, wg["w_b"], l)
    merged, mix = _with_comm(plan, "merge_proj_out", wg, lambda comm: _merge_proj_out(
        f"merge_proj_out_l{l}", pa, pb, z, wg["w_out"], l, comm=comm))
    x1, h2 = _resid_rms_norm_fwd(f"norm_mix_post_ffn_pre_l{l}", x, mix, sm["norm_mix_post"], sm["norm_ffn_pre"])
    gt, up, act = _with_comm(plan, "proj_gate_up", wg, lambda comm: _proj_gate_up(
        f"proj_gate_up_l{l}", h2, wg["w_gate"], wg["w_up"], l, comm=comm))
    ff = _with_comm(plan, "proj_down", wg, lambda comm: _proj_ffn_out(f"proj_down_l{l}", act, wg["w_down"], l, comm=comm))
    x2 = _resid_rms_fwd(f"norm_ffn_post_l{l}", x1, ff, sm["norm_ffn_post"])
    pe = _proj_cols256(f"proj_ple_l{l}", p_l, wg["w_ple"], l)
    gz, x3 = _proj_ple_gate_ple(f"proj_ple_gate_ple_l{l}", x2, wg["w_ple_gate"], pe, l)
    sv.update(h=h, z=z, o=o_sum, st_f=st_f, st_b=st_b, a_out=a_out, b_out=b_out, pa=pa, pb=pb,
              merged=merged, mix=mix, x1=x1, h2=h2, gt=gt, up=up, act=act, ff=ff, x2=x2, pe=pe, gz=gz, p=p_l)
    return x3, sv


def _layer_bwd(l, dx3, sv, wg, sm, gw, parts, plan):
    t = dx3.shape[0]
    nt = t // TT
    sg = {}

    def wgrad(key, *a, **k):
        gw[key] = _wgrad(f"grad_{key}_l{l}", *a, l=l, into=gw.get(key), **k)

    row = lambda width: _bs((TT, width), lambda j, i: (i, 0))
    row_j = lambda width: _bs((TT, width), lambda j, i: (i, j))
    ffn_j = _bs((None, TT, FFN_SHARD), lambda j, i: (j, i, 0))
    blk_j = lambda shape: (tuple(shape), lambda j, i: (j, l, 0, 0))

    dpe, dgz = _ple_bwd(f"ple_bwd_l{l}", dx3, sv["pe"], sv["gz"])
    wgrad("w_ple", sv["p"], row(PLE_DIM), dpe, row_j(256), (PLE_DIM, 256), blk_j((PLE_DIM, 256)), N_CHIPS)
    wgrad("w_ple_gate", sv["x2"], row_j(256), dgz, row(D_MODEL), (256, D_MODEL), blk_j((256, D_MODEL)), N_CHIPS)
    dx2 = _proj_rows_sharded_t(f"proj_ple_gate_t_l{l}", dgz, wg["w_ple_gate"], l, F32, add=dx3)

    dff, sg["norm_ffn_post"] = _rms_bwd(f"norm_ffn_post_bwd_l{l}", sv["ff"], sm["norm_ffn_post"], dx2, None, MXU_DTYPE)
    dgt, dup = _with_comm(plan, "proj_down_t", parts, lambda comm: _proj_down_t_swiglu_bwd(
        f"proj_down_t_swiglu_bwd_l{l}", dff, wg["w_down"], sv["gt"], sv["up"], l, comm=comm))
    wgrad("w_down", sv["act"], ffn_j, dff, row(D_MODEL), (FFN_SHARD, D_MODEL), blk_j((FFN_SHARD, D_MODEL)), N_CHIPS)
    dh2 = _with_comm(plan, "proj_gate_up_t", parts, lambda comm: _proj_ffn_in_t(
        f"proj_gate_up_t_l{l}", [(dgt, wg["w_gate"]), (dup, wg["w_up"])], l, comm=comm))
    wgrad("w_gate", sv["h2"], row(D_MODEL), dgt, ffn_j, (D_MODEL, FFN_SHARD), blk_j((D_MODEL, FFN_SHARD)), N_CHIPS)
    wgrad("w_up", sv["h2"], row(D_MODEL), dup, ffn_j, (D_MODEL, FFN_SHARD), blk_j((D_MODEL, FFN_SHARD)), N_CHIPS)
    dx1, dmix, sg["norm_ffn_pre"], sg["norm_mix_post"] = _rms_bwd_pair(
        f"norm_ffn_pre_mix_post_bwd_l{l}", sv["x1"], sm["norm_ffn_pre"], dh2, dx2, sv["mix"], sm["norm_mix_post"], MXU_DTYPE)

    dmerged = _proj_rows_sharded_t(f"proj_out_t_l{l}", dmix, wg["w_out"], l, F32)
    wgrad("w_out", sv["merged"], row_j(256), dmix, row(D_MODEL), (256, D_MODEL), blk_j((256, D_MODEL)), N_CHIPS)
    dpa, dpb, dz = _merge_bwd(f"merge_bwd_l{l}", dmerged, sv["pa"], sv["pb"], sv["z"])
    da = _proj_rows_sharded_t(f"proj_a_t_l{l}", dpa, wg["w_a"], l, F32)
    wgrad("w_a", sv["a_out"], row_j(256), dpa, row(D_MODEL), (256, D_MODEL), blk_j((256, D_MODEL)), N_CHIPS)
    dbo = _proj_cols256_t(f"proj_b_t_l{l}", dpb, wg["w_b"], l)
    wgrad("w_b", sv["b_out"], row(SG_WIDTH), dpb, row_j(256), (SG_WIDTH, 256), blk_j((SG_WIDTH, 256)), N_CHIPS)

    dz, sg["sg_w"], dsg_sum, sg["sg_ln_g"], sg["sg_ln_b"] = _sgu_bwd(
        f"sgu_bwd_l{l}", dbo, sv["z"], sm["sg_w"], sm["sg_bias_t"], sm["sg_ln_g"], sm["sg_ln_b"], dz)
    sg["sg_b"] = dsg_sum.reshape(SG_CHUNK, N_HEADS, SG_WIDTH // N_HEADS).sum(axis=-1).T
    d_o, dz, sg["hg_norm"] = _hg_post_bwd(f"hg_post_bwd_l{l}", da, sv["o"], sv["z"], sm["hg_norm"], dz)
    dq_f, dv_f, dz, sg["lb_fwd"] = _with_comm(plan, "hgrn_fwd_bwd", parts, lambda comm: _hgrn_bwd(
        f"hgrn_fwd_bwd_l{l}", sv["z"], d_o, sv["st_f"], sm["lb_fwd"], False, dz, comm=comm))
    dzq, dzi, dz, sg["lb_bwd"] = _hgrn_bwd(f"hgrn_rev_bwd_l{l}", sv["z"], d_o, sv["st_b"], sm["lb_bwd"], True, dz,
                                           prev=(dq_f, dv_f))
    dz = _place_in_dz(f"place_dzq_l{l}", dzq, 0, dz)
    dz = _place_in_dz(f"place_dzi_l{l}", dzi, 3, dz)

    gw["w_in"] = _wgrad(f"grad_w_in_l{l}", sv["h"], _bs((TM_WIDE, D_MODEL), lambda n, i: (i, 0)), dz,
                        _bs((TM_WIDE, 1024), lambda n, i: (i, n)), (D_MODEL, 2048),
                        ((D_MODEL, 1024), lambda n, i: (n // 2, l, 0, n % 2)), 8, l, gw.get("w_in"), tt=TM_WIDE)
    dh = _with_comm(plan, "proj_in_t", parts, lambda comm: _proj_in_t(dz, wg["w_in"], l, comm=comm))
    dx, sg["norm_mix_pre"] = _rms_bwd(f"norm_mix_pre_bwd_l{l}", sv["x"], sm["norm_mix_pre"], dh, dx1, F32)
    del nt
    return dx, gw, sg


def _pack(parts):
    return jnp.concatenate([a.reshape(-1, LANES) for a in parts], axis=0)


def _step(x, p, loss_target, w, m, v):
    x = x[0]
    target = loss_target[0]
    depth = w["w_in"].shape[0]

    assert depth == 2, "the exchanges below ride in layer 0's kernels and carry layer 1's data"
    shards = {k: w[k].astype(MXU_DTYPE) for k in BIG}
    rest_a = [k for k in BIG if k not in ("w_in", "w_gate", "w_up")]
    rest_b = ["w_gate", "w_up"]
    rest = rest_a + rest_b

    def gather(keys, l0, extend):
        return lambda wg: (keys, _gather_comm([shards[k] for k in keys], [wg[k] for k in keys] if extend else None, l0, 1))

    wg = {"w_in": _run_comm("gather_w_in_l0", gather(["w_in"], 0, False)(None)[1])[0]}
    fwd_plans = [{"proj_in": gather(rest, 0, False), "merge_proj_out": gather(["w_in"], 1, True),
                  "proj_gate_up": gather(rest_a, 1, True), "proj_down": gather(rest_b, 1, True)}, {}]
    lb_f = _lower_bounds("lower_bounds_fwd", w["lb_gamma_fwd"])
    lb_b = _lower_bounds("lower_bounds_bwd", w["lb_gamma_bwd"])

    def small_of(l):
        sm = {k: w[k][l:l + 1] for k in ("norm_mix_pre", "hg_norm", "sg_ln_g", "sg_ln_b", "norm_mix_post",
                                        "norm_ffn_pre", "norm_ffn_post")}
        sm["lb_fwd"], sm["lb_bwd"] = lb_f[l:l + 1], lb_b[l:l + 1]
        sm["sg_w"] = w["sg_w"][l]
        sm["sg_bias_t"] = jnp.repeat(w["sg_b"][l].T, SG_WIDTH // N_HEADS, axis=1)
        return sm

    saved = []
    h = x
    for l in range(depth):
        h, sv = _layer_fwd(l, h, p[l, 0], wg, small_of(l), fwd_plans[l])
        saved.append(sv)

    dy, sq_err = _loss_bwd("loss", h, target)
    gw, parts = {}, {}

    def exchange(keys, l0, extend):
        return lambda parts: (keys, _exchange_comm([gw[k] for k in keys], [parts[k] for k in keys] if extend else None, l0, 1))

    bwd_plans = [{"proj_down_t": exchange(["w_in"], 1, False), "proj_gate_up_t": exchange(rest, 1, False),
                  "hgrn_fwd_bwd": exchange(rest, 0, True), "proj_in_t": exchange(["w_in"], 0, True)}, {}]
    small_grads = [None] * depth
    for l in reversed(range(depth)):
        dy, gw, small_grads[l] = _layer_bwd(l, dy, saved[l], wg, small_of(l), gw, parts, bwd_plans[l])

    def stack(key):
        return jnp.concatenate([small_grads[l][key].reshape((1,) + w_shape[1:]) for l in range(depth)], axis=0)

    g_small = {}
    for key in SMALL:
        w_shape = w[key].shape
        if key == "lb_gamma_fwd":
            dlb = jnp.concatenate([small_grads[l]["lb_fwd"] for l in range(depth)], axis=0)
            g_small[key] = _lower_bounds_bwd("lower_bounds_fwd_bwd", w[key], dlb)
        elif key == "lb_gamma_bwd":
            dlb = jnp.concatenate([small_grads[l]["lb_bwd"] for l in range(depth)], axis=0)
            g_small[key] = _lower_bounds_bwd("lower_bounds_bwd_bwd", w[key], dlb)
        else:
            g_small[key] = stack(key)

    packed = _pack([g_small[k] for k in SMALL] + [sq_err])
    summed = _all_reduce_small(packed)
    n_small_rows = sum(w[k].size for k in SMALL) // LANES
    loss = 0.5 * jnp.sum(summed[n_small_rows:]) / D_MODEL

    g_rows = summed[:n_small_rows]
    d_rows, m_rows, v_rows = _adam_small("adamw_small", g_rows, _pack([w[k] for k in SMALL]),
                                         _pack([m[k] for k in SMALL]), _pack([v[k] for k in SMALL]))
    out = {}
    off = 0
    for key in SMALL:
        n_rows = w[key].size // LANES
        sl = slice(off, off + n_rows)
        out[key] = tuple(a[sl].reshape(w[key].shape) for a in (g_rows, d_rows, m_rows, v_rows))
        off += n_rows

    for key in BIG:
        out[key] = tuple(_adam_sharded(f"adamw_{key}", parts[key], w[key], m[key], v[key]))
    return loss, dy[None], out


WEIGHTS = ("norm_mix_pre", "w_in", "lb_gamma_fwd", "lb_gamma_bwd", "hg_norm", "sg_w", "sg_b", "sg_ln_g", "sg_ln_b",
           "w_a", "w_b", "w_out", "norm_mix_post", "norm_ffn_pre", "w_gate", "w_up", "w_down", "norm_ffn_post",
           "w_ple", "w_ple_gate")


def kernel(x, p, norm_mix_pre, w_in, lb_gamma_fwd, lb_gamma_bwd, hg_norm, sg_w, sg_b, sg_ln_g, sg_ln_b, w_a, w_b, w_out, norm_mix_post, norm_ffn_pre, w_gate, w_up, w_down, norm_ffn_post, w_ple, w_ple_gate, loss_target, m_norm_mix_pre, m_w_in, m_lb_gamma_fwd, m_lb_gamma_bwd, m_hg_norm, m_sg_w, m_sg_b, m_sg_ln_g, m_sg_ln_b, m_w_a, m_w_b, m_w_out, m_norm_mix_post, m_norm_ffn_pre, m_w_gate, m_w_up, m_w_down, m_norm_ffn_post, m_w_ple, m_w_ple_gate, v_norm_mix_pre, v_w_in, v_lb_gamma_fwd, v_lb_gamma_bwd, v_hg_norm, v_sg_w, v_sg_b, v_sg_ln_g, v_sg_ln_b, v_w_a, v_w_b, v_w_out, v_norm_mix_post, v_norm_ffn_pre, v_w_gate, v_w_up, v_w_down, v_norm_ffn_post, v_w_ple, v_w_ple_gate):
    w = dict(zip(WEIGHTS, (norm_mix_pre, w_in, lb_gamma_fwd, lb_gamma_bwd, hg_norm, sg_w, sg_b, sg_ln_g, sg_ln_b, w_a, w_b, w_out, norm_mix_post, norm_ffn_pre, w_gate, w_up, w_down, norm_ffn_post, w_ple, w_ple_gate)))
    m = dict(zip(WEIGHTS, (m_norm_mix_pre, m_w_in, m_lb_gamma_fwd, m_lb_gamma_bwd, m_hg_norm, m_sg_w, m_sg_b, m_sg_ln_g, m_sg_ln_b, m_w_a, m_w_b, m_w_out, m_norm_mix_post, m_norm_ffn_pre, m_w_gate, m_w_up, m_w_down, m_norm_ffn_post, m_w_ple, m_w_ple_gate)))
    v = dict(zip(WEIGHTS, (v_norm_mix_pre, v_w_in, v_lb_gamma_fwd, v_lb_gamma_bwd, v_hg_norm, v_sg_w, v_sg_b, v_sg_ln_g, v_sg_ln_b, v_w_a, v_w_b, v_w_out, v_norm_mix_post, v_norm_ffn_pre, v_w_gate, v_w_up, v_w_down, v_norm_ffn_post, v_w_ple, v_w_ple_gate)))
    loss, grad_x, out = _step(x, p, loss_target, w, m, v)
    res = [loss, grad_x]
    for i in range(4):
        res += [out[k][i] for k in WEIGHTS]
    return tuple(res)
```

```python
import functools

import jax
import jax.numpy as jnp
from jax import lax
from jax.experimental import pallas as pl
from jax.experimental.pallas import tpu as pltpu

F32 = jnp.float32
MXU_DTYPE = jnp.bfloat16
GRAD_EXCHANGE_DTYPE = jnp.bfloat16

D_MODEL = 1024
N_HEADS = 8
HEAD = 128
HG_CHUNK = 64
HG_SUB = 16
HG_BLOCK = 512
HG_SAFE_EXP = 80.0
SG_CHUNK = 128
SG_WIDTH = 512
FFN_SHARD = 704
PLE_DIM = 256
N_IN = 8192
N_CHIPS = 4
N_DEV = 8
EPS = 1e-6
LANES = 128
VMEM_LIMIT_BYTES = 56 * 2 ** 20

ADAM_LR = 0.001
ADAM_B1 = 0.9
ADAM_B2 = 0.999
ADAM_EPS = 1e-08
ADAM_WD = 0.01
ADAM_STEP = 10
ADAM_TILE_ELEMS = 128 * 1024

MESH = pl.DeviceIdType.MESH
ANY = pl.BlockSpec(memory_space=pl.ANY)


def _pcall(body, **kw):
    return pl.pallas_call(body, **kw)


def _params(n_axes):
    return pltpu.CompilerParams(dimension_semantics=("arbitrary",) * n_axes, vmem_limit_bytes=VMEM_LIMIT_BYTES)


def _dot(a, b, ca, cb):
    return lax.dot_general(a.astype(MXU_DTYPE), b.astype(MXU_DTYPE), (((ca,), (cb,)), ((), ())),
                           preferred_element_type=F32)


def _dot_f32(a, b, ca, cb):
    return lax.dot_general(a, b, (((ca,), (cb,)), ((), ())), precision=lax.Precision.HIGH,
                           preferred_element_type=F32)


def _nn(a, b):
    return _dot(a, b, 1, 0)


def _nt(a, b):
    return _dot(a, b, 1, 1)


def _tn(a, b):
    return _dot(a, b, 0, 0)


NN, NT, TN = (1, 0), (1, 1), (0, 0)


def _sigmoid(x):
    return jax.nn.sigmoid(x)


def _dsilu(x, s):
    return s * (1.0 + x * (1.0 - s))


_SQRT_HALF = 0.7071067811865476
_INV_SQRT_2PI = 0.3989422804014327


def _gelu(x):
    return 0.5 * x * (1.0 + lax.erf(x * _SQRT_HALF))


def _dgelu(x):
    return 0.5 * (1.0 + lax.erf(x * _SQRT_HALF)) + x * jnp.exp(-0.5 * x * x) * _INV_SQRT_2PI


def _mean_last(x):
    return jnp.mean(x, axis=-1, keepdims=True)


def _rowsum(x):
    return jnp.sum(x, axis=0, keepdims=True)


class _Comm:
    def __init__(self, ins, bufs, sem_shapes, make):
        self.ins, self.bufs, self.sem_shapes, self.make = list(ins), list(bufs), list(sem_shapes), make
        self.extends = not isinstance(self.bufs[0], jax.ShapeDtypeStruct)


def _hosted_call(name, compute, grid, args, in_specs, out_shapes, out_specs, scratch, aliases, comm):
    n_in, n_out, n_scr = len(args), len(out_shapes), len(scratch)
    if comm is None:
        def plain(*refs):
            compute(refs[:n_in], refs[n_in:n_in + n_out], refs[n_in + n_out:])

        res = _pcall(plain, name=name, grid=grid, in_specs=list(in_specs), out_specs=list(out_specs),
                     out_shape=list(out_shapes), scratch_shapes=list(scratch), input_output_aliases=dict(aliases),
                     compiler_params=_params(len(grid)))(*args)
        return list(res), []

    n_cin, n_buf = len(comm.ins), len(comm.bufs)
    all_args = list(args) + comm.ins + (comm.bufs if comm.extends else [])
    n_all = len(all_args)
    all_aliases = dict(aliases)
    if comm.extends:
        for j in range(n_buf):
            all_aliases[n_in + n_cin + j] = n_out + j
    buf_shapes = [_sds(b.shape, b.dtype) for b in comm.bufs]

    def body(*refs):
        outs = refs[n_all:n_all + n_out + n_buf]
        scr = refs[n_all + n_out + n_buf:]
        start, finish = comm.make(refs[n_in:n_in + n_cin], outs[n_out:], scr[n_scr:])
        first, last = None, None
        for axis, size in enumerate(grid):
            i = pl.program_id(axis)
            first = (i == 0) if first is None else jnp.logical_and(first, i == 0)
            last = (i == size - 1) if last is None else jnp.logical_and(last, i == size - 1)
        pl.when(first)(start)
        compute(refs[:n_in], outs[:n_out], scr[:n_scr])
        pl.when(last)(finish)

    res = _pcall(body, name=name, grid=grid, in_specs=list(in_specs) + [ANY] * (n_all - n_in),
                 out_specs=list(out_specs) + [ANY] * n_buf, out_shape=list(out_shapes) + buf_shapes,
                 scratch_shapes=list(scratch) + comm.sem_shapes, input_output_aliases=all_aliases,
                 compiler_params=_params(len(grid)))(*all_args)
    return list(res[:n_out]), list(res[n_out:])


def _run_comm(name, comm):
    n_cin, n_buf = len(comm.ins), len(comm.bufs)
    all_args = comm.ins + (comm.bufs if comm.extends else [])
    n_all = len(all_args)

    def body(*refs):
        start, finish = comm.make(refs[:n_cin], refs[n_all:n_all + n_buf], refs[n_all + n_buf:])
        start()
        finish()

    res = _pcall(body, name=name, in_specs=[ANY] * n_all, out_specs=[ANY] * n_buf,
                 out_shape=[_sds(b.shape, b.dtype) for b in comm.bufs], scratch_shapes=comm.sem_shapes,
                 input_output_aliases={n_cin + j: j for j in range(n_buf)} if comm.extends else {})(*all_args)
    return list(res)


def _mm(name, pairs, kind, out_shape, grid, in_specs, out_spec, *, reduce_axis=None, add=None,
        add_spec=None, into=None, prep=None, comm=None):
    n_pairs = len(pairs)
    has_add = add is not None
    staged = reduce_axis is not None and out_shape.dtype != F32

    def compute(in_refs, out_refs, scr):
        o_ref = out_refs[0]
        acc = None
        for i in range(n_pairs):
            a = in_refs[2 * i][...]
            b = in_refs[2 * i + 1][...]
            if prep is not None:
                b = prep(b)
            prod = _dot(a, b, *kind)
            acc = prod if acc is None else acc + prod
        if has_add:
            acc = acc + in_refs[2 * n_pairs][...]
        if reduce_axis is None:
            o_ref[...] = acc.astype(o_ref.dtype)
        else:
            r = pl.program_id(reduce_axis)
            acc_ref = scr[0] if staged else o_ref

            @pl.when(r == 0)
            def _():
                acc_ref[...] = acc

            @pl.when(r > 0)
            def _():
                acc_ref[...] += acc

            if staged:
                @pl.when(r == grid[reduce_axis] - 1)
                def _():
                    o_ref[...] = acc_ref[...].astype(o_ref.dtype)

    scratch = []
    if staged:
        scratch = [pltpu.VMEM(tuple(d for d in out_spec.block_shape if d is not None), F32)]
    args = [t for pair in pairs for t in pair]
    specs = list(in_specs)
    if has_add:
        args.append(add)
        specs.append(add_spec)
    aliases = {}
    if into is not None:
        aliases = {len(args): 0}
        args.append(into)
        specs.append(ANY)
    outs, bufs = _hosted_call(name, compute, grid, args, specs, [out_shape], [out_spec], scratch, aliases, comm)
    return outs[0] if comm is None else (outs[0], bufs)


def _sds(shape, dtype):
    return jax.ShapeDtypeStruct(tuple(shape), dtype)


def _bs(shape, fn):
    return pl.BlockSpec(tuple(shape), fn)


TM = 1024
TM_WIDE = 2048


def _merge_lead(b):
    return b.reshape(b.shape[0] * b.shape[1], b.shape[2])


def _proj_in(h, w_in_g, l, comm=None):
    t = h.shape[0]
    return _mm(f"proj_in_l{l}", [(h, w_in_g)], NN, _sds((t, N_IN), F32), (8, t // TM_WIDE),
               [_bs((TM_WIDE, D_MODEL), lambda n, m: (m, 0)),
                _bs((None, None, D_MODEL, 1024), lambda n, m: (n // 2, l, 0, n % 2))],
               _bs((TM_WIDE, 1024), lambda n, m: (m, n)), comm=comm)


def _proj_rows_sharded(name, a, w_g, l, out_dtype, add=None):
    t = a.shape[0]
    return _mm(name, [(a, w_g)], NN, _sds((t, D_MODEL), out_dtype), (t // TM,),
               [_bs((TM, D_MODEL), lambda m: (m, 0)),
                _bs((N_CHIPS, None, 256, D_MODEL), lambda m: (0, l, 0, 0))],
               _bs((TM, D_MODEL), lambda m: (m, 0)), prep=_merge_lead, add=add,
               add_spec=_bs((TM, D_MODEL), lambda m: (m, 0)))


def _proj_rows_sharded_t(name, g, w_g, l, out_dtype, add=None):
    t = g.shape[0]
    return _mm(name, [(g, w_g)], NT, _sds((t, D_MODEL), out_dtype), (t // TM,),
               [_bs((TM, D_MODEL), lambda m: (m, 0)),
                _bs((N_CHIPS, None, 256, D_MODEL), lambda m: (0, l, 0, 0))],
               _bs((TM, D_MODEL), lambda m: (m, 0)), prep=_merge_lead, add=add,
               add_spec=_bs((TM, D_MODEL), lambda m: (m, 0)))


def _proj_cols256(name, a, w_g, l):
    t, k = a.shape
    return _mm(name, [(a, w_g)], NN, _sds((t, D_MODEL), F32), (N_CHIPS, t // TM),
               [_bs((TM, k), lambda j, m: (m, 0)),
                _bs((None, None, k, 256), lambda j, m: (j, l, 0, 0))],
               _bs((TM, 256), lambda j, m: (m, j)))


def _proj_cols256_t(name, g, w_g, l):
    t = g.shape[0]
    k = w_g.shape[2]
    return _mm(name, [(g, w_g)], NT, _sds((t, k), F32), (t // TM, N_CHIPS),
               [_bs((TM, 256), lambda m, j: (m, j)),
                _bs((None, None, k, 256), lambda m, j: (j, l, 0, 0))],
               _bs((TM, k), lambda m, j: (m, 0)), reduce_axis=1)


def _proj_gate_up(name, h2, wg_g, wu_g, l, comm=None):
    t = h2.shape[0]

    def compute(in_refs, out_refs, scr):
        h_ref, wg_ref, wu_ref = in_refs
        gt_ref, up_ref, act_ref = out_refs
        h = h_ref[...]
        g = _nn(h, wg_ref[...])
        u = _nn(h, wu_ref[...])
        gt_ref[...] = g
        up_ref[...] = u
        act_ref[...] = ((g * _sigmoid(g)) * u).astype(act_ref.dtype)

    w_spec = _bs((None, None, D_MODEL, FFN_SHARD), lambda j, m: (j, l, 0, 0))
    o_spec = _bs((None, TM, FFN_SHARD), lambda j, m: (j, m, 0))
    outs, bufs = _hosted_call(name, compute, (N_CHIPS, t // TM), [h2, wg_g, wu_g],
                        [_bs((TM, D_MODEL), lambda j, m: (m, 0)), w_spec, w_spec],
                        [_sds((N_CHIPS, t, FFN_SHARD), F32), _sds((N_CHIPS, t, FFN_SHARD), F32),
                         _sds((N_CHIPS, t, FFN_SHARD), MXU_DTYPE)], [o_spec, o_spec, o_spec], [], {}, comm)
    return outs if comm is None else (outs, bufs)


def _proj_ffn_in_t(name, pairs, l, comm=None):
    t = pairs[0][0].shape[1]
    specs = []
    for _ in pairs:
        specs += [_bs((None, TM, FFN_SHARD), lambda m, j: (j, m, 0)),
                  _bs((None, None, D_MODEL, FFN_SHARD), lambda m, j: (j, l, 0, 0))]
    return _mm(name, pairs, NT, _sds((t, D_MODEL), F32), (t // TM, N_CHIPS), specs,
               _bs((TM, D_MODEL), lambda m, j: (m, 0)), reduce_axis=1, comm=comm)


def _proj_ffn_out(name, act, w_g, l, comm=None):
    t = act.shape[1]
    specs = []
    for j in range(N_CHIPS):
        specs += [_bs((None, TM, FFN_SHARD), lambda m, j=j: (j, m, 0)),
                  _bs((None, None, FFN_SHARD, D_MODEL), lambda m, j=j: (j, l, 0, 0))]
    return _mm(name, [(act, w_g)] * N_CHIPS, NN, _sds((t, D_MODEL), F32), (t // TM,), specs,
               _bs((TM, D_MODEL), lambda m: (m, 0)), comm=comm)


def _proj_down_t_swiglu_bwd(name, dff, w_g, gt, up, l, comm=None):
    t = dff.shape[0]

    def compute(in_refs, out_refs, scr):
        d_ref, w_ref, gt_ref, up_ref = in_refs
        dgt_ref, dup_ref = out_refs
        dact = _nt(d_ref[...], w_ref[...])
        g = gt_ref[...]
        s = _sigmoid(g)
        dgt_ref[...] = (dact * up_ref[...] * _dsilu(g, s)).astype(dgt_ref.dtype)
        dup_ref[...] = (dact * (g * s)).astype(dup_ref.dtype)

    o_spec = _bs((None, TM, FFN_SHARD), lambda j, m: (j, m, 0))
    outs, bufs = _hosted_call(name, compute, (N_CHIPS, t // TM), [dff, w_g, gt, up],
                        [_bs((TM, D_MODEL), lambda j, m: (m, 0)),
                         _bs((None, None, FFN_SHARD, D_MODEL), lambda j, m: (j, l, 0, 0)), o_spec, o_spec],
                        [_sds((N_CHIPS, t, FFN_SHARD), MXU_DTYPE)] * 2, [o_spec, o_spec], [], {}, comm)
    return outs if comm is None else (outs, bufs)


def _proj_in_t(dz, w_in_g, l, comm=None):
    t = dz.shape[0]
    return _mm(f"proj_in_t_l{l}", [(dz, w_in_g)], NT, _sds((t, D_MODEL), F32), (t // TM_WIDE, 8),
               [_bs((TM_WIDE, 1024), lambda m, n: (m, n)),
                _bs((None, None, D_MODEL, 1024), lambda m, n: (n // 2, l, 0, n % 2))],
               _bs((TM_WIDE, D_MODEL), lambda m, n: (m, 0)), reduce_axis=1, comm=comm)


TT = 1024


def _wgrad(name, a, a_spec, g, g_spec, shard_shape, o_map, n_outer, l, into, tt=TT):
    t = a.shape[-2]
    out = _sds((N_CHIPS, 2) + tuple(shard_shape), GRAD_EXCHANGE_DTYPE)
    return _mm(name, [(a, g)], TN, out, (n_outer, t // tt), [a_spec, g_spec],
               _bs((None, None) + tuple(o_map[0]), o_map[1]), reduce_axis=1, into=into)


def _rows(name, fn, n_tiles, ins, in_specs, out_shapes, out_specs, n_red=0, into=()):
    n_in = len(ins)
    n_out = len(out_shapes)

    def body(*refs):
        in_refs = refs[:n_in]
        out_refs = refs[len(refs) - n_out:]
        vals = fn(*[r[...] for r in in_refs])
        if not isinstance(vals, (tuple, list)):
            vals = (vals,)
        first = pl.program_id(0) == 0
        for j in range(n_out):
            o_ref = out_refs[j]
            val = vals[j]
            if j < n_out - n_red:
                o_ref[...] = val.astype(o_ref.dtype)
            else:
                @pl.when(first)
                def _(o_ref=o_ref, val=val):
                    o_ref[...] = val

                @pl.when(jnp.logical_not(first))
                def _(o_ref=o_ref, val=val):
                    o_ref[...] += val

    args = list(ins)
    specs = list(in_specs)
    aliases = {}
    for buf, out_idx in into:
        aliases[len(args)] = out_idx
        args.append(buf)
        specs.append(ANY)
    res = _pcall(body, name=name, grid=(n_tiles,), in_specs=specs, out_specs=list(out_specs),
                 out_shape=list(out_shapes), input_output_aliases=aliases, compiler_params=_params(1))(*args)
    return res


def _tile_spec(tile, width, blk=0):
    return pl.BlockSpec((tile, width), lambda i: (i, blk))


def _whole(shape):
    nd = len(shape)
    return pl.BlockSpec(tuple(shape), lambda i: (0,) * nd)


def _rms(x, g):
    r = lax.rsqrt(_mean_last(x * x) + EPS)
    return (x * r) * g


def _rms_bwd_math(u, g, dy):
    r = lax.rsqrt(_mean_last(u * u) + EPS)
    uh = u * r
    gdy = dy * g
    du = r * (gdy - uh * _mean_last(gdy * uh))
    return du, _rowsum(dy * uh)


def _rms_fwd(name, x, g):
    t, d = x.shape
    tile = 512
    return _rows(name, lambda xv, gv: (_rms(xv, gv),), t // tile, [x, g],
                 [_tile_spec(tile, d), _whole((1, d))], [_sds((t, d), MXU_DTYPE)], [_tile_spec(tile, d)])[0]


def _resid_rms_fwd(name, x, y, g):
    t, d = x.shape
    tile = 512
    return _rows(name, lambda xv, yv, gv: (xv + _rms(yv, gv),), t // tile, [x, y, g],
                 [_tile_spec(tile, d), _tile_spec(tile, d), _whole((1, d))], [_sds((t, d), F32)],
                 [_tile_spec(tile, d)])[0]


def _resid_rms_norm_fwd(name, x, y, g, g_next):
    t, d = x.shape
    tile = 512

    def fn(xv, yv, gv, gn):
        s = xv + _rms(yv, gv)
        return s, _rms(s, gn)

    return _rows(name, fn, t // tile, [x, y, g, g_next],
                 [_tile_spec(tile, d), _tile_spec(tile, d), _whole((1, d)), _whole((1, d))],
                 [_sds((t, d), F32), _sds((t, d), MXU_DTYPE)], [_tile_spec(tile, d)] * 2)


TF = 512


def _merge_proj_out(name, pa, pb, z, w_g, l, comm=None):
    t = z.shape[0]

    def compute(in_refs, out_refs, scr):
        pa_ref, pb_ref, ga_ref, gb_ref, w_ref = in_refs
        mg_ref, mix_ref = out_refs
        merged = _sigmoid(ga_ref[...]) * pa_ref[...] + _sigmoid(gb_ref[...]) * pb_ref[...]
        mg_ref[...] = merged.astype(mg_ref.dtype)
        mix_ref[...] = _nn(merged, _merge_lead(w_ref[...]))

    row = lambda blk: _bs((TF, D_MODEL), lambda m: (m, blk))
    outs, bufs = _hosted_call(name, compute, (t // TF,), [pa, pb, z, z, w_g],
                        [row(0), row(0), row(6), row(7), _bs((N_CHIPS, None, 256, D_MODEL), lambda m: (0, l, 0, 0))],
                        [_sds((t, D_MODEL), MXU_DTYPE), _sds((t, D_MODEL), F32)], [row(0), row(0)], [], {}, comm)
    return outs if comm is None else (outs, bufs)


def _proj_ple_gate_ple(name, x2, w_g, pe, l):
    t = x2.shape[0]

    def body(x_ref, w_ref, pe_ref, gz_ref, x3_ref):
        x = x_ref[...]
        gz = _nn(x, _merge_lead(w_ref[...]))
        gz_ref[...] = gz
        x3_ref[...] = x + pe_ref[...] * _sigmoid(gz)

    row = _bs((TF, D_MODEL), lambda m: (m, 0))
    return _pcall(body, name=name, grid=(t // TF,),
                  in_specs=[row, _bs((N_CHIPS, None, 256, D_MODEL), lambda m: (0, l, 0, 0)), row],
                  out_specs=[row, row], out_shape=[_sds((t, D_MODEL), F32)] * 2,
                  compiler_params=_params(1))(x2, w_g, pe)


def _rms_bwd(name, u, g, dy, resid, out_dtype):
    t, d = u.shape
    tile = 256

    def fn(uv, gv, dyv, *rest):
        du, dg = _rms_bwd_math(uv, gv, dyv)
        if rest:
            du = du + rest[0]
        return du, dg

    ins = [u, g, dy] + ([resid] if resid is not None else [])
    specs = [_tile_spec(tile, d), _whole((1, d)), _tile_spec(tile, d)] + ([_tile_spec(tile, d)] if resid is not None else [])
    return _rows(name, fn, t // tile, ins, specs, [_sds((t, d), out_dtype), _sds((1, d), F32)],
                 [_tile_spec(tile, d), _whole((1, d))], n_red=1)


def _rms_bwd_pair(name, u1, g1, dy1, resid, u2, g2, out2_dtype):
    t, d = u1.shape
    tile = 256

    def fn(u1v, g1v, dy1v, rv, u2v, g2v):
        d1, dg1 = _rms_bwd_math(u1v, g1v, dy1v)
        d1 = d1 + rv
        d2, dg2 = _rms_bwd_math(u2v, g2v, d1)
        return d1, d2, dg1, dg2

    row, vec = _tile_spec(tile, d), _whole((1, d))
    return _rows(name, fn, t // tile, [u1, g1, dy1, resid, u2, g2], [row, vec, row, row, row, vec],
                 [_sds((t, d), F32), _sds((t, d), out2_dtype), _sds((1, d), F32), _sds((1, d), F32)],
                 [row, row, vec, vec], n_red=2)


def _cumsum_rows(x, group, suffix):
    n = x.shape[0]
    pos = lax.broadcasted_iota(jnp.int32, x.shape, 0) % group
    d = 1
    while d < group:
        if suffix:
            x = x + jnp.where(pos < group - d, pltpu.roll(x, n - d, 0), 0.0)
        else:
            x = x + jnp.where(pos >= d, pltpu.roll(x, d, 0), 0.0)
        d *= 2
    return x


def _hg_gates(zq, zf, lb):
    q = zq * _sigmoid(zq)
    f = lb + (1.0 - lb) * _sigmoid(zf)
    logf = jnp.log(jnp.maximum(f, jnp.finfo(F32).tiny))
    k = (1.0 - lb) * _sigmoid(-zf)
    return q, k, logf, f


def _tri_mask(n, rev):
    t_i = lax.broadcasted_iota(jnp.int32, (n, n), 0)
    s_i = lax.broadcasted_iota(jnp.int32, (n, n), 1)
    return (s_i >= t_i) if rev else (s_i <= t_i)


def _anchors_are_safe(b_s, n_chunks, rev):
    worst = None
    for c in range(n_chunks):
        base = c * HG_CHUNK
        first = base + HG_CHUNK - 1 if rev else base
        last = base if rev else base + HG_CHUNK - 1
        mid = base + HG_CHUNK // 2
        b0, bm, bl = b_s[first:first + 1, :], b_s[mid:mid + 1, :], b_s[last:last + 1, :]
        span = jnp.maximum(b0 - bm, bm - bl)
        worst = span if worst is None else jnp.maximum(worst, span)
    return jnp.max(worst) < HG_SAFE_EXP


def _sub_ranges(base, i_sub, rev):
    r0 = base + i_sub * HG_SUB
    r1 = r0 + HG_SUB
    if rev:
        e0, e1, anchor = r1, base + HG_CHUNK, r1
    else:
        e0, e1, anchor = base, r0, r0 - 1
    return r0, r1, e0, e1, anchor


def _hgrn_fwd(name, z, lb_row, rev, post=None):
    t = z.shape[0]
    nb = t // HG_BLOCK
    ncb = HG_BLOCK // HG_CHUNK
    nsb = HG_CHUNK // HG_SUB
    zf0 = 16 if rev else 8
    n_in = 4 if post is None else 7

    def tmap(i):
        return nb - 1 - i if rev else i

    def body(*refs):
        zq_ref, zf_ref, zi_ref, lb_ref = refs[:4]
        o_ref, st_ref = refs[n_in:n_in + 2]
        state, q_s, k_s, v_s, b_s = refs[len(refs) - 5:]

        @pl.when(pl.program_id(1) == 0)
        def _():
            state[...] = jnp.zeros_like(state)

        q, k, logf, _ = _hg_gates(zq_ref[...], zf_ref[...], lb_ref[...])
        v = zi_ref[...]
        b_all = _cumsum_rows(logf, HG_CHUNK, rev)
        b_s[...] = b_all
        order = range(ncb - 1, -1, -1) if rev else range(ncb)
        safe = _anchors_are_safe(b_s, ncb, rev)

        @pl.when(safe)
        def _():
            cmask = _tri_mask(HG_CHUNK, rev)
            ch = []
            for c in range(ncb):
                base = c * HG_CHUNK
                rows = slice(base, base + HG_CHUNK)
                last = base if rev else base + HG_CHUNK - 1
                mid = base + HG_CHUNK // 2
                q_c, k_c, v_c, b = q[rows, :], k[rows, :], v[rows, :], b_all[rows, :]
                bl, bm = b_s[last:last + 1, :], b_s[mid:mid + 1, :]
                ch.append(dict(v=v_c, qe=q_c * jnp.exp(b), el=jnp.exp(bl), q_t=q_c * jnp.exp(b - bm),
                               k_t=k_c * jnp.exp(bm - b), kd=k_c * jnp.exp(bl - b)))
            for d in ch:
                d["a"] = _nt(d["q_t"], d["k_t"])
                d["inc"] = _tn(d["v"], d["kd"])
            for d in ch:
                d["o"] = _nn(jnp.where(cmask, d["a"], 0.0), d["v"])
            st = state[...]
            for c in order:
                d = ch[c]
                st_ref[c] = st
                d["o"] = d["o"] + _nt(d["qe"], st)
                st = st * d["el"] + d["inc"]
            state[...] = st
            o_ref[...] = jnp.concatenate([d["o"] for d in ch], axis=0)

        @pl.when(jnp.logical_not(safe))
        def _():
            q_s[...] = q
            k_s[...] = k
            v_s[...] = v
            mask = _tri_mask(HG_SUB, rev)
            for c in order:
                base = c * HG_CHUNK
                rows = slice(base, base + HG_CHUNK)
                last = base if rev else base + HG_CHUNK - 1
                st = state[...]
                st_ref[c] = st
                b = b_s[rows, :]
                bl = b_s[last:last + 1, :]
                o_inter = _nt(q_s[rows, :] * jnp.exp(b), st)
                kd = k_s[rows, :] * jnp.exp(bl - b)
                state[...] = st * jnp.exp(bl) + _tn(v_s[rows, :], kd)
                parts = []
                for i_sub in range(nsb):
                    r0, r1, e0, e1, anchor = _sub_ranges(base, i_sub, rev)
                    q_i, k_i, b_i = q_s[r0:r1, :], k_s[r0:r1, :], b_s[r0:r1, :]
                    decay = jnp.exp(jnp.minimum(b_i[:, None, :] - b_i[None, :, :], 0.0))
                    a_d = jnp.where(mask, jnp.sum(q_i[:, None, :] * k_i[None, :, :] * decay, axis=-1), 0.0)
                    o_i = _nn(a_d, v_s[r0:r1, :])
                    if e1 > e0:
                        anc = b_s[anchor:anchor + 1, :]
                        q_t = q_i * jnp.exp(b_i - anc)
                        k_t = k_s[e0:e1, :] * jnp.exp(anc - b_s[e0:e1, :])
                        o_i = o_i + _nn(_nt(q_t, k_t), v_s[e0:e1, :])
                    parts.append(o_i)
                o_ref[rows, :] = o_inter + jnp.concatenate(parts, axis=0)

        if post is not None:
            other_ref, zg_ref, gain_ref = refs[4:7]
            o = o_ref[...] + other_ref[...]
            zg = zg_ref[...]
            o_ref[...] = o
            refs[n_in + 2][...] = (_rms(o, gain_ref[...]) * (zg * _sigmoid(zg))).astype(MXU_DTYPE)

    blk = lambda off: pl.BlockSpec((HG_BLOCK, HEAD), lambda h, i: (tmap(i), off + h))
    vec = pl.BlockSpec((1, HEAD), lambda h, i: (0, h))
    args, in_specs = [z, z, z, lb_row], [blk(0), blk(zf0), blk(24), vec]
    out_specs = [blk(0), pl.BlockSpec((None, ncb, HEAD, HEAD), lambda h, i: (h, tmap(i), 0, 0))]
    out_shape = [_sds((t, D_MODEL), F32), _sds((N_HEADS, t // HG_CHUNK, HEAD, HEAD), F32)]
    if post is not None:
        args += [post[0], z, post[1]]
        in_specs += [blk(0), blk(32), vec]
        out_specs.append(blk(0))
        out_shape.append(_sds((t, D_MODEL), MXU_DTYPE))
    return _pcall(
        body, name=name, grid=(N_HEADS, nb), in_specs=in_specs, out_specs=out_specs, out_shape=out_shape,
        scratch_shapes=[pltpu.VMEM((HEAD, HEAD), F32)] + [pltpu.VMEM((HG_BLOCK, HEAD), F32)] * 4,
        compiler_params=_params(2))(*args)


def _hgrn_bwd(name, z, d_o, states, lb_row, rev, dz, comm=None, prev=None, post=None):
    t = z.shape[0]
    nb = t // HG_BLOCK
    ncb = HG_BLOCK // HG_CHUNK
    nsb = HG_CHUNK // HG_SUB
    zf0 = 16 if rev else 8

    def tmap(i):
        return i if rev else nb - 1 - i

    def compute(in_refs, out_refs, scr):
        zq_ref, zf_ref, zi_ref, do_ref, st_ref, lb_ref = in_refs[:6]
        dq_ref, dv_ref, dzf_ref, dlb_ref = out_refs[:4]
        dstate, q_s, k_s, v_s, b_s, dq_s, dk_s, dv_s, db_s = scr
        first = pl.program_id(1) == 0

        @pl.when(first)
        def _():
            dstate[...] = jnp.zeros_like(dstate)

        if post is not None:
            o_ref, zg_ref, gain_ref = in_refs[7 + len(prev or ()):]
            d_o_ref, dzg_ref, dgain_ref = out_refs[4:]
            o_v, zg, gain, da = o_ref[...], zg_ref[...], gain_ref[...], do_ref[...]
            s = _sigmoid(zg)
            r = lax.rsqrt(_mean_last(o_v * o_v) + EPS)
            oh = o_v * r
            dy = da * (zg * s)
            dzg_ref[...] = (da * (oh * gain) * _dsilu(zg, s)).astype(dzg_ref.dtype)
            gdy = dy * gain
            d_o_ref[...] = r * (gdy - oh * _mean_last(gdy * oh))
            dgain = _rowsum(dy * oh)

            @pl.when(first)
            def _():
                dgain_ref[...] = dgain

            @pl.when(jnp.logical_not(first))
            def _():
                dgain_ref[...] += dgain

            do_ref = d_o_ref

        lb = lb_ref[...]
        zf = zf_ref[...]
        q, k, logf, f = _hg_gates(zq_ref[...], zf, lb)
        v = zi_ref[...]
        b_all = _cumsum_rows(logf, HG_CHUNK, rev)
        b_s[...] = b_all
        order = range(ncb) if rev else range(ncb - 1, -1, -1)
        safe = _anchors_are_safe(b_s, ncb, rev)

        @pl.when(safe)
        def _():
            cmask = _tri_mask(HG_CHUNK, rev)
            row_i = lax.broadcasted_iota(jnp.int32, (HG_CHUNK, HEAD), 0)
            ch = []
            for c in range(ncb):
                base = c * HG_CHUNK
                rows = slice(base, base + HG_CHUNK)
                last = base if rev else base + HG_CHUNK - 1
                mid = base + HG_CHUNK // 2
                q_c, k_c, v_c, b, do_c = q[rows, :], k[rows, :], v[rows, :], b_all[rows, :], do_ref[rows, :]
                bl, bm = b_s[last:last + 1, :], b_s[mid:mid + 1, :]
                e, ebl, e_q, e_k = jnp.exp(b), jnp.exp(bl - b), jnp.exp(b - bm), jnp.exp(bm - b)
                ch.append(dict(q=q_c, k=k_c, v=v_c, do=do_c, e=e, el=jnp.exp(bl), ebl=ebl, e_q=e_q, e_k=e_k,
                               q_t=q_c * e_q, k_t=k_c * e_k, kd=k_c * ebl, last=last - base))
            for c, d in enumerate(ch):
                d["a"] = _nt(d["q_t"], d["k_t"])
                d["da"] = _nt(d["do"], d["v"])
                d["dq"] = _nn(d["do"], st_ref[c]) * d["e"]
                d["inc"] = _tn(d["do"], d["q"] * d["e"])
            for d in ch:
                da = jnp.where(cmask, d["da"], 0.0)
                d["dq"] = d["dq"] + d["e_q"] * _dot_f32(da, d["k_t"], 1, 0)
                d["dk_intra"] = d["e_k"] * _dot_f32(da, d["q_t"], 0, 0)
                d["dv"] = _tn(jnp.where(cmask, d["a"], 0.0), d["do"])
            dst = dstate[...]
            for c in order:
                d = ch[c]
                dk_inter = _nn(d["v"], dst) * d["ebl"]
                d["dk"] = dk_inter + d["dk_intra"]
                d["dv"] = _nt(d["kd"], dst) + d["dv"]
                d["extra"] = d["el"] * _rowsum(dst * st_ref[c]) + _rowsum(d["k"] * dk_inter)
                dst = d["inc"] + dst * d["el"]
            dstate[...] = dst
            for d in ch:
                d["db"] = d["q"] * d["dq"] - d["k"] * d["dk"] + jnp.where(row_i == d["last"], d["extra"], 0.0)
            dq_s[...] = jnp.concatenate([d["dq"] for d in ch], axis=0)
            dk_s[...] = jnp.concatenate([d["dk"] for d in ch], axis=0)
            dv_s[...] = jnp.concatenate([d["dv"] for d in ch], axis=0)
            db_s[...] = jnp.concatenate([d["db"] for d in ch], axis=0)

        @pl.when(jnp.logical_not(safe))
        def _():
            q_s[...] = q
            k_s[...] = k
            v_s[...] = v
            mask = _tri_mask(HG_SUB, rev)
            for c in order:
                base = c * HG_CHUNK
                rows = slice(base, base + HG_CHUNK)
                last = base if rev else base + HG_CHUNK - 1
                st0 = st_ref[c]
                dst1 = dstate[...]
                b = b_s[rows, :]
                bl = b_s[last:last + 1, :]
                e = jnp.exp(b)
                el = jnp.exp(bl)
                ebl = jnp.exp(bl - b)
                q_c, k_c, v_c, do_c = q_s[rows, :], k_s[rows, :], v_s[rows, :], do_ref[rows, :]
                kd = k_c * ebl
                dq_s[rows, :] = _nn(do_c, st0) * e
                dk_inter = _nn(v_c, dst1) * ebl
                dk_s[rows, :] = dk_inter
                dv_s[rows, :] = _nt(kd, dst1)
                extra = el * _rowsum(dst1 * st0) + _rowsum(k_c * dk_inter)
                dstate[...] = _tn(do_c, q_c * e) + dst1 * el
                for i_sub in range(nsb):
                    r0, r1, e0, e1, anchor = _sub_ranges(base, i_sub, rev)
                    q_i, k_i, b_i, v_i, do_i = q_s[r0:r1, :], k_s[r0:r1, :], b_s[r0:r1, :], v_s[r0:r1, :], do_ref[r0:r1, :]
                    decay = jnp.exp(jnp.minimum(b_i[:, None, :] - b_i[None, :, :], 0.0))
                    a_d = jnp.where(mask, jnp.sum(q_i[:, None, :] * k_i[None, :, :] * decay, axis=-1), 0.0)
                    da_d = jnp.where(mask, _nt(do_i, v_i), 0.0)
                    wgt = da_d[:, :, None] * decay
                    dq_s[r0:r1, :] += jnp.sum(wgt * k_i[None, :, :], axis=1)
                    dk_s[r0:r1, :] += jnp.sum(wgt * q_i[:, None, :], axis=0)
                    dv_s[r0:r1, :] += _tn(a_d, do_i)
                    if e1 > e0:
                        anc = b_s[anchor:anchor + 1, :]
                        e_q = jnp.exp(b_i - anc)
                        e_k = jnp.exp(anc - b_s[e0:e1, :])
                        q_t = q_i * e_q
                        k_t = k_s[e0:e1, :] * e_k
                        a_o = _nt(q_t, k_t)
                        da_o = _nt(do_i, v_s[e0:e1, :])
                        dq_s[r0:r1, :] += e_q * _nn(da_o, k_t)
                        dk_s[e0:e1, :] += e_k * _tn(da_o, q_t)
                        dv_s[e0:e1, :] += _tn(a_o, do_i)
                db_s[rows, :] = q_c * dq_s[rows, :] - k_c * dk_s[rows, :]
                db_s[last:last + 1, :] += extra

        dlogf = _cumsum_rows(db_s[...], HG_CHUNK, not rev)
        s_neg = _sigmoid(-zf)
        df = jnp.where(f > jnp.finfo(F32).tiny, dlogf / f, 0.0)
        dfk = df - dk_s[...]
        dzf_ref[...] = ((1.0 - lb) * _sigmoid(zf) * s_neg * dfk).astype(dzf_ref.dtype)
        dlb = _rowsum(s_neg * dfk)

        @pl.when(first)
        def _():
            dlb_ref[...] = dlb

        @pl.when(jnp.logical_not(first))
        def _():
            dlb_ref[...] += dlb

        if prev is None:
            dq_ref[...] = dq_s[...]
            dv_ref[...] = dv_s[...]
        else:
            zq = zq_ref[...]
            dq_ref[...] = ((dq_s[...] + in_refs[7][...]) * _dsilu(zq, _sigmoid(zq))).astype(dq_ref.dtype)
            dv_ref[...] = (dv_s[...] + in_refs[8][...]).astype(dv_ref.dtype)

    blk = lambda off: pl.BlockSpec((HG_BLOCK, HEAD), lambda h, i: (tmap(i), off + h))
    vec = pl.BlockSpec((1, HEAD), lambda h, i: (0, h))
    qv_dtype = F32 if prev is None else dz.dtype
    args = [z, z, z, d_o, states, lb_row, dz] + list(prev or ())
    in_specs = [blk(0), blk(zf0), blk(24), blk(0),
                pl.BlockSpec((None, ncb, HEAD, HEAD), lambda h, i: (h, tmap(i), 0, 0)), vec, ANY] + [blk(0)] * len(prev or ())
    out_shapes = [_sds((t, D_MODEL), qv_dtype), _sds((t, D_MODEL), qv_dtype), _sds(dz.shape, dz.dtype), _sds((1, D_MODEL), F32)]
    out_specs = [blk(0), blk(0), blk(zf0), vec]
    if post is not None:
        args += [post[0], z, post[1]]
        in_specs += [blk(0), blk(32), vec]
        out_shapes += [_sds((t, D_MODEL), F32), _sds((t, D_MODEL), dz.dtype), _sds((1, D_MODEL), F32)]
        out_specs += [blk(0), blk(0), vec]
    outs, bufs = _hosted_call(
        name, compute, (N_HEADS, nb), args, in_specs, out_shapes, out_specs,
        [pltpu.VMEM((HEAD, HEAD), F32)] + [pltpu.VMEM((HG_BLOCK, HEAD), F32)] * 8, {6: 2}, comm)
    return outs if comm is None else (outs, bufs)


def _lower_bounds(name, gamma):
    def body(g_ref, o_ref):
        g0, g1 = g_ref[0:1, :], g_ref[1:2, :]
        m = jnp.maximum(g0, g1)
        e0, e1 = jnp.exp(g0 - m), jnp.exp(g1 - m)
        s0, s1 = e0 / (e0 + e1), e1 / (e0 + e1)
        o_ref[0:1, :] = s0 - s0
        o_ref[1:2, :] = (s0 + s1) - s0

    return _pcall(body, name=name, out_shape=_sds(gamma.shape, F32))(gamma)


def _lower_bounds_bwd(name, gamma, dlb):
    def body(g_ref, d_ref, o_ref):
        g0, g1 = g_ref[0:1, :], g_ref[1:2, :]
        m = jnp.maximum(g0, g1)
        e0, e1 = jnp.exp(g0 - m), jnp.exp(g1 - m)
        s0, s1 = e0 / (e0 + e1), e1 / (e0 + e1)
        d0, d1 = d_ref[0:1, :], d_ref[1:2, :]
        ds0 = (d0 + d1) - (d0 + d1)
        ds1 = d1
        inner = s0 * ds0 + s1 * ds1
        o_ref[0:1, :] = s0 * (ds0 - inner)
        o_ref[1:2, :] = s1 * (ds1 - inner)

    return _pcall(body, name=name, out_shape=_sds(gamma.shape, F32))(gamma, dlb)


def _place_in_dz(name, piece, blk, dz):
    t = piece.shape[0]
    tile = 1024
    return _rows(name, lambda a: (a,), t // tile, [piece], [_tile_spec(tile, D_MODEL)],
                 [_sds(dz.shape, dz.dtype)], [_tile_spec(tile, D_MODEL, blk)], into=[(dz, 0)])[0]


def _sg_norm(zv, ln_g, ln_b):
    gv = _gelu(zv)
    xc = gv - _mean_last(gv)
    rstd = lax.rsqrt(_mean_last(xc * xc) + EPS)
    xhat = xc * rstd
    return xhat * ln_g + ln_b, xhat, rstd


def _lane_lo():
    return lax.broadcasted_iota(jnp.int32, (SG_CHUNK, LANES), 1) < (LANES // 2)


SG_TILE = 512


def _sgu_fwd(name, z, w, bias_t, ln_g, ln_b):
    t = z.shape[0]

    def fn(zu, zv, wv, bt, lg, lb):
        u = _gelu(zu)
        vn, _, _ = _sg_norm(zv, lg, lb)
        lo = _lane_lo()
        out_rows = []
        for c in range(SG_TILE // SG_CHUNK):
            rs = slice(c * SG_CHUNK, (c + 1) * SG_CHUNK)
            cols = []
            for j in range(SG_WIDTH // LANES):
                cs = slice(j * LANES, (j + 1) * LANES)
                vb = vn[rs, cs]
                sg = jnp.where(lo, _nn(wv[2 * j], vb), _nn(wv[2 * j + 1], vb)) + bt[:, cs]
                cols.append(u[rs, cs] * sg)
            out_rows.append(jnp.concatenate(cols, axis=1))
        return (jnp.concatenate(out_rows, axis=0),)

    return _rows(name, fn, t // SG_TILE, [z, z, w, bias_t, ln_g, ln_b],
                 [_tile_spec(SG_TILE, SG_WIDTH, 10), _tile_spec(SG_TILE, SG_WIDTH, 11), _whole(w.shape),
                  _whole(bias_t.shape), _whole((1, SG_WIDTH)), _whole((1, SG_WIDTH))],
                 [_sds((t, SG_WIDTH), MXU_DTYPE)], [_tile_spec(SG_TILE, SG_WIDTH)])[0]


def _sgu_bwd(name, dbo, z, w, bias_t, ln_g, ln_b, dz):
    t = z.shape[0]
    n_grp = w.shape[0]

    def fn(dbov, zu, zv, wv, bt, lg, lb):
        u = _gelu(zu)
        vn, xhat, rstd = _sg_norm(zv, lg, lb)
        lo = _lane_lo()
        dw = [None] * n_grp
        dsg_sum = None
        du_rows, dvn_rows = [], []
        for c in range(SG_TILE // SG_CHUNK):
            rs = slice(c * SG_CHUNK, (c + 1) * SG_CHUNK)
            du_cols, dvn_cols, dsg_cols = [], [], []
            for j in range(SG_WIDTH // LANES):
                cs = slice(j * LANES, (j + 1) * LANES)
                vb = vn[rs, cs]
                sg = jnp.where(lo, _nn(wv[2 * j], vb), _nn(wv[2 * j + 1], vb)) + bt[:, cs]
                du_cols.append(dbov[rs, cs] * sg)
                dsg = dbov[rs, cs] * u[rs, cs]
                dsg_cols.append(dsg)
                d0 = _nt(jnp.where(lo, dsg, 0.0), vb)
                d1 = _nt(jnp.where(lo, 0.0, dsg), vb)
                dw[2 * j] = d0 if dw[2 * j] is None else dw[2 * j] + d0
                dw[2 * j + 1] = d1 if dw[2 * j + 1] is None else dw[2 * j + 1] + d1
                dvn_cols.append(jnp.where(lo, _tn(wv[2 * j], dsg), _tn(wv[2 * j + 1], dsg)))
            du_rows.append(jnp.concatenate(du_cols, axis=1))
            dvn_rows.append(jnp.concatenate(dvn_cols, axis=1))
            dsg_c = jnp.concatenate(dsg_cols, axis=1)
            dsg_sum = dsg_c if dsg_sum is None else dsg_sum + dsg_c
        du = jnp.concatenate(du_rows, axis=0)
        dvn = jnp.concatenate(dvn_rows, axis=0)
        dxhat = dvn * lg
        dgv = rstd * (dxhat - _mean_last(dxhat) - xhat * _mean_last(dxhat * xhat))
        dzuv = jnp.concatenate([du * _dgelu(zu), dgv * _dgelu(zv)], axis=1)
        return dzuv, jnp.stack(dw, axis=0), dsg_sum, _rowsum(dvn * xhat), _rowsum(dvn)

    return _rows(name, fn, t // SG_TILE, [dbo, z, z, w, bias_t, ln_g, ln_b],
                 [_tile_spec(SG_TILE, SG_WIDTH), _tile_spec(SG_TILE, SG_WIDTH, 10), _tile_spec(SG_TILE, SG_WIDTH, 11),
                  _whole(w.shape), _whole(bias_t.shape), _whole((1, SG_WIDTH)), _whole((1, SG_WIDTH))],
                 [_sds(dz.shape, dz.dtype), _sds(w.shape, F32), _sds((SG_CHUNK, SG_WIDTH), F32),
                  _sds((1, SG_WIDTH), F32), _sds((1, SG_WIDTH), F32)],
                 [_tile_spec(SG_TILE, 2 * SG_WIDTH, 5), _whole(w.shape), _whole((SG_CHUNK, SG_WIDTH)),
                  _whole((1, SG_WIDTH)), _whole((1, SG_WIDTH))],
                 n_red=4, into=[(dz, 0)])


def _proj_out_t_merge_bwd(name, dmix, w_g, pa, pb, z, l):
    t = z.shape[0]

    def body(d_ref, w_ref, pa_ref, pb_ref, ga_ref, gb_ref, dpa_ref, dpb_ref, dz_ref):
        d = _nt(d_ref[...], _merge_lead(w_ref[...]))
        sa, sb = _sigmoid(ga_ref[...]), _sigmoid(gb_ref[...])
        dpa_ref[...] = (d * sa).astype(dpa_ref.dtype)
        dpb_ref[...] = (d * sb).astype(dpb_ref.dtype)
        dz_ref[...] = jnp.concatenate([d * pa_ref[...] * sa * (1.0 - sa), d * pb_ref[...] * sb * (1.0 - sb)],
                                      axis=1).astype(dz_ref.dtype)

    row = lambda blk: _bs((TF, D_MODEL), lambda m: (m, blk))
    return _pcall(body, name=name, grid=(t // TF,),
                  in_specs=[row(0), _bs((N_CHIPS, None, 256, D_MODEL), lambda m: (0, l, 0, 0)), row(0), row(0), row(6), row(7)],
                  out_specs=[row(0), row(0), _bs((TF, 2 * D_MODEL), lambda m: (m, 3))],
                  out_shape=[_sds((t, D_MODEL), MXU_DTYPE), _sds((t, D_MODEL), MXU_DTYPE), _sds((t, N_IN), MXU_DTYPE)],
                  compiler_params=_params(1))(dmix, w_g, pa, pb, z, z)


def _ple_bwd(name, dx, pe, gz):
    t, d = dx.shape
    tile = 512

    def fn(dv, p, g):
        s = _sigmoid(g)
        return dv * s, dv * p * s * (1.0 - s)

    return _rows(name, fn, t // tile, [dx, pe, gz], [_tile_spec(tile, d)] * 3, [_sds((t, d), MXU_DTYPE)] * 2,
                 [_tile_spec(tile, d)] * 2)


def _loss_bwd(name, y, target):
    t, d = y.shape
    tile = 512

    def fn(yv, tv):
        err = yv - tv
        return err * (1.0 / d), _rowsum(err * err)

    return _rows(name, fn, t // tile, [y, target], [_tile_spec(tile, d)] * 2, [_sds((t, d), F32), _sds((1, d), F32)],
                 [_tile_spec(tile, d), _whole((1, d))], n_red=1)


def _position():
    return lax.axis_index("x"), lax.axis_index("y"), lax.axis_index("c")


def _gather_comm(shards, bufs, l0, nl):
    n = len(shards)
    if bufs is None:
        bufs = [_sds((N_CHIPS,) + s.shape, s.dtype) for s in shards]

    def make(w_refs, out_refs, sems):
        send_sems, recv_sems, local_sems = sems
        x, y, c = _position()
        me = 2 * x + y
        sibling = (x, y, 1 - c)
        chips = [(1 - x, y), (x, 1 - y), (1 - x, 1 - y)]

        def half(ref, cc):
            rows = ref.shape[1] // 2
            return ref.at[pl.ds(l0, nl), pl.ds(cc * rows, rows)]

        def copy(i, k, src, chip, cc, to):
            return pltpu.make_async_remote_copy(
                src_ref=src, dst_ref=half(out_refs[i].at[chip], cc), send_sem=send_sems.at[6 * i + k],
                recv_sem=recv_sems.at[6 * i + k], device_id=to, device_id_type=MESH)

        def local(i):
            return pltpu.make_async_copy(w_refs[i].at[pl.ds(l0, nl)], out_refs[i].at[me, pl.ds(l0, nl)], local_sems.at[i])

        def sends():
            return [copy(i, j, half(w_refs[i], c), me, c, (px, py, c))
                    for i in range(n) for j, (px, py) in enumerate(chips)]

        def start():
            for i in range(n):
                local(i).start()
            for cp in sends():
                cp.start()

        def finish():
            passed = []
            for i in range(n):
                for j, (px, py) in enumerate(chips):
                    chip = 2 * px + py
                    copy(i, j, half(w_refs[i], c), chip, c, (px, py, c)).wait_recv()
                    fwd = copy(i, 3 + j, half(out_refs[i].at[chip], c), chip, c, sibling)
                    fwd.start()
                    passed.append(fwd)
            for i in range(n):
                for j, (px, py) in enumerate(chips):
                    copy(i, 3 + j, half(w_refs[i], c), 2 * px + py, 1 - c, sibling).wait_recv()
            for cp in sends() + passed:
                cp.wait_send()
            for i in range(n):
                local(i).wait()

        return start, finish

    sems = [pltpu.SemaphoreType.DMA((6 * n,)), pltpu.SemaphoreType.DMA((6 * n,)), pltpu.SemaphoreType.DMA((n,))]
    return _Comm(shards, bufs, sems, make)


def _exchange_comm(grads, bufs, l0, nl):
    n = len(grads)
    if bufs is None:
        bufs = [_sds((N_DEV,) + g.shape[1:], g.dtype) for g in grads]

    def make(g_refs, out_refs, sems):
        send_sems, recv_sems, local_sems = sems
        x, y, c = _position()
        me = 2 * x + y
        sibling = (x, y, 1 - c)
        chips = [(1 - x, y), (x, 1 - y), (1 - x, 1 - y)]

        def lay(ref):
            return ref.at[pl.ds(l0, nl)]

        def copy(i, k, src, slot, to):
            return pltpu.make_async_remote_copy(
                src_ref=src, dst_ref=lay(out_refs[i].at[slot]), send_sem=send_sems.at[7 * i + k],
                recv_sem=recv_sems.at[7 * i + k], device_id=to, device_id_type=MESH)

        def local(i):
            return pltpu.make_async_copy(lay(g_refs[i].at[me]), lay(out_refs[i].at[2 * me + c]), local_sems.at[i])

        def first():
            cps = []
            for i in range(n):
                cps.append(copy(i, 0, lay(g_refs[i].at[me]), 2 * me + c, sibling))
                for j, (px, py) in enumerate(chips):
                    cps.append(copy(i, 1 + j, lay(g_refs[i].at[2 * px + py]), 2 * me + c, (px, py, c)))
            return cps

        def start():
            for i in range(n):
                local(i).start()
            for cp in first():
                cp.start()

        def finish():
            passed = []
            for i in range(n):
                for j, (px, py) in enumerate(chips):
                    slot = 2 * (2 * px + py) + c
                    copy(i, 1 + j, lay(g_refs[i].at[me]), slot, (px, py, c)).wait_recv()
                    fwd = copy(i, 4 + j, lay(out_refs[i].at[slot]), slot, sibling)
                    fwd.start()
                    passed.append(fwd)
            for i in range(n):
                copy(i, 0, lay(g_refs[i].at[me]), 2 * me + (1 - c), sibling).wait_recv()
                for j, (px, py) in enumerate(chips):
                    copy(i, 4 + j, lay(g_refs[i].at[me]), 2 * (2 * px + py) + (1 - c), sibling).wait_recv()
            for cp in first() + passed:
                cp.wait_send()
            for i in range(n):
                local(i).wait()

        return start, finish

    sems = [pltpu.SemaphoreType.DMA((7 * n,)), pltpu.SemaphoreType.DMA((7 * n,)), pltpu.SemaphoreType.DMA((n,))]
    return _Comm(grads, bufs, sems, make)


def _all_reduce_small(packed):
    rows = packed.shape[0]

    def body(x_ref, sum_ref, slots, send_sems, recv_sems, local_sem):
        x, y, c = _position()
        me = 4 * x + 2 * y + c
        mine = pltpu.make_async_copy(x_ref, slots.at[me], local_sem)
        mine.start()
        sends = []
        for k in range(1, N_DEV):
            peer = (x ^ (k >> 2), y ^ ((k >> 1) & 1), c ^ (k & 1))
            cp = pltpu.make_async_remote_copy(src_ref=x_ref, dst_ref=slots.at[me], send_sem=send_sems.at[k - 1],
                                              recv_sem=recv_sems.at[k - 1], device_id=peer, device_id_type=MESH)
            cp.start()
            sends.append(cp)
        for k in range(1, N_DEV):
            px, py, pc = x ^ (k >> 2), y ^ ((k >> 1) & 1), c ^ (k & 1)
            pltpu.make_async_remote_copy(src_ref=x_ref, dst_ref=slots.at[4 * px + 2 * py + pc], send_sem=send_sems.at[k - 1],
                                         recv_sem=recv_sems.at[k - 1], device_id=(px, py, pc), device_id_type=MESH).wait_recv()
        for cp in sends:
            cp.wait_send()
        mine.wait()
        total = slots[0]
        for d in range(1, N_DEV):
            total = total + slots[d]
        sum_ref[...] = total

    vmem = pl.BlockSpec(memory_space=pltpu.VMEM)
    return _pcall(
        body, name="all_reduce_small", in_specs=[vmem], out_specs=vmem, out_shape=_sds(packed.shape, F32),
        scratch_shapes=[pltpu.VMEM((N_DEV, rows, LANES), F32), pltpu.SemaphoreType.DMA((N_DEV - 1,)),
                        pltpu.SemaphoreType.DMA((N_DEV - 1,)), pltpu.SemaphoreType.DMA],
        compiler_params=pltpu.CompilerParams(vmem_limit_bytes=VMEM_LIMIT_BYTES),
    )(packed)


def _adamw(w, g, m, v):
    m = ADAM_B1 * m + (1.0 - ADAM_B1) * g
    v = ADAM_B2 * v + (1.0 - ADAM_B2) * (g * g)
    m_hat = m / (1.0 - ADAM_B1 ** ADAM_STEP)
    v_hat = v / (1.0 - ADAM_B2 ** ADAM_STEP)
    delta = -ADAM_LR * (m_hat / (jnp.sqrt(v_hat) + ADAM_EPS) + ADAM_WD * w)
    return delta, m, v


def _adam_sharded(name, parts, w, m, v):
    shape = w.shape
    cols = shape[-1]
    rows = w.size // cols
    tile = 8
    while tile * 2 * cols <= ADAM_TILE_ELEMS and rows % (tile * 2) == 0:
        tile *= 2

    def fn(p, wv, mv, vv):
        g = p[0].astype(F32)
        for d in range(1, N_DEV):
            g = g + p[d].astype(F32)
        return (g,) + _adamw(wv, g, mv, vv)

    two_d = lambda a: a.reshape(rows, cols)
    outs = _rows(name, fn, rows // tile, [parts.reshape(N_DEV, rows, cols), two_d(w), two_d(m), two_d(v)],
                 [pl.BlockSpec((N_DEV, tile, cols), lambda i: (0, i, 0))] + [_tile_spec(tile, cols)] * 3,
                 [_sds((rows, cols), F32)] * 4, [_tile_spec(tile, cols)] * 4)
    return [o.reshape(shape) for o in outs]


def _adam_small(name, g, w, m, v):
    rows = g.shape[0]
    tile = rows // 2
    return _rows(name, lambda gv, wv, mv, vv: _adamw(wv, gv, mv, vv), rows // tile, [g, w, m, v],
                 [_tile_spec(tile, LANES)] * 4, [_sds(g.shape, F32)] * 3, [_tile_spec(tile, LANES)] * 3)


BIG = ("w_in", "w_a", "w_b", "w_out", "w_gate", "w_up", "w_down", "w_ple", "w_ple_gate")
SMALL = ("norm_mix_pre", "lb_gamma_fwd", "lb_gamma_bwd", "hg_norm", "sg_w", "sg_b", "sg_ln_g", "sg_ln_b",
         "norm_mix_post", "norm_ffn_pre", "norm_ffn_post")


def _with_comm(plan, tag, state, call):
    if tag not in plan:
        return call(None)
    keys, comm = plan[tag](state)
    res, bufs = call(comm)
    state.update(zip(keys, bufs))
    return res


def _layer_fwd(l, x, p_l, wg, sm, plan):
    sv = {"x": x}
    h = _rms_fwd(f"norm_mix_pre_l{l}", x, sm["norm_mix_pre"])
    z = _with_comm(plan, "proj_in", wg, lambda comm: _proj_in(h, wg["w_in"], l, comm=comm))
    o_f, st_f = _hgrn_fwd(f"hgrn_fwd_l{l}", z, sm["lb_fwd"], False)
    o_sum, st_b, a_out = _hgrn_fwd(f"hgrn_rev_l{l}", z, sm["lb_bwd"], True, post=(o_f, sm["hg_norm"]))
    b_out = _sgu_fwd(f"sgu_l{l}", z, sm["sg_w"], sm["sg_bias_t"], sm["sg_ln_g"], sm["sg_ln_b"])
    pa = _proj_rows_sharded(f"proj_a_l{l}", a_out, wg["w_a"], l, F32)
    pb = _proj_cols256(f"proj_b_l{l}", b_out, wg["w_b"], l)
    merged, mix = _with_comm(plan, "merge_proj_out", wg, lambda comm: _merge_proj_out(
        f"merge_proj_out_l{l}", pa, pb, z, wg["w_out"], l, comm=comm))
    x1, h2 = _resid_rms_norm_fwd(f"norm_mix_post_ffn_pre_l{l}", x, mix, sm["norm_mix_post"], sm["norm_ffn_pre"])
    gt, up, act = _with_comm(plan, "proj_gate_up", wg, lambda comm: _proj_gate_up(
        f"proj_gate_up_l{l}", h2, wg["w_gate"], wg["w_up"], l, comm=comm))
    ff = _with_comm(plan, "proj_down", wg, lambda comm: _proj_ffn_out(f"proj_down_l{l}", act, wg["w_down"], l, comm=comm))
    x2 = _resid_rms_fwd(f"norm_ffn_post_l{l}", x1, ff, sm["norm_ffn_post"])
    pe = _proj_cols256(f"proj_ple_l{l}", p_l, wg["w_ple"], l)
    gz, x3 = _proj_ple_gate_ple(f"proj_ple_gate_ple_l{l}", x2, wg["w_ple_gate"], pe, l)
    sv.update(h=h, z=z, o=o_sum, st_f=st_f, st_b=st_b, a_out=a_out, b_out=b_out, pa=pa, pb=pb,
              merged=merged, mix=mix, x1=x1, h2=h2, gt=gt, up=up, act=act, ff=ff, x2=x2, pe=pe, gz=gz, p=p_l)
    return x3, sv


def _layer_bwd(l, dx3, sv, wg, sm, gw, parts, plan):
    t = dx3.shape[0]
    nt = t // TT
    sg = {}

    def wgrad(key, *a, **k):
        gw[key] = _wgrad(f"grad_{key}_l{l}", *a, l=l, into=gw.get(key), **k)

    row = lambda width: _bs((TT, width), lambda j, i: (i, 0))
    row_j = lambda width: _bs((TT, width), lambda j, i: (i, j))
    ffn_j = _bs((None, TT, FFN_SHARD), lambda j, i: (j, i, 0))
    blk_j = lambda shape: (tuple(shape), lambda j, i: (j, l, 0, 0))

    dpe, dgz = _ple_bwd(f"ple_bwd_l{l}", dx3, sv["pe"], sv["gz"])
    wgrad("w_ple", sv["p"], row(PLE_DIM), dpe, row_j(256), (PLE_DIM, 256), blk_j((PLE_DIM, 256)), N_CHIPS)
    wgrad("w_ple_gate", sv["x2"], row_j(256), dgz, row(D_MODEL), (256, D_MODEL), blk_j((256, D_MODEL)), N_CHIPS)
    dx2 = _proj_rows_sharded_t(f"proj_ple_gate_t_l{l}", dgz, wg["w_ple_gate"], l, F32, add=dx3)

    dff, sg["norm_ffn_post"] = _rms_bwd(f"norm_ffn_post_bwd_l{l}", sv["ff"], sm["norm_ffn_post"], dx2, None, MXU_DTYPE)
    dgt, dup = _with_comm(plan, "proj_down_t", parts, lambda comm: _proj_down_t_swiglu_bwd(
        f"proj_down_t_swiglu_bwd_l{l}", dff, wg["w_down"], sv["gt"], sv["up"], l, comm=comm))
    wgrad("w_down", sv["act"], ffn_j, dff, row(D_MODEL), (FFN_SHARD, D_MODEL), blk_j((FFN_SHARD, D_MODEL)), N_CHIPS)
    dh2 = _with_comm(plan, "proj_gate_up_t", parts, lambda comm: _proj_ffn_in_t(
        f"proj_gate_up_t_l{l}", [(dgt, wg["w_gate"]), (dup, wg["w_up"])], l, comm=comm))
    wgrad("w_gate", sv["h2"], row(D_MODEL), dgt, ffn_j, (D_MODEL, FFN_SHARD), blk_j((D_MODEL, FFN_SHARD)), N_CHIPS)
    wgrad("w_up", sv["h2"], row(D_MODEL), dup, ffn_j, (D_MODEL, FFN_SHARD), blk_j((D_MODEL, FFN_SHARD)), N_CHIPS)
    dx1, dmix, sg["norm_ffn_pre"], sg["norm_mix_post"] = _rms_bwd_pair(
        f"norm_ffn_pre_mix_post_bwd_l{l}", sv["x1"], sm["norm_ffn_pre"], dh2, dx2, sv["mix"], sm["norm_mix_post"], MXU_DTYPE)

    dpa, dpb, dz = _proj_out_t_merge_bwd(f"proj_out_t_merge_bwd_l{l}", dmix, wg["w_out"], sv["pa"], sv["pb"], sv["z"], l)
    wgrad("w_out", sv["merged"], row_j(256), dmix, row(D_MODEL), (256, D_MODEL), blk_j((256, D_MODEL)), N_CHIPS)
    da = _proj_rows_sharded_t(f"proj_a_t_l{l}", dpa, wg["w_a"], l, F32)
    wgrad("w_a", sv["a_out"], row_j(256), dpa, row(D_MODEL), (256, D_MODEL), blk_j((256, D_MODEL)), N_CHIPS)
    dbo = _proj_cols256_t(f"proj_b_t_l{l}", dpb, wg["w_b"], l)
    wgrad("w_b", sv["b_out"], row(SG_WIDTH), dpb, row_j(256), (SG_WIDTH, 256), blk_j((SG_WIDTH, 256)), N_CHIPS)

    dz, sg["sg_w"], dsg_sum, sg["sg_ln_g"], sg["sg_ln_b"] = _sgu_bwd(
        f"sgu_bwd_l{l}", dbo, sv["z"], sm["sg_w"], sm["sg_bias_t"], sm["sg_ln_g"], sm["sg_ln_b"], dz)
    sg["sg_b"] = dsg_sum.reshape(SG_CHUNK, N_HEADS, SG_WIDTH // N_HEADS).sum(axis=-1).T
    dq_f, dv_f, dz, sg["lb_fwd"], d_o, dzg, sg["hg_norm"] = _with_comm(plan, "hgrn_fwd_bwd", parts, lambda comm: _hgrn_bwd(
        f"hgrn_fwd_bwd_l{l}", sv["z"], da, sv["st_f"], sm["lb_fwd"], False, dz, comm=comm, post=(sv["o"], sm["hg_norm"])))
    dz = _place_in_dz(f"place_dzg_l{l}", dzg, 4, dz)
    dzq, dzi, dz, sg["lb_bwd"] = _hgrn_bwd(f"hgrn_rev_bwd_l{l}", sv["z"], d_o, sv["st_b"], sm["lb_bwd"], True, dz,
                                           prev=(dq_f, dv_f))
    dz = _place_in_dz(f"place_dzq_l{l}", dzq, 0, dz)
    dz = _place_in_dz(f"place_dzi_l{l}", dzi, 3, dz)

    gw["w_in"] = _wgrad(f"grad_w_in_l{l}", sv["h"], _bs((TM_WIDE, D_MODEL), lambda n, i: (i, 0)), dz,
                        _bs((TM_WIDE, 1024), lambda n, i: (i, n)), (D_MODEL, 2048),
                        ((D_MODEL, 1024), lambda n, i: (n // 2, l, 0, n % 2)), 8, l, gw.get("w_in"), tt=TM_WIDE)
    dh = _with_comm(plan, "proj_in_t", parts, lambda comm: _proj_in_t(dz, wg["w_in"], l, comm=comm))
    dx, sg["norm_mix_pre"] = _rms_bwd(f"norm_mix_pre_bwd_l{l}", sv["x"], sm["norm_mix_pre"], dh, dx1, F32)
    del nt
    return dx, gw, sg


def _pack(parts):
    return jnp.concatenate([a.reshape(-1, LANES) for a in parts], axis=0)


def _step(x, p, loss_target, w, m, v):
    x = x[0]
    target = loss_target[0]
    depth = w["w_in"].shape[0]

    assert depth == 2, "the exchanges below ride in layer 0's kernels and carry layer 1's data"
    shards = {k: w[k].astype(MXU_DTYPE) for k in BIG}
    rest_a = [k for k in BIG if k not in ("w_in", "w_gate", "w_up")]
    rest_b = ["w_gate", "w_up"]
    rest = rest_a + rest_b

    def gather(keys, l0, extend):
        return lambda wg: (keys, _gather_comm([shards[k] for k in keys], [wg[k] for k in keys] if extend else None, l0, 1))

    wg = {"w_in": _run_comm("gather_w_in_l0", gather(["w_in"], 0, False)(None)[1])[0]}
    fwd_plans = [{"proj_in": gather(rest, 0, False), "merge_proj_out": gather(["w_in"], 1, True),
                  "proj_gate_up": gather(rest_a, 1, True), "proj_down": gather(rest_b, 1, True)}, {}]
    lb_f = _lower_bounds("lower_bounds_fwd", w["lb_gamma_fwd"])
    lb_b = _lower_bounds("lower_bounds_bwd", w["lb_gamma_bwd"])

    def small_of(l):
        sm = {k: w[k][l:l + 1] for k in ("norm_mix_pre", "hg_norm", "sg_ln_g", "sg_ln_b", "norm_mix_post",
                                        "norm_ffn_pre", "norm_ffn_post")}
        sm["lb_fwd"], sm["lb_bwd"] = lb_f[l:l + 1], lb_b[l:l + 1]
        sm["sg_w"] = w["sg_w"][l]
        sm["sg_bias_t"] = jnp.repeat(w["sg_b"][l].T, SG_WIDTH // N_HEADS, axis=1)
        return sm

    saved = []
    h = x
    for l in range(depth):
        h, sv = _layer_fwd(l, h, p[l, 0], wg, small_of(l), fwd_plans[l])
        saved.append(sv)

    dy, sq_err = _loss_bwd("loss", h, target)
    gw, parts = {}, {}

    def exchange(keys, l0, extend):
        return lambda parts: (keys, _exchange_comm([gw[k] for k in keys], [parts[k] for k in keys] if extend else None, l0, 1))

    bwd_plans = [{"proj_down_t": exchange(["w_in"], 1, False), "proj_gate_up_t": exchange(rest, 1, False),
                  "hgrn_fwd_bwd": exchange(rest, 0, True), "proj_in_t": exchange(["w_in"], 0, True)}, {}]
    small_grads = [None] * depth
    for l in reversed(range(depth)):
        dy, gw, small_grads[l] = _layer_bwd(l, dy, saved[l], wg, small_of(l), gw, parts, bwd_plans[l])

    def stack(key):
        return jnp.concatenate([small_grads[l][key].reshape((1,) + w_shape[1:]) for l in range(depth)], axis=0)

    g_small = {}
    for key in SMALL:
        w_shape = w[key].shape
        if key == "lb_gamma_fwd":
            dlb = jnp.concatenate([small_grads[l]["lb_fwd"] for l in range(depth)], axis=0)
            g_small[key] = _lower_bounds_bwd("lower_bounds_fwd_bwd", w[key], dlb)
        elif key == "lb_gamma_bwd":
            dlb = jnp.concatenate([small_grads[l]["lb_bwd"] for l in range(depth)], axis=0)
            g_small[key] = _lower_bounds_bwd("lower_bounds_bwd_bwd", w[key], dlb)
        else:
            g_small[key] = stack(key)

    packed = _pack([g_small[k] for k in SMALL] + [sq_err])
    summed = _all_reduce_small(packed)
    n_small_rows = sum(w[k].size for k in SMALL) // LANES
    loss = 0.5 * jnp.sum(summed[n_small_rows:]) / D_MODEL

    g_rows = summed[:n_small_rows]
    d_rows, m_rows, v_rows = _adam_small("adamw_small", g_rows, _pack([w[k] for k in SMALL]),
                                         _pack([m[k] for k in SMALL]), _pack([v[k] for k in SMALL]))
    out = {}
    off = 0
    for key in SMALL:
        n_rows = w[key].size // LANES
        sl = slice(off, off + n_rows)
        out[key] = tuple(a[sl].reshape(w[key].shape) for a in (g_rows, d_rows, m_rows, v_rows))
        off += n_rows

    for key in BIG:
        out[key] = tuple(_adam_sharded(f"adamw_{key}", parts[key], w[key], m[key], v[key]))
    return loss, dy[None], out


WEIGHTS = ("norm_mix_pre", "w_in", "lb_gamma_fwd", "lb_gamma_bwd", "hg_norm", "sg_w", "sg_b", "sg_ln_g", "sg_ln_b",
           "w_a", "w_b", "w_out", "norm_mix_post", "norm_ffn_pre", "w_gate", "w_up", "w_down", "norm_ffn_post",
           "w_ple", "w_ple_gate")


def kernel(x, p, norm_mix_pre, w_in, lb_gamma_fwd, lb_gamma_bwd, hg_norm, sg_w, sg_b, sg_ln_g, sg_ln_b, w_a, w_b, w_out, norm_mix_post, norm_ffn_pre, w_gate, w_up, w_down, norm_ffn_post, w_ple, w_ple_gate, loss_target, m_norm_mix_pre, m_w_in, m_lb_gamma_fwd, m_lb_gamma_bwd, m_hg_norm, m_sg_w, m_sg_b, m_sg_ln_g, m_sg_ln_b, m_w_a, m_w_b, m_w_out, m_norm_mix_post, m_norm_ffn_pre, m_w_gate, m_w_up, m_w_down, m_norm_ffn_post, m_w_ple, m_w_ple_gate, v_norm_mix_pre, v_w_in, v_lb_gamma_fwd, v_lb_gamma_bwd, v_hg_norm, v_sg_w, v_sg_b, v_sg_ln_g, v_sg_ln_b, v_w_a, v_w_b, v_w_out, v_norm_mix_post, v_norm_ffn_pre, v_w_gate, v_w_up, v_w_down, v_norm_ffn_post, v_w_ple, v_w_ple_gate):
    w = dict(zip(WEIGHTS, (norm_mix_pre, w_in, lb_gamma_fwd, lb_gamma_bwd, hg_norm, sg_w, sg_b, sg_ln_g, sg_ln_b, w_a, w_b, w_out, norm_mix_post, norm_ffn_pre, w_gate, w_up, w_down, norm_ffn_post, w_ple, w_ple_gate)))
    m = dict(zip(WEIGHTS, (m_norm_mix_pre, m_w_in, m_lb_gamma_fwd, m_lb_gamma_bwd, m_hg_norm, m_sg_w, m_sg_b, m_sg_ln_g, m_sg_ln_b, m_w_a, m_w_b, m_w_out, m_norm_mix_post, m_norm_ffn_pre, m_w_gate, m_w_up, m_w_down, m_norm_ffn_post, m_w_ple, m_w_ple_gate)))
    v = dict(zip(WEIGHTS, (v_norm_mix_pre, v_w_in, v_lb_gamma_fwd, v_lb_gamma_bwd, v_hg_norm, v_sg_w, v_sg_b, v_sg_ln_g, v_sg_ln_b, v_w_a, v_w_b, v_w_out, v_norm_mix_post, v_norm_ffn_pre, v_w_gate, v_w_up, v_w_down, v_norm_ffn_post, v_w_ple, v_w_ple_gate)))
    loss, grad_x, out = _step(x, p, loss_target, w, m, v)
    res = [loss, grad_x]
    for i in range(4):
        res += [out[k][i] for k in WEIGHTS]
    return tuple(res)
```

```python
import functools

import jax
import jax.numpy as jnp
from jax import lax
from jax.experimental import pallas as pl
from jax.experimental.pallas import tpu as pltpu

F32 = jnp.float32
MXU_DTYPE = jnp.bfloat16
GRAD_EXCHANGE_DTYPE = jnp.bfloat16

D_MODEL = 1024
N_HEADS = 8
HEAD = 128
HG_CHUNK = 64
HG_SUB = 16
HG_BLOCK = 512
HG_SAFE_EXP = 80.0
SG_CHUNK = 128
SG_WIDTH = 512
FFN_SHARD = 704
PLE_DIM = 256
N_IN = 8192
N_CHIPS = 4
N_DEV = 8
EPS = 1e-6
LANES = 128
VMEM_LIMIT_BYTES = 56 * 2 ** 20

ADAM_LR = 0.001
ADAM_B1 = 0.9
ADAM_B2 = 0.999
ADAM_EPS = 1e-08
ADAM_WD = 0.01
ADAM_STEP = 10
ADAM_TILE_ELEMS = 128 * 1024

MESH = pl.DeviceIdType.MESH
ANY = pl.BlockSpec(memory_space=pl.ANY)


def _pcall(body, **kw):
    return pl.pallas_call(body, **kw)


def _params(n_axes):
    return pltpu.CompilerParams(dimension_semantics=("arbitrary",) * n_axes, vmem_limit_bytes=VMEM_LIMIT_BYTES)


def _dot(a, b, ca, cb):
    return lax.dot_general(a.astype(MXU_DTYPE), b.astype(MXU_DTYPE), (((ca,), (cb,)), ((), ())),
                           preferred_element_type=F32)


def _dot_f32(a, b, ca, cb):
    return lax.dot_general(a, b, (((ca,), (cb,)), ((), ())), precision=lax.Precision.HIGH,
                           preferred_element_type=F32)


def _nn(a, b):
    return _dot(a, b, 1, 0)


def _nt(a, b):
    return _dot(a, b, 1, 1)


def _tn(a, b):
    return _dot(a, b, 0, 0)


NN, NT, TN = (1, 0), (1, 1), (0, 0)


def _sigmoid(x):
    return jax.nn.sigmoid(x)


def _dsilu(x, s):
    return s * (1.0 + x * (1.0 - s))


_SQRT_HALF = 0.7071067811865476
_INV_SQRT_2PI = 0.3989422804014327


def _gelu(x):
    return 0.5 * x * (1.0 + lax.erf(x * _SQRT_HALF))


def _dgelu(x):
    return 0.5 * (1.0 + lax.erf(x * _SQRT_HALF)) + x * jnp.exp(-0.5 * x * x) * _INV_SQRT_2PI


def _mean_last(x):
    return jnp.mean(x, axis=-1, keepdims=True)


def _rowsum(x):
    return jnp.sum(x, axis=0, keepdims=True)


class _Comm:
    def __init__(self, ins, bufs, sem_shapes, make):
        self.ins, self.bufs, self.sem_shapes, self.make = list(ins), list(bufs), list(sem_shapes), make
        self.extends = not isinstance(self.bufs[0], jax.ShapeDtypeStruct)


def _hosted_call(name, compute, grid, args, in_specs, out_shapes, out_specs, scratch, aliases, comm):
    n_in, n_out, n_scr = len(args), len(out_shapes), len(scratch)
    if comm is None:
        def plain(*refs):
            compute(refs[:n_in], refs[n_in:n_in + n_out], refs[n_in + n_out:])

        res = _pcall(plain, name=name, grid=grid, in_specs=list(in_specs), out_specs=list(out_specs),
                     out_shape=list(out_shapes), scratch_shapes=list(scratch), input_output_aliases=dict(aliases),
                     compiler_params=_params(len(grid)))(*args)
        return list(res), []

    n_cin, n_buf = len(comm.ins), len(comm.bufs)
    all_args = list(args) + comm.ins + (comm.bufs if comm.extends else [])
    n_all = len(all_args)
    all_aliases = dict(aliases)
    if comm.extends:
        for j in range(n_buf):
            all_aliases[n_in + n_cin + j] = n_out + j
    buf_shapes = [_sds(b.shape, b.dtype) for b in comm.bufs]

    def body(*refs):
        outs = refs[n_all:n_all + n_out + n_buf]
        scr = refs[n_all + n_out + n_buf:]
        start, finish = comm.make(refs[n_in:n_in + n_cin], outs[n_out:], scr[n_scr:])
        first, last = None, None
        for axis, size in enumerate(grid):
            i = pl.program_id(axis)
            first = (i == 0) if first is None else jnp.logical_and(first, i == 0)
            last = (i == size - 1) if last is None else jnp.logical_and(last, i == size - 1)
        pl.when(first)(start)
        compute(refs[:n_in], outs[:n_out], scr[:n_scr])
        pl.when(last)(finish)

    res = _pcall(body, name=name, grid=grid, in_specs=list(in_specs) + [ANY] * (n_all - n_in),
                 out_specs=list(out_specs) + [ANY] * n_buf, out_shape=list(out_shapes) + buf_shapes,
                 scratch_shapes=list(scratch) + comm.sem_shapes, input_output_aliases=all_aliases,
                 compiler_params=_params(len(grid)))(*all_args)
    return list(res[:n_out]), list(res[n_out:])


def _run_comm(name, comm):
    n_cin, n_buf = len(comm.ins), len(comm.bufs)
    all_args = comm.ins + (comm.bufs if comm.extends else [])
    n_all = len(all_args)

    def body(*refs):
        start, finish = comm.make(refs[:n_cin], refs[n_all:n_all + n_buf], refs[n_all + n_buf:])
        start()
        finish()

    res = _pcall(body, name=name, in_specs=[ANY] * n_all, out_specs=[ANY] * n_buf,
                 out_shape=[_sds(b.shape, b.dtype) for b in comm.bufs], scratch_shapes=comm.sem_shapes,
                 input_output_aliases={n_cin + j: j for j in range(n_buf)} if comm.extends else {})(*all_args)
    return list(res)


def _mm(name, pairs, kind, out_shape, grid, in_specs, out_spec, *, reduce_axis=None, add=None,
        add_spec=None, into=None, prep=None, comm=None):
    n_pairs = len(pairs)
    has_add = add is not None
    staged = reduce_axis is not None and out_shape.dtype != F32

    def compute(in_refs, out_refs, scr):
        o_ref = out_refs[0]
        acc = None
        for i in range(n_pairs):
            a = in_refs[2 * i][...]
            b = in_refs[2 * i + 1][...]
            if prep is not None:
                b = prep(b)
            prod = _dot(a, b, *kind)
            acc = prod if acc is None else acc + prod
        if has_add:
            acc = acc + in_refs[2 * n_pairs][...]
        if reduce_axis is None:
            o_ref[...] = acc.astype(o_ref.dtype)
        else:
            r = pl.program_id(reduce_axis)
            acc_ref = scr[0] if staged else o_ref

            @pl.when(r == 0)
            def _():
                acc_ref[...] = acc

            @pl.when(r > 0)
            def _():
                acc_ref[...] += acc

            if staged:
                @pl.when(r == grid[reduce_axis] - 1)
                def _():
                    o_ref[...] = acc_ref[...].astype(o_ref.dtype)

    scratch = []
    if staged:
        scratch = [pltpu.VMEM(tuple(d for d in out_spec.block_shape if d is not None), F32)]
    args = [t for pair in pairs for t in pair]
    specs = list(in_specs)
    if has_add:
        args.append(add)
        specs.append(add_spec)
    aliases = {}
    if into is not None:
        aliases = {len(args): 0}
        args.append(into)
        specs.append(ANY)
    outs, bufs = _hosted_call(name, compute, grid, args, specs, [out_shape], [out_spec], scratch, aliases, comm)
    return outs[0] if comm is None else (outs[0], bufs)


def _sds(shape, dtype):
    return jax.ShapeDtypeStruct(tuple(shape), dtype)


def _bs(shape, fn):
    return pl.BlockSpec(tuple(shape), fn)


TM = 1024
TM_WIDE = 2048


def _merge_lead(b):
    return b.reshape(b.shape[0] * b.shape[1], b.shape[2])


def _proj_in(h, w_in_g, l, comm=None):
    t = h.shape[0]
    return _mm(f"proj_in_l{l}", [(h, w_in_g)], NN, _sds((t, N_IN), F32), (8, t // TM_WIDE),
               [_bs((TM_WIDE, D_MODEL), lambda n, m: (m, 0)),
                _bs((None, None, D_MODEL, 1024), lambda n, m: (n // 2, l, 0, n % 2))],
               _bs((TM_WIDE, 1024), lambda n, m: (m, n)), comm=comm)


def _proj_rows_sharded(name, a, w_g, l, out_dtype, add=None):
    t = a.shape[0]
    return _mm(name, [(a, w_g)], NN, _sds((t, D_MODEL), out_dtype), (t // TM,),
               [_bs((TM, D_MODEL), lambda m: (m, 0)),
                _bs((N_CHIPS, None, 256, D_MODEL), lambda m: (0, l, 0, 0))],
               _bs((TM, D_MODEL), lambda m: (m, 0)), prep=_merge_lead, add=add,
               add_spec=_bs((TM, D_MODEL), lambda m: (m, 0)))


def _proj_rows_sharded_t(name, g, w_g, l, out_dtype, add=None):
    t = g.shape[0]
    return _mm(name, [(g, w_g)], NT, _sds((t, D_MODEL), out_dtype), (t // TM,),
               [_bs((TM, D_MODEL), lambda m: (m, 0)),
                _bs((N_CHIPS, None, 256, D_MODEL), lambda m: (0, l, 0, 0))],
               _bs((TM, D_MODEL), lambda m: (m, 0)), prep=_merge_lead, add=add,
               add_spec=_bs((TM, D_MODEL), lambda m: (m, 0)))


def _proj_cols256(name, a, w_g, l):
    t, k = a.shape
    return _mm(name, [(a, w_g)], NN, _sds((t, D_MODEL), F32), (N_CHIPS, t // TM),
               [_bs((TM, k), lambda j, m: (m, 0)),
                _bs((None, None, k, 256), lambda j, m: (j, l, 0, 0))],
               _bs((TM, 256), lambda j, m: (m, j)))


def _proj_cols256_t(name, g, w_g, l):
    t = g.shape[0]
    k = w_g.shape[2]
    return _mm(name, [(g, w_g)], NT, _sds((t, k), F32), (t // TM, N_CHIPS),
               [_bs((TM, 256), lambda m, j: (m, j)),
                _bs((None, None, k, 256), lambda m, j: (j, l, 0, 0))],
               _bs((TM, k), lambda m, j: (m, 0)), reduce_axis=1)


def _proj_gate_up(name, h2, wg_g, wu_g, l, comm=None):
    t = h2.shape[0]

    def compute(in_refs, out_refs, scr):
        h_ref, wg_ref, wu_ref = in_refs
        gt_ref, up_ref, act_ref = out_refs
        h = h_ref[...]
        g = _nn(h, wg_ref[...])
        u = _nn(h, wu_ref[...])
        gt_ref[...] = g
        up_ref[...] = u
        act_ref[...] = ((g * _sigmoid(g)) * u).astype(act_ref.dtype)

    w_spec = _bs((None, None, D_MODEL, FFN_SHARD), lambda j, m: (j, l, 0, 0))
    o_spec = _bs((None, TM, FFN_SHARD), lambda j, m: (j, m, 0))
    outs, bufs = _hosted_call(name, compute, (N_CHIPS, t // TM), [h2, wg_g, wu_g],
                        [_bs((TM, D_MODEL), lambda j, m: (m, 0)), w_spec, w_spec],
                        [_sds((N_CHIPS, t, FFN_SHARD), F32), _sds((N_CHIPS, t, FFN_SHARD), F32),
                         _sds((N_CHIPS, t, FFN_SHARD), MXU_DTYPE)], [o_spec, o_spec, o_spec], [], {}, comm)
    return outs if comm is None else (outs, bufs)


def _proj_ffn_in_t(name, pairs, l, comm=None):
    t = pairs[0][0].shape[1]
    specs = []
    for _ in pairs:
        specs += [_bs((None, TM, FFN_SHARD), lambda m, j: (j, m, 0)),
                  _bs((None, None, D_MODEL, FFN_SHARD), lambda m, j: (j, l, 0, 0))]
    return _mm(name, pairs, NT, _sds((t, D_MODEL), F32), (t // TM, N_CHIPS), specs,
               _bs((TM, D_MODEL), lambda m, j: (m, 0)), reduce_axis=1, comm=comm)


def _proj_ffn_out(name, act, w_g, l, comm=None):
    t = act.shape[1]
    specs = []
    for j in range(N_CHIPS):
        specs += [_bs((None, TM, FFN_SHARD), lambda m, j=j: (j, m, 0)),
                  _bs((None, None, FFN_SHARD, D_MODEL), lambda m, j=j: (j, l, 0, 0))]
    return _mm(name, [(act, w_g)] * N_CHIPS, NN, _sds((t, D_MODEL), F32), (t // TM,), specs,
               _bs((TM, D_MODEL), lambda m: (m, 0)), comm=comm)


def _proj_down_t_swiglu_bwd(name, dff, w_g, gt, up, l, comm=None):
    t = dff.shape[0]

    def compute(in_refs, out_refs, scr):
        d_ref, w_ref, gt_ref, up_ref = in_refs
        dgt_ref, dup_ref = out_refs
        dact = _nt(d_ref[...], w_ref[...])
        g = gt_ref[...]
        s = _sigmoid(g)
        dgt_ref[...] = (dact * up_ref[...] * _dsilu(g, s)).astype(dgt_ref.dtype)
        dup_ref[...] = (dact * (g * s)).astype(dup_ref.dtype)

    o_spec = _bs((None, TM, FFN_SHARD), lambda j, m: (j, m, 0))
    outs, bufs = _hosted_call(name, compute, (N_CHIPS, t // TM), [dff, w_g, gt, up],
                        [_bs((TM, D_MODEL), lambda j, m: (m, 0)),
                         _bs((None, None, FFN_SHARD, D_MODEL), lambda j, m: (j, l, 0, 0)), o_spec, o_spec],
                        [_sds((N_CHIPS, t, FFN_SHARD), MXU_DTYPE)] * 2, [o_spec, o_spec], [], {}, comm)
    return outs if comm is None else (outs, bufs)


def _proj_in_t(dz, w_in_g, l, comm=None):
    t = dz.shape[0]
    return _mm(f"proj_in_t_l{l}", [(dz, w_in_g)], NT, _sds((t, D_MODEL), F32), (t // TM_WIDE, 8),
               [_bs((TM_WIDE, 1024), lambda m, n: (m, n)),
                _bs((None, None, D_MODEL, 1024), lambda m, n: (n // 2, l, 0, n % 2))],
               _bs((TM_WIDE, D_MODEL), lambda m, n: (m, 0)), reduce_axis=1, comm=comm)


TT = 1024


def _wgrad(name, a, a_spec, g, g_spec, shard_shape, o_map, n_outer, l, into, tt=TT):
    t = a.shape[-2]
    out = _sds((N_CHIPS, 2) + tuple(shard_shape), GRAD_EXCHANGE_DTYPE)
    return _mm(name, [(a, g)], TN, out, (n_outer, t // tt), [a_spec, g_spec],
               _bs((None, None) + tuple(o_map[0]), o_map[1]), reduce_axis=1, into=into)


def _rows(name, fn, n_tiles, ins, in_specs, out_shapes, out_specs, n_red=0, into=()):
    n_in = len(ins)
    n_out = len(out_shapes)

    def body(*refs):
        in_refs = refs[:n_in]
        out_refs = refs[len(refs) - n_out:]
        vals = fn(*[r[...] for r in in_refs])
        if not isinstance(vals, (tuple, list)):
            vals = (vals,)
        first = pl.program_id(0) == 0
        for j in range(n_out):
            o_ref = out_refs[j]
            val = vals[j]
            if j < n_out - n_red:
                o_ref[...] = val.astype(o_ref.dtype)
            else:
                @pl.when(first)
                def _(o_ref=o_ref, val=val):
                    o_ref[...] = val

                @pl.when(jnp.logical_not(first))
                def _(o_ref=o_ref, val=val):
                    o_ref[...] += val

    args = list(ins)
    specs = list(in_specs)
    aliases = {}
    for buf, out_idx in into:
        aliases[len(args)] = out_idx
        args.append(buf)
        specs.append(ANY)
    res = _pcall(body, name=name, grid=(n_tiles,), in_specs=specs, out_specs=list(out_specs),
                 out_shape=list(out_shapes), input_output_aliases=aliases, compiler_params=_params(1))(*args)
    return res


def _tile_spec(tile, width, blk=0):
    return pl.BlockSpec((tile, width), lambda i: (i, blk))


def _whole(shape):
    nd = len(shape)
    return pl.BlockSpec(tuple(shape), lambda i: (0,) * nd)


def _rms(x, g):
    r = lax.rsqrt(_mean_last(x * x) + EPS)
    return (x * r) * g


def _rms_bwd_math(u, g, dy):
    r = lax.rsqrt(_mean_last(u * u) + EPS)
    uh = u * r
    gdy = dy * g
    du = r * (gdy - uh * _mean_last(gdy * uh))
    return du, _rowsum(dy * uh)


def _rms_fwd(name, x, g):
    t, d = x.shape
    tile = 512
    return _rows(name, lambda xv, gv: (_rms(xv, gv),), t // tile, [x, g],
                 [_tile_spec(tile, d), _whole((1, d))], [_sds((t, d), MXU_DTYPE)], [_tile_spec(tile, d)])[0]


def _resid_rms_fwd(name, x, y, g):
    t, d = x.shape
    tile = 512
    return _rows(name, lambda xv, yv, gv: (xv + _rms(yv, gv),), t // tile, [x, y, g],
                 [_tile_spec(tile, d), _tile_spec(tile, d), _whole((1, d))], [_sds((t, d), F32)],
                 [_tile_spec(tile, d)])[0]


def _resid_rms_norm_fwd(name, x, y, g, g_next):
    t, d = x.shape
    tile = 512

    def fn(xv, yv, gv, gn):
        s = xv + _rms(yv, gv)
        return s, _rms(s, gn)

    return _rows(name, fn, t // tile, [x, y, g, g_next],
                 [_tile_spec(tile, d), _tile_spec(tile, d), _whole((1, d)), _whole((1, d))],
                 [_sds((t, d), F32), _sds((t, d), MXU_DTYPE)], [_tile_spec(tile, d)] * 2)


TF = 512


def _merge_proj_out(name, pa, pb, z, w_g, l, comm=None):
    t = z.shape[0]

    def compute(in_refs, out_refs, scr):
        pa_ref, pb_ref, ga_ref, gb_ref, w_ref = in_refs
        mg_ref, mix_ref = out_refs
        merged = _sigmoid(ga_ref[...]) * pa_ref[...] + _sigmoid(gb_ref[...]) * pb_ref[...]
        mg_ref[...] = merged.astype(mg_ref.dtype)
        mix_ref[...] = _nn(merged, _merge_lead(w_ref[...]))

    row = lambda blk: _bs((TF, D_MODEL), lambda m: (m, blk))
    outs, bufs = _hosted_call(name, compute, (t // TF,), [pa, pb, z, z, w_g],
                        [row(0), row(0), row(6), row(7), _bs((N_CHIPS, None, 256, D_MODEL), lambda m: (0, l, 0, 0))],
                        [_sds((t, D_MODEL), MXU_DTYPE), _sds((t, D_MODEL), F32)], [row(0), row(0)], [], {}, comm)
    return outs if comm is None else (outs, bufs)


def _proj_ple_gate_ple(name, x2, w_g, pe, l):
    t = x2.shape[0]

    def body(x_ref, w_ref, pe_ref, gz_ref, x3_ref):
        x = x_ref[...]
        gz = _nn(x, _merge_lead(w_ref[...]))
        gz_ref[...] = gz
        x3_ref[...] = x + pe_ref[...] * _sigmoid(gz)

    row = _bs((TF, D_MODEL), lambda m: (m, 0))
    return _pcall(body, name=name, grid=(t // TF,),
                  in_specs=[row, _bs((N_CHIPS, None, 256, D_MODEL), lambda m: (0, l, 0, 0)), row],
                  out_specs=[row, row], out_shape=[_sds((t, D_MODEL), F32)] * 2,
                  compiler_params=_params(1))(x2, w_g, pe)


def _rms_bwd(name, u, g, dy, resid, out_dtype):
    t, d = u.shape
    tile = 256

    def fn(uv, gv, dyv, *rest):
        du, dg = _rms_bwd_math(uv, gv, dyv)
        if rest:
            du = du + rest[0]
        return du, dg

    ins = [u, g, dy] + ([resid] if resid is not None else [])
    specs = [_tile_spec(tile, d), _whole((1, d)), _tile_spec(tile, d)] + ([_tile_spec(tile, d)] if resid is not None else [])
    return _rows(name, fn, t // tile, ins, specs, [_sds((t, d), out_dtype), _sds((1, d), F32)],
                 [_tile_spec(tile, d), _whole((1, d))], n_red=1)


def _rms_bwd_pair(name, u1, g1, dy1, resid, u2, g2, out2_dtype):
    t, d = u1.shape
    tile = 256

    def fn(u1v, g1v, dy1v, rv, u2v, g2v):
        d1, dg1 = _rms_bwd_math(u1v, g1v, dy1v)
        d1 = d1 + rv
        d2, dg2 = _rms_bwd_math(u2v, g2v, d1)
        return d1, d2, dg1, dg2

    row, vec = _tile_spec(tile, d), _whole((1, d))
    return _rows(name, fn, t // tile, [u1, g1, dy1, resid, u2, g2], [row, vec, row, row, row, vec],
                 [_sds((t, d), F32), _sds((t, d), out2_dtype), _sds((1, d), F32), _sds((1, d), F32)],
                 [row, row, vec, vec], n_red=2)


def _cumsum_rows(x, group, suffix):
    n = x.shape[0]
    pos = lax.broadcasted_iota(jnp.int32, x.shape, 0) % group
    d = 1
    while d < group:
        if suffix:
            x = x + jnp.where(pos < group - d, pltpu.roll(x, n - d, 0), 0.0)
        else:
            x = x + jnp.where(pos >= d, pltpu.roll(x, d, 0), 0.0)
        d *= 2
    return x


def _hg_gates(zq, zf, lb):
    q = zq * _sigmoid(zq)
    f = lb + (1.0 - lb) * _sigmoid(zf)
    logf = jnp.log(jnp.maximum(f, jnp.finfo(F32).tiny))
    k = (1.0 - lb) * _sigmoid(-zf)
    return q, k, logf, f


def _tri_mask(n, rev):
    t_i = lax.broadcasted_iota(jnp.int32, (n, n), 0)
    s_i = lax.broadcasted_iota(jnp.int32, (n, n), 1)
    return (s_i >= t_i) if rev else (s_i <= t_i)


def _anchors_are_safe(b_s, n_chunks, rev):
    worst = None
    for c in range(n_chunks):
        base = c * HG_CHUNK
        first = base + HG_CHUNK - 1 if rev else base
        last = base if rev else base + HG_CHUNK - 1
        mid = base + HG_CHUNK // 2
        b0, bm, bl = b_s[first:first + 1, :], b_s[mid:mid + 1, :], b_s[last:last + 1, :]
        span = jnp.maximum(b0 - bm, bm - bl)
        worst = span if worst is None else jnp.maximum(worst, span)
    return jnp.max(worst) < HG_SAFE_EXP


def _sub_ranges(base, i_sub, rev):
    r0 = base + i_sub * HG_SUB
    r1 = r0 + HG_SUB
    if rev:
        e0, e1, anchor = r1, base + HG_CHUNK, r1
    else:
        e0, e1, anchor = base, r0, r0 - 1
    return r0, r1, e0, e1, anchor


def _hgrn_fwd(name, z, lb_row, rev, post=None):
    t = z.shape[0]
    nb = t // HG_BLOCK
    ncb = HG_BLOCK // HG_CHUNK
    nsb = HG_CHUNK // HG_SUB
    zf0 = 16 if rev else 8
    n_in = 4 if post is None else 7

    def tmap(i):
        return nb - 1 - i if rev else i

    def body(*refs):
        zq_ref, zf_ref, zi_ref, lb_ref = refs[:4]
        o_ref, st_ref = refs[n_in:n_in + 2]
        state, q_s, k_s, v_s, b_s = refs[len(refs) - 5:]

        @pl.when(pl.program_id(1) == 0)
        def _():
            state[...] = jnp.zeros_like(state)

        for c in range(ncb):
            rows = slice(c * HG_CHUNK, (c + 1) * HG_CHUNK)
            q_c, k_c, logf, _ = _hg_gates(zq_ref[rows, :], zf_ref[rows, :], lb_ref[...])
            q_s[rows, :] = q_c
            k_s[rows, :] = k_c
            b_s[rows, :] = _cumsum_rows(logf, HG_CHUNK, rev)
        order = range(ncb - 1, -1, -1) if rev else range(ncb)
        safe = _anchors_are_safe(b_s, ncb, rev)

        @pl.when(safe)
        def _():
            cmask = _tri_mask(HG_CHUNK, rev)
            ch = []
            for c in range(ncb):
                base = c * HG_CHUNK
                rows = slice(base, base + HG_CHUNK)
                last = base if rev else base + HG_CHUNK - 1
                mid = base + HG_CHUNK // 2
                q_c, k_c, v_c, b = q_s[rows, :], k_s[rows, :], zi_ref[rows, :], b_s[rows, :]
                bl, bm = b_s[last:last + 1, :], b_s[mid:mid + 1, :]
                ch.append(dict(v=v_c, qe=q_c * jnp.exp(b), el=jnp.exp(bl), q_t=q_c * jnp.exp(b - bm),
                               k_t=k_c * jnp.exp(bm - b), kd=k_c * jnp.exp(bl - b)))
            for d in ch:
                d["a"] = _nt(d["q_t"], d["k_t"])
                d["inc"] = _tn(d["v"], d["kd"])
            for d in ch:
                d["o"] = _nn(jnp.where(cmask, d["a"], 0.0), d["v"])
            st = state[...]
            for c in order:
                d = ch[c]
                st_ref[c] = st
                d["o"] = d["o"] + _nt(d["qe"], st)
                st = st * d["el"] + d["inc"]
            state[...] = st
            o_ref[...] = jnp.concatenate([d["o"] for d in ch], axis=0)

        @pl.when(jnp.logical_not(safe))
        def _():
            v_s[...] = zi_ref[...]
            mask = _tri_mask(HG_SUB, rev)
            for c in order:
                base = c * HG_CHUNK
                rows = slice(base, base + HG_CHUNK)
                last = base if rev else base + HG_CHUNK - 1
                st = state[...]
                st_ref[c] = st
                b = b_s[rows, :]
                bl = b_s[last:last + 1, :]
                o_inter = _nt(q_s[rows, :] * jnp.exp(b), st)
                kd = k_s[rows, :] * jnp.exp(bl - b)
                state[...] = st * jnp.exp(bl) + _tn(v_s[rows, :], kd)
                parts = []
                for i_sub in range(nsb):
                    r0, r1, e0, e1, anchor = _sub_ranges(base, i_sub, rev)
                    q_i, k_i, b_i = q_s[r0:r1, :], k_s[r0:r1, :], b_s[r0:r1, :]
                    decay = jnp.exp(jnp.minimum(b_i[:, None, :] - b_i[None, :, :], 0.0))
                    a_d = jnp.where(mask, jnp.sum(q_i[:, None, :] * k_i[None, :, :] * decay, axis=-1), 0.0)
                    o_i = _nn(a_d, v_s[r0:r1, :])
                    if e1 > e0:
                        anc = b_s[anchor:anchor + 1, :]
                        q_t = q_i * jnp.exp(b_i - anc)
                        k_t = k_s[e0:e1, :] * jnp.exp(anc - b_s[e0:e1, :])
                        o_i = o_i + _nn(_nt(q_t, k_t), v_s[e0:e1, :])
                    parts.append(o_i)
                o_ref[rows, :] = o_inter + jnp.concatenate(parts, axis=0)

        if post is not None:
            other_ref, zg_ref, gain_ref = refs[4:7]
            o = o_ref[...] + other_ref[...]
            zg = zg_ref[...]
            o_ref[...] = o
            refs[n_in + 2][...] = (_rms(o, gain_ref[...]) * (zg * _sigmoid(zg))).astype(MXU_DTYPE)

    blk = lambda off: pl.BlockSpec((HG_BLOCK, HEAD), lambda h, i: (tmap(i), off + h))
    vec = pl.BlockSpec((1, HEAD), lambda h, i: (0, h))
    args, in_specs = [z, z, z, lb_row], [blk(0), blk(zf0), blk(24), vec]
    out_specs = [blk(0), pl.BlockSpec((None, ncb, HEAD, HEAD), lambda h, i: (h, tmap(i), 0, 0))]
    out_shape = [_sds((t, D_MODEL), F32), _sds((N_HEADS, t // HG_CHUNK, HEAD, HEAD), F32)]
    if post is not None:
        args += [post[0], z, post[1]]
        in_specs += [blk(0), blk(32), vec]
        out_specs.append(blk(0))
        out_shape.append(_sds((t, D_MODEL), MXU_DTYPE))
    return _pcall(
        body, name=name, grid=(N_HEADS, nb), in_specs=in_specs, out_specs=out_specs, out_shape=out_shape,
        scratch_shapes=[pltpu.VMEM((HEAD, HEAD), F32)] + [pltpu.VMEM((HG_BLOCK, HEAD), F32)] * 4,
        compiler_params=_params(2))(*args)


def _hgrn_bwd(name, z, d_o, states, lb_row, rev, dz, comm=None, prev=None, post=None):
    t = z.shape[0]
    nb = t // HG_BLOCK
    ncb = HG_BLOCK // HG_CHUNK
    nsb = HG_CHUNK // HG_SUB
    zf0 = 16 if rev else 8

    def tmap(i):
        return i if rev else nb - 1 - i

    def compute(in_refs, out_refs, scr):
        zq_ref, zf_ref, zi_ref, do_ref, st_ref, lb_ref = in_refs[:6]
        dq_ref, dv_ref, dzf_ref, dlb_ref = out_refs[:4]
        dstate, q_s, k_s, v_s, b_s, dq_s, dk_s, dv_s, db_s = scr
        first = pl.program_id(1) == 0

        @pl.when(first)
        def _():
            dstate[...] = jnp.zeros_like(dstate)

        if post is not None:
            o_ref, zg_ref, gain_ref = in_refs[7 + len(prev or ()):]
            d_o_ref, dzg_ref, dgain_ref = out_refs[4:]
            o_v, zg, gain, da = o_ref[...], zg_ref[...], gain_ref[...], do_ref[...]
            s = _sigmoid(zg)
            r = lax.rsqrt(_mean_last(o_v * o_v) + EPS)
            oh = o_v * r
            dy = da * (zg * s)
            dzg_ref[...] = (da * (oh * gain) * _dsilu(zg, s)).astype(dzg_ref.dtype)
            gdy = dy * gain
            d_o_ref[...] = r * (gdy - oh * _mean_last(gdy * oh))
            dgain = _rowsum(dy * oh)

            @pl.when(first)
            def _():
                dgain_ref[...] = dgain

            @pl.when(jnp.logical_not(first))
            def _():
                dgain_ref[...] += dgain

            do_ref = d_o_ref

        lb = lb_ref[...]
        for c in range(ncb):
            rows = slice(c * HG_CHUNK, (c + 1) * HG_CHUNK)
            q_c, k_c, logf, _ = _hg_gates(zq_ref[rows, :], zf_ref[rows, :], lb)
            q_s[rows, :] = q_c
            k_s[rows, :] = k_c
            b_s[rows, :] = _cumsum_rows(logf, HG_CHUNK, rev)
        order = range(ncb) if rev else range(ncb - 1, -1, -1)
        safe = _anchors_are_safe(b_s, ncb, rev)

        @pl.when(safe)
        def _():
            cmask = _tri_mask(HG_CHUNK, rev)
            row_i = lax.broadcasted_iota(jnp.int32, (HG_CHUNK, HEAD), 0)
            ch = []
            for c in range(ncb):
                base = c * HG_CHUNK
                rows = slice(base, base + HG_CHUNK)
                last = base if rev else base + HG_CHUNK - 1
                mid = base + HG_CHUNK // 2
                q_c, k_c, v_c, b, do_c = q_s[rows, :], k_s[rows, :], zi_ref[rows, :], b_s[rows, :], do_ref[rows, :]
                bl, bm = b_s[last:last + 1, :], b_s[mid:mid + 1, :]
                e, ebl, e_q, e_k = jnp.exp(b), jnp.exp(bl - b), jnp.exp(b - bm), jnp.exp(bm - b)
                ch.append(dict(q=q_c, k=k_c, v=v_c, do=do_c, e=e, el=jnp.exp(bl), ebl=ebl, e_q=e_q, e_k=e_k,
                               q_t=q_c * e_q, k_t=k_c * e_k, kd=k_c * ebl, last=last - base))
            for c, d in enumerate(ch):
                d["a"] = _nt(d["q_t"], d["k_t"])
                d["da"] = _nt(d["do"], d["v"])
                d["dq"] = _nn(d["do"], st_ref[c]) * d["e"]
                d["inc"] = _tn(d["do"], d["q"] * d["e"])
            for d in ch:
                da = jnp.where(cmask, d["da"], 0.0)
                d["dq"] = d["dq"] + d["e_q"] * _dot_f32(da, d["k_t"], 1, 0)
                d["dk_intra"] = d["e_k"] * _dot_f32(da, d["q_t"], 0, 0)
                d["dv"] = _tn(jnp.where(cmask, d["a"], 0.0), d["do"])
            dst = dstate[...]
            for c in order:
                d = ch[c]
                dk_inter = _nn(d["v"], dst) * d["ebl"]
                d["dk"] = dk_inter + d["dk_intra"]
                d["dv"] = _nt(d["kd"], dst) + d["dv"]
                d["extra"] = d["el"] * _rowsum(dst * st_ref[c]) + _rowsum(d["k"] * dk_inter)
                dst = d["inc"] + dst * d["el"]
            dstate[...] = dst
            for d in ch:
                d["db"] = d["q"] * d["dq"] - d["k"] * d["dk"] + jnp.where(row_i == d["last"], d["extra"], 0.0)
            dq_s[...] = jnp.concatenate([d["dq"] for d in ch], axis=0)
            dk_s[...] = jnp.concatenate([d["dk"] for d in ch], axis=0)
            dv_s[...] = jnp.concatenate([d["dv"] for d in ch], axis=0)
            db_s[...] = jnp.concatenate([d["db"] for d in ch], axis=0)

        @pl.when(jnp.logical_not(safe))
        def _():
            v_s[...] = zi_ref[...]
            mask = _tri_mask(HG_SUB, rev)
            for c in order:
                base = c * HG_CHUNK
                rows = slice(base, base + HG_CHUNK)
                last = base if rev else base + HG_CHUNK - 1
                st0 = st_ref[c]
                dst1 = dstate[...]
                b = b_s[rows, :]
                bl = b_s[last:last + 1, :]
                e = jnp.exp(b)
                el = jnp.exp(bl)
                ebl = jnp.exp(bl - b)
                q_c, k_c, v_c, do_c = q_s[rows, :], k_s[rows, :], v_s[rows, :], do_ref[rows, :]
                kd = k_c * ebl
                dq_s[rows, :] = _nn(do_c, st0) * e
                dk_inter = _nn(v_c, dst1) * ebl
                dk_s[rows, :] = dk_inter
                dv_s[rows, :] = _nt(kd, dst1)
                extra = el * _rowsum(dst1 * st0) + _rowsum(k_c * dk_inter)
                dstate[...] = _tn(do_c, q_c * e) + dst1 * el
                for i_sub in range(nsb):
                    r0, r1, e0, e1, anchor = _sub_ranges(base, i_sub, rev)
                    q_i, k_i, b_i, v_i, do_i = q_s[r0:r1, :], k_s[r0:r1, :], b_s[r0:r1, :], v_s[r0:r1, :], do_ref[r0:r1, :]
                    decay = jnp.exp(jnp.minimum(b_i[:, None, :] - b_i[None, :, :], 0.0))
                    a_d = jnp.where(mask, jnp.sum(q_i[:, None, :] * k_i[None, :, :] * decay, axis=-1), 0.0)
                    da_d = jnp.where(mask, _nt(do_i, v_i), 0.0)
                    wgt = da_d[:, :, None] * decay
                    dq_s[r0:r1, :] += jnp.sum(wgt * k_i[None, :, :], axis=1)
                    dk_s[r0:r1, :] += jnp.sum(wgt * q_i[:, None, :], axis=0)
                    dv_s[r0:r1, :] += _tn(a_d, do_i)
                    if e1 > e0:
                        anc = b_s[anchor:anchor + 1, :]
                        e_q = jnp.exp(b_i - anc)
                        e_k = jnp.exp(anc - b_s[e0:e1, :])
                        q_t = q_i * e_q
                        k_t = k_s[e0:e1, :] * e_k
                        a_o = _nt(q_t, k_t)
                        da_o = _nt(do_i, v_s[e0:e1, :])
                        dq_s[r0:r1, :] += e_q * _nn(da_o, k_t)
                        dk_s[e0:e1, :] += e_k * _tn(da_o, q_t)
                        dv_s[e0:e1, :] += _tn(a_o, do_i)
                db_s[rows, :] = q_c * dq_s[rows, :] - k_c * dk_s[rows, :]
                db_s[last:last + 1, :] += extra

        dlb = None
        for c in range(ncb):
            rows = slice(c * HG_CHUNK, (c + 1) * HG_CHUNK)
            zf = zf_ref[rows, :]
            s_pos, s_neg = _sigmoid(zf), _sigmoid(-zf)
            f = lb + (1.0 - lb) * s_pos
            dlogf = _cumsum_rows(db_s[rows, :], HG_CHUNK, not rev)
            df = jnp.where(f > jnp.finfo(F32).tiny, dlogf / f, 0.0)
            dfk = df - dk_s[rows, :]
            dzf_ref[rows, :] = ((1.0 - lb) * s_pos * s_neg * dfk).astype(dzf_ref.dtype)
            part = _rowsum(s_neg * dfk)
            dlb = part if dlb is None else dlb + part

        @pl.when(first)
        def _():
            dlb_ref[...] = dlb

        @pl.when(jnp.logical_not(first))
        def _():
            dlb_ref[...] += dlb

        if prev is None:
            dq_ref[...] = dq_s[...]
            dv_ref[...] = dv_s[...]
        else:
            zq = zq_ref[...]
            dq_ref[...] = ((dq_s[...] + in_refs[7][...]) * _dsilu(zq, _sigmoid(zq))).astype(dq_ref.dtype)
            dv_ref[...] = (dv_s[...] + in_refs[8][...]).astype(dv_ref.dtype)

    blk = lambda off: pl.BlockSpec((HG_BLOCK, HEAD), lambda h, i: (tmap(i), off + h))
    vec = pl.BlockSpec((1, HEAD), lambda h, i: (0, h))
    qv_dtype = F32 if prev is None else dz.dtype
    args = [z, z, z, d_o, states, lb_row, dz] + list(prev or ())
    in_specs = [blk(0), blk(zf0), blk(24), blk(0),
                pl.BlockSpec((None, ncb, HEAD, HEAD), lambda h, i: (h, tmap(i), 0, 0)), vec, ANY] + [blk(0)] * len(prev or ())
    out_shapes = [_sds((t, D_MODEL), qv_dtype), _sds((t, D_MODEL), qv_dtype), _sds(dz.shape, dz.dtype), _sds((1, D_MODEL), F32)]
    out_specs = [blk(0), blk(0), blk(zf0), vec]
    if post is not None:
        args += [post[0], z, post[1]]
        in_specs += [blk(0), blk(32), vec]
        out_shapes += [_sds((t, D_MODEL), F32), _sds((t, D_MODEL), dz.dtype), _sds((1, D_MODEL), F32)]
        out_specs += [blk(0), blk(0), vec]
    outs, bufs = _hosted_call(
        name, compute, (N_HEADS, nb), args, in_specs, out_shapes, out_specs,
        [pltpu.VMEM((HEAD, HEAD), F32)] + [pltpu.VMEM((HG_BLOCK, HEAD), F32)] * 8, {6: 2}, comm)
    return outs if comm is None else (outs, bufs)


def _lower_bounds(name, gamma):
    def body(g_ref, o_ref):
        g0, g1 = g_ref[0:1, :], g_ref[1:2, :]
        m = jnp.maximum(g0, g1)
        e0, e1 = jnp.exp(g0 - m), jnp.exp(g1 - m)
        s0, s1 = e0 / (e0 + e1), e1 / (e0 + e1)
        o_ref[0:1, :] = s0 - s0
        o_ref[1:2, :] = (s0 + s1) - s0

    return _pcall(body, name=name, out_shape=_sds(gamma.shape, F32))(gamma)


def _lower_bounds_bwd(name, gamma, dlb):
    def body(g_ref, d_ref, o_ref):
        g0, g1 = g_ref[0:1, :], g_ref[1:2, :]
        m = jnp.maximum(g0, g1)
        e0, e1 = jnp.exp(g0 - m), jnp.exp(g1 - m)
        s0, s1 = e0 / (e0 + e1), e1 / (e0 + e1)
        d0, d1 = d_ref[0:1, :], d_ref[1:2, :]
        ds0 = (d0 + d1) - (d0 + d1)
        ds1 = d1
        inner = s0 * ds0 + s1 * ds1
        o_ref[0:1, :] = s0 * (ds0 - inner)
        o_ref[1:2, :] = s1 * (ds1 - inner)

    return _pcall(body, name=name, out_shape=_sds(gamma.shape, F32))(gamma, dlb)


def _place_in_dz(name, piece, blk, dz):
    t = piece.shape[0]
    tile = 1024
    return _rows(name, lambda a: (a,), t // tile, [piece], [_tile_spec(tile, D_MODEL)],
                 [_sds(dz.shape, dz.dtype)], [_tile_spec(tile, D_MODEL, blk)], into=[(dz, 0)])[0]


def _sg_norm(zv, ln_g, ln_b):
    gv = _gelu(zv)
    xc = gv - _mean_last(gv)
    rstd = lax.rsqrt(_mean_last(xc * xc) + EPS)
    xhat = xc * rstd
    return xhat * ln_g + ln_b, xhat, rstd


def _lane_lo():
    return lax.broadcasted_iota(jnp.int32, (SG_CHUNK, LANES), 1) < (LANES // 2)


SG_TILE = 512


def _sgu_fwd(name, z, w, bias_t, ln_g, ln_b):
    t = z.shape[0]

    def fn(zu, zv, wv, bt, lg, lb):
        u = _gelu(zu)
        vn, _, _ = _sg_norm(zv, lg, lb)
        lo = _lane_lo()
        out_rows = []
        for c in range(SG_TILE // SG_CHUNK):
            rs = slice(c * SG_CHUNK, (c + 1) * SG_CHUNK)
            cols = []
            for j in range(SG_WIDTH // LANES):
                cs = slice(j * LANES, (j + 1) * LANES)
                vb = vn[rs, cs]
                sg = jnp.where(lo, _nn(wv[2 * j], vb), _nn(wv[2 * j + 1], vb)) + bt[:, cs]
                cols.append(u[rs, cs] * sg)
            out_rows.append(jnp.concatenate(cols, axis=1))
        return (jnp.concatenate(out_rows, axis=0),)

    return _rows(name, fn, t // SG_TILE, [z, z, w, bias_t, ln_g, ln_b],
                 [_tile_spec(SG_TILE, SG_WIDTH, 10), _tile_spec(SG_TILE, SG_WIDTH, 11), _whole(w.shape),
                  _whole(bias_t.shape), _whole((1, SG_WIDTH)), _whole((1, SG_WIDTH))],
                 [_sds((t, SG_WIDTH), MXU_DTYPE)], [_tile_spec(SG_TILE, SG_WIDTH)])[0]


def _sgu_bwd(name, dbo, z, w, bias_t, ln_g, ln_b, dz):
    t = z.shape[0]
    n_grp = w.shape[0]

    def fn(dbov, zu, zv, wv, bt, lg, lb):
        u = _gelu(zu)
        vn, xhat, rstd = _sg_norm(zv, lg, lb)
        lo = _lane_lo()
        dw = [None] * n_grp
        dsg_sum = None
        du_rows, dvn_rows = [], []
        for c in range(SG_TILE // SG_CHUNK):
            rs = slice(c * SG_CHUNK, (c + 1) * SG_CHUNK)
            du_cols, dvn_cols, dsg_cols = [], [], []
            for j in range(SG_WIDTH // LANES):
                cs = slice(j * LANES, (j + 1) * LANES)
                vb = vn[rs, cs]
                sg = jnp.where(lo, _nn(wv[2 * j], vb), _nn(wv[2 * j + 1], vb)) + bt[:, cs]
                du_cols.append(dbov[rs, cs] * sg)
                dsg = dbov[rs, cs] * u[rs, cs]
                dsg_cols.append(dsg)
                d0 = _nt(jnp.where(lo, dsg, 0.0), vb)
                d1 = _nt(jnp.where(lo, 0.0, dsg), vb)
                dw[2 * j] = d0 if dw[2 * j] is None else dw[2 * j] + d0
                dw[2 * j + 1] = d1 if dw[2 * j + 1] is None else dw[2 * j + 1] + d1
                dvn_cols.append(jnp.where(lo, _tn(wv[2 * j], dsg), _tn(wv[2 * j + 1], dsg)))
            du_rows.append(jnp.concatenate(du_cols, axis=1))
            dvn_rows.append(jnp.concatenate(dvn_cols, axis=1))
            dsg_c = jnp.concatenate(dsg_cols, axis=1)
            dsg_sum = dsg_c if dsg_sum is None else dsg_sum + dsg_c
        du = jnp.concatenate(du_rows, axis=0)
        dvn = jnp.concatenate(dvn_rows, axis=0)
        dxhat = dvn * lg
        dgv = rstd * (dxhat - _mean_last(dxhat) - xhat * _mean_last(dxhat * xhat))
        dzuv = jnp.concatenate([du * _dgelu(zu), dgv * _dgelu(zv)], axis=1)
        return dzuv, jnp.stack(dw, axis=0), dsg_sum, _rowsum(dvn * xhat), _rowsum(dvn)

    return _rows(name, fn, t // SG_TILE, [dbo, z, z, w, bias_t, ln_g, ln_b],
                 [_tile_spec(SG_TILE, SG_WIDTH), _tile_spec(SG_TILE, SG_WIDTH, 10), _tile_spec(SG_TILE, SG_WIDTH, 11),
                  _whole(w.shape), _whole(bias_t.shape), _whole((1, SG_WIDTH)), _whole((1, SG_WIDTH))],
                 [_sds(dz.shape, dz.dtype), _sds(w.shape, F32), _sds((SG_CHUNK, SG_WIDTH), F32),
                  _sds((1, SG_WIDTH), F32), _sds((1, SG_WIDTH), F32)],
                 [_tile_spec(SG_TILE, 2 * SG_WIDTH, 5), _whole(w.shape), _whole((SG_CHUNK, SG_WIDTH)),
                  _whole((1, SG_WIDTH)), _whole((1, SG_WIDTH))],
                 n_red=4, into=[(dz, 0)])


def _proj_out_t_merge_bwd(name, dmix, w_g, pa, pb, z, l):
    t = z.shape[0]

    def body(d_ref, w_ref, pa_ref, pb_ref, ga_ref, gb_ref, dpa_ref, dpb_ref, dz_ref):
        d = _nt(d_ref[...], _merge_lead(w_ref[...]))
        sa, sb = _sigmoid(ga_ref[...]), _sigmoid(gb_ref[...])
        dpa_ref[...] = (d * sa).astype(dpa_ref.dtype)
        dpb_ref[...] = (d * sb).astype(dpb_ref.dtype)
        dz_ref[...] = jnp.concatenate([d * pa_ref[...] * sa * (1.0 - sa), d * pb_ref[...] * sb * (1.0 - sb)],
                                      axis=1).astype(dz_ref.dtype)

    row = lambda blk: _bs((TF, D_MODEL), lambda m: (m, blk))
    return _pcall(body, name=name, grid=(t // TF,),
                  in_specs=[row(0), _bs((N_CHIPS, None, 256, D_MODEL), lambda m: (0, l, 0, 0)), row(0), row(0), row(6), row(7)],
                  out_specs=[row(0), row(0), _bs((TF, 2 * D_MODEL), lambda m: (m, 3))],
                  out_shape=[_sds((t, D_MODEL), MXU_DTYPE), _sds((t, D_MODEL), MXU_DTYPE), _sds((t, N_IN), MXU_DTYPE)],
                  compiler_params=_params(1))(dmix, w_g, pa, pb, z, z)


def _ple_bwd(name, dx, pe, gz):
    t, d = dx.shape
    tile = 512

    def fn(dv, p, g):
        s = _sigmoid(g)
        return dv * s, dv * p * s * (1.0 - s)

    return _rows(name, fn, t // tile, [dx, pe, gz], [_tile_spec(tile, d)] * 3, [_sds((t, d), MXU_DTYPE)] * 2,
                 [_tile_spec(tile, d)] * 2)


def _loss_bwd(name, y, target):
    t, d = y.shape
    tile = 512

    def fn(yv, tv):
        err = yv - tv
        return err * (1.0 / d), _rowsum(err * err)

    return _rows(name, fn, t // tile, [y, target], [_tile_spec(tile, d)] * 2, [_sds((t, d), F32), _sds((1, d), F32)],
                 [_tile_spec(tile, d), _whole((1, d))], n_red=1)


def _position():
    return lax.axis_index("x"), lax.axis_index("y"), lax.axis_index("c")


def _gather_comm(shards, bufs, l0, nl):
    n = len(shards)
    if bufs is None:
        bufs = [_sds((N_CHIPS,) + s.shape, s.dtype) for s in shards]

    def make(w_refs, out_refs, sems):
        send_sems, recv_sems, local_sems = sems
        x, y, c = _position()
        me = 2 * x + y
        sibling = (x, y, 1 - c)
        chips = [(1 - x, y), (x, 1 - y), (1 - x, 1 - y)]

        def half(ref, cc):
            rows = ref.shape[1] // 2
            return ref.at[pl.ds(l0, nl), pl.ds(cc * rows, rows)]

        def copy(i, k, src, chip, cc, to):
            return pltpu.make_async_remote_copy(
                src_ref=src, dst_ref=half(out_refs[i].at[chip], cc), send_sem=send_sems.at[6 * i + k],
                recv_sem=recv_sems.at[6 * i + k], device_id=to, device_id_type=MESH)

        def local(i):
            return pltpu.make_async_copy(w_refs[i].at[pl.ds(l0, nl)], out_refs[i].at[me, pl.ds(l0, nl)], local_sems.at[i])

        def sends():
            return [copy(i, j, half(w_refs[i], c), me, c, (px, py, c))
                    for i in range(n) for j, (px, py) in enumerate(chips)]

        def start():
            for i in range(n):
                local(i).start()
            for cp in sends():
                cp.start()

        def finish():
            passed = []
            for i in range(n):
                for j, (px, py) in enumerate(chips):
                    chip = 2 * px + py
                    copy(i, j, half(w_refs[i], c), chip, c, (px, py, c)).wait_recv()
                    fwd = copy(i, 3 + j, half(out_refs[i].at[chip], c), chip, c, sibling)
                    fwd.start()
                    passed.append(fwd)
            for i in range(n):
                for j, (px, py) in enumerate(chips):
                    copy(i, 3 + j, half(w_refs[i], c), 2 * px + py, 1 - c, sibling).wait_recv()
            for cp in sends() + passed:
                cp.wait_send()
            for i in range(n):
                local(i).wait()

        return start, finish

    sems = [pltpu.SemaphoreType.DMA((6 * n,)), pltpu.SemaphoreType.DMA((6 * n,)), pltpu.SemaphoreType.DMA((n,))]
    return _Comm(shards, bufs, sems, make)


def _exchange_comm(grads, bufs, l0, nl):
    n = len(grads)
    if bufs is None:
        bufs = [_sds((N_DEV,) + g.shape[1:], g.dtype) for g in grads]

    def make(g_refs, out_refs, sems):
        send_sems, recv_sems, local_sems = sems
        x, y, c = _position()
        me = 2 * x + y
        sibling = (x, y, 1 - c)
        chips = [(1 - x, y), (x, 1 - y), (1 - x, 1 - y)]

        def lay(ref):
            return ref.at[pl.ds(l0, nl)]

        def copy(i, k, src, slot, to):
            return pltpu.make_async_remote_copy(
                src_ref=src, dst_ref=lay(out_refs[i].at[slot]), send_sem=send_sems.at[7 * i + k],
                recv_sem=recv_sems.at[7 * i + k], device_id=to, device_id_type=MESH)

        def local(i):
            return pltpu.make_async_copy(lay(g_refs[i].at[me]), lay(out_refs[i].at[2 * me + c]), local_sems.at[i])

        def first():
            cps = []
            for i in range(n):
                cps.append(copy(i, 0, lay(g_refs[i].at[me]), 2 * me + c, sibling))
                for j, (px, py) in enumerate(chips):
                    cps.append(copy(i, 1 + j, lay(g_refs[i].at[2 * px + py]), 2 * me + c, (px, py, c)))
            return cps

        def start():
            for i in range(n):
                local(i).start()
            for cp in first():
                cp.start()

        def finish():
            passed = []
            for i in range(n):
                for j, (px, py) in enumerate(chips):
                    slot = 2 * (2 * px + py) + c
                    copy(i, 1 + j, lay(g_refs[i].at[me]), slot, (px, py, c)).wait_recv()
                    fwd = copy(i, 4 + j, lay(out_refs[i].at[slot]), slot, sibling)
                    fwd.start()
                    passed.append(fwd)
            for i in range(n):
                copy(i, 0, lay(g_refs[i].at[me]), 2 * me + (1 - c), sibling).wait_recv()
                for j, (px, py) in enumerate(chips):
                    copy(i, 4 + j, lay(g_refs[i].at[me]), 2 * (2 * px + py) + (1 - c), sibling).wait_recv()
            for cp in first() + passed:
                cp.wait_send()
            for i in range(n):
                local(i).wait()

        return start, finish

    sems = [pltpu.SemaphoreType.DMA((7 * n,)), pltpu.SemaphoreType.DMA((7 * n,)), pltpu.SemaphoreType.DMA((n,))]
    return _Comm(grads, bufs, sems, make)


def _all_reduce_small(packed):
    rows = packed.shape[0]

    def body(x_ref, sum_ref, slots, send_sems, recv_sems, local_sem):
        x, y, c = _position()
        me = 4 * x + 2 * y + c
        mine = pltpu.make_async_copy(x_ref, slots.at[me], local_sem)
        mine.start()
        sends = []
        for k in range(1, N_DEV):
            peer = (x ^ (k >> 2), y ^ ((k >> 1) & 1), c ^ (k & 1))
            cp = pltpu.make_async_remote_copy(src_ref=x_ref, dst_ref=slots.at[me], send_sem=send_sems.at[k - 1],
                                              recv_sem=recv_sems.at[k - 1], device_id=peer, device_id_type=MESH)
            cp.start()
            sends.append(cp)
        for k in range(1, N_DEV):
            px, py, pc = x ^ (k >> 2), y ^ ((k >> 1) & 1), c ^ (k & 1)
            pltpu.make_async_remote_copy(src_ref=x_ref, dst_ref=slots.at[4 * px + 2 * py + pc], send_sem=send_sems.at[k - 1],
                                         recv_sem=recv_sems.at[k - 1], device_id=(px, py, pc), device_id_type=MESH).wait_recv()
        for cp in sends:
            cp.wait_send()
        mine.wait()
        total = slots[0]
        for d in range(1, N_DEV):
            total = total + slots[d]
        sum_ref[...] = total

    vmem = pl.BlockSpec(memory_space=pltpu.VMEM)
    return _pcall(
        body, name="all_reduce_small", in_specs=[vmem], out_specs=vmem, out_shape=_sds(packed.shape, F32),
        scratch_shapes=[pltpu.VMEM((N_DEV, rows, LANES), F32), pltpu.SemaphoreType.DMA((N_DEV - 1,)),
                        pltpu.SemaphoreType.DMA((N_DEV - 1,)), pltpu.SemaphoreType.DMA],
        compiler_params=pltpu.CompilerParams(vmem_limit_bytes=VMEM_LIMIT_BYTES),
    )(packed)


def _adamw(w, g, m, v):
    m = ADAM_B1 * m + (1.0 - ADAM_B1) * g
    v = ADAM_B2 * v + (1.0 - ADAM_B2) * (g * g)
    m_hat = m / (1.0 - ADAM_B1 ** ADAM_STEP)
    v_hat = v / (1.0 - ADAM_B2 ** ADAM_STEP)
    delta = -ADAM_LR * (m_hat / (jnp.sqrt(v_hat) + ADAM_EPS) + ADAM_WD * w)
    return delta, m, v


def _adam_sharded(name, parts, w, m, v):
    shape = w.shape
    cols = shape[-1]
    rows = w.size // cols
    tile = 8
    while tile * 2 * cols <= ADAM_TILE_ELEMS and rows % (tile * 2) == 0:
        tile *= 2

    def fn(p, wv, mv, vv):
        g = p[0].astype(F32)
        for d in range(1, N_DEV):
            g = g + p[d].astype(F32)
        return (g,) + _adamw(wv, g, mv, vv)

    two_d = lambda a: a.reshape(rows, cols)
    outs = _rows(name, fn, rows // tile, [parts.reshape(N_DEV, rows, cols), two_d(w), two_d(m), two_d(v)],
                 [pl.BlockSpec((N_DEV, tile, cols), lambda i: (0, i, 0))] + [_tile_spec(tile, cols)] * 3,
                 [_sds((rows, cols), F32)] * 4, [_tile_spec(tile, cols)] * 4)
    return [o.reshape(shape) for o in outs]


def _adam_small(name, g, w, m, v):
    rows = g.shape[0]
    tile = rows // 2
    return _rows(name, lambda gv, wv, mv, vv: _adamw(wv, gv, mv, vv), rows // tile, [g, w, m, v],
                 [_tile_spec(tile, LANES)] * 4, [_sds(g.shape, F32)] * 3, [_tile_spec(tile, LANES)] * 3)


BIG = ("w_in", "w_a", "w_b", "w_out", "w_gate", "w_up", "w_down", "w_ple", "w_ple_gate")
SMALL = ("norm_mix_pre", "lb_gamma_fwd", "lb_gamma_bwd", "hg_norm", "sg_w", "sg_b", "sg_ln_g", "sg_ln_b",
         "norm_mix_post", "norm_ffn_pre", "norm_ffn_post")


def _with_comm(plan, tag, state, call):
    if tag not in plan:
        return call(None)
    keys, comm = plan[tag](state)
    res, bufs = call(comm)
    state.update(zip(keys, bufs))
    return res


def _layer_fwd(l, x, p_l, wg, sm, plan):
    sv = {"x": x}
    h = _rms_fwd(f"norm_mix_pre_l{l}", x, sm["norm_mix_pre"])
    z = _with_comm(plan, "proj_in", wg, lambda comm: _proj_in(h, wg["w_in"], l, comm=comm))
    o_f, st_f = _hgrn_fwd(f"hgrn_fwd_l{l}", z, sm["lb_fwd"], False)
    o_sum, st_b, a_out = _hgrn_fwd(f"hgrn_rev_l{l}", z, sm["lb_bwd"], True, post=(o_f, sm["hg_norm"]))
    b_out = _sgu_fwd(f"sgu_l{l}", z, sm["sg_w"], sm["sg_bias_t"], sm["sg_ln_g"], sm["sg_ln_b"])
    pa = _proj_rows_sharded(f"proj_a_l{l}", a_out, wg["w_a"], l, F32)
    pb = _proj_cols256(f"proj_b_l{l}", b_out, wg["w_b"], l)
    merged, mix = _with_comm(plan, "merge_proj_out", wg, lambda comm: _merge_proj_out(
        f"merge_proj_out_l{l}", pa, pb, z, wg["w_out"], l, comm=comm))
    x1, h2 = _resid_rms_norm_fwd(f"norm_mix_post_ffn_pre_l{l}", x, mix, sm["norm_mix_post"], sm["norm_ffn_pre"])
    gt, up, act = _with_comm(plan, "proj_gate_up", wg, lambda comm: _proj_gate_up(
        f"proj_gate_up_l{l}", h2, wg["w_gate"], wg["w_up"], l, comm=comm))
    ff = _with_comm(plan, "proj_down", wg, lambda comm: _proj_ffn_out(f"proj_down_l{l}", act, wg["w_down"], l, comm=comm))
    x2 = _resid_rms_fwd(f"norm_ffn_post_l{l}", x1, ff, sm["norm_ffn_post"])
    pe = _proj_cols256(f"proj_ple_l{l}", p_l, wg["w_ple"], l)
    gz, x3 = _proj_ple_gate_ple(f"proj_ple_gate_ple_l{l}", x2, wg["w_ple_gate"], pe, l)
    sv.update(h=h, z=z, o=o_sum, st_f=st_f, st_b=st_b, a_out=a_out, b_out=b_out, pa=pa, pb=pb,
              merged=merged, mix=mix, x1=x1, h2=h2, gt=gt, up=up, act=act, ff=ff, x2=x2, pe=pe, gz=gz, p=p_l)
    return x3, sv


def _layer_bwd(l, dx3, sv, wg, sm, gw, parts, plan):
    t = dx3.shape[0]
    nt = t // TT
    sg = {}

    def wgrad(key, *a, **k):
        gw[key] = _wgrad(f"grad_{key}_l{l}", *a, l=l, into=gw.get(key), **k)

    row = lambda width: _bs((TT, width), lambda j, i: (i, 0))
    row_j = lambda width: _bs((TT, width), lambda j, i: (i, j))
    ffn_j = _bs((None, TT, FFN_SHARD), lambda j, i: (j, i, 0))
    blk_j = lambda shape: (tuple(shape), lambda j, i: (j, l, 0, 0))

    dpe, dgz = _ple_bwd(f"ple_bwd_l{l}", dx3, sv["pe"], sv["gz"])
    wgrad("w_ple", sv["p"], row(PLE_DIM), dpe, row_j(256), (PLE_DIM, 256), blk_j((PLE_DIM, 256)), N_CHIPS)
    wgrad("w_ple_gate", sv["x2"], row_j(256), dgz, row(D_MODEL), (256, D_MODEL), blk_j((256, D_MODEL)), N_CHIPS)
    dx2 = _proj_rows_sharded_t(f"proj_ple_gate_t_l{l}", dgz, wg["w_ple_gate"], l, F32, add=dx3)

    dff, sg["norm_ffn_post"] = _rms_bwd(f"norm_ffn_post_bwd_l{l}", sv["ff"], sm["norm_ffn_post"], dx2, None, MXU_DTYPE)
    dgt, dup = _with_comm(plan, "proj_down_t", parts, lambda comm: _proj_down_t_swiglu_bwd(
        f"proj_down_t_swiglu_bwd_l{l}", dff, wg["w_down"], sv["gt"], sv["up"], l, comm=comm))
    wgrad("w_down", sv["act"], ffn_j, dff, row(D_MODEL), (FFN_SHARD, D_MODEL), blk_j((FFN_SHARD, D_MODEL)), N_CHIPS)
    dh2 = _with_comm(plan, "proj_gate_up_t", parts, lambda comm: _proj_ffn_in_t(
        f"proj_gate_up_t_l{l}", [(dgt, wg["w_gate"]), (dup, wg["w_up"])], l, comm=comm))
    wgrad("w_gate", sv["h2"], row(D_MODEL), dgt, ffn_j, (D_MODEL, FFN_SHARD), blk_j((D_MODEL, FFN_SHARD)), N_CHIPS)
    wgrad("w_up", sv["h2"], row(D_MODEL), dup, ffn_j, (D_MODEL, FFN_SHARD), blk_j((D_MODEL, FFN_SHARD)), N_CHIPS)
    dx1, dmix, sg["norm_ffn_pre"], sg["norm_mix_post"] = _rms_bwd_pair(
        f"norm_ffn_pre_mix_post_bwd_l{l}", sv["x1"], sm["norm_ffn_pre"], dh2, dx2, sv["mix"], sm["norm_mix_post"], MXU_DTYPE)

    dpa, dpb, dz = _proj_out_t_merge_bwd(f"proj_out_t_merge_bwd_l{l}", dmix, wg["w_out"], sv["pa"], sv["pb"], sv["z"], l)
    wgrad("w_out", sv["merged"], row_j(256), dmix, row(D_MODEL), (256, D_MODEL), blk_j((256, D_MODEL)), N_CHIPS)
    da = _proj_rows_sharded_t(f"proj_a_t_l{l}", dpa, wg["w_a"], l, F32)
    wgrad("w_a", sv["a_out"], row_j(256), dpa, row(D_MODEL), (256, D_MODEL), blk_j((256, D_MODEL)), N_CHIPS)
    dbo = _proj_cols256_t(f"proj_b_t_l{l}", dpb, wg["w_b"], l)
    wgrad("w_b", sv["b_out"], row(SG_WIDTH), dpb, row_j(256), (SG_WIDTH, 256), blk_j((SG_WIDTH, 256)), N_CHIPS)

    dz, sg["sg_w"], dsg_sum, sg["sg_ln_g"], sg["sg_ln_b"] = _sgu_bwd(
        f"sgu_bwd_l{l}", dbo, sv["z"], sm["sg_w"], sm["sg_bias_t"], sm["sg_ln_g"], sm["sg_ln_b"], dz)
    sg["sg_b"] = dsg_sum.reshape(SG_CHUNK, N_HEADS, SG_WIDTH // N_HEADS).sum(axis=-1).T
    dq_f, dv_f, dz, sg["lb_fwd"], d_o, dzg, sg["hg_norm"] = _with_comm(plan, "hgrn_fwd_bwd", parts, lambda comm: _hgrn_bwd(
        f"hgrn_fwd_bwd_l{l}", sv["z"], da, sv["st_f"], sm["lb_fwd"], False, dz, comm=comm, post=(sv["o"], sm["hg_norm"])))
    dz = _place_in_dz(f"place_dzg_l{l}", dzg, 4, dz)
    dzq, dzi, dz, sg["lb_bwd"] = _hgrn_bwd(f"hgrn_rev_bwd_l{l}", sv["z"], d_o, sv["st_b"], sm["lb_bwd"], True, dz,
                                           prev=(dq_f, dv_f))
    dz = _place_in_dz(f"place_dzq_l{l}", dzq, 0, dz)
    dz = _place_in_dz(f"place_dzi_l{l}", dzi, 3, dz)

    gw["w_in"] = _wgrad(f"grad_w_in_l{l}", sv["h"], _bs((TM_WIDE, D_MODEL), lambda n, i: (i, 0)), dz,
                        _bs((TM_WIDE, 1024), lambda n, i: (i, n)), (D_MODEL, 2048),
                        ((D_MODEL, 1024), lambda n, i: (n // 2, l, 0, n % 2)), 8, l, gw.get("w_in"), tt=TM_WIDE)
    dh = _with_comm(plan, "proj_in_t", parts, lambda comm: _proj_in_t(dz, wg["w_in"], l, comm=comm))
    dx, sg["norm_mix_pre"] = _rms_bwd(f"norm_mix_pre_bwd_l{l}", sv["x"], sm["norm_mix_pre"], dh, dx1, F32)
    del nt
    return dx, gw, sg


def _pack(parts):
    return jnp.concatenate([a.reshape(-1, LANES) for a in parts], axis=0)


def _step(x, p, loss_target, w, m, v):
    x = x[0]
    target = loss_target[0]
    depth = w["w_in"].shape[0]

    assert depth == 2, "the exchanges below ride in layer 0's kernels and carry layer 1's data"
    shards = {k: w[k].astype(MXU_DTYPE) for k in BIG}
    rest_a = [k for k in BIG if k not in ("w_in", "w_gate", "w_up")]
    rest_b = ["w_gate", "w_up"]
    rest = rest_a + rest_b

    def gather(keys, l0, extend):
        return lambda wg: (keys, _gather_comm([shards[k] for k in keys], [wg[k] for k in keys] if extend else None, l0, 1))

    wg = {"w_in": _run_comm("gather_w_in_l0", gather(["w_in"], 0, False)(None)[1])[0]}
    fwd_plans = [{"proj_in": gather(rest, 0, False), "merge_proj_out": gather(["w_in"], 1, True),
                  "proj_gate_up": gather(rest_a, 1, True), "proj_down": gather(rest_b, 1, True)}, {}]
    lb_f = _lower_bounds("lower_bounds_fwd", w["lb_gamma_fwd"])
    lb_b = _lower_bounds("lower_bounds_bwd", w["lb_gamma_bwd"])

    def small_of(l):
        sm = {k: w[k][l:l + 1] for k in ("norm_mix_pre", "hg_norm", "sg_ln_g", "sg_ln_b", "norm_mix_post",
                                        "norm_ffn_pre", "norm_ffn_post")}
        sm["lb_fwd"], sm["lb_bwd"] = lb_f[l:l + 1], lb_b[l:l + 1]
        sm["sg_w"] = w["sg_w"][l]
        sm["sg_bias_t"] = jnp.repeat(w["sg_b"][l].T, SG_WIDTH // N_HEADS, axis=1)
        return sm

    saved = []
    h = x
    for l in range(depth):
        h, sv = _layer_fwd(l, h, p[l, 0], wg, small_of(l), fwd_plans[l])
        saved.append(sv)

    dy, sq_err = _loss_bwd("loss", h, target)
    gw, parts = {}, {}

    def exchange(keys, l0, extend):
        return lambda parts: (keys, _exchange_comm([gw[k] for k in keys], [parts[k] for k in keys] if extend else None, l0, 1))

    bwd_plans = [{"proj_down_t": exchange(["w_in"], 1, False), "proj_gate_up_t": exchange(rest, 1, False),
                  "hgrn_fwd_bwd": exchange(rest, 0, True), "proj_in_t": exchange(["w_in"], 0, True)}, {}]
    small_grads = [None] * depth
    for l in reversed(range(depth)):
        dy, gw, small_grads[l] = _layer_bwd(l, dy, saved[l], wg, small_of(l), gw, parts, bwd_plans[l])

    def stack(key):
        return jnp.concatenate([small_grads[l][key].reshape((1,) + w_shape[1:]) for l in range(depth)], axis=0)

    g_small = {}
    for key in SMALL:
        w_shape = w[key].shape
        if key == "lb_gamma_fwd":
            dlb = jnp.concatenate([small_grads[l]["lb_fwd"] for l in range(depth)], axis=0)
            g_small[key] = _lower_bounds_bwd("lower_bounds_fwd_bwd", w[key], dlb)
        elif key == "lb_gamma_bwd":
            dlb = jnp.concatenate([small_grads[l]["lb_bwd"] for l in range(depth)], axis=0)
            g_small[key] = _lower_bounds_bwd("lower_bounds_bwd_bwd", w[key], dlb)
        else:
            g_small[key] = stack(key)

    packed = _pack([g_small[k] for k in SMALL] + [sq_err])
    summed = _all_reduce_small(packed)
    n_small_rows = sum(w[k].size for k in SMALL) // LANES
    loss = 0.5 * jnp.sum(summed[n_small_rows:]) / D_MODEL

    g_rows = summed[:n_small_rows]
    d_rows, m_rows, v_rows = _adam_small("adamw_small", g_rows, _pack([w[k] for k in SMALL]),
                                         _pack([m[k] for k in SMALL]), _pack([v[k] for k in SMALL]))
    out = {}
    off = 0
    for key in SMALL:
        n_rows = w[key].size // LANES
        sl = slice(off, off + n_rows)
        out[key] = tuple(a[sl].reshape(w[key].shape) for a in (g_rows, d_rows, m_rows, v_rows))
        off += n_rows

    for key in BIG:
        out[key] = tuple(_adam_sharded(f"adamw_{key}", parts[key], w[key], m[key], v[key]))
    return loss, dy[None], out


WEIGHTS = ("norm_mix_pre", "w_in", "lb_gamma_fwd", "lb_gamma_bwd", "hg_norm", "sg_w", "sg_b", "sg_ln_g", "sg_ln_b",
           "w_a", "w_b", "w_out", "norm_mix_post", "norm_ffn_pre", "w_gate", "w_up", "w_down", "norm_ffn_post",
           "w_ple", "w_ple_gate")


def kernel(x, p, norm_mix_pre, w_in, lb_gamma_fwd, lb_gamma_bwd, hg_norm, sg_w, sg_b, sg_ln_g, sg_ln_b, w_a, w_b, w_out, norm_mix_post, norm_ffn_pre, w_gate, w_up, w_down, norm_ffn_post, w_ple, w_ple_gate, loss_target, m_norm_mix_pre, m_w_in, m_lb_gamma_fwd, m_lb_gamma_bwd, m_hg_norm, m_sg_w, m_sg_b, m_sg_ln_g, m_sg_ln_b, m_w_a, m_w_b, m_w_out, m_norm_mix_post, m_norm_ffn_pre, m_w_gate, m_w_up, m_w_down, m_norm_ffn_post, m_w_ple, m_w_ple_gate, v_norm_mix_pre, v_w_in, v_lb_gamma_fwd, v_lb_gamma_bwd, v_hg_norm, v_sg_w, v_sg_b, v_sg_ln_g, v_sg_ln_b, v_w_a, v_w_b, v_w_out, v_norm_mix_post, v_norm_ffn_pre, v_w_gate, v_w_up, v_w_down, v_norm_ffn_post, v_w_ple, v_w_ple_gate):
    w = dict(zip(WEIGHTS, (norm_mix_pre, w_in, lb_gamma_fwd, lb_gamma_bwd, hg_norm, sg_w, sg_b, sg_ln_g, sg_ln_b, w_a, w_b, w_out, norm_mix_post, norm_ffn_pre, w_gate, w_up, w_down, norm_ffn_post, w_ple, w_ple_gate)))
    m = dict(zip(WEIGHTS, (m_norm_mix_pre, m_w_in, m_lb_gamma_fwd, m_lb_gamma_bwd, m_hg_norm, m_sg_w, m_sg_b, m_sg_ln_g, m_sg_ln_b, m_w_a, m_w_b, m_w_out, m_norm_mix_post, m_norm_ffn_pre, m_w_gate, m_w_up, m_w_down, m_norm_ffn_post, m_w_ple, m_w_ple_gate)))
    v = dict(zip(WEIGHTS, (v_norm_mix_pre, v_w_in, v_lb_gamma_fwd, v_lb_gamma_bwd, v_hg_norm, v_sg_w, v_sg_b, v_sg_ln_g, v_sg_ln_b, v_w_a, v_w_b, v_w_out, v_norm_mix_post, v_norm_ffn_pre, v_w_gate, v_w_up, v_w_down, v_norm_ffn_post, v_w_ple, v_w_ple_gate)))
    loss, grad_x, out = _step(x, p, loss_target, w, m, v)
    res = [loss, grad_x]
    for i in range(4):
        res += [out[k][i] for k in WEIGHTS]
    return tuple(res)
```

```python
import functools

import jax
import jax.numpy as jnp
from jax import lax
from jax.experimental import pallas as pl
from jax.experimental.pallas import tpu as pltpu

F32 = jnp.float32
MXU_DTYPE = jnp.bfloat16
GRAD_EXCHANGE_DTYPE = jnp.bfloat16

D_MODEL = 1024
N_HEADS = 8
HEAD = 128
HG_CHUNK = 64
HG_SUB = 16
HG_BLOCK = 512
HG_SAFE_EXP = 80.0
SG_CHUNK = 128
SG_WIDTH = 512
FFN_SHARD = 704
PLE_DIM = 256
N_IN = 8192
N_CHIPS = 4
N_DEV = 8
EPS = 1e-6
LANES = 128
VMEM_LIMIT_BYTES = 56 * 2 ** 20

ADAM_LR = 0.001
ADAM_B1 = 0.9
ADAM_B2 = 0.999
ADAM_EPS = 1e-08
ADAM_WD = 0.01
ADAM_STEP = 10
ADAM_TILE_ELEMS = 128 * 1024

MESH = pl.DeviceIdType.MESH
ANY = pl.BlockSpec(memory_space=pl.ANY)


def _pcall(body, **kw):
    return pl.pallas_call(body, **kw)


def _params(n_axes):
    return pltpu.CompilerParams(dimension_semantics=("arbitrary",) * n_axes, vmem_limit_bytes=VMEM_LIMIT_BYTES)


def _dot(a, b, ca, cb):
    return lax.dot_general(a.astype(MXU_DTYPE), b.astype(MXU_DTYPE), (((ca,), (cb,)), ((), ())),
                           preferred_element_type=F32)


def _dot_f32(a, b, ca, cb):
    return lax.dot_general(a, b, (((ca,), (cb,)), ((), ())), precision=lax.Precision.HIGH,
                           preferred_element_type=F32)


def _nn(a, b):
    return _dot(a, b, 1, 0)


def _nt(a, b):
    return _dot(a, b, 1, 1)


def _tn(a, b):
    return _dot(a, b, 0, 0)


NN, NT, TN = (1, 0), (1, 1), (0, 0)


def _sigmoid(x):
    return jax.nn.sigmoid(x)


def _dsilu(x, s):
    return s * (1.0 + x * (1.0 - s))


_SQRT_HALF = 0.7071067811865476
_INV_SQRT_2PI = 0.3989422804014327


def _gelu(x):
    return 0.5 * x * (1.0 + lax.erf(x * _SQRT_HALF))


def _dgelu(x):
    return 0.5 * (1.0 + lax.erf(x * _SQRT_HALF)) + x * jnp.exp(-0.5 * x * x) * _INV_SQRT_2PI


def _mean_last(x):
    return jnp.mean(x, axis=-1, keepdims=True)


def _rowsum(x):
    return jnp.sum(x, axis=0, keepdims=True)


class _Comm:
    def __init__(self, ins, bufs, sem_shapes, make):
        self.ins, self.bufs, self.sem_shapes, self.make = list(ins), list(bufs), list(sem_shapes), make
        self.extends = not isinstance(self.bufs[0], jax.ShapeDtypeStruct)


def _hosted_call(name, compute, grid, args, in_specs, out_shapes, out_specs, scratch, aliases, comm):
    n_in, n_out, n_scr = len(args), len(out_shapes), len(scratch)
    if comm is None:
        def plain(*refs):
            compute(refs[:n_in], refs[n_in:n_in + n_out], refs[n_in + n_out:])

        res = _pcall(plain, name=name, grid=grid, in_specs=list(in_specs), out_specs=list(out_specs),
                     out_shape=list(out_shapes), scratch_shapes=list(scratch), input_output_aliases=dict(aliases),
                     compiler_params=_params(len(grid)))(*args)
        return list(res), []

    n_cin, n_buf = len(comm.ins), len(comm.bufs)
    all_args = list(args) + comm.ins + (comm.bufs if comm.extends else [])
    n_all = len(all_args)
    all_aliases = dict(aliases)
    if comm.extends:
        for j in range(n_buf):
            all_aliases[n_in + n_cin + j] = n_out + j
    buf_shapes = [_sds(b.shape, b.dtype) for b in comm.bufs]

    def body(*refs):
        outs = refs[n_all:n_all + n_out + n_buf]
        scr = refs[n_all + n_out + n_buf:]
        start, finish = comm.make(refs[n_in:n_in + n_cin], outs[n_out:], scr[n_scr:])
        first, last = None, None
        for axis, size in enumerate(grid):
            i = pl.program_id(axis)
            first = (i == 0) if first is None else jnp.logical_and(first, i == 0)
            last = (i == size - 1) if last is None else jnp.logical_and(last, i == size - 1)
        pl.when(first)(start)
        compute(refs[:n_in], outs[:n_out], scr[:n_scr])
        pl.when(last)(finish)

    res = _pcall(body, name=name, grid=grid, in_specs=list(in_specs) + [ANY] * (n_all - n_in),
                 out_specs=list(out_specs) + [ANY] * n_buf, out_shape=list(out_shapes) + buf_shapes,
                 scratch_shapes=list(scratch) + comm.sem_shapes, input_output_aliases=all_aliases,
                 compiler_params=_params(len(grid)))(*all_args)
    return list(res[:n_out]), list(res[n_out:])


def _run_comm(name, comm):
    n_cin, n_buf = len(comm.ins), len(comm.bufs)
    all_args = comm.ins + (comm.bufs if comm.extends else [])
    n_all = len(all_args)

    def body(*refs):
        start, finish = comm.make(refs[:n_cin], refs[n_all:n_all + n_buf], refs[n_all + n_buf:])
        start()
        finish()

    res = _pcall(body, name=name, in_specs=[ANY] * n_all, out_specs=[ANY] * n_buf,
                 out_shape=[_sds(b.shape, b.dtype) for b in comm.bufs], scratch_shapes=comm.sem_shapes,
                 input_output_aliases={n_cin + j: j for j in range(n_buf)} if comm.extends else {})(*all_args)
    return list(res)


def _mm(name, pairs, kind, out_shape, grid, in_specs, out_spec, *, reduce_axis=None, add=None,
        add_spec=None, into=None, prep=None, comm=None):
    n_pairs = len(pairs)
    has_add = add is not None
    staged = reduce_axis is not None and out_shape.dtype != F32

    def compute(in_refs, out_refs, scr):
        o_ref = out_refs[0]
        acc = None
        for i in range(n_pairs):
            a = in_refs[2 * i][...]
            b = in_refs[2 * i + 1][...]
            if prep is not None:
                b = prep(b)
            prod = _dot(a, b, *kind)
            acc = prod if acc is None else acc + prod
        if has_add:
            acc = acc + in_refs[2 * n_pairs][...]
        if reduce_axis is None:
            o_ref[...] = acc.astype(o_ref.dtype)
        else:
            r = pl.program_id(reduce_axis)
            acc_ref = scr[0] if staged else o_ref

            @pl.when(r == 0)
            def _():
                acc_ref[...] = acc

            @pl.when(r > 0)
            def _():
                acc_ref[...] += acc

            if staged:
                @pl.when(r == grid[reduce_axis] - 1)
                def _():
                    o_ref[...] = acc_ref[...].astype(o_ref.dtype)

    scratch = []
    if staged:
        scratch = [pltpu.VMEM(tuple(d for d in out_spec.block_shape if d is not None), F32)]
    args = [t for pair in pairs for t in pair]
    specs = list(in_specs)
    if has_add:
        args.append(add)
        specs.append(add_spec)
    aliases = {}
    if into is not None:
        aliases = {len(args): 0}
        args.append(into)
        specs.append(ANY)
    outs, bufs = _hosted_call(name, compute, grid, args, specs, [out_shape], [out_spec], scratch, aliases, comm)
    return outs[0] if comm is None else (outs[0], bufs)


def _sds(shape, dtype):
    return jax.ShapeDtypeStruct(tuple(shape), dtype)


def _bs(shape, fn):
    return pl.BlockSpec(tuple(shape), fn)


TM = 1024
TM_WIDE = 2048


def _merge_lead(b):
    return b.reshape(b.shape[0] * b.shape[1], b.shape[2])


def _proj_in(h, w_in_g, l, comm=None):
    t = h.shape[0]
    return _mm(f"proj_in_l{l}", [(h, w_in_g)], NN, _sds((t, N_IN), F32), (8, t // TM_WIDE),
               [_bs((TM_WIDE, D_MODEL), lambda n, m: (m, 0)),
                _bs((None, None, D_MODEL, 1024), lambda n, m: (n // 2, l, 0, n % 2))],
               _bs((TM_WIDE, 1024), lambda n, m: (m, n)), comm=comm)


def _proj_rows_sharded(name, a, w_g, l, out_dtype, add=None):
    t = a.shape[0]
    return _mm(name, [(a, w_g)], NN, _sds((t, D_MODEL), out_dtype), (t // TM,),
               [_bs((TM, D_MODEL), lambda m: (m, 0)),
                _bs((N_CHIPS, None, 256, D_MODEL), lambda m: (0, l, 0, 0))],
               _bs((TM, D_MODEL), lambda m: (m, 0)), prep=_merge_lead, add=add,
               add_spec=_bs((TM, D_MODEL), lambda m: (m, 0)))


def _proj_rows_sharded_t(name, g, w_g, l, out_dtype, add=None):
    t = g.shape[0]
    return _mm(name, [(g, w_g)], NT, _sds((t, D_MODEL), out_dtype), (t // TM,),
               [_bs((TM, D_MODEL), lambda m: (m, 0)),
                _bs((N_CHIPS, None, 256, D_MODEL), lambda m: (0, l, 0, 0))],
               _bs((TM, D_MODEL), lambda m: (m, 0)), prep=_merge_lead, add=add,
               add_spec=_bs((TM, D_MODEL), lambda m: (m, 0)))


def _proj_cols256(name, a, w_g, l):
    t, k = a.shape
    return _mm(name, [(a, w_g)], NN, _sds((t, D_MODEL), F32), (N_CHIPS, t // TM),
               [_bs((TM, k), lambda j, m: (m, 0)),
                _bs((None, None, k, 256), lambda j, m: (j, l, 0, 0))],
               _bs((TM, 256), lambda j, m: (m, j)))


def _proj_cols256_t(name, g, w_g, l):
    t = g.shape[0]
    k = w_g.shape[2]
    specs = []
    for j in range(N_CHIPS):
        specs += [_bs((TM, 256), lambda m, j=j: (m, j)), _bs((None, None, k, 256), lambda m, j=j: (j, l, 0, 0))]
    return _mm(name, [(g, w_g)] * N_CHIPS, NT, _sds((t, k), F32), (t // TM,), specs, _bs((TM, k), lambda m: (m, 0)))


def _proj_gate_up(name, h2, wg_g, wu_g, l, comm=None):
    t = h2.shape[0]

    def compute(in_refs, out_refs, scr):
        h_ref, wg_ref, wu_ref = in_refs
        gt_ref, up_ref, act_ref = out_refs
        h = h_ref[...]
        g = _nn(h, wg_ref[...])
        u = _nn(h, wu_ref[...])
        gt_ref[...] = g
        up_ref[...] = u
        act_ref[...] = ((g * _sigmoid(g)) * u).astype(act_ref.dtype)

    w_spec = _bs((None, None, D_MODEL, FFN_SHARD), lambda j, m: (j, l, 0, 0))
    o_spec = _bs((None, TM, FFN_SHARD), lambda j, m: (j, m, 0))
    outs, bufs = _hosted_call(name, compute, (N_CHIPS, t // TM), [h2, wg_g, wu_g],
                        [_bs((TM, D_MODEL), lambda j, m: (m, 0)), w_spec, w_spec],
                        [_sds((N_CHIPS, t, FFN_SHARD), F32), _sds((N_CHIPS, t, FFN_SHARD), F32),
                         _sds((N_CHIPS, t, FFN_SHARD), MXU_DTYPE)], [o_spec, o_spec, o_spec], [], {}, comm)
    return outs if comm is None else (outs, bufs)


def _proj_ffn_in_t(name, pairs, l, comm=None):
    t = pairs[0][0].shape[1]
    specs, all_pairs = [], []
    for pair in pairs:
        for j in range(N_CHIPS):
            all_pairs.append(pair)
            specs += [_bs((None, TF, FFN_SHARD), lambda m, j=j: (j, m, 0)),
                      _bs((None, None, D_MODEL, FFN_SHARD), lambda m, j=j: (j, l, 0, 0))]
    return _mm(name, all_pairs, NT, _sds((t, D_MODEL), F32), (t // TF,), specs,
               _bs((TF, D_MODEL), lambda m: (m, 0)), comm=comm)


def _proj_ffn_out(name, act, w_g, l, comm=None):
    t = act.shape[1]
    specs = []
    for j in range(N_CHIPS):
        specs += [_bs((None, TM, FFN_SHARD), lambda m, j=j: (j, m, 0)),
                  _bs((None, None, FFN_SHARD, D_MODEL), lambda m, j=j: (j, l, 0, 0))]
    return _mm(name, [(act, w_g)] * N_CHIPS, NN, _sds((t, D_MODEL), F32), (t // TM,), specs,
               _bs((TM, D_MODEL), lambda m: (m, 0)), comm=comm)


def _proj_down_t_swiglu_bwd(name, dff, w_g, gt, up, l, comm=None):
    t = dff.shape[0]

    def compute(in_refs, out_refs, scr):
        d_ref, w_ref, gt_ref, up_ref = in_refs
        dgt_ref, dup_ref = out_refs
        dact = _nt(d_ref[...], w_ref[...])
        g = gt_ref[...]
        s = _sigmoid(g)
        dgt_ref[...] = (dact * up_ref[...] * _dsilu(g, s)).astype(dgt_ref.dtype)
        dup_ref[...] = (dact * (g * s)).astype(dup_ref.dtype)

    o_spec = _bs((None, TM, FFN_SHARD), lambda j, m: (j, m, 0))
    outs, bufs = _hosted_call(name, compute, (N_CHIPS, t // TM), [dff, w_g, gt, up],
                        [_bs((TM, D_MODEL), lambda j, m: (m, 0)),
                         _bs((None, None, FFN_SHARD, D_MODEL), lambda j, m: (j, l, 0, 0)), o_spec, o_spec],
                        [_sds((N_CHIPS, t, FFN_SHARD), MXU_DTYPE)] * 2, [o_spec, o_spec], [], {}, comm)
    return outs if comm is None else (outs, bufs)


def _proj_in_t(dz, w_in_g, l, comm=None):
    t = dz.shape[0]
    return _mm(f"proj_in_t_l{l}", [(dz, w_in_g)], NT, _sds((t, D_MODEL), F32), (t // TM_WIDE, 8),
               [_bs((TM_WIDE, 1024), lambda m, n: (m, n)),
                _bs((None, None, D_MODEL, 1024), lambda m, n: (n // 2, l, 0, n % 2))],
               _bs((TM_WIDE, D_MODEL), lambda m, n: (m, 0)), reduce_axis=1, comm=comm)


TT = 1024


def _wgrad(name, a, a_spec, g, g_spec, shard_shape, o_map, n_outer, l, into, tt=TT):
    t = a.shape[-2]
    out = _sds((N_CHIPS, 2) + tuple(shard_shape), GRAD_EXCHANGE_DTYPE)
    return _mm(name, [(a, g)], TN, out, (n_outer, t // tt), [a_spec, g_spec],
               _bs((None, None) + tuple(o_map[0]), o_map[1]), reduce_axis=1, into=into)


def _rows(name, fn, n_tiles, ins, in_specs, out_shapes, out_specs, n_red=0, into=()):
    n_in = len(ins)
    n_out = len(out_shapes)

    def body(*refs):
        in_refs = refs[:n_in]
        out_refs = refs[len(refs) - n_out:]
        vals = fn(*[r[...] for r in in_refs])
        if not isinstance(vals, (tuple, list)):
            vals = (vals,)
        first = pl.program_id(0) == 0
        for j in range(n_out):
            o_ref = out_refs[j]
            val = vals[j]
            if j < n_out - n_red:
                o_ref[...] = val.astype(o_ref.dtype)
            else:
                @pl.when(first)
                def _(o_ref=o_ref, val=val):
                    o_ref[...] = val

                @pl.when(jnp.logical_not(first))
                def _(o_ref=o_ref, val=val):
                    o_ref[...] += val

    args = list(ins)
    specs = list(in_specs)
    aliases = {}
    for buf, out_idx in into:
        aliases[len(args)] = out_idx
        args.append(buf)
        specs.append(ANY)
    res = _pcall(body, name=name, grid=(n_tiles,), in_specs=specs, out_specs=list(out_specs),
                 out_shape=list(out_shapes), input_output_aliases=aliases, compiler_params=_params(1))(*args)
    return res


def _tile_spec(tile, width, blk=0):
    return pl.BlockSpec((tile, width), lambda i: (i, blk))


def _whole(shape):
    nd = len(shape)
    return pl.BlockSpec(tuple(shape), lambda i: (0,) * nd)


def _rms(x, g):
    r = lax.rsqrt(_mean_last(x * x) + EPS)
    return (x * r) * g


def _rms_bwd_math(u, g, dy):
    r = lax.rsqrt(_mean_last(u * u) + EPS)
    uh = u * r
    gdy = dy * g
    du = r * (gdy - uh * _mean_last(gdy * uh))
    return du, _rowsum(dy * uh)


def _rms_fwd(name, x, g):
    t, d = x.shape
    tile = 512
    return _rows(name, lambda xv, gv: (_rms(xv, gv),), t // tile, [x, g],
                 [_tile_spec(tile, d), _whole((1, d))], [_sds((t, d), MXU_DTYPE)], [_tile_spec(tile, d)])[0]


def _resid_rms_fwd(name, x, y, g):
    t, d = x.shape
    tile = 512
    return _rows(name, lambda xv, yv, gv: (xv + _rms(yv, gv),), t // tile, [x, y, g],
                 [_tile_spec(tile, d), _tile_spec(tile, d), _whole((1, d))], [_sds((t, d), F32)],
                 [_tile_spec(tile, d)])[0]


def _resid_rms_norm_fwd(name, x, y, g, g_next):
    t, d = x.shape
    tile = 512

    def fn(xv, yv, gv, gn):
        s = xv + _rms(yv, gv)
        return s, _rms(s, gn)

    return _rows(name, fn, t // tile, [x, y, g, g_next],
                 [_tile_spec(tile, d), _tile_spec(tile, d), _whole((1, d)), _whole((1, d))],
                 [_sds((t, d), F32), _sds((t, d), MXU_DTYPE)], [_tile_spec(tile, d)] * 2)


TF = 512


def _merge_proj_out(name, pa, pb, z, w_g, l, comm=None):
    t = z.shape[0]

    def compute(in_refs, out_refs, scr):
        pa_ref, pb_ref, ga_ref, gb_ref, w_ref = in_refs
        mg_ref, mix_ref = out_refs
        merged = _sigmoid(ga_ref[...]) * pa_ref[...] + _sigmoid(gb_ref[...]) * pb_ref[...]
        mg_ref[...] = merged.astype(mg_ref.dtype)
        mix_ref[...] = _nn(merged, _merge_lead(w_ref[...]))

    row = lambda blk: _bs((TF, D_MODEL), lambda m: (m, blk))
    outs, bufs = _hosted_call(name, compute, (t // TF,), [pa, pb, z, z, w_g],
                        [row(0), row(0), row(6), row(7), _bs((N_CHIPS, None, 256, D_MODEL), lambda m: (0, l, 0, 0))],
                        [_sds((t, D_MODEL), MXU_DTYPE), _sds((t, D_MODEL), F32)], [row(0), row(0)], [], {}, comm)
    return outs if comm is None else (outs, bufs)


def _proj_ple_gate_ple(name, x2, w_g, pe, l):
    t = x2.shape[0]

    def body(x_ref, w_ref, pe_ref, gz_ref, x3_ref):
        x = x_ref[...]
        gz = _nn(x, _merge_lead(w_ref[...]))
        gz_ref[...] = gz
        x3_ref[...] = x + pe_ref[...] * _sigmoid(gz)

    row = _bs((TF, D_MODEL), lambda m: (m, 0))
    return _pcall(body, name=name, grid=(t // TF,),
                  in_specs=[row, _bs((N_CHIPS, None, 256, D_MODEL), lambda m: (0, l, 0, 0)), row],
                  out_specs=[row, row], out_shape=[_sds((t, D_MODEL), F32)] * 2,
                  compiler_params=_params(1))(x2, w_g, pe)


def _rms_bwd(name, u, g, dy, resid, out_dtype):
    t, d = u.shape
    tile = 256

    def fn(uv, gv, dyv, *rest):
        du, dg = _rms_bwd_math(uv, gv, dyv)
        if rest:
            du = du + rest[0]
        return du, dg

    ins = [u, g, dy] + ([resid] if resid is not None else [])
    specs = [_tile_spec(tile, d), _whole((1, d)), _tile_spec(tile, d)] + ([_tile_spec(tile, d)] if resid is not None else [])
    return _rows(name, fn, t // tile, ins, specs, [_sds((t, d), out_dtype), _sds((1, d), F32)],
                 [_tile_spec(tile, d), _whole((1, d))], n_red=1)


def _rms_bwd_pair(name, u1, g1, dy1, resid, u2, g2, out2_dtype):
    t, d = u1.shape
    tile = 256

    def fn(u1v, g1v, dy1v, rv, u2v, g2v):
        d1, dg1 = _rms_bwd_math(u1v, g1v, dy1v)
        d1 = d1 + rv
        d2, dg2 = _rms_bwd_math(u2v, g2v, d1)
        return d1, d2, dg1, dg2

    row, vec = _tile_spec(tile, d), _whole((1, d))
    return _rows(name, fn, t // tile, [u1, g1, dy1, resid, u2, g2], [row, vec, row, row, row, vec],
                 [_sds((t, d), F32), _sds((t, d), out2_dtype), _sds((1, d), F32), _sds((1, d), F32)],
                 [row, row, vec, vec], n_red=2)


def _cumsum_rows(x, group, suffix):
    n = x.shape[0]
    pos = lax.broadcasted_iota(jnp.int32, x.shape, 0) % group
    d = 1
    while d < group:
        if suffix:
            x = x + jnp.where(pos < group - d, pltpu.roll(x, n - d, 0), 0.0)
        else:
            x = x + jnp.where(pos >= d, pltpu.roll(x, d, 0), 0.0)
        d *= 2
    return x


def _hg_gates(zq, zf, lb):
    q = zq * _sigmoid(zq)
    f = lb + (1.0 - lb) * _sigmoid(zf)
    logf = jnp.log(jnp.maximum(f, jnp.finfo(F32).tiny))
    k = (1.0 - lb) * _sigmoid(-zf)
    return q, k, logf, f


def _tri_mask(n, rev):
    t_i = lax.broadcasted_iota(jnp.int32, (n, n), 0)
    s_i = lax.broadcasted_iota(jnp.int32, (n, n), 1)
    return (s_i >= t_i) if rev else (s_i <= t_i)


def _anchors_are_safe(b_s, n_chunks, rev):
    worst = None
    for c in range(n_chunks):
        base = c * HG_CHUNK
        first = base + HG_CHUNK - 1 if rev else base
        last = base if rev else base + HG_CHUNK - 1
        mid = base + HG_CHUNK // 2
        b0, bm, bl = b_s[first:first + 1, :], b_s[mid:mid + 1, :], b_s[last:last + 1, :]
        span = jnp.maximum(b0 - bm, bm - bl)
        worst = span if worst is None else jnp.maximum(worst, span)
    return jnp.max(worst) < HG_SAFE_EXP


def _sub_ranges(base, i_sub, rev):
    r0 = base + i_sub * HG_SUB
    r1 = r0 + HG_SUB
    if rev:
        e0, e1, anchor = r1, base + HG_CHUNK, r1
    else:
        e0, e1, anchor = base, r0, r0 - 1
    return r0, r1, e0, e1, anchor


def _hgrn_fwd(name, z, lb_row, rev, post=None):
    t = z.shape[0]
    nb = t // HG_BLOCK
    ncb = HG_BLOCK // HG_CHUNK
    nsb = HG_CHUNK // HG_SUB
    zf0 = 16 if rev else 8
    n_in = 4 if post is None else 7

    def tmap(i):
        return nb - 1 - i if rev else i

    def body(*refs):
        zq_ref, zf_ref, zi_ref, lb_ref = refs[:4]
        o_ref, st_ref = refs[n_in:n_in + 2]
        state, q_s, k_s, v_s, b_s = refs[len(refs) - 5:]

        @pl.when(pl.program_id(1) == 0)
        def _():
            state[...] = jnp.zeros_like(state)

        for c in range(ncb):
            rows = slice(c * HG_CHUNK, (c + 1) * HG_CHUNK)
            q_c, k_c, logf, _ = _hg_gates(zq_ref[rows, :], zf_ref[rows, :], lb_ref[...])
            q_s[rows, :] = q_c
            k_s[rows, :] = k_c
            b_s[rows, :] = _cumsum_rows(logf, HG_CHUNK, rev)
        order = range(ncb - 1, -1, -1) if rev else range(ncb)
        safe = _anchors_are_safe(b_s, ncb, rev)

        @pl.when(safe)
        def _():
            cmask = _tri_mask(HG_CHUNK, rev)
            ch = []
            for c in range(ncb):
                base = c * HG_CHUNK
                rows = slice(base, base + HG_CHUNK)
                last = base if rev else base + HG_CHUNK - 1
                mid = base + HG_CHUNK // 2
                q_c, k_c, v_c, b = q_s[rows, :], k_s[rows, :], zi_ref[rows, :], b_s[rows, :]
                bl, bm = b_s[last:last + 1, :], b_s[mid:mid + 1, :]
                ch.append(dict(v=v_c, qe=q_c * jnp.exp(b), el=jnp.exp(bl), q_t=q_c * jnp.exp(b - bm),
                               k_t=k_c * jnp.exp(bm - b), kd=k_c * jnp.exp(bl - b)))
            for d in ch:
                d["a"] = _nt(d["q_t"], d["k_t"])
                d["inc"] = _tn(d["v"], d["kd"])
            for d in ch:
                d["o"] = _nn(jnp.where(cmask, d["a"], 0.0), d["v"])
            st = state[...]
            for c in order:
                d = ch[c]
                st_ref[c] = st
                d["o"] = d["o"] + _nt(d["qe"], st)
                st = st * d["el"] + d["inc"]
            state[...] = st
            o_ref[...] = jnp.concatenate([d["o"] for d in ch], axis=0)

        @pl.when(jnp.logical_not(safe))
        def _():
            v_s[...] = zi_ref[...]
            mask = _tri_mask(HG_SUB, rev)
            for c in order:
                base = c * HG_CHUNK
                rows = slice(base, base + HG_CHUNK)
                last = base if rev else base + HG_CHUNK - 1
                st = state[...]
                st_ref[c] = st
                b = b_s[rows, :]
                bl = b_s[last:last + 1, :]
                o_inter = _nt(q_s[rows, :] * jnp.exp(b), st)
                kd = k_s[rows, :] * jnp.exp(bl - b)
                state[...] = st * jnp.exp(bl) + _tn(v_s[rows, :], kd)
                parts = []
                for i_sub in range(nsb):
                    r0, r1, e0, e1, anchor = _sub_ranges(base, i_sub, rev)
                    q_i, k_i, b_i = q_s[r0:r1, :], k_s[r0:r1, :], b_s[r0:r1, :]
                    decay = jnp.exp(jnp.minimum(b_i[:, None, :] - b_i[None, :, :], 0.0))
                    a_d = jnp.where(mask, jnp.sum(q_i[:, None, :] * k_i[None, :, :] * decay, axis=-1), 0.0)
                    o_i = _nn(a_d, v_s[r0:r1, :])
                    if e1 > e0:
                        anc = b_s[anchor:anchor + 1, :]
                        q_t = q_i * jnp.exp(b_i - anc)
                        k_t = k_s[e0:e1, :] * jnp.exp(anc - b_s[e0:e1, :])
                        o_i = o_i + _nn(_nt(q_t, k_t), v_s[e0:e1, :])
                    parts.append(o_i)
                o_ref[rows, :] = o_inter + jnp.concatenate(parts, axis=0)

        if post is not None:
            other_ref, zg_ref, gain_ref = refs[4:7]
            o = o_ref[...] + other_ref[...]
            zg = zg_ref[...]
            o_ref[...] = o
            refs[n_in + 2][...] = (_rms(o, gain_ref[...]) * (zg * _sigmoid(zg))).astype(MXU_DTYPE)

    blk = lambda off: pl.BlockSpec((HG_BLOCK, HEAD), lambda h, i: (tmap(i), off + h))
    vec = pl.BlockSpec((1, HEAD), lambda h, i: (0, h))
    args, in_specs = [z, z, z, lb_row], [blk(0), blk(zf0), blk(24), vec]
    out_specs = [blk(0), pl.BlockSpec((None, ncb, HEAD, HEAD), lambda h, i: (h, tmap(i), 0, 0))]
    out_shape = [_sds((t, D_MODEL), F32), _sds((N_HEADS, t // HG_CHUNK, HEAD, HEAD), F32)]
    if post is not None:
        args += [post[0], z, post[1]]
        in_specs += [blk(0), blk(32), vec]
        out_specs.append(blk(0))
        out_shape.append(_sds((t, D_MODEL), MXU_DTYPE))
    return _pcall(
        body, name=name, grid=(N_HEADS, nb), in_specs=in_specs, out_specs=out_specs, out_shape=out_shape,
        scratch_shapes=[pltpu.VMEM((HEAD, HEAD), F32)] + [pltpu.VMEM((HG_BLOCK, HEAD), F32)] * 4,
        compiler_params=_params(2))(*args)


def _hgrn_bwd(name, z, d_o, states, lb_row, rev, dz, comm=None, prev=None, post=None):
    t = z.shape[0]
    nb = t // HG_BLOCK
    ncb = HG_BLOCK // HG_CHUNK
    nsb = HG_CHUNK // HG_SUB
    zf0 = 16 if rev else 8

    def tmap(i):
        return i if rev else nb - 1 - i

    def compute(in_refs, out_refs, scr):
        zq_ref, zf_ref, zi_ref, do_ref, st_ref, lb_ref = in_refs[:6]
        dq_ref, dv_ref, dzf_ref, dlb_ref = out_refs[:4]
        dstate, q_s, k_s, v_s, b_s, dq_s, dk_s, dv_s, db_s = scr
        first = pl.program_id(1) == 0

        @pl.when(first)
        def _():
            dstate[...] = jnp.zeros_like(dstate)

        if post is not None:
            o_ref, zg_ref, gain_ref = in_refs[7 + len(prev or ()):]
            d_o_ref, dzg_ref, dgain_ref = out_refs[4:]
            o_v, zg, gain, da = o_ref[...], zg_ref[...], gain_ref[...], do_ref[...]
            s = _sigmoid(zg)
            r = lax.rsqrt(_mean_last(o_v * o_v) + EPS)
            oh = o_v * r
            dy = da * (zg * s)
            dzg_ref[...] = (da * (oh * gain) * _dsilu(zg, s)).astype(dzg_ref.dtype)
            gdy = dy * gain
            d_o_ref[...] = r * (gdy - oh * _mean_last(gdy * oh))
            dgain = _rowsum(dy * oh)

            @pl.when(first)
            def _():
                dgain_ref[...] = dgain

            @pl.when(jnp.logical_not(first))
            def _():
                dgain_ref[...] += dgain

            do_ref = d_o_ref

        lb = lb_ref[...]
        for c in range(ncb):
            rows = slice(c * HG_CHUNK, (c + 1) * HG_CHUNK)
            q_c, k_c, logf, _ = _hg_gates(zq_ref[rows, :], zf_ref[rows, :], lb)
            q_s[rows, :] = q_c
            k_s[rows, :] = k_c
            b_s[rows, :] = _cumsum_rows(logf, HG_CHUNK, rev)
        order = range(ncb) if rev else range(ncb - 1, -1, -1)
        safe = _anchors_are_safe(b_s, ncb, rev)

        @pl.when(safe)
        def _():
            cmask = _tri_mask(HG_CHUNK, rev)
            row_i = lax.broadcasted_iota(jnp.int32, (HG_CHUNK, HEAD), 0)
            ch = []
            for c in range(ncb):
                base = c * HG_CHUNK
                rows = slice(base, base + HG_CHUNK)
                last = base if rev else base + HG_CHUNK - 1
                mid = base + HG_CHUNK // 2
                q_c, k_c, v_c, b, do_c = q_s[rows, :], k_s[rows, :], zi_ref[rows, :], b_s[rows, :], do_ref[rows, :]
                bl, bm = b_s[last:last + 1, :], b_s[mid:mid + 1, :]
                e, ebl, e_q, e_k = jnp.exp(b), jnp.exp(bl - b), jnp.exp(b - bm), jnp.exp(bm - b)
                ch.append(dict(q=q_c, k=k_c, v=v_c, do=do_c, e=e, el=jnp.exp(bl), ebl=ebl, e_q=e_q, e_k=e_k,
                               q_t=q_c * e_q, k_t=k_c * e_k, kd=k_c * ebl, last=last - base))
            for c, d in enumerate(ch):
                d["a"] = _nt(d["q_t"], d["k_t"])
                d["da"] = _nt(d["do"], d["v"])
                d["dq"] = _nn(d["do"], st_ref[c]) * d["e"]
                d["inc"] = _tn(d["do"], d["q"] * d["e"])
            for d in ch:
                da = jnp.where(cmask, d["da"], 0.0)
                d["dq"] = d["dq"] + d["e_q"] * _dot_f32(da, d["k_t"], 1, 0)
                d["dk_intra"] = d["e_k"] * _dot_f32(da, d["q_t"], 0, 0)
                d["dv"] = _tn(jnp.where(cmask, d["a"], 0.0), d["do"])
            dst = dstate[...]
            for c in order:
                d = ch[c]
                dk_inter = _nn(d["v"], dst) * d["ebl"]
                d["dk"] = dk_inter + d["dk_intra"]
                d["dv"] = _nt(d["kd"], dst) + d["dv"]
                d["extra"] = d["el"] * _rowsum(dst * st_ref[c]) + _rowsum(d["k"] * dk_inter)
                dst = d["inc"] + dst * d["el"]
            dstate[...] = dst
            for d in ch:
                d["db"] = d["q"] * d["dq"] - d["k"] * d["dk"] + jnp.where(row_i == d["last"], d["extra"], 0.0)
            dq_s[...] = jnp.concatenate([d["dq"] for d in ch], axis=0)
            dk_s[...] = jnp.concatenate([d["dk"] for d in ch], axis=0)
            dv_s[...] = jnp.concatenate([d["dv"] for d in ch], axis=0)
            db_s[...] = jnp.concatenate([d["db"] for d in ch], axis=0)

        @pl.when(jnp.logical_not(safe))
        def _():
            v_s[...] = zi_ref[...]
            mask = _tri_mask(HG_SUB, rev)
            for c in order:
                base = c * HG_CHUNK
                rows = slice(base, base + HG_CHUNK)
                last = base if rev else base + HG_CHUNK - 1
                st0 = st_ref[c]
                dst1 = dstate[...]
                b = b_s[rows, :]
                bl = b_s[last:last + 1, :]
                e = jnp.exp(b)
                el = jnp.exp(bl)
                ebl = jnp.exp(bl - b)
                q_c, k_c, v_c, do_c = q_s[rows, :], k_s[rows, :], v_s[rows, :], do_ref[rows, :]
                kd = k_c * ebl
                dq_s[rows, :] = _nn(do_c, st0) * e
                dk_inter = _nn(v_c, dst1) * ebl
                dk_s[rows, :] = dk_inter
                dv_s[rows, :] = _nt(kd, dst1)
                extra = el * _rowsum(dst1 * st0) + _rowsum(k_c * dk_inter)
                dstate[...] = _tn(do_c, q_c * e) + dst1 * el
                for i_sub in range(nsb):
                    r0, r1, e0, e1, anchor = _sub_ranges(base, i_sub, rev)
                    q_i, k_i, b_i, v_i, do_i = q_s[r0:r1, :], k_s[r0:r1, :], b_s[r0:r1, :], v_s[r0:r1, :], do_ref[r0:r1, :]
                    decay = jnp.exp(jnp.minimum(b_i[:, None, :] - b_i[None, :, :], 0.0))
                    a_d = jnp.where(mask, jnp.sum(q_i[:, None, :] * k_i[None, :, :] * decay, axis=-1), 0.0)
                    da_d = jnp.where(mask, _nt(do_i, v_i), 0.0)
                    wgt = da_d[:, :, None] * decay
                    dq_s[r0:r1, :] += jnp.sum(wgt * k_i[None, :, :], axis=1)
                    dk_s[r0:r1, :] += jnp.sum(wgt * q_i[:, None, :], axis=0)
                    dv_s[r0:r1, :] += _tn(a_d, do_i)
                    if e1 > e0:
                        anc = b_s[anchor:anchor + 1, :]
                        e_q = jnp.exp(b_i - anc)
                        e_k = jnp.exp(anc - b_s[e0:e1, :])
                        q_t = q_i * e_q
                        k_t = k_s[e0:e1, :] * e_k
                        a_o = _nt(q_t, k_t)
                        da_o = _nt(do_i, v_s[e0:e1, :])
                        dq_s[r0:r1, :] += e_q * _nn(da_o, k_t)
                        dk_s[e0:e1, :] += e_k * _tn(da_o, q_t)
                        dv_s[e0:e1, :] += _tn(a_o, do_i)
                db_s[rows, :] = q_c * dq_s[rows, :] - k_c * dk_s[rows, :]
                db_s[last:last + 1, :] += extra

        dlb = None
        for c in range(ncb):
            rows = slice(c * HG_CHUNK, (c + 1) * HG_CHUNK)
            zf = zf_ref[rows, :]
            s_pos, s_neg = _sigmoid(zf), _sigmoid(-zf)
            f = lb + (1.0 - lb) * s_pos
            dlogf = _cumsum_rows(db_s[rows, :], HG_CHUNK, not rev)
            df = jnp.where(f > jnp.finfo(F32).tiny, dlogf / f, 0.0)
            dfk = df - dk_s[rows, :]
            dzf_ref[rows, :] = ((1.0 - lb) * s_pos * s_neg * dfk).astype(dzf_ref.dtype)
            part = _rowsum(s_neg * dfk)
            dlb = part if dlb is None else dlb + part

        @pl.when(first)
        def _():
            dlb_ref[...] = dlb

        @pl.when(jnp.logical_not(first))
        def _():
            dlb_ref[...] += dlb

        if prev is None:
            dq_ref[...] = dq_s[...]
            dv_ref[...] = dv_s[...]
        else:
            zq = zq_ref[...]
            dq_ref[...] = ((dq_s[...] + in_refs[7][...]) * _dsilu(zq, _sigmoid(zq))).astype(dq_ref.dtype)
            dv_ref[...] = (dv_s[...] + in_refs[8][...]).astype(dv_ref.dtype)

    blk = lambda off: pl.BlockSpec((HG_BLOCK, HEAD), lambda h, i: (tmap(i), off + h))
    vec = pl.BlockSpec((1, HEAD), lambda h, i: (0, h))
    qv_dtype = F32 if prev is None else dz.dtype
    args = [z, z, z, d_o, states, lb_row, dz] + list(prev or ())
    in_specs = [blk(0), blk(zf0), blk(24), blk(0),
                pl.BlockSpec((None, ncb, HEAD, HEAD), lambda h, i: (h, tmap(i), 0, 0)), vec, ANY] + [blk(0)] * len(prev or ())
    out_shapes = [_sds((t, D_MODEL), qv_dtype), _sds((t, D_MODEL), qv_dtype), _sds(dz.shape, dz.dtype), _sds((1, D_MODEL), F32)]
    out_specs = [blk(0), blk(0), blk(zf0), vec]
    if post is not None:
        args += [post[0], z, post[1]]
        in_specs += [blk(0), blk(32), vec]
        out_shapes += [_sds((t, D_MODEL), F32), _sds((t, D_MODEL), dz.dtype), _sds((1, D_MODEL), F32)]
        out_specs += [blk(0), blk(0), vec]
    outs, bufs = _hosted_call(
        name, compute, (N_HEADS, nb), args, in_specs, out_shapes, out_specs,
        [pltpu.VMEM((HEAD, HEAD), F32)] + [pltpu.VMEM((HG_BLOCK, HEAD), F32)] * 8, {6: 2}, comm)
    return outs if comm is None else (outs, bufs)


def _lower_bounds(name, gamma):
    def body(g_ref, o_ref):
        g0, g1 = g_ref[0:1, :], g_ref[1:2, :]
        m = jnp.maximum(g0, g1)
        e0, e1 = jnp.exp(g0 - m), jnp.exp(g1 - m)
        s0, s1 = e0 / (e0 + e1), e1 / (e0 + e1)
        o_ref[0:1, :] = s0 - s0
        o_ref[1:2, :] = (s0 + s1) - s0

    return _pcall(body, name=name, out_shape=_sds(gamma.shape, F32))(gamma)


def _lower_bounds_bwd(name, gamma, dlb):
    def body(g_ref, d_ref, o_ref):
        g0, g1 = g_ref[0:1, :], g_ref[1:2, :]
        m = jnp.maximum(g0, g1)
        e0, e1 = jnp.exp(g0 - m), jnp.exp(g1 - m)
        s0, s1 = e0 / (e0 + e1), e1 / (e0 + e1)
        d0, d1 = d_ref[0:1, :], d_ref[1:2, :]
        ds0 = (d0 + d1) - (d0 + d1)
        ds1 = d1
        inner = s0 * ds0 + s1 * ds1
        o_ref[0:1, :] = s0 * (ds0 - inner)
        o_ref[1:2, :] = s1 * (ds1 - inner)

    return _pcall(body, name=name, out_shape=_sds(gamma.shape, F32))(gamma, dlb)


def _place_in_dz(name, piece, blk, dz):
    t = piece.shape[0]
    tile = 1024
    return _rows(name, lambda a: (a,), t // tile, [piece], [_tile_spec(tile, D_MODEL)],
                 [_sds(dz.shape, dz.dtype)], [_tile_spec(tile, D_MODEL, blk)], into=[(dz, 0)])[0]


def _sg_norm(zv, ln_g, ln_b):
    gv = _gelu(zv)
    xc = gv - _mean_last(gv)
    rstd = lax.rsqrt(_mean_last(xc * xc) + EPS)
    xhat = xc * rstd
    return xhat * ln_g + ln_b, xhat, rstd


def _lane_lo():
    return lax.broadcasted_iota(jnp.int32, (SG_CHUNK, LANES), 1) < (LANES // 2)


SG_TILE = 512


def _sgu_fwd(name, z, w, bias_t, ln_g, ln_b):
    t = z.shape[0]

    def fn(zu, zv, wv, bt, lg, lb):
        u = _gelu(zu)
        vn, _, _ = _sg_norm(zv, lg, lb)
        lo = _lane_lo()
        out_rows = []
        for c in range(SG_TILE // SG_CHUNK):
            rs = slice(c * SG_CHUNK, (c + 1) * SG_CHUNK)
            cols = []
            for j in range(SG_WIDTH // LANES):
                cs = slice(j * LANES, (j + 1) * LANES)
                vb = vn[rs, cs]
                sg = jnp.where(lo, _nn(wv[2 * j], vb), _nn(wv[2 * j + 1], vb)) + bt[:, cs]
                cols.append(u[rs, cs] * sg)
            out_rows.append(jnp.concatenate(cols, axis=1))
        return (jnp.concatenate(out_rows, axis=0),)

    return _rows(name, fn, t // SG_TILE, [z, z, w, bias_t, ln_g, ln_b],
                 [_tile_spec(SG_TILE, SG_WIDTH, 10), _tile_spec(SG_TILE, SG_WIDTH, 11), _whole(w.shape),
                  _whole(bias_t.shape), _whole((1, SG_WIDTH)), _whole((1, SG_WIDTH))],
                 [_sds((t, SG_WIDTH), MXU_DTYPE)], [_tile_spec(SG_TILE, SG_WIDTH)])[0]


def _sgu_bwd(name, dbo, z, w, bias_t, ln_g, ln_b, dz):
    t = z.shape[0]
    n_grp = w.shape[0]

    def fn(dbov, zu, zv, wv, bt, lg, lb):
        u = _gelu(zu)
        vn, xhat, rstd = _sg_norm(zv, lg, lb)
        lo = _lane_lo()
        dw = [None] * n_grp
        dsg_sum = None
        du_rows, dvn_rows = [], []
        for c in range(SG_TILE // SG_CHUNK):
            rs = slice(c * SG_CHUNK, (c + 1) * SG_CHUNK)
            du_cols, dvn_cols, dsg_cols = [], [], []
            for j in range(SG_WIDTH // LANES):
                cs = slice(j * LANES, (j + 1) * LANES)
                vb = vn[rs, cs]
                sg = jnp.where(lo, _nn(wv[2 * j], vb), _nn(wv[2 * j + 1], vb)) + bt[:, cs]
                du_cols.append(dbov[rs, cs] * sg)
                dsg = dbov[rs, cs] * u[rs, cs]
                dsg_cols.append(dsg)
                d0 = _nt(jnp.where(lo, dsg, 0.0), vb)
                d1 = _nt(jnp.where(lo, 0.0, dsg), vb)
                dw[2 * j] = d0 if dw[2 * j] is None else dw[2 * j] + d0
                dw[2 * j + 1] = d1 if dw[2 * j + 1] is None else dw[2 * j + 1] + d1
                dvn_cols.append(jnp.where(lo, _tn(wv[2 * j], dsg), _tn(wv[2 * j + 1], dsg)))
            du_rows.append(jnp.concatenate(du_cols, axis=1))
            dvn_rows.append(jnp.concatenate(dvn_cols, axis=1))
            dsg_c = jnp.concatenate(dsg_cols, axis=1)
            dsg_sum = dsg_c if dsg_sum is None else dsg_sum + dsg_c
        du = jnp.concatenate(du_rows, axis=0)
        dvn = jnp.concatenate(dvn_rows, axis=0)
        dxhat = dvn * lg
        dgv = rstd * (dxhat - _mean_last(dxhat) - xhat * _mean_last(dxhat * xhat))
        dzuv = jnp.concatenate([du * _dgelu(zu), dgv * _dgelu(zv)], axis=1)
        return dzuv, jnp.stack(dw, axis=0), dsg_sum, _rowsum(dvn * xhat), _rowsum(dvn)

    return _rows(name, fn, t // SG_TILE, [dbo, z, z, w, bias_t, ln_g, ln_b],
                 [_tile_spec(SG_TILE, SG_WIDTH), _tile_spec(SG_TILE, SG_WIDTH, 10), _tile_spec(SG_TILE, SG_WIDTH, 11),
                  _whole(w.shape), _whole(bias_t.shape), _whole((1, SG_WIDTH)), _whole((1, SG_WIDTH))],
                 [_sds(dz.shape, dz.dtype), _sds(w.shape, F32), _sds((SG_CHUNK, SG_WIDTH), F32),
                  _sds((1, SG_WIDTH), F32), _sds((1, SG_WIDTH), F32)],
                 [_tile_spec(SG_TILE, 2 * SG_WIDTH, 5), _whole(w.shape), _whole((SG_CHUNK, SG_WIDTH)),
                  _whole((1, SG_WIDTH)), _whole((1, SG_WIDTH))],
                 n_red=4, into=[(dz, 0)])


def _proj_out_t_merge_bwd(name, dmix, w_g, pa, pb, z, l):
    t = z.shape[0]

    def body(d_ref, w_ref, pa_ref, pb_ref, ga_ref, gb_ref, dpa_ref, dpb_ref, dz_ref):
        d = _nt(d_ref[...], _merge_lead(w_ref[...]))
        sa, sb = _sigmoid(ga_ref[...]), _sigmoid(gb_ref[...])
        dpa_ref[...] = (d * sa).astype(dpa_ref.dtype)
        dpb_ref[...] = (d * sb).astype(dpb_ref.dtype)
        dz_ref[...] = jnp.concatenate([d * pa_ref[...] * sa * (1.0 - sa), d * pb_ref[...] * sb * (1.0 - sb)],
                                      axis=1).astype(dz_ref.dtype)

    row = lambda blk: _bs((TF, D_MODEL), lambda m: (m, blk))
    return _pcall(body, name=name, grid=(t // TF,),
                  in_specs=[row(0), _bs((N_CHIPS, None, 256, D_MODEL), lambda m: (0, l, 0, 0)), row(0), row(0), row(6), row(7)],
                  out_specs=[row(0), row(0), _bs((TF, 2 * D_MODEL), lambda m: (m, 3))],
                  out_shape=[_sds((t, D_MODEL), MXU_DTYPE), _sds((t, D_MODEL), MXU_DTYPE), _sds((t, N_IN), MXU_DTYPE)],
                  compiler_params=_params(1))(dmix, w_g, pa, pb, z, z)


def _ple_bwd(name, dx, pe, gz):
    t, d = dx.shape
    tile = 512

    def fn(dv, p, g):
        s = _sigmoid(g)
        return dv * s, dv * p * s * (1.0 - s)

    return _rows(name, fn, t // tile, [dx, pe, gz], [_tile_spec(tile, d)] * 3, [_sds((t, d), MXU_DTYPE)] * 2,
                 [_tile_spec(tile, d)] * 2)


def _loss_bwd(name, y, target):
    t, d = y.shape
    tile = 512

    def fn(yv, tv):
        err = yv - tv
        return err * (1.0 / d), _rowsum(err * err)

    return _rows(name, fn, t // tile, [y, target], [_tile_spec(tile, d)] * 2, [_sds((t, d), F32), _sds((1, d), F32)],
                 [_tile_spec(tile, d), _whole((1, d))], n_red=1)


def _position():
    return lax.axis_index("x"), lax.axis_index("y"), lax.axis_index("c")


def _gather_comm(shards, bufs, l0, nl):
    n = len(shards)
    if bufs is None:
        bufs = [_sds((N_CHIPS,) + s.shape, s.dtype) for s in shards]

    def make(w_refs, out_refs, sems):
        send_sems, recv_sems, local_sems = sems
        x, y, c = _position()
        me = 2 * x + y
        sibling = (x, y, 1 - c)
        chips = [(1 - x, y), (x, 1 - y), (1 - x, 1 - y)]

        def half(ref, cc):
            rows = ref.shape[1] // 2
            return ref.at[pl.ds(l0, nl), pl.ds(cc * rows, rows)]

        def copy(i, k, src, chip, cc, to):
            return pltpu.make_async_remote_copy(
                src_ref=src, dst_ref=half(out_refs[i].at[chip], cc), send_sem=send_sems.at[6 * i + k],
                recv_sem=recv_sems.at[6 * i + k], device_id=to, device_id_type=MESH)

        def local(i):
            return pltpu.make_async_copy(w_refs[i].at[pl.ds(l0, nl)], out_refs[i].at[me, pl.ds(l0, nl)], local_sems.at[i])

        def sends():
            return [copy(i, j, half(w_refs[i], c), me, c, (px, py, c))
                    for i in range(n) for j, (px, py) in enumerate(chips)]

        def start():
            for i in range(n):
                local(i).start()
            for cp in sends():
                cp.start()

        def finish():
            passed = []
            for i in range(n):
                for j, (px, py) in enumerate(chips):
                    chip = 2 * px + py
                    copy(i, j, half(w_refs[i], c), chip, c, (px, py, c)).wait_recv()
                    fwd = copy(i, 3 + j, half(out_refs[i].at[chip], c), chip, c, sibling)
                    fwd.start()
                    passed.append(fwd)
            for i in range(n):
                for j, (px, py) in enumerate(chips):
                    copy(i, 3 + j, half(w_refs[i], c), 2 * px + py, 1 - c, sibling).wait_recv()
            for cp in sends() + passed:
                cp.wait_send()
            for i in range(n):
                local(i).wait()

        return start, finish

    sems = [pltpu.SemaphoreType.DMA((6 * n,)), pltpu.SemaphoreType.DMA((6 * n,)), pltpu.SemaphoreType.DMA((n,))]
    return _Comm(shards, bufs, sems, make)


def _exchange_comm(grads, bufs, l0, nl):
    n = len(grads)
    if bufs is None:
        bufs = [_sds((N_DEV,) + g.shape[1:], g.dtype) for g in grads]

    def make(g_refs, out_refs, sems):
        send_sems, recv_sems, local_sems = sems
        x, y, c = _position()
        me = 2 * x + y
        sibling = (x, y, 1 - c)
        chips = [(1 - x, y), (x, 1 - y), (1 - x, 1 - y)]

        def lay(ref):
            return ref.at[pl.ds(l0, nl)]

        def copy(i, k, src, slot, to):
            return pltpu.make_async_remote_copy(
                src_ref=src, dst_ref=lay(out_refs[i].at[slot]), send_sem=send_sems.at[7 * i + k],
                recv_sem=recv_sems.at[7 * i + k], device_id=to, device_id_type=MESH)

        def local(i):
            return pltpu.make_async_copy(lay(g_refs[i].at[me]), lay(out_refs[i].at[2 * me + c]), local_sems.at[i])

        def first():
            cps = []
            for i in range(n):
                cps.append(copy(i, 0, lay(g_refs[i].at[me]), 2 * me + c, sibling))
                for j, (px, py) in enumerate(chips):
                    cps.append(copy(i, 1 + j, lay(g_refs[i].at[2 * px + py]), 2 * me + c, (px, py, c)))
            return cps

        def start():
            for i in range(n):
                local(i).start()
            for cp in first():
                cp.start()

        def finish():
            passed = []
            for i in range(n):
                for j, (px, py) in enumerate(chips):
                    slot = 2 * (2 * px + py) + c
                    copy(i, 1 + j, lay(g_refs[i].at[me]), slot, (px, py, c)).wait_recv()
                    fwd = copy(i, 4 + j, lay(out_refs[i].at[slot]), slot, sibling)
                    fwd.start()
                    passed.append(fwd)
            for i in range(n):
                copy(i, 0, lay(g_refs[i].at[me]), 2 * me + (1 - c), sibling).wait_recv()
                for j, (px, py) in enumerate(chips):
                    copy(i, 4 + j, lay(g_refs[i].at[me]), 2 * (2 * px + py) + (1 - c), sibling).wait_recv()
            for cp in first() + passed:
                cp.wait_send()
            for i in range(n):
                local(i).wait()

        return start, finish

    sems = [pltpu.SemaphoreType.DMA((7 * n,)), pltpu.SemaphoreType.DMA((7 * n,)), pltpu.SemaphoreType.DMA((n,))]
    return _Comm(grads, bufs, sems, make)


def _all_reduce_small(packed):
    rows = packed.shape[0]

    def body(x_ref, sum_ref, slots, send_sems, recv_sems, local_sem):
        x, y, c = _position()
        me = 4 * x + 2 * y + c
        mine = pltpu.make_async_copy(x_ref, slots.at[me], local_sem)
        mine.start()
        sends = []
        for k in range(1, N_DEV):
            peer = (x ^ (k >> 2), y ^ ((k >> 1) & 1), c ^ (k & 1))
            cp = pltpu.make_async_remote_copy(src_ref=x_ref, dst_ref=slots.at[me], send_sem=send_sems.at[k - 1],
                                              recv_sem=recv_sems.at[k - 1], device_id=peer, device_id_type=MESH)
            cp.start()
            sends.append(cp)
        for k in range(1, N_DEV):
            px, py, pc = x ^ (k >> 2), y ^ ((k >> 1) & 1), c ^ (k & 1)
            pltpu.make_async_remote_copy(src_ref=x_ref, dst_ref=slots.at[4 * px + 2 * py + pc], send_sem=send_sems.at[k - 1],
                                         recv_sem=recv_sems.at[k - 1], device_id=(px, py, pc), device_id_type=MESH).wait_recv()
        for cp in sends:
            cp.wait_send()
        mine.wait()
        total = slots[0]
        for d in range(1, N_DEV):
            total = total + slots[d]
        sum_ref[...] = total

    vmem = pl.BlockSpec(memory_space=pltpu.VMEM)
    return _pcall(
        body, name="all_reduce_small", in_specs=[vmem], out_specs=vmem, out_shape=_sds(packed.shape, F32),
        scratch_shapes=[pltpu.VMEM((N_DEV, rows, LANES), F32), pltpu.SemaphoreType.DMA((N_DEV - 1,)),
                        pltpu.SemaphoreType.DMA((N_DEV - 1,)), pltpu.SemaphoreType.DMA],
        compiler_params=pltpu.CompilerParams(vmem_limit_bytes=VMEM_LIMIT_BYTES),
    )(packed)


def _adamw(w, g, m, v):
    m = ADAM_B1 * m + (1.0 - ADAM_B1) * g
    v = ADAM_B2 * v + (1.0 - ADAM_B2) * (g * g)
    m_hat = m / (1.0 - ADAM_B1 ** ADAM_STEP)
    v_hat = v / (1.0 - ADAM_B2 ** ADAM_STEP)
    delta = -ADAM_LR * (m_hat / (jnp.sqrt(v_hat) + ADAM_EPS) + ADAM_WD * w)
    return delta, m, v


def _adam_sharded(name, parts, w, m, v):
    shape = w.shape
    cols = shape[-1]
    rows = w.size // cols
    tile = 8
    while tile * 2 * cols <= ADAM_TILE_ELEMS and rows % (tile * 2) == 0:
        tile *= 2

    def fn(p, wv, mv, vv):
        g = p[0].astype(F32)
        for d in range(1, N_DEV):
            g = g + p[d].astype(F32)
        return (g,) + _adamw(wv, g, mv, vv)

    two_d = lambda a: a.reshape(rows, cols)
    outs = _rows(name, fn, rows // tile, [parts.reshape(N_DEV, rows, cols), two_d(w), two_d(m), two_d(v)],
                 [pl.BlockSpec((N_DEV, tile, cols), lambda i: (0, i, 0))] + [_tile_spec(tile, cols)] * 3,
                 [_sds((rows, cols), F32)] * 4, [_tile_spec(tile, cols)] * 4)
    return [o.reshape(shape) for o in outs]


def _adam_small(name, g, w, m, v):
    rows = g.shape[0]
    tile = rows // 2
    return _rows(name, lambda gv, wv, mv, vv: _adamw(wv, gv, mv, vv), rows // tile, [g, w, m, v],
                 [_tile_spec(tile, LANES)] * 4, [_sds(g.shape, F32)] * 3, [_tile_spec(tile, LANES)] * 3)


BIG = ("w_in", "w_a", "w_b", "w_out", "w_gate", "w_up", "w_down", "w_ple", "w_ple_gate")
SMALL = ("norm_mix_pre", "lb_gamma_fwd", "lb_gamma_bwd", "hg_norm", "sg_w", "sg_b", "sg_ln_g", "sg_ln_b",
         "norm_mix_post", "norm_ffn_pre", "norm_ffn_post")


def _with_comm(plan, tag, state, call):
    if tag not in plan:
        return call(None)
    keys, comm = plan[tag](state)
    res, bufs = call(comm)
    state.update(zip(keys, bufs))
    return res


def _layer_fwd(l, x, p_l, wg, sm, plan):
    sv = {"x": x}
    h = _rms_fwd(f"norm_mix_pre_l{l}", x, sm["norm_mix_pre"])
    z = _with_comm(plan, "proj_in", wg, lambda comm: _proj_in(h, wg["w_in"], l, comm=comm))
    o_f, st_f = _hgrn_fwd(f"hgrn_fwd_l{l}", z, sm["lb_fwd"], False)
    o_sum, st_b, a_out = _hgrn_fwd(f"hgrn_rev_l{l}", z, sm["lb_bwd"], True, post=(o_f, sm["hg_norm"]))
    b_out = _sgu_fwd(f"sgu_l{l}", z, sm["sg_w"], sm["sg_bias_t"], sm["sg_ln_g"], sm["sg_ln_b"])
    pa = _proj_rows_sharded(f"proj_a_l{l}", a_out, wg["w_a"], l, F32)
    pb = _proj_cols256(f"proj_b_l{l}", b_out, wg["w_b"], l)
    merged, mix = _with_comm(plan, "merge_proj_out", wg, lambda comm: _merge_proj_out(
        f"merge_proj_out_l{l}", pa, pb, z, wg["w_out"], l, comm=comm))
    x1, h2 = _resid_rms_norm_fwd(f"norm_mix_post_ffn_pre_l{l}", x, mix, sm["norm_mix_post"], sm["norm_ffn_pre"])
    gt, up, act = _with_comm(plan, "proj_gate_up", wg, lambda comm: _proj_gate_up(
        f"proj_gate_up_l{l}", h2, wg["w_gate"], wg["w_up"], l, comm=comm))
    ff = _with_comm(plan, "proj_down", wg, lambda comm: _proj_ffn_out(f"proj_down_l{l}", act, wg["w_down"], l, comm=comm))
    x2 = _resid_rms_fwd(f"norm_ffn_post_l{l}", x1, ff, sm["norm_ffn_post"])
    pe = _proj_cols256(f"proj_ple_l{l}", p_l, wg["w_ple"], l)
    gz, x3 = _proj_ple_gate_ple(f"proj_ple_gate_ple_l{l}", x2, wg["w_ple_gate"], pe, l)
    sv.update(h=h, z=z, o=o_sum, st_f=st_f, st_b=st_b, a_out=a_out, b_out=b_out, pa=pa, pb=pb,
              merged=merged, mix=mix, x1=x1, h2=h2, gt=gt, up=up, act=act, ff=ff, x2=x2, pe=pe, gz=gz, p=p_l)
    return x3, sv


def _layer_bwd(l, dx3, sv, wg, sm, gw, parts, plan):
    t = dx3.shape[0]
    nt = t // TT
    sg = {}

    def wgrad(key, *a, **k):
        gw[key] = _wgrad(f"grad_{key}_l{l}", *a, l=l, into=gw.get(key), **k)

    row = lambda width: _bs((TT, width), lambda j, i: (i, 0))
    row_j = lambda width: _bs((TT, width), lambda j, i: (i, j))
    ffn_j = _bs((None, TT, FFN_SHARD), lambda j, i: (j, i, 0))
    blk_j = lambda shape: (tuple(shape), lambda j, i: (j, l, 0, 0))

    dpe, dgz = _ple_bwd(f"ple_bwd_l{l}", dx3, sv["pe"], sv["gz"])
    wgrad("w_ple", sv["p"], row(PLE_DIM), dpe, row_j(256), (PLE_DIM, 256), blk_j((PLE_DIM, 256)), N_CHIPS)
    wgrad("w_ple_gate", sv["x2"], row_j(256), dgz, row(D_MODEL), (256, D_MODEL), blk_j((256, D_MODEL)), N_CHIPS)
    dx2 = _proj_rows_sharded_t(f"proj_ple_gate_t_l{l}", dgz, wg["w_ple_gate"], l, F32, add=dx3)

    dff, sg["norm_ffn_post"] = _rms_bwd(f"norm_ffn_post_bwd_l{l}", sv["ff"], sm["norm_ffn_post"], dx2, None, MXU_DTYPE)
    dgt, dup = _with_comm(plan, "proj_down_t", parts, lambda comm: _proj_down_t_swiglu_bwd(
        f"proj_down_t_swiglu_bwd_l{l}", dff, wg["w_down"], sv["gt"], sv["up"], l, comm=comm))
    wgrad("w_down", sv["act"], ffn_j, dff, row(D_MODEL), (FFN_SHARD, D_MODEL), blk_j((FFN_SHARD, D_MODEL)), N_CHIPS)
    dh2 = _with_comm(plan, "proj_gate_up_t", parts, lambda comm: _proj_ffn_in_t(
        f"proj_gate_up_t_l{l}", [(dgt, wg["w_gate"]), (dup, wg["w_up"])], l, comm=comm))
    wgrad("w_gate", sv["h2"], row(D_MODEL), dgt, ffn_j, (D_MODEL, FFN_SHARD), blk_j((D_MODEL, FFN_SHARD)), N_CHIPS)
    wgrad("w_up", sv["h2"], row(D_MODEL), dup, ffn_j, (D_MODEL, FFN_SHARD), blk_j((D_MODEL, FFN_SHARD)), N_CHIPS)
    dx1, dmix, sg["norm_ffn_pre"], sg["norm_mix_post"] = _rms_bwd_pair(
        f"norm_ffn_pre_mix_post_bwd_l{l}", sv["x1"], sm["norm_ffn_pre"], dh2, dx2, sv["mix"], sm["norm_mix_post"], MXU_DTYPE)

    dpa, dpb, dz = _proj_out_t_merge_bwd(f"proj_out_t_merge_bwd_l{l}", dmix, wg["w_out"], sv["pa"], sv["pb"], sv["z"], l)
    wgrad("w_out", sv["merged"], row_j(256), dmix, row(D_MODEL), (256, D_MODEL), blk_j((256, D_MODEL)), N_CHIPS)
    da = _proj_rows_sharded_t(f"proj_a_t_l{l}", dpa, wg["w_a"], l, F32)
    wgrad("w_a", sv["a_out"], row_j(256), dpa, row(D_MODEL), (256, D_MODEL), blk_j((256, D_MODEL)), N_CHIPS)
    dbo = _proj_cols256_t(f"proj_b_t_l{l}", dpb, wg["w_b"], l)
    wgrad("w_b", sv["b_out"], row(SG_WIDTH), dpb, row_j(256), (SG_WIDTH, 256), blk_j((SG_WIDTH, 256)), N_CHIPS)

    dz, sg["sg_w"], dsg_sum, sg["sg_ln_g"], sg["sg_ln_b"] = _sgu_bwd(
        f"sgu_bwd_l{l}", dbo, sv["z"], sm["sg_w"], sm["sg_bias_t"], sm["sg_ln_g"], sm["sg_ln_b"], dz)
    sg["sg_b"] = dsg_sum.reshape(SG_CHUNK, N_HEADS, SG_WIDTH // N_HEADS).sum(axis=-1).T
    dq_f, dv_f, dz, sg["lb_fwd"], d_o, dzg, sg["hg_norm"] = _with_comm(plan, "hgrn_fwd_bwd", parts, lambda comm: _hgrn_bwd(
        f"hgrn_fwd_bwd_l{l}", sv["z"], da, sv["st_f"], sm["lb_fwd"], False, dz, comm=comm, post=(sv["o"], sm["hg_norm"])))
    dz = _place_in_dz(f"place_dzg_l{l}", dzg, 4, dz)
    dzq, dzi, dz, sg["lb_bwd"] = _hgrn_bwd(f"hgrn_rev_bwd_l{l}", sv["z"], d_o, sv["st_b"], sm["lb_bwd"], True, dz,
                                           prev=(dq_f, dv_f))
    dz = _place_in_dz(f"place_dzq_l{l}", dzq, 0, dz)
    dz = _place_in_dz(f"place_dzi_l{l}", dzi, 3, dz)

    gw["w_in"] = _wgrad(f"grad_w_in_l{l}", sv["h"], _bs((TM_WIDE, D_MODEL), lambda n, i: (i, 0)), dz,
                        _bs((TM_WIDE, 1024), lambda n, i: (i, n)), (D_MODEL, 2048),
                        ((D_MODEL, 1024), lambda n, i: (n // 2, l, 0, n % 2)), 8, l, gw.get("w_in"), tt=TM_WIDE)
    dh = _with_comm(plan, "proj_in_t", parts, lambda comm: _proj_in_t(dz, wg["w_in"], l, comm=comm))
    dx, sg["norm_mix_pre"] = _rms_bwd(f"norm_mix_pre_bwd_l{l}", sv["x"], sm["norm_mix_pre"], dh, dx1, F32)
    del nt
    return dx, gw, sg


def _pack(parts):
    return jnp.concatenate([a.reshape(-1, LANES) for a in parts], axis=0)


def _step(x, p, loss_target, w, m, v):
    x = x[0]
    target = loss_target[0]
    depth = w["w_in"].shape[0]

    assert depth == 2, "the exchanges below ride in layer 0's kernels and carry layer 1's data"
    shards = {k: w[k].astype(MXU_DTYPE) for k in BIG}
    rest_a = [k for k in BIG if k not in ("w_in", "w_gate", "w_up")]
    rest_b = ["w_gate", "w_up"]
    rest = rest_a + rest_b

    def gather(keys, l0, extend):
        return lambda wg: (keys, _gather_comm([shards[k] for k in keys], [wg[k] for k in keys] if extend else None, l0, 1))

    wg = {"w_in": _run_comm("gather_w_in_l0", gather(["w_in"], 0, False)(None)[1])[0]}
    fwd_plans = [{"proj_in": gather(rest, 0, False), "merge_proj_out": gather(["w_in"], 1, True),
                  "proj_gate_up": gather(rest_a, 1, True), "proj_down": gather(rest_b, 1, True)}, {}]
    lb_f = _lower_bounds("lower_bounds_fwd", w["lb_gamma_fwd"])
    lb_b = _lower_bounds("lower_bounds_bwd", w["lb_gamma_bwd"])

    def small_of(l):
        sm = {k: w[k][l:l + 1] for k in ("norm_mix_pre", "hg_norm", "sg_ln_g", "sg_ln_b", "norm_mix_post",
                                        "norm_ffn_pre", "norm_ffn_post")}
        sm["lb_fwd"], sm["lb_bwd"] = lb_f[l:l + 1], lb_b[l:l + 1]
        sm["sg_w"] = w["sg_w"][l]
        sm["sg_bias_t"] = jnp.repeat(w["sg_b"][l].T, SG_WIDTH // N_HEADS, axis=1)
        return sm

    saved = []
    h = x
    for l in range(depth):
        h, sv = _layer_fwd(l, h, p[l, 0], wg, small_of(l), fwd_plans[l])
        saved.append(sv)

    dy, sq_err = _loss_bwd("loss", h, target)
    gw, parts = {}, {}

    def exchange(keys, l0, extend):
        return lambda parts: (keys, _exchange_comm([gw[k] for k in keys], [parts[k] for k in keys] if extend else None, l0, 1))

    bwd_plans = [{"proj_down_t": exchange(["w_in"], 1, False), "proj_gate_up_t": exchange(rest, 1, False),
                  "hgrn_fwd_bwd": exchange(rest, 0, True), "proj_in_t": exchange(["w_in"], 0, True)}, {}]
    small_grads = [None] * depth
    for l in reversed(range(depth)):
        dy, gw, small_grads[l] = _layer_bwd(l, dy, saved[l], wg, small_of(l), gw, parts, bwd_plans[l])

    def stack(key):
        return jnp.concatenate([small_grads[l][key].reshape((1,) + w_shape[1:]) for l in range(depth)], axis=0)

    g_small = {}
    for key in SMALL:
        w_shape = w[key].shape
        if key == "lb_gamma_fwd":
            dlb = jnp.concatenate([small_grads[l]["lb_fwd"] for l in range(depth)], axis=0)
            g_small[key] = _lower_bounds_bwd("lower_bounds_fwd_bwd", w[key], dlb)
        elif key == "lb_gamma_bwd":
            dlb = jnp.concatenate([small_grads[l]["lb_bwd"] for l in range(depth)], axis=0)
            g_small[key] = _lower_bounds_bwd("lower_bounds_bwd_bwd", w[key], dlb)
        else:
            g_small[key] = stack(key)

    packed = _pack([g_small[k] for k in SMALL] + [sq_err])
    summed = _all_reduce_small(packed)
    n_small_rows = sum(w[k].size for k in SMALL) // LANES
    loss = 0.5 * jnp.sum(summed[n_small_rows:]) / D_MODEL

    g_rows = summed[:n_small_rows]
    d_rows, m_rows, v_rows = _adam_small("adamw_small", g_rows, _pack([w[k] for k in SMALL]),
                                         _pack([m[k] for k in SMALL]), _pack([v[k] for k in SMALL]))
    out = {}
    off = 0
    for key in SMALL:
        n_rows = w[key].size // LANES
        sl = slice(off, off + n_rows)
        out[key] = tuple(a[sl].reshape(w[key].shape) for a in (g_rows, d_rows, m_rows, v_rows))
        off += n_rows

    for key in BIG:
        out[key] = tuple(_adam_sharded(f"adamw_{key}", parts[key], w[key], m[key], v[key]))
    return loss, dy[None], out


WEIGHTS = ("norm_mix_pre", "w_in", "lb_gamma_fwd", "lb_gamma_bwd", "hg_norm", "sg_w", "sg_b", "sg_ln_g", "sg_ln_b",
           "w_a", "w_b", "w_out", "norm_mix_post", "norm_ffn_pre", "w_gate", "w_up", "w_down", "norm_ffn_post",
           "w_ple", "w_ple_gate")


def kernel(x, p, norm_mix_pre, w_in, lb_gamma_fwd, lb_gamma_bwd, hg_norm, sg_w, sg_b, sg_ln_g, sg_ln_b, w_a, w_b, w_out, norm_mix_post, norm_ffn_pre, w_gate, w_up, w_down, norm_ffn_post, w_ple, w_ple_gate, loss_target, m_norm_mix_pre, m_w_in, m_lb_gamma_fwd, m_lb_gamma_bwd, m_hg_norm, m_sg_w, m_sg_b, m_sg_ln_g, m_sg_ln_b, m_w_a, m_w_b, m_w_out, m_norm_mix_post, m_norm_ffn_pre, m_w_gate, m_w_up, m_w_down, m_norm_ffn_post, m_w_ple, m_w_ple_gate, v_norm_mix_pre, v_w_in, v_lb_gamma_fwd, v_lb_gamma_bwd, v_hg_norm, v_sg_w, v_sg_b, v_sg_ln_g, v_sg_ln_b, v_w_a, v_w_b, v_w_out, v_norm_mix_post, v_norm_ffn_pre, v_w_gate, v_w_up, v_w_down, v_norm_ffn_post, v_w_ple, v_w_ple_gate):
    w = dict(zip(WEIGHTS, (norm_mix_pre, w_in, lb_gamma_fwd, lb_gamma_bwd, hg_norm, sg_w, sg_b, sg_ln_g, sg_ln_b, w_a, w_b, w_out, norm_mix_post, norm_ffn_pre, w_gate, w_up, w_down, norm_ffn_post, w_ple, w_ple_gate)))
    m = dict(zip(WEIGHTS, (m_norm_mix_pre, m_w_in, m_lb_gamma_fwd, m_lb_gamma_bwd, m_hg_norm, m_sg_w, m_sg_b, m_sg_ln_g, m_sg_ln_b, m_w_a, m_w_b, m_w_out, m_norm_mix_post, m_norm_ffn_pre, m_w_gate, m_w_up, m_w_down, m_norm_ffn_post, m_w_ple, m_w_ple_gate)))
    v = dict(zip(WEIGHTS, (v_norm_mix_pre, v_w_in, v_lb_gamma_fwd, v_lb_gamma_bwd, v_hg_norm, v_sg_w, v_sg_b, v_sg_ln_g, v_sg_ln_b, v_w_a, v_w_b, v_w_out, v_norm_mix_post, v_norm_ffn_pre, v_w_gate, v_w_up, v_w_down, v_norm_ffn_post, v_w_ple, v_w_ple_gate)))
    loss, grad_x, out = _step(x, p, loss_target, w, m, v)
    res = [loss, grad_x]
    for i in range(4):
        res += [out[k][i] for k in WEIGHTS]
    return tuple(res)
```

```python
import functools

import jax
import jax.numpy as jnp
from jax import lax
from jax.experimental import pallas as pl
from jax.experimental.pallas import tpu as pltpu

F32 = jnp.float32
MXU_DTYPE = jnp.bfloat16
GRAD_EXCHANGE_DTYPE = jnp.bfloat16

D_MODEL = 1024
N_HEADS = 8
HEAD = 128
HG_CHUNK = 64
HG_SUB = 16
HG_BLOCK = 512
HG_SAFE_EXP = 80.0
SG_CHUNK = 128
SG_WIDTH = 512
FFN_SHARD = 704
PLE_DIM = 256
N_IN = 8192
N_CHIPS = 4
N_DEV = 8
EPS = 1e-6
LANES = 128
VMEM_LIMIT_BYTES = 56 * 2 ** 20

ADAM_LR = 0.001
ADAM_B1 = 0.9
ADAM_B2 = 0.999
ADAM_EPS = 1e-08
ADAM_WD = 0.01
ADAM_STEP = 10
ADAM_TILE_ELEMS = 128 * 1024

MESH = pl.DeviceIdType.MESH
ANY = pl.BlockSpec(memory_space=pl.ANY)


def _pcall(body, **kw):
    return pl.pallas_call(body, **kw)


def _params(n_axes):
    return pltpu.CompilerParams(dimension_semantics=("arbitrary",) * n_axes, vmem_limit_bytes=VMEM_LIMIT_BYTES)


def _dot(a, b, ca, cb):
    return lax.dot_general(a.astype(MXU_DTYPE), b.astype(MXU_DTYPE), (((ca,), (cb,)), ((), ())),
                           preferred_element_type=F32)


def _dot_f32(a, b, ca, cb):
    return lax.dot_general(a, b, (((ca,), (cb,)), ((), ())), precision=lax.Precision.HIGH,
                           preferred_element_type=F32)


def _nn(a, b):
    return _dot(a, b, 1, 0)


def _nt(a, b):
    return _dot(a, b, 1, 1)


def _tn(a, b):
    return _dot(a, b, 0, 0)


NN, NT, TN = (1, 0), (1, 1), (0, 0)


def _sigmoid(x):
    return jax.nn.sigmoid(x)


def _dsilu(x, s):
    return s * (1.0 + x * (1.0 - s))


_SQRT_HALF = 0.7071067811865476
_INV_SQRT_2PI = 0.3989422804014327


def _gelu(x):
    return 0.5 * x * (1.0 + lax.erf(x * _SQRT_HALF))


def _dgelu(x):
    return 0.5 * (1.0 + lax.erf(x * _SQRT_HALF)) + x * jnp.exp(-0.5 * x * x) * _INV_SQRT_2PI


def _mean_last(x):
    return jnp.mean(x, axis=-1, keepdims=True)


def _rowsum(x):
    return jnp.sum(x, axis=0, keepdims=True)


class _Comm:
    def __init__(self, ins, bufs, sem_shapes, make):
        self.ins, self.bufs, self.sem_shapes, self.make = list(ins), list(bufs), list(sem_shapes), make
        self.extends = not isinstance(self.bufs[0], jax.ShapeDtypeStruct)


def _hosted_call(name, compute, grid, args, in_specs, out_shapes, out_specs, scratch, aliases, comm):
    n_in, n_out, n_scr = len(args), len(out_shapes), len(scratch)
    if comm is None:
        def plain(*refs):
            compute(refs[:n_in], refs[n_in:n_in + n_out], refs[n_in + n_out:])

        res = _pcall(plain, name=name, grid=grid, in_specs=list(in_specs), out_specs=list(out_specs),
                     out_shape=list(out_shapes), scratch_shapes=list(scratch), input_output_aliases=dict(aliases),
                     compiler_params=_params(len(grid)))(*args)
        return list(res), []

    n_cin, n_buf = len(comm.ins), len(comm.bufs)
    all_args = list(args) + comm.ins + (comm.bufs if comm.extends else [])
    n_all = len(all_args)
    all_aliases = dict(aliases)
    if comm.extends:
        for j in range(n_buf):
            all_aliases[n_in + n_cin + j] = n_out + j
    buf_shapes = [_sds(b.shape, b.dtype) for b in comm.bufs]

    def body(*refs):
        outs = refs[n_all:n_all + n_out + n_buf]
        scr = refs[n_all + n_out + n_buf:]
        start, finish = comm.make(refs[n_in:n_in + n_cin], outs[n_out:], scr[n_scr:])
        first, last = None, None
        for axis, size in enumerate(grid):
            i = pl.program_id(axis)
            first = (i == 0) if first is None else jnp.logical_and(first, i == 0)
            last = (i == size - 1) if last is None else jnp.logical_and(last, i == size - 1)
        pl.when(first)(start)
        compute(refs[:n_in], outs[:n_out], scr[:n_scr])
        pl.when(last)(finish)

    res = _pcall(body, name=name, grid=grid, in_specs=list(in_specs) + [ANY] * (n_all - n_in),
                 out_specs=list(out_specs) + [ANY] * n_buf, out_shape=list(out_shapes) + buf_shapes,
                 scratch_shapes=list(scratch) + comm.sem_shapes, input_output_aliases=all_aliases,
                 compiler_params=_params(len(grid)))(*all_args)
    return list(res[:n_out]), list(res[n_out:])


def _run_comm(name, comm):
    n_cin, n_buf = len(comm.ins), len(comm.bufs)
    all_args = comm.ins + (comm.bufs if comm.extends else [])
    n_all = len(all_args)

    def body(*refs):
        start, finish = comm.make(refs[:n_cin], refs[n_all:n_all + n_buf], refs[n_all + n_buf:])
        start()
        finish()

    res = _pcall(body, name=name, in_specs=[ANY] * n_all, out_specs=[ANY] * n_buf,
                 out_shape=[_sds(b.shape, b.dtype) for b in comm.bufs], scratch_shapes=comm.sem_shapes,
                 input_output_aliases={n_cin + j: j for j in range(n_buf)} if comm.extends else {})(*all_args)
    return list(res)


def _mm(name, pairs, kind, out_shape, grid, in_specs, out_spec, *, reduce_axis=None, add=None,
        add_spec=None, into=None, prep=None, comm=None):
    n_pairs = len(pairs)
    has_add = add is not None
    staged = reduce_axis is not None and out_shape.dtype != F32

    def compute(in_refs, out_refs, scr):
        o_ref = out_refs[0]
        acc = None
        for i in range(n_pairs):
            a = in_refs[2 * i][...]
            b = in_refs[2 * i + 1][...]
            if prep is not None:
                b = prep(b)
            prod = _dot(a, b, *kind)
            acc = prod if acc is None else acc + prod
        if has_add:
            acc = acc + in_refs[2 * n_pairs][...]
        if reduce_axis is None:
            o_ref[...] = acc.astype(o_ref.dtype)
        else:
            r = pl.program_id(reduce_axis)
            acc_ref = scr[0] if staged else o_ref

            @pl.when(r == 0)
            def _():
                acc_ref[...] = acc

            @pl.when(r > 0)
            def _():
                acc_ref[...] += acc

            if staged:
                @pl.when(r == grid[reduce_axis] - 1)
                def _():
                    o_ref[...] = acc_ref[...].astype(o_ref.dtype)

    scratch = []
    if staged:
        scratch = [pltpu.VMEM(tuple(d for d in out_spec.block_shape if d is not None), F32)]
    args = [t for pair in pairs for t in pair]
    specs = list(in_specs)
    if has_add:
        args.append(add)
        specs.append(add_spec)
    aliases = {}
    if into is not None:
        aliases = {len(args): 0}
        args.append(into)
        specs.append(ANY)
    outs, bufs = _hosted_call(name, compute, grid, args, specs, [out_shape], [out_spec], scratch, aliases, comm)
    return outs[0] if comm is None else (outs[0], bufs)


def _sds(shape, dtype):
    return jax.ShapeDtypeStruct(tuple(shape), dtype)


def _bs(shape, fn):
    return pl.BlockSpec(tuple(shape), fn)


TM = 1024
TM_WIDE = 2048


def _merge_lead(b):
    return b.reshape(b.shape[0] * b.shape[1], b.shape[2])


def _proj_in(h, w_in_g, l, comm=None):
    t = h.shape[0]
    return _mm(f"proj_in_l{l}", [(h, w_in_g)], NN, _sds((t, N_IN), F32), (8, t // TM_WIDE),
               [_bs((TM_WIDE, D_MODEL), lambda n, m: (m, 0)),
                _bs((None, None, D_MODEL, 1024), lambda n, m: (n // 2, l, 0, n % 2))],
               _bs((TM_WIDE, 1024), lambda n, m: (m, n)), comm=comm)


def _proj_rows_sharded(name, a, w_g, l, out_dtype, add=None):
    t = a.shape[0]
    return _mm(name, [(a, w_g)], NN, _sds((t, D_MODEL), out_dtype), (t // TM,),
               [_bs((TM, D_MODEL), lambda m: (m, 0)),
                _bs((N_CHIPS, None, 256, D_MODEL), lambda m: (0, l, 0, 0))],
               _bs((TM, D_MODEL), lambda m: (m, 0)), prep=_merge_lead, add=add,
               add_spec=_bs((TM, D_MODEL), lambda m: (m, 0)))


def _proj_rows_sharded_t(name, g, w_g, l, out_dtype, add=None):
    t = g.shape[0]
    return _mm(name, [(g, w_g)], NT, _sds((t, D_MODEL), out_dtype), (t // TM,),
               [_bs((TM, D_MODEL), lambda m: (m, 0)),
                _bs((N_CHIPS, None, 256, D_MODEL), lambda m: (0, l, 0, 0))],
               _bs((TM, D_MODEL), lambda m: (m, 0)), prep=_merge_lead, add=add,
               add_spec=_bs((TM, D_MODEL), lambda m: (m, 0)))


def _proj_cols256(name, a, w_g, l):
    t, k = a.shape
    return _mm(name, [(a, w_g)], NN, _sds((t, D_MODEL), F32), (N_CHIPS, t // TM),
               [_bs((TM, k), lambda j, m: (m, 0)),
                _bs((None, None, k, 256), lambda j, m: (j, l, 0, 0))],
               _bs((TM, 256), lambda j, m: (m, j)))


def _proj_cols256_t(name, g, w_g, l):
    t = g.shape[0]
    k = w_g.shape[2]
    specs = []
    for j in range(N_CHIPS):
        specs += [_bs((TM, 256), lambda m, j=j: (m, j)), _bs((None, None, k, 256), lambda m, j=j: (j, l, 0, 0))]
    return _mm(name, [(g, w_g)] * N_CHIPS, NT, _sds((t, k), F32), (t // TM,), specs, _bs((TM, k), lambda m: (m, 0)))


def _proj_gate_up(name, h2, wg_g, wu_g, l, comm=None):
    t = h2.shape[0]

    def compute(in_refs, out_refs, scr):
        h_ref, wg_ref, wu_ref = in_refs
        gt_ref, up_ref, act_ref = out_refs
        h = h_ref[...]
        g = _nn(h, wg_ref[...])
        u = _nn(h, wu_ref[...])
        gt_ref[...] = g
        up_ref[...] = u
        act_ref[...] = ((g * _sigmoid(g)) * u).astype(act_ref.dtype)

    w_spec = _bs((None, None, D_MODEL, FFN_SHARD), lambda j, m: (j, l, 0, 0))
    o_spec = _bs((None, TM, FFN_SHARD), lambda j, m: (j, m, 0))
    outs, bufs = _hosted_call(name, compute, (N_CHIPS, t // TM), [h2, wg_g, wu_g],
                        [_bs((TM, D_MODEL), lambda j, m: (m, 0)), w_spec, w_spec],
                        [_sds((N_CHIPS, t, FFN_SHARD), F32), _sds((N_CHIPS, t, FFN_SHARD), F32),
                         _sds((N_CHIPS, t, FFN_SHARD), MXU_DTYPE)], [o_spec, o_spec, o_spec], [], {}, comm)
    return outs if comm is None else (outs, bufs)


def _proj_ffn_in_t(name, pairs, l, comm=None):
    t = pairs[0][0].shape[1]
    specs, all_pairs = [], []
    for pair in pairs:
        for j in range(N_CHIPS):
            all_pairs.append(pair)
            specs += [_bs((None, TM, FFN_SHARD), lambda m, j=j: (j, m, 0)),
                      pl.BlockSpec((None, None, D_MODEL, FFN_SHARD), lambda m, j=j: (j, l, 0, 0),
                                   pipeline_mode=pl.Buffered(1))]
    return _mm(name, all_pairs, NT, _sds((t, D_MODEL), F32), (t // TM,), specs,
               _bs((TM, D_MODEL), lambda m: (m, 0)), comm=comm)


def _proj_ffn_out(name, act, w_g, l, comm=None):
    t = act.shape[1]
    specs = []
    for j in range(N_CHIPS):
        specs += [_bs((None, TM, FFN_SHARD), lambda m, j=j: (j, m, 0)),
                  _bs((None, None, FFN_SHARD, D_MODEL), lambda m, j=j: (j, l, 0, 0))]
    return _mm(name, [(act, w_g)] * N_CHIPS, NN, _sds((t, D_MODEL), F32), (t // TM,), specs,
               _bs((TM, D_MODEL), lambda m: (m, 0)), comm=comm)


def _proj_down_t_swiglu_bwd(name, dff, w_g, gt, up, l, comm=None):
    t = dff.shape[0]

    def compute(in_refs, out_refs, scr):
        d_ref, w_ref, gt_ref, up_ref = in_refs
        dgt_ref, dup_ref = out_refs
        dact = _nt(d_ref[...], w_ref[...])
        g = gt_ref[...]
        s = _sigmoid(g)
        dgt_ref[...] = (dact * up_ref[...] * _dsilu(g, s)).astype(dgt_ref.dtype)
        dup_ref[...] = (dact * (g * s)).astype(dup_ref.dtype)

    o_spec = _bs((None, TM, FFN_SHARD), lambda j, m: (j, m, 0))
    outs, bufs = _hosted_call(name, compute, (N_CHIPS, t // TM), [dff, w_g, gt, up],
                        [_bs((TM, D_MODEL), lambda j, m: (m, 0)),
                         _bs((None, None, FFN_SHARD, D_MODEL), lambda j, m: (j, l, 0, 0)), o_spec, o_spec],
                        [_sds((N_CHIPS, t, FFN_SHARD), MXU_DTYPE)] * 2, [o_spec, o_spec], [], {}, comm)
    return outs if comm is None else (outs, bufs)


def _proj_in_t(dz, w_in_g, l, comm=None):
    t = dz.shape[0]
    return _mm(f"proj_in_t_l{l}", [(dz, w_in_g)], NT, _sds((t, D_MODEL), F32), (t // TM_WIDE, 8),
               [_bs((TM_WIDE, 1024), lambda m, n: (m, n)),
                _bs((None, None, D_MODEL, 1024), lambda m, n: (n // 2, l, 0, n % 2))],
               _bs((TM_WIDE, D_MODEL), lambda m, n: (m, 0)), reduce_axis=1, comm=comm)


TT = 1024


def _wgrad(name, a, a_spec, g, g_spec, shard_shape, o_map, n_outer, l, into, tt=TT):
    t = a.shape[-2]
    out = _sds((N_CHIPS, 2) + tuple(shard_shape), GRAD_EXCHANGE_DTYPE)
    return _mm(name, [(a, g)], TN, out, (n_outer, t // tt), [a_spec, g_spec],
               _bs((None, None) + tuple(o_map[0]), o_map[1]), reduce_axis=1, into=into)


def _rows(name, fn, n_tiles, ins, in_specs, out_shapes, out_specs, n_red=0, into=()):
    n_in = len(ins)
    n_out = len(out_shapes)

    def body(*refs):
        in_refs = refs[:n_in]
        out_refs = refs[len(refs) - n_out:]
        vals = fn(*[r[...] for r in in_refs])
        if not isinstance(vals, (tuple, list)):
            vals = (vals,)
        first = pl.program_id(0) == 0
        for j in range(n_out):
            o_ref = out_refs[j]
            val = vals[j]
            if j < n_out - n_red:
                o_ref[...] = val.astype(o_ref.dtype)
            else:
                @pl.when(first)
                def _(o_ref=o_ref, val=val):
                    o_ref[...] = val

                @pl.when(jnp.logical_not(first))
                def _(o_ref=o_ref, val=val):
                    o_ref[...] += val

    args = list(ins)
    specs = list(in_specs)
    aliases = {}
    for buf, out_idx in into:
        aliases[len(args)] = out_idx
        args.append(buf)
        specs.append(ANY)
    res = _pcall(body, name=name, grid=(n_tiles,), in_specs=specs, out_specs=list(out_specs),
                 out_shape=list(out_shapes), input_output_aliases=aliases, compiler_params=_params(1))(*args)
    return res


def _tile_spec(tile, width, blk=0):
    return pl.BlockSpec((tile, width), lambda i: (i, blk))


def _whole(shape):
    nd = len(shape)
    return pl.BlockSpec(tuple(shape), lambda i: (0,) * nd)


def _rms(x, g):
    r = lax.rsqrt(_mean_last(x * x) + EPS)
    return (x * r) * g


def _rms_bwd_math(u, g, dy):
    r = lax.rsqrt(_mean_last(u * u) + EPS)
    uh = u * r
    gdy = dy * g
    du = r * (gdy - uh * _mean_last(gdy * uh))
    return du, _rowsum(dy * uh)


def _rms_fwd(name, x, g):
    t, d = x.shape
    tile = 512
    return _rows(name, lambda xv, gv: (_rms(xv, gv),), t // tile, [x, g],
                 [_tile_spec(tile, d), _whole((1, d))], [_sds((t, d), MXU_DTYPE)], [_tile_spec(tile, d)])[0]


def _resid_rms_fwd(name, x, y, g):
    t, d = x.shape
    tile = 512
    return _rows(name, lambda xv, yv, gv: (xv + _rms(yv, gv),), t // tile, [x, y, g],
                 [_tile_spec(tile, d), _tile_spec(tile, d), _whole((1, d))], [_sds((t, d), F32)],
                 [_tile_spec(tile, d)])[0]


def _resid_rms_norm_fwd(name, x, y, g, g_next):
    t, d = x.shape
    tile = 512

    def fn(xv, yv, gv, gn):
        s = xv + _rms(yv, gv)
        return s, _rms(s, gn)

    return _rows(name, fn, t // tile, [x, y, g, g_next],
                 [_tile_spec(tile, d), _tile_spec(tile, d), _whole((1, d)), _whole((1, d))],
                 [_sds((t, d), F32), _sds((t, d), MXU_DTYPE)], [_tile_spec(tile, d)] * 2)


TF = 512


def _merge_proj_out(name, pa, pb, z, w_g, l, comm=None):
    t = z.shape[0]

    def compute(in_refs, out_refs, scr):
        pa_ref, pb_ref, ga_ref, gb_ref, w_ref = in_refs
        mg_ref, mix_ref = out_refs
        merged = _sigmoid(ga_ref[...]) * pa_ref[...] + _sigmoid(gb_ref[...]) * pb_ref[...]
        mg_ref[...] = merged.astype(mg_ref.dtype)
        mix_ref[...] = _nn(merged, _merge_lead(w_ref[...]))

    row = lambda blk: _bs((TF, D_MODEL), lambda m: (m, blk))
    outs, bufs = _hosted_call(name, compute, (t // TF,), [pa, pb, z, z, w_g],
                        [row(0), row(0), row(6), row(7), _bs((N_CHIPS, None, 256, D_MODEL), lambda m: (0, l, 0, 0))],
                        [_sds((t, D_MODEL), MXU_DTYPE), _sds((t, D_MODEL), F32)], [row(0), row(0)], [], {}, comm)
    return outs if comm is None else (outs, bufs)


def _proj_ple_gate_ple(name, x2, w_g, pe, l):
    t = x2.shape[0]

    def body(x_ref, w_ref, pe_ref, gz_ref, x3_ref):
        x = x_ref[...]
        gz = _nn(x, _merge_lead(w_ref[...]))
        gz_ref[...] = gz
        x3_ref[...] = x + pe_ref[...] * _sigmoid(gz)

    row = _bs((TF, D_MODEL), lambda m: (m, 0))
    return _pcall(body, name=name, grid=(t // TF,),
                  in_specs=[row, _bs((N_CHIPS, None, 256, D_MODEL), lambda m: (0, l, 0, 0)), row],
                  out_specs=[row, row], out_shape=[_sds((t, D_MODEL), F32)] * 2,
                  compiler_params=_params(1))(x2, w_g, pe)


def _rms_bwd(name, u, g, dy, resid, out_dtype):
    t, d = u.shape
    tile = 256

    def fn(uv, gv, dyv, *rest):
        du, dg = _rms_bwd_math(uv, gv, dyv)
        if rest:
            du = du + rest[0]
        return du, dg

    ins = [u, g, dy] + ([resid] if resid is not None else [])
    specs = [_tile_spec(tile, d), _whole((1, d)), _tile_spec(tile, d)] + ([_tile_spec(tile, d)] if resid is not None else [])
    return _rows(name, fn, t // tile, ins, specs, [_sds((t, d), out_dtype), _sds((1, d), F32)],
                 [_tile_spec(tile, d), _whole((1, d))], n_red=1)


def _rms_bwd_pair(name, u1, g1, dy1, resid, u2, g2, out2_dtype):
    t, d = u1.shape
    tile = 256

    def fn(u1v, g1v, dy1v, rv, u2v, g2v):
        d1, dg1 = _rms_bwd_math(u1v, g1v, dy1v)
        d1 = d1 + rv
        d2, dg2 = _rms_bwd_math(u2v, g2v, d1)
        return d1, d2, dg1, dg2

    row, vec = _tile_spec(tile, d), _whole((1, d))
    return _rows(name, fn, t // tile, [u1, g1, dy1, resid, u2, g2], [row, vec, row, row, row, vec],
                 [_sds((t, d), F32), _sds((t, d), out2_dtype), _sds((1, d), F32), _sds((1, d), F32)],
                 [row, row, vec, vec], n_red=2)


def _cumsum_rows(x, group, suffix):
    n = x.shape[0]
    pos = lax.broadcasted_iota(jnp.int32, x.shape, 0) % group
    d = 1
    while d < group:
        if suffix:
            x = x + jnp.where(pos < group - d, pltpu.roll(x, n - d, 0), 0.0)
        else:
            x = x + jnp.where(pos >= d, pltpu.roll(x, d, 0), 0.0)
        d *= 2
    return x


def _hg_gates(zq, zf, lb):
    q = zq * _sigmoid(zq)
    f = lb + (1.0 - lb) * _sigmoid(zf)
    logf = jnp.log(jnp.maximum(f, jnp.finfo(F32).tiny))
    k = (1.0 - lb) * _sigmoid(-zf)
    return q, k, logf, f


def _tri_mask(n, rev):
    t_i = lax.broadcasted_iota(jnp.int32, (n, n), 0)
    s_i = lax.broadcasted_iota(jnp.int32, (n, n), 1)
    return (s_i >= t_i) if rev else (s_i <= t_i)


def _anchors_are_safe(b_s, n_chunks, rev):
    worst = None
    for c in range(n_chunks):
        base = c * HG_CHUNK
        first = base + HG_CHUNK - 1 if rev else base
        last = base if rev else base + HG_CHUNK - 1
        mid = base + HG_CHUNK // 2
        b0, bm, bl = b_s[first:first + 1, :], b_s[mid:mid + 1, :], b_s[last:last + 1, :]
        span = jnp.maximum(b0 - bm, bm - bl)
        worst = span if worst is None else jnp.maximum(worst, span)
    return jnp.max(worst) < HG_SAFE_EXP


def _sub_ranges(base, i_sub, rev):
    r0 = base + i_sub * HG_SUB
    r1 = r0 + HG_SUB
    if rev:
        e0, e1, anchor = r1, base + HG_CHUNK, r1
    else:
        e0, e1, anchor = base, r0, r0 - 1
    return r0, r1, e0, e1, anchor


def _hgrn_fwd(name, z, lb_row, rev, post=None):
    t = z.shape[0]
    nb = t // HG_BLOCK
    ncb = HG_BLOCK // HG_CHUNK
    nsb = HG_CHUNK // HG_SUB
    zf0 = 16 if rev else 8
    n_in = 4 if post is None else 7

    def tmap(i):
        return nb - 1 - i if rev else i

    def body(*refs):
        zq_ref, zf_ref, zi_ref, lb_ref = refs[:4]
        o_ref, st_ref = refs[n_in:n_in + 2]
        state, q_s, k_s, v_s, b_s = refs[len(refs) - 5:]

        @pl.when(pl.program_id(1) == 0)
        def _():
            state[...] = jnp.zeros_like(state)

        for c in range(ncb):
            rows = slice(c * HG_CHUNK, (c + 1) * HG_CHUNK)
            q_c, k_c, logf, _ = _hg_gates(zq_ref[rows, :], zf_ref[rows, :], lb_ref[...])
            q_s[rows, :] = q_c
            k_s[rows, :] = k_c
            b_s[rows, :] = _cumsum_rows(logf, HG_CHUNK, rev)
        order = range(ncb - 1, -1, -1) if rev else range(ncb)
        safe = _anchors_are_safe(b_s, ncb, rev)

        @pl.when(safe)
        def _():
            cmask = _tri_mask(HG_CHUNK, rev)
            ch = []
            for c in range(ncb):
                base = c * HG_CHUNK
                rows = slice(base, base + HG_CHUNK)
                last = base if rev else base + HG_CHUNK - 1
                mid = base + HG_CHUNK // 2
                q_c, k_c, v_c, b = q_s[rows, :], k_s[rows, :], zi_ref[rows, :], b_s[rows, :]
                bl, bm = b_s[last:last + 1, :], b_s[mid:mid + 1, :]
                ch.append(dict(v=v_c, qe=q_c * jnp.exp(b), el=jnp.exp(bl), q_t=q_c * jnp.exp(b - bm),
                               k_t=k_c * jnp.exp(bm - b), kd=k_c * jnp.exp(bl - b)))
            for d in ch:
                d["a"] = _nt(d["q_t"], d["k_t"])
                d["inc"] = _tn(d["v"], d["kd"])
            for d in ch:
                d["o"] = _nn(jnp.where(cmask, d["a"], 0.0), d["v"])
            st = state[...]
            for c in order:
                d = ch[c]
                st_ref[c] = st
                d["o"] = d["o"] + _nt(d["qe"], st)
                st = st * d["el"] + d["inc"]
            state[...] = st
            o_ref[...] = jnp.concatenate([d["o"] for d in ch], axis=0)

        @pl.when(jnp.logical_not(safe))
        def _():
            v_s[...] = zi_ref[...]
            mask = _tri_mask(HG_SUB, rev)
            for c in order:
                base = c * HG_CHUNK
                rows = slice(base, base + HG_CHUNK)
                last = base if rev else base + HG_CHUNK - 1
                st = state[...]
                st_ref[c] = st
                b = b_s[rows, :]
                bl = b_s[last:last + 1, :]
                o_inter = _nt(q_s[rows, :] * jnp.exp(b), st)
                kd = k_s[rows, :] * jnp.exp(bl - b)
                state[...] = st * jnp.exp(bl) + _tn(v_s[rows, :], kd)
                parts = []
                for i_sub in range(nsb):
                    r0, r1, e0, e1, anchor = _sub_ranges(base, i_sub, rev)
                    q_i, k_i, b_i = q_s[r0:r1, :], k_s[r0:r1, :], b_s[r0:r1, :]
                    decay = jnp.exp(jnp.minimum(b_i[:, None, :] - b_i[None, :, :], 0.0))
                    a_d = jnp.where(mask, jnp.sum(q_i[:, None, :] * k_i[None, :, :] * decay, axis=-1), 0.0)
                    o_i = _nn(a_d, v_s[r0:r1, :])
                    if e1 > e0:
                        anc = b_s[anchor:anchor + 1, :]
                        q_t = q_i * jnp.exp(b_i - anc)
                        k_t = k_s[e0:e1, :] * jnp.exp(anc - b_s[e0:e1, :])
                        o_i = o_i + _nn(_nt(q_t, k_t), v_s[e0:e1, :])
                    parts.append(o_i)
                o_ref[rows, :] = o_inter + jnp.concatenate(parts, axis=0)

        if post is not None:
            other_ref, zg_ref, gain_ref = refs[4:7]
            o = o_ref[...] + other_ref[...]
            zg = zg_ref[...]
            o_ref[...] = o
            refs[n_in + 2][...] = (_rms(o, gain_ref[...]) * (zg * _sigmoid(zg))).astype(MXU_DTYPE)

    blk = lambda off: pl.BlockSpec((HG_BLOCK, HEAD), lambda h, i: (tmap(i), off + h))
    vec = pl.BlockSpec((1, HEAD), lambda h, i: (0, h))
    args, in_specs = [z, z, z, lb_row], [blk(0), blk(zf0), blk(24), vec]
    out_specs = [blk(0), pl.BlockSpec((None, ncb, HEAD, HEAD), lambda h, i: (h, tmap(i), 0, 0))]
    out_shape = [_sds((t, D_MODEL), F32), _sds((N_HEADS, t // HG_CHUNK, HEAD, HEAD), F32)]
    if post is not None:
        args += [post[0], z, post[1]]
        in_specs += [blk(0), blk(32), vec]
        out_specs.append(blk(0))
        out_shape.append(_sds((t, D_MODEL), MXU_DTYPE))
    return _pcall(
        body, name=name, grid=(N_HEADS, nb), in_specs=in_specs, out_specs=out_specs, out_shape=out_shape,
        scratch_shapes=[pltpu.VMEM((HEAD, HEAD), F32)] + [pltpu.VMEM((HG_BLOCK, HEAD), F32)] * 4,
        compiler_params=_params(2))(*args)


def _hgrn_bwd(name, z, d_o, states, lb_row, rev, dz, comm=None, prev=None, post=None):
    t = z.shape[0]
    nb = t // HG_BLOCK
    ncb = HG_BLOCK // HG_CHUNK
    nsb = HG_CHUNK // HG_SUB
    zf0 = 16 if rev else 8

    def tmap(i):
        return i if rev else nb - 1 - i

    def compute(in_refs, out_refs, scr):
        zq_ref, zf_ref, zi_ref, do_ref, st_ref, lb_ref = in_refs[:6]
        dq_ref, dv_ref, dzf_ref, dlb_ref = out_refs[:4]
        dstate, q_s, k_s, v_s, b_s, dq_s, dk_s, dv_s, db_s = scr
        first = pl.program_id(1) == 0

        @pl.when(first)
        def _():
            dstate[...] = jnp.zeros_like(dstate)

        if post is not None:
            o_ref, zg_ref, gain_ref = in_refs[7 + len(prev or ()):]
            d_o_ref, dzg_ref, dgain_ref = out_refs[4:]
            o_v, zg, gain, da = o_ref[...], zg_ref[...], gain_ref[...], do_ref[...]
            s = _sigmoid(zg)
            r = lax.rsqrt(_mean_last(o_v * o_v) + EPS)
            oh = o_v * r
            dy = da * (zg * s)
            dzg_ref[...] = (da * (oh * gain) * _dsilu(zg, s)).astype(dzg_ref.dtype)
            gdy = dy * gain
            d_o_ref[...] = r * (gdy - oh * _mean_last(gdy * oh))
            dgain = _rowsum(dy * oh)

            @pl.when(first)
            def _():
                dgain_ref[...] = dgain

            @pl.when(jnp.logical_not(first))
            def _():
                dgain_ref[...] += dgain

            do_ref = d_o_ref

        lb = lb_ref[...]
        for c in range(ncb):
            rows = slice(c * HG_CHUNK, (c + 1) * HG_CHUNK)
            q_c, k_c, logf, _ = _hg_gates(zq_ref[rows, :], zf_ref[rows, :], lb)
            q_s[rows, :] = q_c
            k_s[rows, :] = k_c
            b_s[rows, :] = _cumsum_rows(logf, HG_CHUNK, rev)
        order = range(ncb) if rev else range(ncb - 1, -1, -1)
        safe = _anchors_are_safe(b_s, ncb, rev)

        @pl.when(safe)
        def _():
            cmask = _tri_mask(HG_CHUNK, rev)
            row_i = lax.broadcasted_iota(jnp.int32, (HG_CHUNK, HEAD), 0)
            ch = []
            for c in range(ncb):
                base = c * HG_CHUNK
                rows = slice(base, base + HG_CHUNK)
                last = base if rev else base + HG_CHUNK - 1
                mid = base + HG_CHUNK // 2
                q_c, k_c, v_c, b, do_c = q_s[rows, :], k_s[rows, :], zi_ref[rows, :], b_s[rows, :], do_ref[rows, :]
                bl, bm = b_s[last:last + 1, :], b_s[mid:mid + 1, :]
                e, ebl, e_q, e_k = jnp.exp(b), jnp.exp(bl - b), jnp.exp(b - bm), jnp.exp(bm - b)
                ch.append(dict(q=q_c, k=k_c, v=v_c, do=do_c, e=e, el=jnp.exp(bl), ebl=ebl, e_q=e_q, e_k=e_k,
                               q_t=q_c * e_q, k_t=k_c * e_k, kd=k_c * ebl, last=last - base))
            for c, d in enumerate(ch):
                d["a"] = _nt(d["q_t"], d["k_t"])
                d["da"] = _nt(d["do"], d["v"])
                d["dq"] = _nn(d["do"], st_ref[c]) * d["e"]
                d["inc"] = _tn(d["do"], d["q"] * d["e"])
            for d in ch:
                da = jnp.where(cmask, d["da"], 0.0)
                d["dq"] = d["dq"] + d["e_q"] * _dot_f32(da, d["k_t"], 1, 0)
                d["dk_intra"] = d["e_k"] * _dot_f32(da, d["q_t"], 0, 0)
                d["dv"] = _tn(jnp.where(cmask, d["a"], 0.0), d["do"])
            dst = dstate[...]
            for c in order:
                d = ch[c]
                dk_inter = _nn(d["v"], dst) * d["ebl"]
                d["dk"] = dk_inter + d["dk_intra"]
                d["dv"] = _nt(d["kd"], dst) + d["dv"]
                d["extra"] = d["el"] * _rowsum(dst * st_ref[c]) + _rowsum(d["k"] * dk_inter)
                dst = d["inc"] + dst * d["el"]
            dstate[...] = dst
            for d in ch:
                d["db"] = d["q"] * d["dq"] - d["k"] * d["dk"] + jnp.where(row_i == d["last"], d["extra"], 0.0)
            dq_s[...] = jnp.concatenate([d["dq"] for d in ch], axis=0)
            dk_s[...] = jnp.concatenate([d["dk"] for d in ch], axis=0)
            dv_s[...] = jnp.concatenate([d["dv"] for d in ch], axis=0)
            db_s[...] = jnp.concatenate([d["db"] for d in ch], axis=0)

        @pl.when(jnp.logical_not(safe))
        def _():
            v_s[...] = zi_ref[...]
            mask = _tri_mask(HG_SUB, rev)
            for c in order:
                base = c * HG_CHUNK
                rows = slice(base, base + HG_CHUNK)
                last = base if rev else base + HG_CHUNK - 1
                st0 = st_ref[c]
                dst1 = dstate[...]
                b = b_s[rows, :]
                bl = b_s[last:last + 1, :]
                e = jnp.exp(b)
                el = jnp.exp(bl)
                ebl = jnp.exp(bl - b)
                q_c, k_c, v_c, do_c = q_s[rows, :], k_s[rows, :], v_s[rows, :], do_ref[rows, :]
                kd = k_c * ebl
                dq_s[rows, :] = _nn(do_c, st0) * e
                dk_inter = _nn(v_c, dst1) * ebl
                dk_s[rows, :] = dk_inter
                dv_s[rows, :] = _nt(kd, dst1)
                extra = el * _rowsum(dst1 * st0) + _rowsum(k_c * dk_inter)
                dstate[...] = _tn(do_c, q_c * e) + dst1 * el
                for i_sub in range(nsb):
                    r0, r1, e0, e1, anchor = _sub_ranges(base, i_sub, rev)
                    q_i, k_i, b_i, v_i, do_i = q_s[r0:r1, :], k_s[r0:r1, :], b_s[r0:r1, :], v_s[r0:r1, :], do_ref[r0:r1, :]
                    decay = jnp.exp(jnp.minimum(b_i[:, None, :] - b_i[None, :, :], 0.0))
                    a_d = jnp.where(mask, jnp.sum(q_i[:, None, :] * k_i[None, :, :] * decay, axis=-1), 0.0)
                    da_d = jnp.where(mask, _nt(do_i, v_i), 0.0)
                    wgt = da_d[:, :, None] * decay
                    dq_s[r0:r1, :] += jnp.sum(wgt * k_i[None, :, :], axis=1)
                    dk_s[r0:r1, :] += jnp.sum(wgt * q_i[:, None, :], axis=0)
                    dv_s[r0:r1, :] += _tn(a_d, do_i)
                    if e1 > e0:
                        anc = b_s[anchor:anchor + 1, :]
                        e_q = jnp.exp(b_i - anc)
                        e_k = jnp.exp(anc - b_s[e0:e1, :])
                        q_t = q_i * e_q
                        k_t = k_s[e0:e1, :] * e_k
                        a_o = _nt(q_t, k_t)
                        da_o = _nt(do_i, v_s[e0:e1, :])
                        dq_s[r0:r1, :] += e_q * _nn(da_o, k_t)
                        dk_s[e0:e1, :] += e_k * _tn(da_o, q_t)
                        dv_s[e0:e1, :] += _tn(a_o, do_i)
                db_s[rows, :] = q_c * dq_s[rows, :] - k_c * dk_s[rows, :]
                db_s[last:last + 1, :] += extra

        dlb = None
        for c in range(ncb):
            rows = slice(c * HG_CHUNK, (c + 1) * HG_CHUNK)
            zf = zf_ref[rows, :]
            s_pos, s_neg = _sigmoid(zf), _sigmoid(-zf)
            f = lb + (1.0 - lb) * s_pos
            dlogf = _cumsum_rows(db_s[rows, :], HG_CHUNK, not rev)
            df = jnp.where(f > jnp.finfo(F32).tiny, dlogf / f, 0.0)
            dfk = df - dk_s[rows, :]
            dzf_ref[rows, :] = ((1.0 - lb) * s_pos * s_neg * dfk).astype(dzf_ref.dtype)
            part = _rowsum(s_neg * dfk)
            dlb = part if dlb is None else dlb + part

        @pl.when(first)
        def _():
            dlb_ref[...] = dlb

        @pl.when(jnp.logical_not(first))
        def _():
            dlb_ref[...] += dlb

        if prev is None:
            dq_ref[...] = dq_s[...]
            dv_ref[...] = dv_s[...]
        else:
            zq = zq_ref[...]
            dq_ref[...] = ((dq_s[...] + in_refs[7][...]) * _dsilu(zq, _sigmoid(zq))).astype(dq_ref.dtype)
            dv_ref[...] = (dv_s[...] + in_refs[8][...]).astype(dv_ref.dtype)

    blk = lambda off: pl.BlockSpec((HG_BLOCK, HEAD), lambda h, i: (tmap(i), off + h))
    vec = pl.BlockSpec((1, HEAD), lambda h, i: (0, h))
    qv_dtype = F32 if prev is None else dz.dtype
    args = [z, z, z, d_o, states, lb_row, dz] + list(prev or ())
    in_specs = [blk(0), blk(zf0), blk(24), blk(0),
                pl.BlockSpec((None, ncb, HEAD, HEAD), lambda h, i: (h, tmap(i), 0, 0)), vec, ANY] + [blk(0)] * len(prev or ())
    out_shapes = [_sds((t, D_MODEL), qv_dtype), _sds((t, D_MODEL), qv_dtype), _sds(dz.shape, dz.dtype), _sds((1, D_MODEL), F32)]
    out_specs = [blk(0), blk(0), blk(zf0), vec]
    if post is not None:
        args += [post[0], z, post[1]]
        in_specs += [blk(0), blk(32), vec]
        out_shapes += [_sds((t, D_MODEL), F32), _sds((t, D_MODEL), dz.dtype), _sds((1, D_MODEL), F32)]
        out_specs += [blk(0), blk(0), vec]
    outs, bufs = _hosted_call(
        name, compute, (N_HEADS, nb), args, in_specs, out_shapes, out_specs,
        [pltpu.VMEM((HEAD, HEAD), F32)] + [pltpu.VMEM((HG_BLOCK, HEAD), F32)] * 8, {6: 2}, comm)
    return outs if comm is None else (outs, bufs)


def _lower_bounds(name, gamma):
    def body(g_ref, o_ref):
        g0, g1 = g_ref[0:1, :], g_ref[1:2, :]
        m = jnp.maximum(g0, g1)
        e0, e1 = jnp.exp(g0 - m), jnp.exp(g1 - m)
        s0, s1 = e0 / (e0 + e1), e1 / (e0 + e1)
        o_ref[0:1, :] = s0 - s0
        o_ref[1:2, :] = (s0 + s1) - s0

    return _pcall(body, name=name, out_shape=_sds(gamma.shape, F32))(gamma)


def _lower_bounds_bwd(name, gamma, dlb):
    def body(g_ref, d_ref, o_ref):
        g0, g1 = g_ref[0:1, :], g_ref[1:2, :]
        m = jnp.maximum(g0, g1)
        e0, e1 = jnp.exp(g0 - m), jnp.exp(g1 - m)
        s0, s1 = e0 / (e0 + e1), e1 / (e0 + e1)
        d0, d1 = d_ref[0:1, :], d_ref[1:2, :]
        ds0 = (d0 + d1) - (d0 + d1)
        ds1 = d1
        inner = s0 * ds0 + s1 * ds1
        o_ref[0:1, :] = s0 * (ds0 - inner)
        o_ref[1:2, :] = s1 * (ds1 - inner)

    return _pcall(body, name=name, out_shape=_sds(gamma.shape, F32))(gamma, dlb)


def _place_in_dz(name, piece, blk, dz):
    t = piece.shape[0]
    tile = 1024
    return _rows(name, lambda a: (a,), t // tile, [piece], [_tile_spec(tile, D_MODEL)],
                 [_sds(dz.shape, dz.dtype)], [_tile_spec(tile, D_MODEL, blk)], into=[(dz, 0)])[0]


def _sg_norm(zv, ln_g, ln_b):
    gv = _gelu(zv)
    xc = gv - _mean_last(gv)
    rstd = lax.rsqrt(_mean_last(xc * xc) + EPS)
    xhat = xc * rstd
    return xhat * ln_g + ln_b, xhat, rstd


def _lane_lo():
    return lax.broadcasted_iota(jnp.int32, (SG_CHUNK, LANES), 1) < (LANES // 2)


SG_TILE = 512


def _sgu_fwd(name, z, w, bias_t, ln_g, ln_b):
    t = z.shape[0]

    def fn(zu, zv, wv, bt, lg, lb):
        u = _gelu(zu)
        vn, _, _ = _sg_norm(zv, lg, lb)
        lo = _lane_lo()
        out_rows = []
        for c in range(SG_TILE // SG_CHUNK):
            rs = slice(c * SG_CHUNK, (c + 1) * SG_CHUNK)
            cols = []
            for j in range(SG_WIDTH // LANES):
                cs = slice(j * LANES, (j + 1) * LANES)
                vb = vn[rs, cs]
                sg = jnp.where(lo, _nn(wv[2 * j], vb), _nn(wv[2 * j + 1], vb)) + bt[:, cs]
                cols.append(u[rs, cs] * sg)
            out_rows.append(jnp.concatenate(cols, axis=1))
        return (jnp.concatenate(out_rows, axis=0),)

    return _rows(name, fn, t // SG_TILE, [z, z, w, bias_t, ln_g, ln_b],
                 [_tile_spec(SG_TILE, SG_WIDTH, 10), _tile_spec(SG_TILE, SG_WIDTH, 11), _whole(w.shape),
                  _whole(bias_t.shape), _whole((1, SG_WIDTH)), _whole((1, SG_WIDTH))],
                 [_sds((t, SG_WIDTH), MXU_DTYPE)], [_tile_spec(SG_TILE, SG_WIDTH)])[0]


def _sgu_bwd(name, dbo, z, w, bias_t, ln_g, ln_b, dz):
    t = z.shape[0]
    n_grp = w.shape[0]

    def fn(dbov, zu, zv, wv, bt, lg, lb):
        u = _gelu(zu)
        vn, xhat, rstd = _sg_norm(zv, lg, lb)
        lo = _lane_lo()
        dw = [None] * n_grp
        dsg_sum = None
        du_rows, dvn_rows = [], []
        for c in range(SG_TILE // SG_CHUNK):
            rs = slice(c * SG_CHUNK, (c + 1) * SG_CHUNK)
            du_cols, dvn_cols, dsg_cols = [], [], []
            for j in range(SG_WIDTH // LANES):
                cs = slice(j * LANES, (j + 1) * LANES)
                vb = vn[rs, cs]
                sg = jnp.where(lo, _nn(wv[2 * j], vb), _nn(wv[2 * j + 1], vb)) + bt[:, cs]
                du_cols.append(dbov[rs, cs] * sg)
                dsg = dbov[rs, cs] * u[rs, cs]
                dsg_cols.append(dsg)
                d0 = _nt(jnp.where(lo, dsg, 0.0), vb)
                d1 = _nt(jnp.where(lo, 0.0, dsg), vb)
                dw[2 * j] = d0 if dw[2 * j] is None else dw[2 * j] + d0
                dw[2 * j + 1] = d1 if dw[2 * j + 1] is None else dw[2 * j + 1] + d1
                dvn_cols.append(jnp.where(lo, _tn(wv[2 * j], dsg), _tn(wv[2 * j + 1], dsg)))
            du_rows.append(jnp.concatenate(du_cols, axis=1))
            dvn_rows.append(jnp.concatenate(dvn_cols, axis=1))
            dsg_c = jnp.concatenate(dsg_cols, axis=1)
            dsg_sum = dsg_c if dsg_sum is None else dsg_sum + dsg_c
        du = jnp.concatenate(du_rows, axis=0)
        dvn = jnp.concatenate(dvn_rows, axis=0)
        dxhat = dvn * lg
        dgv = rstd * (dxhat - _mean_last(dxhat) - xhat * _mean_last(dxhat * xhat))
        dzuv = jnp.concatenate([du * _dgelu(zu), dgv * _dgelu(zv)], axis=1)
        return dzuv, jnp.stack(dw, axis=0), dsg_sum, _rowsum(dvn * xhat), _rowsum(dvn)

    return _rows(name, fn, t // SG_TILE, [dbo, z, z, w, bias_t, ln_g, ln_b],
                 [_tile_spec(SG_TILE, SG_WIDTH), _tile_spec(SG_TILE, SG_WIDTH, 10), _tile_spec(SG_TILE, SG_WIDTH, 11),
                  _whole(w.shape), _whole(bias_t.shape), _whole((1, SG_WIDTH)), _whole((1, SG_WIDTH))],
                 [_sds(dz.shape, dz.dtype), _sds(w.shape, F32), _sds((SG_CHUNK, SG_WIDTH), F32),
                  _sds((1, SG_WIDTH), F32), _sds((1, SG_WIDTH), F32)],
                 [_tile_spec(SG_TILE, 2 * SG_WIDTH, 5), _whole(w.shape), _whole((SG_CHUNK, SG_WIDTH)),
                  _whole((1, SG_WIDTH)), _whole((1, SG_WIDTH))],
                 n_red=4, into=[(dz, 0)])


def _proj_out_t_merge_bwd(name, dmix, w_g, pa, pb, z, l):
    t = z.shape[0]

    def body(d_ref, w_ref, pa_ref, pb_ref, ga_ref, gb_ref, dpa_ref, dpb_ref, dz_ref):
        d = _nt(d_ref[...], _merge_lead(w_ref[...]))
        sa, sb = _sigmoid(ga_ref[...]), _sigmoid(gb_ref[...])
        dpa_ref[...] = (d * sa).astype(dpa_ref.dtype)
        dpb_ref[...] = (d * sb).astype(dpb_ref.dtype)
        dz_ref[...] = jnp.concatenate([d * pa_ref[...] * sa * (1.0 - sa), d * pb_ref[...] * sb * (1.0 - sb)],
                                      axis=1).astype(dz_ref.dtype)

    row = lambda blk: _bs((TF, D_MODEL), lambda m: (m, blk))
    return _pcall(body, name=name, grid=(t // TF,),
                  in_specs=[row(0), _bs((N_CHIPS, None, 256, D_MODEL), lambda m: (0, l, 0, 0)), row(0), row(0), row(6), row(7)],
                  out_specs=[row(0), row(0), _bs((TF, 2 * D_MODEL), lambda m: (m, 3))],
                  out_shape=[_sds((t, D_MODEL), MXU_DTYPE), _sds((t, D_MODEL), MXU_DTYPE), _sds((t, N_IN), MXU_DTYPE)],
                  compiler_params=_params(1))(dmix, w_g, pa, pb, z, z)


def _ple_bwd(name, dx, pe, gz):
    t, d = dx.shape
    tile = 512

    def fn(dv, p, g):
        s = _sigmoid(g)
        return dv * s, dv * p * s * (1.0 - s)

    return _rows(name, fn, t // tile, [dx, pe, gz], [_tile_spec(tile, d)] * 3, [_sds((t, d), MXU_DTYPE)] * 2,
                 [_tile_spec(tile, d)] * 2)


def _loss_bwd(name, y, target):
    t, d = y.shape
    tile = 512

    def fn(yv, tv):
        err = yv - tv
        return err * (1.0 / d), _rowsum(err * err)

    return _rows(name, fn, t // tile, [y, target], [_tile_spec(tile, d)] * 2, [_sds((t, d), F32), _sds((1, d), F32)],
                 [_tile_spec(tile, d), _whole((1, d))], n_red=1)


def _position():
    return lax.axis_index("x"), lax.axis_index("y"), lax.axis_index("c")


def _gather_comm(shards, bufs, l0, nl):
    n = len(shards)
    if bufs is None:
        bufs = [_sds((N_CHIPS,) + s.shape, s.dtype) for s in shards]

    def make(w_refs, out_refs, sems):
        send_sems, recv_sems, local_sems = sems
        x, y, c = _position()
        me = 2 * x + y
        sibling = (x, y, 1 - c)
        chips = [(1 - x, y), (x, 1 - y), (1 - x, 1 - y)]

        def half(ref, cc):
            rows = ref.shape[1] // 2
            return ref.at[pl.ds(l0, nl), pl.ds(cc * rows, rows)]

        def copy(i, k, src, chip, cc, to):
            return pltpu.make_async_remote_copy(
                src_ref=src, dst_ref=half(out_refs[i].at[chip], cc), send_sem=send_sems.at[6 * i + k],
                recv_sem=recv_sems.at[6 * i + k], device_id=to, device_id_type=MESH)

        def local(i):
            return pltpu.make_async_copy(w_refs[i].at[pl.ds(l0, nl)], out_refs[i].at[me, pl.ds(l0, nl)], local_sems.at[i])

        def sends():
            return [copy(i, j, half(w_refs[i], c), me, c, (px, py, c))
                    for i in range(n) for j, (px, py) in enumerate(chips)]

        def start():
            for i in range(n):
                local(i).start()
            for cp in sends():
                cp.start()

        def finish():
            passed = []
            for i in range(n):
                for j, (px, py) in enumerate(chips):
                    chip = 2 * px + py
                    copy(i, j, half(w_refs[i], c), chip, c, (px, py, c)).wait_recv()
                    fwd = copy(i, 3 + j, half(out_refs[i].at[chip], c), chip, c, sibling)
                    fwd.start()
                    passed.append(fwd)
            for i in range(n):
                for j, (px, py) in enumerate(chips):
                    copy(i, 3 + j, half(w_refs[i], c), 2 * px + py, 1 - c, sibling).wait_recv()
            for cp in sends() + passed:
                cp.wait_send()
            for i in range(n):
                local(i).wait()

        return start, finish

    sems = [pltpu.SemaphoreType.DMA((6 * n,)), pltpu.SemaphoreType.DMA((6 * n,)), pltpu.SemaphoreType.DMA((n,))]
    return _Comm(shards, bufs, sems, make)


def _exchange_comm(grads, bufs, l0, nl):
    n = len(grads)
    if bufs is None:
        bufs = [_sds((N_DEV,) + g.shape[1:], g.dtype) for g in grads]

    def make(g_refs, out_refs, sems):
        send_sems, recv_sems, local_sems = sems
        x, y, c = _position()
        me = 2 * x + y
        sibling = (x, y, 1 - c)
        chips = [(1 - x, y), (x, 1 - y), (1 - x, 1 - y)]

        def lay(ref):
            return ref.at[pl.ds(l0, nl)]

        def copy(i, k, src, slot, to):
            return pltpu.make_async_remote_copy(
                src_ref=src, dst_ref=lay(out_refs[i].at[slot]), send_sem=send_sems.at[7 * i + k],
                recv_sem=recv_sems.at[7 * i + k], device_id=to, device_id_type=MESH)

        def local(i):
            return pltpu.make_async_copy(lay(g_refs[i].at[me]), lay(out_refs[i].at[2 * me + c]), local_sems.at[i])

        def first():
            cps = []
            for i in range(n):
                cps.append(copy(i, 0, lay(g_refs[i].at[me]), 2 * me + c, sibling))
                for j, (px, py) in enumerate(chips):
                    cps.append(copy(i, 1 + j, lay(g_refs[i].at[2 * px + py]), 2 * me + c, (px, py, c)))
            return cps

        def start():
            for i in range(n):
                local(i).start()
            for cp in first():
                cp.start()

        def finish():
            passed = []
            for i in range(n):
                for j, (px, py) in enumerate(chips):
                    slot = 2 * (2 * px + py) + c
                    copy(i, 1 + j, lay(g_refs[i].at[me]), slot, (px, py, c)).wait_recv()
                    fwd = copy(i, 4 + j, lay(out_refs[i].at[slot]), slot, sibling)
                    fwd.start()
                    passed.append(fwd)
            for i in range(n):
                copy(i, 0, lay(g_refs[i].at[me]), 2 * me + (1 - c), sibling).wait_recv()
                for j, (px, py) in enumerate(chips):
                    copy(i, 4 + j, lay(g_refs[i].at[me]), 2 * (2 * px + py) + (1 - c), sibling).wait_recv()
            for cp in first() + passed:
                cp.wait_send()
            for i in range(n):
                local(i).wait()

        return start, finish

    sems = [pltpu.SemaphoreType.DMA((7 * n,)), pltpu.SemaphoreType.DMA((7 * n,)), pltpu.SemaphoreType.DMA((n,))]
    return _Comm(grads, bufs, sems, make)


def _all_reduce_small(packed):
    rows = packed.shape[0]

    def body(x_ref, sum_ref, slots, send_sems, recv_sems, local_sem):
        x, y, c = _position()
        me = 4 * x + 2 * y + c
        mine = pltpu.make_async_copy(x_ref, slots.at[me], local_sem)
        mine.start()
        sends = []
        for k in range(1, N_DEV):
            peer = (x ^ (k >> 2), y ^ ((k >> 1) & 1), c ^ (k & 1))
            cp = pltpu.make_async_remote_copy(src_ref=x_ref, dst_ref=slots.at[me], send_sem=send_sems.at[k - 1],
                                              recv_sem=recv_sems.at[k - 1], device_id=peer, device_id_type=MESH)
            cp.start()
            sends.append(cp)
        for k in range(1, N_DEV):
            px, py, pc = x ^ (k >> 2), y ^ ((k >> 1) & 1), c ^ (k & 1)
            pltpu.make_async_remote_copy(src_ref=x_ref, dst_ref=slots.at[4 * px + 2 * py + pc], send_sem=send_sems.at[k - 1],
                                         recv_sem=recv_sems.at[k - 1], device_id=(px, py, pc), device_id_type=MESH).wait_recv()
        for cp in sends:
            cp.wait_send()
        mine.wait()
        total = slots[0]
        for d in range(1, N_DEV):
            total = total + slots[d]
        sum_ref[...] = total

    vmem = pl.BlockSpec(memory_space=pltpu.VMEM)
    return _pcall(
        body, name="all_reduce_small", in_specs=[vmem], out_specs=vmem, out_shape=_sds(packed.shape, F32),
        scratch_shapes=[pltpu.VMEM((N_DEV, rows, LANES), F32), pltpu.SemaphoreType.DMA((N_DEV - 1,)),
                        pltpu.SemaphoreType.DMA((N_DEV - 1,)), pltpu.SemaphoreType.DMA],
        compiler_params=pltpu.CompilerParams(vmem_limit_bytes=VMEM_LIMIT_BYTES),
    )(packed)


def _adamw(w, g, m, v):
    m = ADAM_B1 * m + (1.0 - ADAM_B1) * g
    v = ADAM_B2 * v + (1.0 - ADAM_B2) * (g * g)
    m_hat = m / (1.0 - ADAM_B1 ** ADAM_STEP)
    v_hat = v / (1.0 - ADAM_B2 ** ADAM_STEP)
    delta = -ADAM_LR * (m_hat / (jnp.sqrt(v_hat) + ADAM_EPS) + ADAM_WD * w)
    return delta, m, v


def _adam_sharded(name, parts, w, m, v):
    shape = w.shape
    cols = shape[-1]
    rows = w.size // cols
    tile = 8
    while tile * 2 * cols <= ADAM_TILE_ELEMS and rows % (tile * 2) == 0:
        tile *= 2

    def fn(p, wv, mv, vv):
        g = p[0].astype(F32)
        for d in range(1, N_DEV):
            g = g + p[d].astype(F32)
        return (g,) + _adamw(wv, g, mv, vv)

    two_d = lambda a: a.reshape(rows, cols)
    outs = _rows(name, fn, rows // tile, [parts.reshape(N_DEV, rows, cols), two_d(w), two_d(m), two_d(v)],
                 [pl.BlockSpec((N_DEV, tile, cols), lambda i: (0, i, 0))] + [_tile_spec(tile, cols)] * 3,
                 [_sds((rows, cols), F32)] * 4, [_tile_spec(tile, cols)] * 4)
    return [o.reshape(shape) for o in outs]


def _adam_small(name, g, w, m, v):
    rows = g.shape[0]
    tile = rows // 2
    return _rows(name, lambda gv, wv, mv, vv: _adamw(wv, gv, mv, vv), rows // tile, [g, w, m, v],
                 [_tile_spec(tile, LANES)] * 4, [_sds(g.shape, F32)] * 3, [_tile_spec(tile, LANES)] * 3)


BIG = ("w_in", "w_a", "w_b", "w_out", "w_gate", "w_up", "w_down", "w_ple", "w_ple_gate")
SMALL = ("norm_mix_pre", "lb_gamma_fwd", "lb_gamma_bwd", "hg_norm", "sg_w", "sg_b", "sg_ln_g", "sg_ln_b",
         "norm_mix_post", "norm_ffn_pre", "norm_ffn_post")


def _with_comm(plan, tag, state, call):
    if tag not in plan:
        return call(None)
    keys, comm = plan[tag](state)
    res, bufs = call(comm)
    state.update(zip(keys, bufs))
    return res


def _layer_fwd(l, x, p_l, wg, sm, plan):
    sv = {"x": x}
    h = _rms_fwd(f"norm_mix_pre_l{l}", x, sm["norm_mix_pre"])
    z = _with_comm(plan, "proj_in", wg, lambda comm: _proj_in(h, wg["w_in"], l, comm=comm))
    o_f, st_f = _hgrn_fwd(f"hgrn_fwd_l{l}", z, sm["lb_fwd"], False)
    o_sum, st_b, a_out = _hgrn_fwd(f"hgrn_rev_l{l}", z, sm["lb_bwd"], True, post=(o_f, sm["hg_norm"]))
    b_out = _sgu_fwd(f"sgu_l{l}", z, sm["sg_w"], sm["sg_bias_t"], sm["sg_ln_g"], sm["sg_ln_b"])
    pa = _proj_rows_sharded(f"proj_a_l{l}", a_out, wg["w_a"], l, F32)
    pb = _proj_cols256(f"proj_b_l{l}", b_out, wg["w_b"], l)
    merged, mix = _with_comm(plan, "merge_proj_out", wg, lambda comm: _merge_proj_out(
        f"merge_proj_out_l{l}", pa, pb, z, wg["w_out"], l, comm=comm))
    x1, h2 = _resid_rms_norm_fwd(f"norm_mix_post_ffn_pre_l{l}", x, mix, sm["norm_mix_post"], sm["norm_ffn_pre"])
    gt, up, act = _with_comm(plan, "proj_gate_up", wg, lambda comm: _proj_gate_up(
        f"proj_gate_up_l{l}", h2, wg["w_gate"], wg["w_up"], l, comm=comm))
    ff = _with_comm(plan, "proj_down", wg, lambda comm: _proj_ffn_out(f"proj_down_l{l}", act, wg["w_down"], l, comm=comm))
    x2 = _resid_rms_fwd(f"norm_ffn_post_l{l}", x1, ff, sm["norm_ffn_post"])
    pe = _proj_cols256(f"proj_ple_l{l}", p_l, wg["w_ple"], l)
    gz, x3 = _proj_ple_gate_ple(f"proj_ple_gate_ple_l{l}", x2, wg["w_ple_gate"], pe, l)
    sv.update(h=h, z=z, o=o_sum, st_f=st_f, st_b=st_b, a_out=a_out, b_out=b_out, pa=pa, pb=pb,
              merged=merged, mix=mix, x1=x1, h2=h2, gt=gt, up=up, act=act, ff=ff, x2=x2, pe=pe, gz=gz, p=p_l)
    return x3, sv


def _layer_bwd(l, dx3, sv, wg, sm, gw, parts, plan):
    t = dx3.shape[0]
    nt = t // TT
    sg = {}

    def wgrad(key, *a, **k):
        gw[key] = _wgrad(f"grad_{key}_l{l}", *a, l=l, into=gw.get(key), **k)

    row = lambda width: _bs((TT, width), lambda j, i: (i, 0))
    row_j = lambda width: _bs((TT, width), lambda j, i: (i, j))
    ffn_j = _bs((None, TT, FFN_SHARD), lambda j, i: (j, i, 0))
    blk_j = lambda shape: (tuple(shape), lambda j, i: (j, l, 0, 0))

    dpe, dgz = _ple_bwd(f"ple_bwd_l{l}", dx3, sv["pe"], sv["gz"])
    wgrad("w_ple", sv["p"], row(PLE_DIM), dpe, row_j(256), (PLE_DIM, 256), blk_j((PLE_DIM, 256)), N_CHIPS)
    wgrad("w_ple_gate", sv["x2"], row_j(256), dgz, row(D_MODEL), (256, D_MODEL), blk_j((256, D_MODEL)), N_CHIPS)
    dx2 = _proj_rows_sharded_t(f"proj_ple_gate_t_l{l}", dgz, wg["w_ple_gate"], l, F32, add=dx3)

    dff, sg["norm_ffn_post"] = _rms_bwd(f"norm_ffn_post_bwd_l{l}", sv["ff"], sm["norm_ffn_post"], dx2, None, MXU_DTYPE)
    dgt, dup = _with_comm(plan, "proj_down_t", parts, lambda comm: _proj_down_t_swiglu_bwd(
        f"proj_down_t_swiglu_bwd_l{l}", dff, wg["w_down"], sv["gt"], sv["up"], l, comm=comm))
    wgrad("w_down", sv["act"], ffn_j, dff, row(D_MODEL), (FFN_SHARD, D_MODEL), blk_j((FFN_SHARD, D_MODEL)), N_CHIPS)
    dh2 = _with_comm(plan, "proj_gate_up_t", parts, lambda comm: _proj_ffn_in_t(
        f"proj_gate_up_t_l{l}", [(dgt, wg["w_gate"]), (dup, wg["w_up"])], l, comm=comm))
    wgrad("w_gate", sv["h2"], row(D_MODEL), dgt, ffn_j, (D_MODEL, FFN_SHARD), blk_j((D_MODEL, FFN_SHARD)), N_CHIPS)
    wgrad("w_up", sv["h2"], row(D_MODEL), dup, ffn_j, (D_MODEL, FFN_SHARD), blk_j((D_MODEL, FFN_SHARD)), N_CHIPS)
    dx1, dmix, sg["norm_ffn_pre"], sg["norm_mix_post"] = _rms_bwd_pair(
        f"norm_ffn_pre_mix_post_bwd_l{l}", sv["x1"], sm["norm_ffn_pre"], dh2, dx2, sv["mix"], sm["norm_mix_post"], MXU_DTYPE)

    dpa, dpb, dz = _proj_out_t_merge_bwd(f"proj_out_t_merge_bwd_l{l}", dmix, wg["w_out"], sv["pa"], sv["pb"], sv["z"], l)
    wgrad("w_out", sv["merged"], row_j(256), dmix, row(D_MODEL), (256, D_MODEL), blk_j((256, D_MODEL)), N_CHIPS)
    da = _proj_rows_sharded_t(f"proj_a_t_l{l}", dpa, wg["w_a"], l, F32)
    wgrad("w_a", sv["a_out"], row_j(256), dpa, row(D_MODEL), (256, D_MODEL), blk_j((256, D_MODEL)), N_CHIPS)
    dbo = _proj_cols256_t(f"proj_b_t_l{l}", dpb, wg["w_b"], l)
    wgrad("w_b", sv["b_out"], row(SG_WIDTH), dpb, row_j(256), (SG_WIDTH, 256), blk_j((SG_WIDTH, 256)), N_CHIPS)

    dz, sg["sg_w"], dsg_sum, sg["sg_ln_g"], sg["sg_ln_b"] = _sgu_bwd(
        f"sgu_bwd_l{l}", dbo, sv["z"], sm["sg_w"], sm["sg_bias_t"], sm["sg_ln_g"], sm["sg_ln_b"], dz)
    sg["sg_b"] = dsg_sum.reshape(SG_CHUNK, N_HEADS, SG_WIDTH // N_HEADS).sum(axis=-1).T
    dq_f, dv_f, dz, sg["lb_fwd"], d_o, dzg, sg["hg_norm"] = _with_comm(plan, "hgrn_fwd_bwd", parts, lambda comm: _hgrn_bwd(
        f"hgrn_fwd_bwd_l{l}", sv["z"], da, sv["st_f"], sm["lb_fwd"], False, dz, comm=comm, post=(sv["o"], sm["hg_norm"])))
    dz = _place_in_dz(f"place_dzg_l{l}", dzg, 4, dz)
    dzq, dzi, dz, sg["lb_bwd"] = _hgrn_bwd(f"hgrn_rev_bwd_l{l}", sv["z"], d_o, sv["st_b"], sm["lb_bwd"], True, dz,
                                           prev=(dq_f, dv_f))
    dz = _place_in_dz(f"place_dzq_l{l}", dzq, 0, dz)
    dz = _place_in_dz(f"place_dzi_l{l}", dzi, 3, dz)

    gw["w_in"] = _wgrad(f"grad_w_in_l{l}", sv["h"], _bs((TM_WIDE, D_MODEL), lambda n, i: (i, 0)), dz,
                        _bs((TM_WIDE, 1024), lambda n, i: (i, n)), (D_MODEL, 2048),
                        ((D_MODEL, 1024), lambda n, i: (n // 2, l, 0, n % 2)), 8, l, gw.get("w_in"), tt=TM_WIDE)
    dh = _with_comm(plan, "proj_in_t", parts, lambda comm: _proj_in_t(dz, wg["w_in"], l, comm=comm))
    dx, sg["norm_mix_pre"] = _rms_bwd(f"norm_mix_pre_bwd_l{l}", sv["x"], sm["norm_mix_pre"], dh, dx1, F32)
    del nt
    return dx, gw, sg


def _pack(parts):
    return jnp.concatenate([a.reshape(-1, LANES) for a in parts], axis=0)


def _step(x, p, loss_target, w, m, v):
    x = x[0]
    target = loss_target[0]
    depth = w["w_in"].shape[0]

    assert depth == 2, "the exchanges below ride in layer 0's kernels and carry layer 1's data"
    shards = {k: w[k].astype(MXU_DTYPE) for k in BIG}
    rest_a = [k for k in BIG if k not in ("w_in", "w_gate", "w_up")]
    rest_b = ["w_gate", "w_up"]
    rest = rest_a + rest_b

    def gather(keys, l0, extend):
        return lambda wg: (keys, _gather_comm([shards[k] for k in keys], [wg[k] for k in keys] if extend else None, l0, 1))

    wg = {"w_in": _run_comm("gather_w_in_l0", gather(["w_in"], 0, False)(None)[1])[0]}
    fwd_plans = [{"proj_in": gather(rest, 0, False), "merge_proj_out": gather(["w_in"], 1, True),
                  "proj_gate_up": gather(rest_a, 1, True), "proj_down": gather(rest_b, 1, True)}, {}]
    lb_f = _lower_bounds("lower_bounds_fwd", w["lb_gamma_fwd"])
    lb_b = _lower_bounds("lower_bounds_bwd", w["lb_gamma_bwd"])

    def small_of(l):
        sm = {k: w[k][l:l + 1] for k in ("norm_mix_pre", "hg_norm", "sg_ln_g", "sg_ln_b", "norm_mix_post",
                                        "norm_ffn_pre", "norm_ffn_post")}
        sm["lb_fwd"], sm["lb_bwd"] = lb_f[l:l + 1], lb_b[l:l + 1]
        sm["sg_w"] = w["sg_w"][l]
        sm["sg_bias_t"] = jnp.repeat(w["sg_b"][l].T, SG_WIDTH // N_HEADS, axis=1)
        return sm

    saved = []
    h = x
    for l in range(depth):
        h, sv = _layer_fwd(l, h, p[l, 0], wg, small_of(l), fwd_plans[l])
        saved.append(sv)

    dy, sq_err = _loss_bwd("loss", h, target)
    gw, parts = {}, {}

    def exchange(keys, l0, extend):
        return lambda parts: (keys, _exchange_comm([gw[k] for k in keys], [parts[k] for k in keys] if extend else None, l0, 1))

    bwd_plans = [{"proj_down_t": exchange(["w_in"], 1, False), "proj_gate_up_t": exchange(rest, 1, False),
                  "hgrn_fwd_bwd": exchange(rest, 0, True), "proj_in_t": exchange(["w_in"], 0, True)}, {}]
    small_grads = [None] * depth
    for l in reversed(range(depth)):
        dy, gw, small_grads[l] = _layer_bwd(l, dy, saved[l], wg, small_of(l), gw, parts, bwd_plans[l])

    def stack(key):
        return jnp.concatenate([small_grads[l][key].reshape((1,) + w_shape[1:]) for l in range(depth)], axis=0)

    g_small = {}
    for key in SMALL:
        w_shape = w[key].shape
        if key == "lb_gamma_fwd":
            dlb = jnp.concatenate([small_grads[l]["lb_fwd"] for l in range(depth)], axis=0)
            g_small[key] = _lower_bounds_bwd("lower_bounds_fwd_bwd", w[key], dlb)
        elif key == "lb_gamma_bwd":
            dlb = jnp.concatenate([small_grads[l]["lb_bwd"] for l in range(depth)], axis=0)
            g_small[key] = _lower_bounds_bwd("lower_bounds_bwd_bwd", w[key], dlb)
        else:
            g_small[key] = stack(key)

    packed = _pack([g_small[k] for k in SMALL] + [sq_err])
    summed = _all_reduce_small(packed)
    n_small_rows = sum(w[k].size for k in SMALL) // LANES
    loss = 0.5 * jnp.sum(summed[n_small_rows:]) / D_MODEL

    g_rows = summed[:n_small_rows]
    d_rows, m_rows, v_rows = _adam_small("adamw_small", g_rows, _pack([w[k] for k in SMALL]),
                                         _pack([m[k] for k in SMALL]), _pack([v[k] for k in SMALL]))
    out = {}
    off = 0
    for key in SMALL:
        n_rows = w[key].size // LANES
        sl = slice(off, off + n_rows)
        out[key] = tuple(a[sl].reshape(w[key].shape) for a in (g_rows, d_rows, m_rows, v_rows))
        off += n_rows

    for key in BIG:
        out[key] = tuple(_adam_sharded(f"adamw_{key}", parts[key], w[key], m[key], v[key]))
    return loss, dy[None], out


WEIGHTS = ("norm_mix_pre", "w_in", "lb_gamma_fwd", "lb_gamma_bwd", "hg_norm", "sg_w", "sg_b", "sg_ln_g", "sg_ln_b",
           "w_a", "w_b", "w_out", "norm_mix_post", "norm_ffn_pre", "w_gate", "w_up", "w_down", "norm_ffn_post",
           "w_ple", "w_ple_gate")


def kernel(x, p, norm_mix_pre, w_in, lb_gamma_fwd, lb_gamma_bwd, hg_norm, sg_w, sg_b, sg_ln_g, sg_ln_b, w_a, w_b, w_out, norm_mix_post, norm_ffn_pre, w_gate, w_up, w_down, norm_ffn_post, w_ple, w_ple_gate, loss_target, m_norm_mix_pre, m_w_in, m_lb_gamma_fwd, m_lb_gamma_bwd, m_hg_norm, m_sg_w, m_sg_b, m_sg_ln_g, m_sg_ln_b, m_w_a, m_w_b, m_w_out, m_norm_mix_post, m_norm_ffn_pre, m_w_gate, m_w_up, m_w_down, m_norm_ffn_post, m_w_ple, m_w_ple_gate, v_norm_mix_pre, v_w_in, v_lb_gamma_fwd, v_lb_gamma_bwd, v_hg_norm, v_sg_w, v_sg_b, v_sg_ln_g, v_sg_ln_b, v_w_a, v_w_b, v_w_out, v_norm_mix_post, v_norm_ffn_pre, v_w_gate, v_w_up, v_w_down, v_norm_ffn_post, v_w_ple, v_w_ple_gate):
    w = dict(zip(WEIGHTS, (norm_mix_pre, w_in, lb_gamma_fwd, lb_gamma_bwd, hg_norm, sg_w, sg_b, sg_ln_g, sg_ln_b, w_a, w_b, w_out, norm_mix_post, norm_ffn_pre, w_gate, w_up, w_down, norm_ffn_post, w_ple, w_ple_gate)))
    m = dict(zip(WEIGHTS, (m_norm_mix_pre, m_w_in, m_lb_gamma_fwd, m_lb_gamma_bwd, m_hg_norm, m_sg_w, m_sg_b, m_sg_ln_g, m_sg_ln_b, m_w_a, m_w_b, m_w_out, m_norm_mix_post, m_norm_ffn_pre, m_w_gate, m_w_up, m_w_down, m_norm_ffn_post, m_w_ple, m_w_ple_gate)))
    v = dict(zip(WEIGHTS, (v_norm_mix_pre, v_w_in, v_lb_gamma_fwd, v_lb_gamma_bwd, v_hg_norm, v_sg_w, v_sg_b, v_sg_ln_g, v_sg_ln_b, v_w_a, v_w_b, v_w_out, v_norm_mix_post, v_norm_ffn_pre, v_w_gate, v_w_up, v_w_down, v_norm_ffn_post, v_w_ple, v_w_ple_gate)))
    loss, grad_x, out = _step(x, p, loss_target, w, m, v)
    res = [loss, grad_x]
    for i in range(4):
        res += [out[k][i] for k in WEIGHTS]
    return tuple(res)
```
